```python
import math
import jax, jax.numpy as jnp
from jax import lax
import numpy as np

D_MODEL = 1024
BATCH = 16
SEQ = 2048
DEPTH = 1

MIX_WIDTH = D_MODEL
ATT_WIDTH = MIX_WIDTH // 2
SGU_WIDTH = MIX_WIDTH - ATT_WIDTH
DIFF_HEADS = 4
DIFF_HEAD_DIM = ATT_WIDTH // DIFF_HEADS // 2
DIFF_VALUE_DIM = 2 * DIFF_HEAD_DIM
ROPE_DIM = DIFF_HEAD_DIM // 4
ROPE_THETA = 500000.0
QBLOCK = 128
SGU_GROUPS = 4
SGU_GROUP_WIDTH = SGU_WIDTH // SGU_GROUPS
CHUNK = 128
Q_COLS = DIFF_HEADS * 2 * DIFF_HEAD_DIM
K_COLS = DIFF_HEADS * 2 * DIFF_HEAD_DIM
V_COLS = DIFF_HEADS * DIFF_VALUE_DIM
IN_COLS = Q_COLS + K_COLS + V_COLS + 2 * SGU_WIDTH
N_GROUPS = 4
EXPERTS_PER_GROUP = 8
N_EXPERTS = N_GROUPS * EXPERTS_PER_GROUP
TOP_K_IN_GROUP = 2
D_FF_EXPERT = D_MODEL // 2
RMS_EPS = 1e-6
NEG_INF = -1e30

kernel_name = "hymba_diffattn_sgu_hmoe_layer"


def rms_norm(x, g):
    xf = x.astype(jnp.float32)
    y = xf * lax.rsqrt(jnp.mean(xf * xf, axis=-1, keepdims=True) + RMS_EPS)
    return (y * g.astype(jnp.float32)).astype(x.dtype)


def layer_norm(x, g, b):
    xf = x.astype(jnp.float32)
    mu = jnp.mean(xf, axis=-1, keepdims=True)
    xc = xf - mu
    y = xc * lax.rsqrt(jnp.mean(xc * xc, axis=-1, keepdims=True) + RMS_EPS)
    return (y * g.astype(jnp.float32) + b.astype(jnp.float32)).astype(x.dtype)


def rope_tables(positions):
    inv_freq = ROPE_THETA ** (-jnp.arange(0, ROPE_DIM, 2, dtype=jnp.float32) / ROPE_DIM)
    ang = positions.astype(jnp.float32)[..., None] * inv_freq
    return jnp.cos(ang), jnp.sin(ang)


def apply_partial_rope(x, cos, sin):
    half = ROPE_DIM // 2
    c = cos[:, :, None, None, :]
    s = sin[:, :, None, None, :]
    x1 = x[..., :half].astype(jnp.float32)
    x2 = x[..., half:ROPE_DIM].astype(jnp.float32)
    r1 = (x1 * c - x2 * s).astype(x.dtype)
    r2 = (x1 * s + x2 * c).astype(x.dtype)
    return jnp.concatenate([r1, r2, x[..., ROPE_DIM:]], axis=-1)


def diff_attention(q, k, v, lam):
    B, S, H, _, d = q.shape
    nb = S // QBLOCK
    kt = k.transpose(0, 2, 3, 1, 4)
    vt = v.transpose(0, 2, 1, 3)
    qb = q.transpose(0, 2, 3, 1, 4).reshape(B, H, 2, nb, QBLOCK, d).transpose(3, 0, 1, 2, 4, 5)
    key_pos = jnp.arange(S)

    def block(args):
        q_blk, i = args
        s = jnp.einsum('bhmqd,bhmkd->bhmqk', q_blk, kt, preferred_element_type=jnp.float32)
        q_pos = i * QBLOCK + jnp.arange(QBLOCK)
        mask = key_pos[None, :] <= q_pos[:, None]
        s = jnp.where(mask, s, NEG_INF)
        p = jax.nn.softmax(s, axis=-1)
        a = p[:, :, 0] - lam * p[:, :, 1]
        return jnp.einsum('bhqk,bhkv->bhqv', a.astype(vt.dtype), vt)

    out = lax.map(block, (qb, jnp.arange(nb)))
    return out.transpose(1, 0, 3, 2, 4).reshape(B, S, H, DIFF_VALUE_DIM)


def spatial_gating(u, v, ln_g, ln_b, w_s, b_s):
    B, S, _ = v.shape
    nc = S // CHUNK
    vn = layer_norm(v, ln_g, ln_b)
    vc = vn.reshape(B, nc, CHUNK, SGU_GROUPS, SGU_GROUP_WIDTH)
    w = w_s * jnp.tril(jnp.ones((CHUNK, CHUNK), dtype=w_s.dtype))
    s = jnp.einsum('gts,bcsgd->bctgd', w, vc) + b_s.T[None, None, :, :, None]
    return u * s.reshape(B, S, SGU_WIDTH)


def hier_moe(h, wg, bg, we, be, w_gate, w_up, w_down):
    B, S, D = h.shape
    T = B * S
    xt = h.reshape(T, D)
    g_logits = (xt @ wg).astype(jnp.float32) + bg.astype(jnp.float32)
    g_probs = jax.nn.softmax(g_logits, axis=-1)
    p_grp, grp = lax.top_k(g_probs, 1)
    e_logits = ((xt @ we).astype(jnp.float32) + be.astype(jnp.float32)).reshape(T, N_GROUPS, EXPERTS_PER_GROUP)
    e_logits = jnp.take_along_axis(e_logits, grp[:, :, None], axis=1)[:, 0]
    top_logit, top_e = lax.top_k(e_logits, TOP_K_IN_GROUP)
    gates = p_grp * jax.nn.softmax(top_logit, axis=-1)
    flat_id = (grp * EXPERTS_PER_GROUP + top_e).reshape(-1)
    order = jnp.argsort(flat_id)
    tok = order // TOP_K_IN_GROUP
    xs = xt[tok]
    sizes = jnp.bincount(flat_id, length=N_EXPERTS).astype(jnp.int32)
    hg = lax.ragged_dot(xs, w_gate, sizes)
    hu = lax.ragged_dot(xs, w_up, sizes)
    ys = lax.ragged_dot(jax.nn.silu(hg) * hu, w_down, sizes)
    ys = ys * gates.reshape(-1)[order][:, None].astype(ys.dtype)
    out = jnp.zeros((T, D), dtype=ys.dtype).at[tok].add(ys)
    return out.reshape(B, S, D)


def setup_inputs(seed: int = 0) -> dict:
    key = jax.random.key(seed)
    ks = jax.random.split(key, 24)
    f32 = jnp.float32
    L = DEPTH

    def nrm(k, shape, scale):
        return jax.random.normal(k, shape, f32) * scale

    x = jax.random.normal(ks[0], (BATCH, SEQ, D_MODEL), f32)
    offset = jax.random.randint(ks[1], (BATCH, 1), 0, 4096, dtype=jnp.int32)
    positions = (offset + jnp.arange(SEQ, dtype=jnp.int32)[None, :]).astype(jnp.int32)
    return {
        "x": x,
        "positions": positions,
        "norm_mix_g": 1.0 + nrm(ks[2], (L, D_MODEL), 0.02),
        "w_in": nrm(ks[3], (L, D_MODEL, IN_COLS), D_MODEL ** -0.5),
        "q_norm_g": 1.0 + nrm(ks[4], (L, DIFF_HEAD_DIM), 0.02),
        "k_norm_g": 1.0 + nrm(ks[5], (L, DIFF_HEAD_DIM), 0.02),
        "lambda_q1": nrm(ks[6], (L, DIFF_HEAD_DIM), 0.1),
        "lambda_k1": nrm(ks[7], (L, DIFF_HEAD_DIM), 0.1),
        "lambda_q2": nrm(ks[8], (L, DIFF_HEAD_DIM), 0.1),
        "lambda_k2": nrm(ks[9], (L, DIFF_HEAD_DIM), 0.1),
        "diff_out_norm_g": 1.0 + nrm(ks[10], (L, DIFF_VALUE_DIM), 0.02),
        "sgu_ln_g": 1.0 + nrm(ks[11], (L, SGU_WIDTH), 0.02),
        "sgu_ln_b": nrm(ks[12], (L, SGU_WIDTH), 0.02),
        "sgu_w": nrm(ks[13], (L, SGU_GROUPS, CHUNK, CHUNK), 0.5 * CHUNK ** -0.5),
        "sgu_b": 1.0 + nrm(ks[14], (L, SGU_GROUPS, CHUNK), 0.01),
        "sgu_out_norm_g": 1.0 + nrm(ks[15], (L, SGU_GROUPS, SGU_GROUP_WIDTH), 0.02),
        "w_out": nrm(ks[16], (L, MIX_WIDTH, D_MODEL), MIX_WIDTH ** -0.5),
        "norm_ffn_g": 1.0 + nrm(ks[17], (L, D_MODEL), 0.02),
        "router_group_w": nrm(ks[18], (L, D_MODEL, N_GROUPS), D_MODEL ** -0.5),
        "router_group_b": nrm(ks[19], (L, N_GROUPS), 0.01),
        "router_expert_w": nrm(ks[20], (L, D_MODEL, N_EXPERTS), D_MODEL ** -0.5),
        "router_expert_b": nrm(ks[21], (L, N_EXPERTS), 0.01),
        "expert_w_gate": nrm(ks[22], (L, N_EXPERTS, D_MODEL, D_FF_EXPERT), D_MODEL ** -0.5),
        "expert_w_up": nrm(jax.random.fold_in(ks[22], 1), (L, N_EXPERTS, D_MODEL, D_FF_EXPERT), D_MODEL ** -0.5),
        "expert_w_down": nrm(ks[23], (L, N_EXPERTS, D_FF_EXPERT, D_MODEL), D_FF_EXPERT ** -0.5),
    }


def reference(x, positions, norm_mix_g, w_in, q_norm_g, k_norm_g, lambda_q1, lambda_k1,
              lambda_q2, lambda_k2, diff_out_norm_g, sgu_ln_g, sgu_ln_b, sgu_w, sgu_b,
              sgu_out_norm_g, w_out, norm_ffn_g, router_group_w, router_group_b,
              router_expert_w, router_expert_b, expert_w_gate, expert_w_up, expert_w_down):
    B, S, _ = x.shape
    cos, sin = rope_tables(positions)
    scale = DIFF_HEAD_DIM ** -0.5
    h = x
    for l in range(DEPTH):
        lambda_init = 0.8 - 0.6 * math.exp(-0.3 * l)
        hn = rms_norm(h, norm_mix_g[l])
        proj = hn @ w_in[l]
        q, k, v, su, sv = jnp.split(
            proj, [Q_COLS, Q_COLS + K_COLS, Q_COLS + K_COLS + V_COLS, Q_COLS + K_COLS + V_COLS + SGU_WIDTH], axis=-1)
        q = q.reshape(B, S, DIFF_HEADS, 2, DIFF_HEAD_DIM)
        k = k.reshape(B, S, DIFF_HEADS, 2, DIFF_HEAD_DIM)
        v = v.reshape(B, S, DIFF_HEADS, DIFF_VALUE_DIM)
        q = apply_partial_rope(rms_norm(q, q_norm_g[l]), cos, sin) * scale
        k = apply_partial_rope(rms_norm(k, k_norm_g[l]), cos, sin)
        lam = (jnp.exp(jnp.sum(lambda_q1[l].astype(jnp.float32) * lambda_k1[l].astype(jnp.float32)))
               - jnp.exp(jnp.sum(lambda_q2[l].astype(jnp.float32) * lambda_k2[l].astype(jnp.float32)))
               + lambda_init)
        att = diff_attention(q, k, v, lam)
        att = rms_norm(att, diff_out_norm_g[l]) * (1.0 - lambda_init)
        att = att.reshape(B, S, ATT_WIDTH)
        su = jax.nn.gelu(su, approximate=False)
        sv = jax.nn.gelu(sv, approximate=False)
        sg = spatial_gating(su, sv, sgu_ln_g[l], sgu_ln_b[l], sgu_w[l], sgu_b[l])
        sg = rms_norm(sg.reshape(B, S, SGU_GROUPS, SGU_GROUP_WIDTH), sgu_out_norm_g[l]).reshape(B, S, SGU_WIDTH)
        mix = jnp.concatenate([att, sg], axis=-1)
        h = h + mix @ w_out[l]
        hn = rms_norm(h, norm_ffn_g[l])
        h = h + hier_moe(hn, router_group_w[l], router_group_b[l], router_expert_w[l], router_expert_b[l],
                         expert_w_gate[l], expert_w_up[l], expert_w_down[l])
    return h
```

```python
import functools
import math

import jax
import jax.numpy as jnp
from jax import lax
from jax.experimental import pallas as pl
from jax.experimental.pallas import tpu as pltpu

F32 = jnp.float32
BF16 = jnp.bfloat16

RMS_EPS = 1e-6
NEG_INF = -1e30
ROPE_THETA = 500000.0

LANES = 128
SUBLANES = 8
MXU_WIDTH = 256
VMEM_LIMIT_BYTES = 56 * 1024 * 1024

N_GROUPS = 4
EXPERTS_PER_GROUP = 8
N_EXPERTS = N_GROUPS * EXPERTS_PER_GROUP
TOP_K = 2
ROPE_FRACTION = 4
CHUNK = 128
QBLOCK = 256
ROW_TILE = 512
EXPERT_TILE = 256
GATHER_TILE = 512


def _params(n_grid_dims):
    return pltpu.CompilerParams(
        dimension_semantics=("arbitrary",) * n_grid_dims,
        vmem_limit_bytes=VMEM_LIMIT_BYTES,
    )


def _gelu(x):
    return 0.5 * x * (1.0 + lax.erf(x * (1.0 / math.sqrt(2.0))))


def _inproj_kernel(x_ref, g_ref, w_ref, gqk_ref, cos_ref, sa_ref, sb_ref, ones_ref, lng_ref, lnb_ref,
                   qk_ref, v_ref, su_ref, sv_ref, *, qk_cols, v_cols, sgu_cols, head_dim, rope_half):
    x = x_ref[...]
    ms = jnp.mean(x * x, axis=-1, keepdims=True)
    xn = (x * lax.rsqrt(ms + RMS_EPS) * g_ref[...]).astype(BF16)
    cw = MXU_WIDTH

    def proj(col0):
        return jnp.dot(xn, w_ref[:, col0:col0 + cw], preferred_element_type=F32)

    cos = cos_ref[...]
    sa = sa_ref[...]
    sb = sb_ref[...]
    for c in range(qk_cols // cw):
        p = proj(c * cw)
        ss = jnp.dot((p * p).astype(BF16), ones_ref[...], preferred_element_type=F32)
        pn = p * lax.rsqrt(ss * (1.0 / head_dim) + RMS_EPS) * gqk_ref[:, c * cw:(c + 1) * cw]
        for hh in range(cw // LANES):
            blk = pn[:, hh * LANES:(hh + 1) * LANES]
            rot = (blk * cos
                   + pltpu.roll(blk, LANES - rope_half, 1) * sa
                   + pltpu.roll(blk, rope_half, 1) * sb)
            col = c * cw + hh * LANES
            qk_ref[:, col:col + LANES] = rot.astype(BF16)

    for c in range(v_cols // cw):
        v_ref[:, c * cw:(c + 1) * cw] = proj(qk_cols + c * cw).astype(BF16)

    for c in range(sgu_cols // cw):
        su_ref[:, c * cw:(c + 1) * cw] = _gelu(proj(qk_cols + v_cols + c * cw)).astype(BF16)

    sv = jnp.concatenate(
        [_gelu(proj(qk_cols + v_cols + sgu_cols + c * cw)) for c in range(sgu_cols // cw)], axis=1)
    mu = jnp.mean(sv, axis=-1, keepdims=True)
    svc = sv - mu
    var = jnp.mean(svc * svc, axis=-1, keepdims=True)
    sv_ref[...] = (svc * lax.rsqrt(var + RMS_EPS) * lng_ref[...] + lnb_ref[...]).astype(BF16)


def _inproj(x2d, norm_g, w_in, gqk, cos_t, sa_t, sb_t, ones_blk, ln_g, ln_b, *, qk_cols, v_cols,
            sgu_cols, head_dim, rope_half):
    t, d = x2d.shape
    tm = ROW_TILE
    row = lambda i: (i, 0)
    fixed = lambda i: (0, 0)
    kern = functools.partial(_inproj_kernel, qk_cols=qk_cols, v_cols=v_cols, sgu_cols=sgu_cols,
                             head_dim=head_dim, rope_half=rope_half)
    return pl.pallas_call(
        kern,
        grid=(t // tm,),
        in_specs=[
            pl.BlockSpec((tm, d), row),
            pl.BlockSpec((1, d), fixed),
            pl.BlockSpec(w_in.shape, fixed),
            pl.BlockSpec((1, qk_cols), fixed),
            pl.BlockSpec((tm, LANES), row),
            pl.BlockSpec((tm, LANES), row),
            pl.BlockSpec((tm, LANES), row),
            pl.BlockSpec((MXU_WIDTH, MXU_WIDTH), fixed),
            pl.BlockSpec((1, sgu_cols), fixed),
            pl.BlockSpec((1, sgu_cols), fixed),
        ],
        out_specs=[
            pl.BlockSpec((tm, qk_cols), row),
            pl.BlockSpec((tm, v_cols), row),
            pl.BlockSpec((tm, sgu_cols), row),
            pl.BlockSpec((tm, sgu_cols), row),
        ],
        out_shape=[
            jax.ShapeDtypeStruct((t, qk_cols), BF16),
            jax.ShapeDtypeStruct((t, v_cols), BF16),
            jax.ShapeDtypeStruct((t, sgu_cols), BF16),
            jax.ShapeDtypeStruct((t, sgu_cols), BF16),
        ],
        compiler_params=_params(1),
        name="inproj",
    )(x2d, norm_g, w_in, gqk, cos_t, sa_t, sb_t, ones_blk, ln_g, ln_b)


def _attn_kernel(lam_ref, gout_ref, q_ref, k_ref, v_ref, o_ref, *, tq, lambda_init, head_dim):
    qi = pl.program_id(2)
    lv = lam_ref[...]
    lam = (jnp.exp(jnp.sum(lv[0:1] * lv[1:2], axis=-1, keepdims=True))
           - jnp.exp(jnp.sum(lv[2:3] * lv[3:4], axis=-1, keepdims=True))
           + lambda_init)

    q = q_ref[0]
    lane = lax.broadcasted_iota(jnp.int32, (1, 2 * head_dim), 1)
    first = lane < head_dim
    zero = jnp.zeros_like(q)
    q1 = jnp.where(first, q, zero)
    q2 = jnp.where(first, zero, q)
    vd = v_ref.shape[-1]
    ones = jnp.ones((tq, vd), BF16)
    row_id = lax.broadcasted_iota(jnp.int32, (tq, tq), 0)
    col_id = lax.broadcasted_iota(jnp.int32, (tq, tq), 1)
    causal = col_id <= row_id

    def step(j, carry, masked):
        start = pl.multiple_of(j * tq, tq)
        kc = k_ref[0, pl.ds(start, tq), :]
        vext = jnp.concatenate([v_ref[0, pl.ds(start, tq), :], ones], axis=1)
        out = []
        for qm, (m, acc) in zip((q1, q2), carry):
            s = lax.dot_general(qm, kc, (((1,), (1,)), ((), ())), preferred_element_type=F32)
            if masked:
                s = jnp.where(causal, s, NEG_INF)
            m_new = jnp.maximum(m, jnp.max(s, axis=-1, keepdims=True))
            alpha = jnp.exp(m - m_new)
            p = jnp.exp(s - m_new).astype(BF16)
            acc = alpha * acc + jnp.dot(p, vext, preferred_element_type=F32)
            out.append((m_new, acc))
        return tuple(out)

    init = tuple((jnp.full((tq, 1), NEG_INF, F32), jnp.zeros((tq, 2 * vd), F32)) for _ in range(2))
    carry = lax.fori_loop(0, qi, lambda j, c: step(j, c, False), init)
    (_, acc1), (_, acc2) = step(qi, carry, True)

    att = acc1[:, :vd] / acc1[:, vd:] - lam * (acc2[:, :vd] / acc2[:, vd:])
    ms = jnp.mean(att * att, axis=-1, keepdims=True)
    o_ref[0] = (att * lax.rsqrt(ms + RMS_EPS) * gout_ref[...] * (1.0 - lambda_init)).astype(BF16)


def _attention(lam_vecs, gout, qk3, v3, *, n_heads, head_dim, lambda_init):
    b, s, _ = qk3.shape
    vd = v3.shape[-1] // n_heads
    tq = QBLOCK
    kern = functools.partial(_attn_kernel, tq=tq, lambda_init=lambda_init, head_dim=head_dim)
    return pl.pallas_call(
        kern,
        grid=(b, n_heads, s // tq),
        in_specs=[
            pl.BlockSpec(lam_vecs.shape, lambda bi, h, qi: (0, 0)),
            pl.BlockSpec((1, vd), lambda bi, h, qi: (0, 0)),
            pl.BlockSpec((1, tq, 2 * head_dim), lambda bi, h, qi: (bi, qi, h)),
            pl.BlockSpec((1, s, 2 * head_dim), lambda bi, h, qi: (bi, 0, n_heads + h)),
            pl.BlockSpec((1, s, vd), lambda bi, h, qi: (bi, 0, h)),
        ],
        out_specs=pl.BlockSpec((1, tq, vd), lambda bi, h, qi: (bi, qi, h)),
        out_shape=jax.ShapeDtypeStruct(v3.shape, BF16),
        compiler_params=_params(3),
        name="diff_attn",
    )(lam_vecs, gout, qk3, qk3, v3)


def _mixout_kernel(att_ref, su_ref, sv_ref, x_ref, sw_ref, sbt_ref, sgn_ref, wo_ref, gffn_ref,
                   rwh_ref, rwl_ref, rb_ref, h_ref, hn_ref, route_ref, *, n_sgu_groups, att_cols):
    tm = x_ref.shape[0]
    gw = CHUNK
    r_id = lax.broadcasted_iota(jnp.int32, (gw, gw), 0)
    c_id = lax.broadcasted_iota(jnp.int32, (gw, gw), 1)
    tril = c_id <= r_id

    acc = jnp.dot(att_ref[...], wo_ref[0:att_cols, :], preferred_element_type=F32)

    sg_cols = []
    for g in range(n_sgu_groups):
        w = jnp.where(tril, sw_ref[g], 0.0).astype(BF16)
        bias = sbt_ref[:, g:g + 1]
        gn = sgn_ref[g:g + 1, :]
        rows = []
        for c in range(tm // gw):
            vblk = sv_ref[c * gw:(c + 1) * gw, g * gw:(g + 1) * gw]
            s = jnp.dot(w, vblk, preferred_element_type=F32) + bias
            sg = su_ref[c * gw:(c + 1) * gw, g * gw:(g + 1) * gw].astype(F32) * s
            ms = jnp.mean(sg * sg, axis=-1, keepdims=True)
            rows.append((sg * lax.rsqrt(ms + RMS_EPS) * gn).astype(BF16))
        sg_cols.append(jnp.concatenate(rows, axis=0))
    sgn = jnp.concatenate(sg_cols, axis=1)
    acc = acc + jnp.dot(sgn, wo_ref[att_cols:, :], preferred_element_type=F32)

    h = x_ref[...] + acc
    h_ref[...] = h
    ms = jnp.mean(h * h, axis=-1, keepdims=True)
    hn = h * lax.rsqrt(ms + RMS_EPS) * gffn_ref[...]
    hn_bf = hn.astype(BF16)
    for c in range(hn.shape[1] // LANES):
        hn_ref[:, c, :] = hn[:, c * LANES:(c + 1) * LANES]

    hn_lo = (hn - hn_bf.astype(F32)).astype(BF16)
    logits = (jnp.dot(hn_bf, rwh_ref[...], preferred_element_type=F32)
              + jnp.dot(hn_lo, rwh_ref[...], preferred_element_type=F32)
              + jnp.dot(hn_bf, rwl_ref[...], preferred_element_type=F32)
              + rb_ref[...])

    lane = lax.broadcasted_iota(jnp.int32, logits.shape, 1).astype(F32)
    big = float(LANES)
    is_grp = lane < N_GROUPS
    gl = jnp.where(is_grp, logits, NEG_INF)
    gmax = jnp.max(gl, axis=-1, keepdims=True)
    gidx = jnp.min(jnp.where(gl == gmax, lane, big), axis=-1, keepdims=True)
    psum = jnp.sum(jnp.where(is_grp, jnp.exp(gl - gmax), 0.0), axis=-1, keepdims=True)
    p_grp = 1.0 / psum
    e_lo = N_GROUPS + gidx * EXPERTS_PER_GROUP
    in_grp = (lane >= e_lo) & (lane < e_lo + EXPERTS_PER_GROUP)
    el = jnp.where(in_grp, logits, NEG_INF)
    t1 = jnp.max(el, axis=-1, keepdims=True)
    i1 = jnp.min(jnp.where(el == t1, lane, big), axis=-1, keepdims=True)
    el2 = jnp.where(lane == i1, NEG_INF, el)
    t2 = jnp.max(el2, axis=-1, keepdims=True)
    i2 = jnp.min(jnp.where(el2 == t2, lane, big), axis=-1, keepdims=True)
    e21 = jnp.exp(t2 - t1)
    den = 1.0 + e21
    g1 = p_grp / den
    g2 = p_grp * (e21 / den)
    route = jnp.where(lane == 0, i1 - N_GROUPS,
                      jnp.where(lane == 1, i2 - N_GROUPS,
                                jnp.where(lane == 2, g1, jnp.where(lane == 3, g2, 0.0))))
    route_ref[...] = route


def _mixout(att, su, sv, x2d, sgu_w, sgu_bt, sgu_gn, w_out, g_ffn, rw_hi, rw_lo, rb, *, att_cols):
    t, d = x2d.shape
    tm = ROW_TILE
    row = lambda i: (i, 0)
    fixed2 = lambda i: (0, 0)
    n_groups = sgu_w.shape[0]
    kern = functools.partial(_mixout_kernel, n_sgu_groups=n_groups, att_cols=att_cols)
    return pl.pallas_call(
        kern,
        grid=(t // tm,),
        in_specs=[
            pl.BlockSpec((tm, att.shape[1]), row),
            pl.BlockSpec((tm, su.shape[1]), row),
            pl.BlockSpec((tm, sv.shape[1]), row),
            pl.BlockSpec((tm, d), row),
            pl.BlockSpec(sgu_w.shape, lambda i: (0, 0, 0)),
            pl.BlockSpec(sgu_bt.shape, fixed2),
            pl.BlockSpec(sgu_gn.shape, fixed2),
            pl.BlockSpec(w_out.shape, fixed2),
            pl.BlockSpec((1, d), fixed2),
            pl.BlockSpec(rw_hi.shape, fixed2),
            pl.BlockSpec(rw_lo.shape, fixed2),
            pl.BlockSpec((1, LANES), fixed2),
        ],
        out_specs=[
            pl.BlockSpec((tm, d), row),
            pl.BlockSpec((tm, d // LANES, LANES), lambda i: (i, 0, 0)),
            pl.BlockSpec((tm, LANES), row),
        ],
        out_shape=[
            jax.ShapeDtypeStruct((t, d), F32),
            jax.ShapeDtypeStruct((t, d // LANES, LANES), F32),
            jax.ShapeDtypeStruct((t, LANES), F32),
        ],
        compiler_params=_params(1),
        name="mixout",
    )(att, su, sv, x2d, sgu_w, sgu_bt, sgu_gn, w_out, g_ffn, rw_hi, rw_lo, rb)


def _gather_kernel(idx_ref, table_ref, out_ref, sem, *, rows):
    base = pl.program_id(0) * rows

    def row_copy(r):
        return pltpu.make_async_copy(table_ref.at[idx_ref[r]], out_ref.at[base + r], sem)

    def issue(r, c):
        row_copy(r).start()
        return c

    def drain(r, c):
        row_copy(r).wait()
        return c

    lax.fori_loop(0, rows, issue, 0)
    lax.fori_loop(0, rows, drain, 0)


def _gather_rows(table, idx):
    m = idx.shape[0]
    rows = GATHER_TILE
    kern = functools.partial(_gather_kernel, rows=rows)
    return pl.pallas_call(
        kern,
        grid=(m // rows,),
        in_specs=[
            pl.BlockSpec((rows,), lambda i: (i,), memory_space=pltpu.SMEM),
            pl.BlockSpec(memory_space=pl.ANY),
        ],
        out_specs=pl.BlockSpec(memory_space=pl.ANY),
        out_shape=jax.ShapeDtypeStruct((m,) + table.shape[1:], table.dtype),
        scratch_shapes=[pltpu.SemaphoreType.DMA],
        compiler_params=_params(1),
        name="row_gather",
    )(idx, table)


def _experts_kernel(te_ref, tv_ref, xs_ref, gate_ref, wg_ref, wu_ref, wd_ref, ys_ref):
    i = pl.program_id(0)
    n_chunks = xs_ref.shape[1]

    @pl.when(tv_ref[i] == 1)
    def _():
        x = jnp.concatenate([xs_ref[:, c, :] for c in range(n_chunks)], axis=1).astype(BF16)
        hg = jnp.dot(x, wg_ref[0], preferred_element_type=F32)
        hu = jnp.dot(x, wu_ref[0], preferred_element_type=F32)
        a = (hg * (1.0 / (1.0 + jnp.exp(-hg))) * hu).astype(BF16)
        y = jnp.dot(a, wd_ref[0], preferred_element_type=F32) * gate_ref[...]
        for c in range(n_chunks):
            ys_ref[:, c, :] = y[:, c * LANES:(c + 1) * LANES]

    @pl.when(tv_ref[i] == 0)
    def _():
        ys_ref[...] = jnp.zeros_like(ys_ref)


def _experts(tile_expert, tile_valid, xs, gates, w_gate, w_up, w_down):
    p, n_chunks, _ = xs.shape
    tm = EXPERT_TILE
    d = n_chunks * LANES
    f = w_gate.shape[-1]
    grid_spec = pltpu.PrefetchScalarGridSpec(
        num_scalar_prefetch=2,
        grid=(p // tm,),
        in_specs=[
            pl.BlockSpec((tm, n_chunks, LANES), lambda i, te, tv: (i, 0, 0)),
            pl.BlockSpec((tm, 1), lambda i, te, tv: (i, 0)),
            pl.BlockSpec((1, d, f), lambda i, te, tv: (te[i], 0, 0)),
            pl.BlockSpec((1, d, f), lambda i, te, tv: (te[i], 0, 0)),
            pl.BlockSpec((1, f, d), lambda i, te, tv: (te[i], 0, 0)),
        ],
        out_specs=pl.BlockSpec((tm, n_chunks, LANES), lambda i, te, tv: (i, 0, 0)),
    )
    return pl.pallas_call(
        _experts_kernel,
        grid_spec=grid_spec,
        out_shape=jax.ShapeDtypeStruct(xs.shape, F32),
        compiler_params=_params(1),
        name="experts",
    )(tile_expert, tile_valid, xs, gates, w_gate, w_up, w_down)


def _combine_kernel(h_ref, y_ref, o_ref):
    n_chunks = y_ref.shape[2]
    y = jnp.concatenate([y_ref[:, 0, c, :] + y_ref[:, 1, c, :] for c in range(n_chunks)], axis=1)
    o_ref[...] = h_ref[...] + y


def _combine(h, y4):
    t, d = h.shape
    tm = ROW_TILE
    return pl.pallas_call(
        _combine_kernel,
        grid=(t // tm,),
        in_specs=[
            pl.BlockSpec((tm, d), lambda i: (i, 0)),
            pl.BlockSpec((tm,) + y4.shape[1:], lambda i: (i, 0, 0, 0)),
        ],
        out_specs=pl.BlockSpec((tm, d), lambda i: (i, 0)),
        out_shape=jax.ShapeDtypeStruct((t, d), F32),
        compiler_params=_params(1),
        name="combine",
    )(h, y4)


def _dispatch_plan(expert_ids, gates, n_rows_padded):
    t = expert_ids.shape[0]
    flat = expert_ids.reshape(-1)
    onehot = (flat[:, None] == jnp.arange(N_EXPERTS, dtype=jnp.int32)[None, :]).astype(jnp.int32)
    csum = jnp.cumsum(onehot, axis=0)
    counts = csum[-1]
    rank = jnp.sum((csum - onehot) * onehot, axis=1)
    tm = EXPERT_TILE
    padded = ((counts + tm - 1) // tm) * tm
    ends = jnp.cumsum(padded)
    offs = ends - padded
    dest = offs[flat] + rank
    tok = jnp.arange(t * TOP_K, dtype=jnp.int32) // TOP_K
    tok_of_row = jnp.zeros((n_rows_padded,), jnp.int32).at[dest].set(tok)
    gate_of_row = jnp.zeros((n_rows_padded,), F32).at[dest].set(gates.reshape(-1))
    tile_start = jnp.arange(n_rows_padded // tm, dtype=jnp.int32) * tm
    tile_expert = jnp.minimum(jnp.searchsorted(ends, tile_start, side="right"),
                              N_EXPERTS - 1).astype(jnp.int32)
    tile_valid = (tile_start < ends[-1]).astype(jnp.int32)
    last_expert = jnp.max(jnp.where(tile_valid == 1, tile_expert, 0))
    tile_expert = jnp.where(tile_valid == 1, tile_expert, last_expert)
    return dest.astype(jnp.int32), tok_of_row, gate_of_row.reshape(-1, 1), tile_expert, tile_valid


def kernel(x, positions, norm_mix_g, w_in, q_norm_g, k_norm_g, lambda_q1, lambda_k1, lambda_q2,
           lambda_k2, diff_out_norm_g, sgu_ln_g, sgu_ln_b, sgu_w, sgu_b, sgu_out_norm_g, w_out,
           norm_ffn_g, router_group_w, router_group_b, router_expert_w, router_expert_b,
           expert_w_gate, expert_w_up, expert_w_down):
    b, s, d = x.shape
    t = b * s
    depth = w_in.shape[0]
    head_dim = q_norm_g.shape[-1]
    vd = diff_out_norm_g.shape[-1]
    n_sgu_groups = sgu_w.shape[1]
    sgu_cols = n_sgu_groups * sgu_w.shape[-1]
    in_cols = w_in.shape[-1]
    att_cols = (in_cols - 2 * sgu_cols) // 3
    n_heads = att_cols // vd
    qk_cols = 2 * att_cols
    rope_dim = head_dim // ROPE_FRACTION
    rope_half = rope_dim // 2
    scale = head_dim ** -0.5

    inv_freq = ROPE_THETA ** (-jnp.arange(0, rope_dim, 2, dtype=F32) / rope_dim)
    ang = positions.reshape(t, 1).astype(F32) * inv_freq[None, :]
    cos, sin = jnp.cos(ang), jnp.sin(ang)
    pad = jnp.zeros((t, head_dim - rope_dim), F32)
    cos_p = jnp.concatenate([cos, cos, pad + 1.0], axis=1)
    sa_p = jnp.concatenate([-sin, jnp.zeros_like(sin), pad], axis=1)
    sb_p = jnp.concatenate([jnp.zeros_like(sin), sin, pad], axis=1)
    reps = LANES // head_dim
    cos_t, sa_t, sb_t = (jnp.tile(a, (1, reps)) for a in (cos_p, sa_p, sb_p))

    blk = jnp.arange(MXU_WIDTH, dtype=jnp.int32) // head_dim
    ones_blk = (blk[:, None] == blk[None, :]).astype(BF16)

    h = x.reshape(t, d)
    for l in range(depth):
        lambda_init = 0.8 - 0.6 * math.exp(-0.3 * l)
        gqk = jnp.concatenate([jnp.tile(q_norm_g[l] * scale, att_cols // head_dim),
                               jnp.tile(k_norm_g[l], att_cols // head_dim)]).reshape(1, qk_cols)
        qk, v, su, sv = _inproj(
            h, norm_mix_g[l].reshape(1, d), w_in[l].astype(BF16), gqk, cos_t, sa_t, sb_t, ones_blk,
            sgu_ln_g[l].reshape(1, sgu_cols), sgu_ln_b[l].reshape(1, sgu_cols),
            qk_cols=qk_cols, v_cols=att_cols, sgu_cols=sgu_cols, head_dim=head_dim,
            rope_half=rope_half)

        lam_vecs = jnp.stack([lambda_q1[l], lambda_k1[l], lambda_q2[l], lambda_k2[l]]).astype(F32)
        att = _attention(lam_vecs, diff_out_norm_g[l].reshape(1, vd), qk.reshape(b, s, qk_cols),
                         v.reshape(b, s, att_cols), n_heads=n_heads, head_dim=head_dim,
                         lambda_init=lambda_init).reshape(t, att_cols)

        rw = jnp.concatenate([router_group_w[l], router_expert_w[l]], axis=1)
        rw = jnp.pad(rw, ((0, 0), (0, LANES - rw.shape[1])))
        rw_hi = rw.astype(BF16)
        rw_lo = (rw - rw_hi.astype(F32)).astype(BF16)
        rb = jnp.pad(jnp.concatenate([router_group_b[l], router_expert_b[l]]),
                     (0, LANES - N_GROUPS - N_EXPERTS)).reshape(1, LANES)
        h, hn3, route = _mixout(att, su, sv, h, sgu_w[l], sgu_b[l].T, sgu_out_norm_g[l],
                                w_out[l].astype(BF16), norm_ffn_g[l].reshape(1, d), rw_hi, rw_lo, rb,
                                att_cols=att_cols)

        expert_ids = route[:, 0:TOP_K].astype(jnp.int32)
        gates = route[:, TOP_K:2 * TOP_K]
        n_rows_padded = t * TOP_K + N_EXPERTS * EXPERT_TILE
        dest, tok_of_row, gate_of_row, tile_expert, tile_valid = _dispatch_plan(
            expert_ids, gates, n_rows_padded)
        xs = _gather_rows(hn3, tok_of_row)
        ys = _experts(tile_expert, tile_valid, xs, gate_of_row, expert_w_gate[l].astype(BF16),
                      expert_w_up[l].astype(BF16), expert_w_down[l].astype(BF16))
        y_tok = _gather_rows(ys, dest)
        h = _combine(h, y_tok.reshape(t, TOP_K, d // LANES, LANES))
    return h.reshape(b, s, d)
```

```python
import functools
import math

import jax
import jax.numpy as jnp
from jax import lax
from jax.experimental import pallas as pl
from jax.experimental.pallas import tpu as pltpu

F32 = jnp.float32
BF16 = jnp.bfloat16
I32 = jnp.int32

RMS_EPS = 1e-6
NEG_INF = -1e30
ROPE_THETA = 500000.0

LANES = 128
SUBLANES = 8
MXU_WIDTH = 256
VMEM_LIMIT_BYTES = 56 * 1024 * 1024

N_GROUPS = 4
EXPERTS_PER_GROUP = 8
N_EXPERTS = N_GROUPS * EXPERTS_PER_GROUP
TOP_K = 2
ROPE_FRACTION = 4
CHUNK = 128
QBLOCK = 256
ROW_TILE = 512
EXPERT_TILE = 256

ROUTE_EXPERT = 0
ROUTE_GATE = 2
ROUTE_RANK = 4


def _params(n_grid_dims):
    return pltpu.CompilerParams(
        dimension_semantics=("arbitrary",) * n_grid_dims,
        vmem_limit_bytes=VMEM_LIMIT_BYTES,
    )


def _gelu(x):
    return 0.5 * x * (1.0 + lax.erf(x * (1.0 / math.sqrt(2.0))))


def _inproj_kernel(x_ref, g_ref, w_ref, gqk_ref, cos_ref, sa_ref, sb_ref, ones_ref, lng_ref, lnb_ref,
                   qk_ref, v_ref, su_ref, sv_ref, *, qk_cols, v_cols, sgu_cols, head_dim, rope_half):
    x = x_ref[...]
    ms = jnp.mean(x * x, axis=-1, keepdims=True)
    xn = (x * lax.rsqrt(ms + RMS_EPS) * g_ref[...]).astype(BF16)
    cw = MXU_WIDTH

    def proj(col0):
        return jnp.dot(xn, w_ref[:, col0:col0 + cw], preferred_element_type=F32)

    cos = cos_ref[...]
    sa = sa_ref[...]
    sb = sb_ref[...]
    for c in range(qk_cols // cw):
        p = proj(c * cw)
        ss = jnp.dot((p * p).astype(BF16), ones_ref[...], preferred_element_type=F32)
        pn = p * lax.rsqrt(ss * (1.0 / head_dim) + RMS_EPS) * gqk_ref[:, c * cw:(c + 1) * cw]
        for hh in range(cw // LANES):
            blk = pn[:, hh * LANES:(hh + 1) * LANES]
            rot = (blk * cos
                   + pltpu.roll(blk, LANES - rope_half, 1) * sa
                   + pltpu.roll(blk, rope_half, 1) * sb)
            col = c * cw + hh * LANES
            qk_ref[:, col:col + LANES] = rot.astype(BF16)

    for c in range(v_cols // cw):
        v_ref[:, c * cw:(c + 1) * cw] = proj(qk_cols + c * cw).astype(BF16)

    for c in range(sgu_cols // cw):
        su_ref[:, c * cw:(c + 1) * cw] = _gelu(proj(qk_cols + v_cols + c * cw)).astype(BF16)

    sv = jnp.concatenate(
        [_gelu(proj(qk_cols + v_cols + sgu_cols + c * cw)) for c in range(sgu_cols // cw)], axis=1)
    mu = jnp.mean(sv, axis=-1, keepdims=True)
    svc = sv - mu
    var = jnp.mean(svc * svc, axis=-1, keepdims=True)
    sv_ref[...] = (svc * lax.rsqrt(var + RMS_EPS) * lng_ref[...] + lnb_ref[...]).astype(BF16)


def _inproj(x2d, norm_g, w_in, gqk, cos_t, sa_t, sb_t, ones_blk, ln_g, ln_b, *, qk_cols, v_cols,
            sgu_cols, head_dim, rope_half):
    t, d = x2d.shape
    tm = ROW_TILE
    row = lambda i: (i, 0)
    fixed = lambda i: (0, 0)
    kern = functools.partial(_inproj_kernel, qk_cols=qk_cols, v_cols=v_cols, sgu_cols=sgu_cols,
                             head_dim=head_dim, rope_half=rope_half)
    return pl.pallas_call(
        kern,
        grid=(t // tm,),
        in_specs=[
            pl.BlockSpec((tm, d), row),
            pl.BlockSpec((1, d), fixed),
            pl.BlockSpec(w_in.shape, fixed),
            pl.BlockSpec((1, qk_cols), fixed),
            pl.BlockSpec((tm, LANES), row),
            pl.BlockSpec((tm, LANES), row),
            pl.BlockSpec((tm, LANES), row),
            pl.BlockSpec((MXU_WIDTH, MXU_WIDTH), fixed),
            pl.BlockSpec((1, sgu_cols), fixed),
            pl.BlockSpec((1, sgu_cols), fixed),
        ],
        out_specs=[
            pl.BlockSpec((tm, qk_cols), row),
            pl.BlockSpec((tm, v_cols), row),
            pl.BlockSpec((tm, sgu_cols), row),
            pl.BlockSpec((tm, sgu_cols), row),
        ],
        out_shape=[
            jax.ShapeDtypeStruct((t, qk_cols), BF16),
            jax.ShapeDtypeStruct((t, v_cols), BF16),
            jax.ShapeDtypeStruct((t, sgu_cols), BF16),
            jax.ShapeDtypeStruct((t, sgu_cols), BF16),
        ],
        compiler_params=_params(1),
        name="inproj",
    )(x2d, norm_g, w_in, gqk, cos_t, sa_t, sb_t, ones_blk, ln_g, ln_b)


def _attn_kernel(lam_ref, gout_ref, q_ref, k_ref, v_ref, o_ref, *, tq, lambda_init, head_dim):
    qi = pl.program_id(2)
    lv = lam_ref[...]
    lam = (jnp.exp(jnp.sum(lv[0:1] * lv[1:2], axis=-1, keepdims=True))
           - jnp.exp(jnp.sum(lv[2:3] * lv[3:4], axis=-1, keepdims=True))
           + lambda_init)

    q = q_ref[0]
    lane = lax.broadcasted_iota(jnp.int32, (1, 2 * head_dim), 1)
    first = lane < head_dim
    zero = jnp.zeros_like(q)
    q1 = jnp.where(first, q, zero)
    q2 = jnp.where(first, zero, q)
    vd = v_ref.shape[-1]
    ones = jnp.ones((tq, vd), BF16)
    row_id = lax.broadcasted_iota(jnp.int32, (tq, tq), 0)
    col_id = lax.broadcasted_iota(jnp.int32, (tq, tq), 1)
    causal = col_id <= row_id

    def step(j, carry, masked):
        start = pl.multiple_of(j * tq, tq)
        kc = k_ref[0, pl.ds(start, tq), :]
        vext = jnp.concatenate([v_ref[0, pl.ds(start, tq), :], ones], axis=1)
        out = []
        for qm, (m, acc) in zip((q1, q2), carry):
            s = lax.dot_general(qm, kc, (((1,), (1,)), ((), ())), preferred_element_type=F32)
            if masked:
                s = jnp.where(causal, s, NEG_INF)
            m_new = jnp.maximum(m, jnp.max(s, axis=-1, keepdims=True))
            alpha = jnp.exp(m - m_new)
            p = jnp.exp(s - m_new).astype(BF16)
            acc = alpha * acc + jnp.dot(p, vext, preferred_element_type=F32)
            out.append((m_new, acc))
        return tuple(out)

    init = tuple((jnp.full((tq, 1), NEG_INF, F32), jnp.zeros((tq, 2 * vd), F32)) for _ in range(2))
    carry = lax.fori_loop(0, qi, lambda j, c: step(j, c, False), init)
    (_, acc1), (_, acc2) = step(qi, carry, True)

    att = acc1[:, :vd] / acc1[:, vd:] - lam * (acc2[:, :vd] / acc2[:, vd:])
    ms = jnp.mean(att * att, axis=-1, keepdims=True)
    o_ref[0] = (att * lax.rsqrt(ms + RMS_EPS) * gout_ref[...] * (1.0 - lambda_init)).astype(BF16)


def _attention(lam_vecs, gout, qk3, v3, *, n_heads, head_dim, lambda_init):
    b, s, _ = qk3.shape
    vd = v3.shape[-1] // n_heads
    tq = QBLOCK
    kern = functools.partial(_attn_kernel, tq=tq, lambda_init=lambda_init, head_dim=head_dim)
    return pl.pallas_call(
        kern,
        grid=(b, n_heads, s // tq),
        in_specs=[
            pl.BlockSpec(lam_vecs.shape, lambda bi, h, qi: (0, 0)),
            pl.BlockSpec((1, vd), lambda bi, h, qi: (0, 0)),
            pl.BlockSpec((1, tq, 2 * head_dim), lambda bi, h, qi: (bi, qi, h)),
            pl.BlockSpec((1, s, 2 * head_dim), lambda bi, h, qi: (bi, 0, n_heads + h)),
            pl.BlockSpec((1, s, vd), lambda bi, h, qi: (bi, 0, h)),
        ],
        out_specs=pl.BlockSpec((1, tq, vd), lambda bi, h, qi: (bi, qi, h)),
        out_shape=jax.ShapeDtypeStruct(v3.shape, BF16),
        compiler_params=_params(3),
        name="diff_attn",
    )(lam_vecs, gout, qk3, qk3, v3)


def _mixout_kernel(att_ref, su_ref, sv_ref, x_ref, sw_ref, sbt_ref, sgn_ref, wo_ref, gffn_ref,
                   rwh_ref, rwl_ref, rb_ref, lstrict_ref, h_ref, hn_ref, route_ref, counts_ref, cnt_ref,
                   *, n_sgu_groups, att_cols):
    tm = x_ref.shape[0]
    gw = CHUNK
    r_id = lax.broadcasted_iota(jnp.int32, (gw, gw), 0)
    c_id = lax.broadcasted_iota(jnp.int32, (gw, gw), 1)
    tril = c_id <= r_id

    @pl.when(pl.program_id(0) == 0)
    def _():
        cnt_ref[...] = jnp.zeros_like(cnt_ref)

    acc = jnp.dot(att_ref[...], wo_ref[0:att_cols, :], preferred_element_type=F32)

    sg_cols = []
    for g in range(n_sgu_groups):
        w = jnp.where(tril, sw_ref[g], 0.0).astype(BF16)
        bias = sbt_ref[:, g:g + 1]
        gn = sgn_ref[g:g + 1, :]
        rows = []
        for c in range(tm // gw):
            vblk = sv_ref[c * gw:(c + 1) * gw, g * gw:(g + 1) * gw]
            s = jnp.dot(w, vblk, preferred_element_type=F32) + bias
            sg = su_ref[c * gw:(c + 1) * gw, g * gw:(g + 1) * gw].astype(F32) * s
            ms = jnp.mean(sg * sg, axis=-1, keepdims=True)
            rows.append((sg * lax.rsqrt(ms + RMS_EPS) * gn).astype(BF16))
        sg_cols.append(jnp.concatenate(rows, axis=0))
    sgn = jnp.concatenate(sg_cols, axis=1)
    acc = acc + jnp.dot(sgn, wo_ref[att_cols:, :], preferred_element_type=F32)

    h = x_ref[...] + acc
    h_ref[...] = h
    ms = jnp.mean(h * h, axis=-1, keepdims=True)
    hn = h * lax.rsqrt(ms + RMS_EPS) * gffn_ref[...]
    hn_bf = hn.astype(BF16)
    for c in range(hn.shape[1] // LANES):
        hn_ref[:, c, :] = hn[:, c * LANES:(c + 1) * LANES]

    hn_lo = (hn - hn_bf.astype(F32)).astype(BF16)
    logits = (jnp.dot(hn_bf, rwh_ref[...], preferred_element_type=F32)
              + jnp.dot(hn_lo, rwh_ref[...], preferred_element_type=F32)
              + jnp.dot(hn_bf, rwl_ref[...], preferred_element_type=F32)
              + rb_ref[...])

    lane = lax.broadcasted_iota(jnp.int32, logits.shape, 1).astype(F32)
    big = float(LANES)
    is_grp = lane < N_GROUPS
    gl = jnp.where(is_grp, logits, NEG_INF)
    gmax = jnp.max(gl, axis=-1, keepdims=True)
    gidx = jnp.min(jnp.where(gl == gmax, lane, big), axis=-1, keepdims=True)
    psum = jnp.sum(jnp.where(is_grp, jnp.exp(gl - gmax), 0.0), axis=-1, keepdims=True)
    p_grp = 1.0 / psum
    e_lo = N_GROUPS + gidx * EXPERTS_PER_GROUP
    in_grp = (lane >= e_lo) & (lane < e_lo + EXPERTS_PER_GROUP)
    el = jnp.where(in_grp, logits, NEG_INF)
    t1 = jnp.max(el, axis=-1, keepdims=True)
    i1 = jnp.min(jnp.where(el == t1, lane, big), axis=-1, keepdims=True)
    el2 = jnp.where(lane == i1, NEG_INF, el)
    t2 = jnp.max(el2, axis=-1, keepdims=True)
    i2 = jnp.min(jnp.where(el2 == t2, lane, big), axis=-1, keepdims=True)
    e21 = jnp.exp(t2 - t1)
    den = 1.0 + e21
    g1 = p_grp / den
    g2 = p_grp * (e21 / den)
    e1 = i1 - N_GROUPS
    e2 = i2 - N_GROUPS

    hit1 = lane == e1
    hit2 = lane == e2
    onehot = jnp.where(hit1 | hit2, 1.0, 0.0)
    before = jnp.dot(lstrict_ref[...], onehot.astype(BF16), preferred_element_type=F32) + cnt_ref[...]
    r1 = jnp.sum(jnp.where(hit1, before, 0.0), axis=-1, keepdims=True)
    r2 = jnp.sum(jnp.where(hit2, before, 0.0), axis=-1, keepdims=True)
    cnt_ref[...] = cnt_ref[...] + jnp.sum(onehot, axis=0, keepdims=True)
    counts_ref[...] = cnt_ref[...]

    route = jnp.zeros_like(logits)
    for k, val in enumerate((e1, e2, g1, g2, r1, r2)):
        route = jnp.where(lane == k, val, route)
    route_ref[...] = route


def _mixout(att, su, sv, x2d, sgu_w, sgu_bt, sgu_gn, w_out, g_ffn, rw_hi, rw_lo, rb, lstrict, *,
            att_cols):
    t, d = x2d.shape
    tm = ROW_TILE
    row = lambda i: (i, 0)
    fixed2 = lambda i: (0, 0)
    n_groups = sgu_w.shape[0]
    kern = functools.partial(_mixout_kernel, n_sgu_groups=n_groups, att_cols=att_cols)
    return pl.pallas_call(
        kern,
        grid=(t // tm,),
        in_specs=[
            pl.BlockSpec((tm, att.shape[1]), row),
            pl.BlockSpec((tm, su.shape[1]), row),
            pl.BlockSpec((tm, sv.shape[1]), row),
            pl.BlockSpec((tm, d), row),
            pl.BlockSpec(sgu_w.shape, lambda i: (0, 0, 0)),
            pl.BlockSpec(sgu_bt.shape, fixed2),
            pl.BlockSpec(sgu_gn.shape, fixed2),
            pl.BlockSpec(w_out.shape, fixed2),
            pl.BlockSpec((1, d), fixed2),
            pl.BlockSpec(rw_hi.shape, fixed2),
            pl.BlockSpec(rw_lo.shape, fixed2),
            pl.BlockSpec((1, LANES), fixed2),
            pl.BlockSpec((tm, tm), fixed2),
        ],
        out_specs=[
            pl.BlockSpec((tm, d), row),
            pl.BlockSpec((tm, d // LANES, LANES), lambda i: (i, 0, 0)),
            pl.BlockSpec((tm, LANES), row),
            pl.BlockSpec((1, LANES), fixed2),
        ],
        out_shape=[
            jax.ShapeDtypeStruct((t, d), F32),
            jax.ShapeDtypeStruct((t, d // LANES, LANES), F32),
            jax.ShapeDtypeStruct((t, LANES), F32),
            jax.ShapeDtypeStruct((1, LANES), F32),
        ],
        scratch_shapes=[pltpu.VMEM((1, LANES), F32)],
        compiler_params=_params(1),
        name="mixout",
    )(att, su, sv, x2d, sgu_w, sgu_bt, sgu_gn, w_out, g_ffn, rw_hi, rw_lo, rb, lstrict)


def _dispatch_kernel(dest_ref, hn_ref, xs_ref, sem):
    rows = hn_ref.shape[0]

    def issue(r, c):
        for k in range(TOP_K):
            pltpu.make_async_copy(hn_ref.at[r], xs_ref.at[dest_ref[TOP_K * r + k]], sem).start()
        return c

    lax.fori_loop(0, rows, issue, 0, unroll=4)
    for k in range(TOP_K):
        pltpu.make_async_copy(hn_ref, xs_ref.at[pl.ds(0, rows)], sem).wait()


def _dispatch(hn3, dest):
    t = hn3.shape[0]
    rows = ROW_TILE
    return pl.pallas_call(
        _dispatch_kernel,
        grid=(t // rows,),
        in_specs=[
            pl.BlockSpec((rows * TOP_K,), lambda i: (i,), memory_space=pltpu.SMEM),
            pl.BlockSpec((rows,) + hn3.shape[1:], lambda i: (i, 0, 0)),
        ],
        out_specs=pl.BlockSpec(memory_space=pl.ANY),
        out_shape=jax.ShapeDtypeStruct((t * TOP_K,) + hn3.shape[1:], hn3.dtype),
        scratch_shapes=[pltpu.SemaphoreType.DMA],
        compiler_params=_params(1),
        name="dispatch",
    )(dest, hn3)


def _experts_kernel(vt_ref, ve_ref, vf_ref, lo_ref, hi_ref, xs_ref, wg_ref, wu_ref, wd_ref, ys_ref):
    v = pl.program_id(0)
    flag = vf_ref[v]
    tm, n_chunks, _ = xs_ref.shape

    @pl.when(flag > 0)
    def _():
        e = ve_ref[v]
        base = vt_ref[v] * tm
        rid = lax.broadcasted_iota(jnp.int32, (tm, 1), 0) + base
        mine = (rid >= lo_ref[e]) & (rid < hi_ref[e])
        x = jnp.concatenate([xs_ref[:, c, :] for c in range(n_chunks)], axis=1).astype(BF16)
        hg = jnp.dot(x, wg_ref[0], preferred_element_type=F32)
        hu = jnp.dot(x, wu_ref[0], preferred_element_type=F32)
        a = (hg * (1.0 / (1.0 + jnp.exp(-hg))) * hu).astype(BF16)
        y = jnp.dot(a, wd_ref[0], preferred_element_type=F32)

        @pl.when(flag == 1)
        def _():
            for c in range(n_chunks):
                ys_ref[:, c, :] = jnp.where(mine, y[:, c * LANES:(c + 1) * LANES], 0.0)

        @pl.when(flag == 2)
        def _():
            for c in range(n_chunks):
                ys_ref[:, c, :] = jnp.where(mine, y[:, c * LANES:(c + 1) * LANES], ys_ref[:, c, :])


def _experts(visit_tile, visit_expert, visit_flag, row_lo, row_hi, xs, w_gate, w_up, w_down):
    p, n_chunks, _ = xs.shape
    tm = EXPERT_TILE
    d = n_chunks * LANES
    f = w_gate.shape[-1]
    rows_map = lambda v, vt, ve, vf, lo, hi: (vt[v], 0, 0)
    w_map = lambda v, vt, ve, vf, lo, hi: (ve[v], 0, 0)
    grid_spec = pltpu.PrefetchScalarGridSpec(
        num_scalar_prefetch=5,
        grid=(visit_tile.shape[0],),
        in_specs=[
            pl.BlockSpec((tm, n_chunks, LANES), rows_map),
            pl.BlockSpec((1, d, f), w_map),
            pl.BlockSpec((1, d, f), w_map),
            pl.BlockSpec((1, f, d), w_map),
        ],
        out_specs=pl.BlockSpec((tm, n_chunks, LANES), rows_map),
    )
    return pl.pallas_call(
        _experts_kernel,
        grid_spec=grid_spec,
        out_shape=jax.ShapeDtypeStruct(xs.shape, F32),
        compiler_params=_params(1),
        name="experts",
    )(visit_tile, visit_expert, visit_flag, row_lo, row_hi, xs, w_gate, w_up, w_down)


def _combine_kernel(dest_ref, h_ref, route_ref, ys_ref, o_ref, ybuf, sem):
    rows = h_ref.shape[0]
    n_chunks = ybuf.shape[2]

    def issue(r, c):
        for k in range(TOP_K):
            pltpu.make_async_copy(ys_ref.at[dest_ref[TOP_K * r + k]], ybuf.at[k, r], sem).start()
        return c

    lax.fori_loop(0, rows, issue, 0, unroll=4)
    for k in range(TOP_K):
        pltpu.make_async_copy(ys_ref.at[pl.ds(0, rows)], ybuf.at[k], sem).wait()

    route = route_ref[...]
    moe = None
    for k in range(TOP_K):
        gate = route[:, ROUTE_GATE + k:ROUTE_GATE + k + 1]
        yk = jnp.concatenate([ybuf[k, :, c, :] for c in range(n_chunks)], axis=1)
        moe = gate * yk if moe is None else moe + gate * yk
    o_ref[...] = h_ref[...] + moe


def _combine(dest, h, route, ys):
    t, d = h.shape
    rows = ROW_TILE
    return pl.pallas_call(
        _combine_kernel,
        grid=(t // rows,),
        in_specs=[
            pl.BlockSpec((rows * TOP_K,), lambda i: (i,), memory_space=pltpu.SMEM),
            pl.BlockSpec((rows, d), lambda i: (i, 0)),
            pl.BlockSpec((rows, LANES), lambda i: (i, 0)),
            pl.BlockSpec(memory_space=pl.ANY),
        ],
        out_specs=pl.BlockSpec((rows, d), lambda i: (i, 0)),
        out_shape=jax.ShapeDtypeStruct((t, d), F32),
        scratch_shapes=[pltpu.VMEM((TOP_K, rows) + ys.shape[1:], F32), pltpu.SemaphoreType.DMA],
        compiler_params=_params(1),
        name="combine",
    )(dest, h, route, ys)


def _plan(route, counts_row):
    t = route.shape[0]
    counts = counts_row[0, :N_EXPERTS].astype(I32)
    ends = jnp.cumsum(counts)
    offs = ends - counts
    experts = route[:, ROUTE_EXPERT:ROUTE_EXPERT + TOP_K].astype(I32)
    rank = route[:, ROUTE_RANK:ROUTE_RANK + TOP_K].astype(I32)
    dest = (jnp.take(offs, experts) + rank).reshape(-1)

    tm = EXPERT_TILE
    n_visits = (t * TOP_K) // tm + N_EXPERTS - 1
    first_tile = offs // tm
    last_tile = (ends - 1) // tm
    nvis = jnp.where(counts > 0, last_tile - first_tile + 1, 0)
    vend = jnp.cumsum(nvis)
    vstart = vend - nvis
    v = jnp.arange(n_visits, dtype=I32)
    ev = jnp.minimum(jnp.searchsorted(vend, v, side="right"), N_EXPERTS - 1).astype(I32)
    tv = jnp.take(first_tile, ev) + (v - jnp.take(vstart, ev))
    valid = v < vend[-1]
    last_v = vend[-1] - 1
    ev = jnp.where(valid, ev, jnp.take(ev, last_v))
    tv = jnp.where(valid, tv, jnp.take(tv, last_v))
    prev_t = jnp.concatenate([jnp.full((1,), -1, I32), tv[:-1]])
    flag = jnp.where(valid, jnp.where(tv != prev_t, 1, 2), 0).astype(I32)
    return dest.astype(I32), tv.astype(I32), ev, flag, offs.astype(I32), ends.astype(I32)


def kernel(x, positions, norm_mix_g, w_in, q_norm_g, k_norm_g, lambda_q1, lambda_k1, lambda_q2,
           lambda_k2, diff_out_norm_g, sgu_ln_g, sgu_ln_b, sgu_w, sgu_b, sgu_out_norm_g, w_out,
           norm_ffn_g, router_group_w, router_group_b, router_expert_w, router_expert_b,
           expert_w_gate, expert_w_up, expert_w_down):
    b, s, d = x.shape
    t = b * s
    depth = w_in.shape[0]
    head_dim = q_norm_g.shape[-1]
    vd = diff_out_norm_g.shape[-1]
    n_sgu_groups = sgu_w.shape[1]
    sgu_cols = n_sgu_groups * sgu_w.shape[-1]
    in_cols = w_in.shape[-1]
    att_cols = (in_cols - 2 * sgu_cols) // 3
    n_heads = att_cols // vd
    qk_cols = 2 * att_cols
    rope_dim = head_dim // ROPE_FRACTION
    rope_half = rope_dim // 2
    scale = head_dim ** -0.5

    inv_freq = ROPE_THETA ** (-jnp.arange(0, rope_dim, 2, dtype=F32) / rope_dim)
    ang = positions.reshape(t, 1).astype(F32) * inv_freq[None, :]
    cos, sin = jnp.cos(ang), jnp.sin(ang)
    pad = jnp.zeros((t, head_dim - rope_dim), F32)
    cos_p = jnp.concatenate([cos, cos, pad + 1.0], axis=1)
    sa_p = jnp.concatenate([-sin, jnp.zeros_like(sin), pad], axis=1)
    sb_p = jnp.concatenate([jnp.zeros_like(sin), sin, pad], axis=1)
    reps = LANES // head_dim
    cos_t, sa_t, sb_t = (jnp.tile(a, (1, reps)) for a in (cos_p, sa_p, sb_p))

    blk = jnp.arange(MXU_WIDTH, dtype=I32) // head_dim
    ones_blk = (blk[:, None] == blk[None, :]).astype(BF16)
    tri = jnp.arange(ROW_TILE, dtype=I32)
    lstrict = (tri[None, :] < tri[:, None]).astype(BF16)

    h = x.reshape(t, d)
    for l in range(depth):
        lambda_init = 0.8 - 0.6 * math.exp(-0.3 * l)
        gqk = jnp.concatenate([jnp.tile(q_norm_g[l] * scale, att_cols // head_dim),
                               jnp.tile(k_norm_g[l], att_cols // head_dim)]).reshape(1, qk_cols)
        qk, v, su, sv = _inproj(
            h, norm_mix_g[l].reshape(1, d), w_in[l].astype(BF16), gqk, cos_t, sa_t, sb_t, ones_blk,
            sgu_ln_g[l].reshape(1, sgu_cols), sgu_ln_b[l].reshape(1, sgu_cols),
            qk_cols=qk_cols, v_cols=att_cols, sgu_cols=sgu_cols, head_dim=head_dim,
            rope_half=rope_half)

        lam_vecs = jnp.stack([lambda_q1[l], lambda_k1[l], lambda_q2[l], lambda_k2[l]]).astype(F32)
        att = _attention(lam_vecs, diff_out_norm_g[l].reshape(1, vd), qk.reshape(b, s, qk_cols),
                         v.reshape(b, s, att_cols), n_heads=n_heads, head_dim=head_dim,
                         lambda_init=lambda_init).reshape(t, att_cols)

        rw = jnp.concatenate([router_group_w[l], router_expert_w[l]], axis=1)
        rw = jnp.pad(rw, ((0, 0), (0, LANES - rw.shape[1])))
        rw_hi = rw.astype(BF16)
        rw_lo = (rw - rw_hi.astype(F32)).astype(BF16)
        rb = jnp.pad(jnp.concatenate([router_group_b[l], router_expert_b[l]]),
                     (0, LANES - N_GROUPS - N_EXPERTS)).reshape(1, LANES)
        h, hn3, route, counts_row = _mixout(
            att, su, sv, h, sgu_w[l], sgu_b[l].T, sgu_out_norm_g[l], w_out[l].astype(BF16),
            norm_ffn_g[l].reshape(1, d), rw_hi, rw_lo, rb, lstrict, att_cols=att_cols)

        dest, visit_tile, visit_expert, visit_flag, row_lo, row_hi = _plan(route, counts_row)
        xs = _dispatch(hn3, dest)
        ys = _experts(visit_tile, visit_expert, visit_flag, row_lo, row_hi, xs,
                      expert_w_gate[l].astype(BF16), expert_w_up[l].astype(BF16),
                      expert_w_down[l].astype(BF16))
        h = _combine(dest, h, route, ys)
    return h.reshape(b, s, d)
```

```python
import functools
import math

import jax
import jax.numpy as jnp
from jax import lax
from jax.experimental import pallas as pl
from jax.experimental.pallas import tpu as pltpu

F32 = jnp.float32
BF16 = jnp.bfloat16
I32 = jnp.int32

RMS_EPS = 1e-6
NEG_INF = -1e30
ROPE_THETA = 500000.0
MAX_UNSHIFTED_SCORE = 40.0

LANES = 128
SUBLANES = 8
MXU_WIDTH = 256
VMEM_LIMIT_BYTES = 56 * 1024 * 1024

N_GROUPS = 4
EXPERTS_PER_GROUP = 8
N_EXPERTS = N_GROUPS * EXPERTS_PER_GROUP
TOP_K = 2
ROPE_FRACTION = 4
CHUNK = 128
QBLOCK = 256
ROW_TILE = 512
EXPERT_TILE = 256

ROUTE_EXPERT = 0
ROUTE_GATE = 2
ROUTE_RANK = 4


def _params(n_grid_dims):
    return pltpu.CompilerParams(
        dimension_semantics=("arbitrary",) * n_grid_dims,
        vmem_limit_bytes=VMEM_LIMIT_BYTES,
    )


def _gelu(x):
    return 0.5 * x * (1.0 + lax.erf(x * (1.0 / math.sqrt(2.0))))


def _inproj_kernel(x_ref, g_ref, w_ref, gqk_ref, cos_ref, sa_ref, sb_ref, ones_ref, lng_ref, lnb_ref,
                   qk_ref, v_ref, su_ref, sv_ref, *, qk_cols, v_cols, sgu_cols, head_dim, rope_half):
    x = x_ref[...]
    ms = jnp.mean(x * x, axis=-1, keepdims=True)
    xn = (x * lax.rsqrt(ms + RMS_EPS) * g_ref[...]).astype(BF16)
    cw = MXU_WIDTH

    def proj(col0):
        return jnp.dot(xn, w_ref[:, col0:col0 + cw], preferred_element_type=F32)

    cos = cos_ref[...]
    sa = sa_ref[...]
    sb = sb_ref[...]
    for c in range(qk_cols // cw):
        p = proj(c * cw)
        ss = jnp.dot((p * p).astype(BF16), ones_ref[...], preferred_element_type=F32)
        pn = p * lax.rsqrt(ss * (1.0 / head_dim) + RMS_EPS) * gqk_ref[:, c * cw:(c + 1) * cw]
        for hh in range(cw // LANES):
            blk = pn[:, hh * LANES:(hh + 1) * LANES]
            rot = (blk * cos
                   + pltpu.roll(blk, LANES - rope_half, 1) * sa
                   + pltpu.roll(blk, rope_half, 1) * sb)
            col = c * cw + hh * LANES
            qk_ref[:, col:col + LANES] = rot.astype(BF16)

    for c in range(v_cols // cw):
        v_ref[:, c * cw:(c + 1) * cw] = proj(qk_cols + c * cw).astype(BF16)

    for c in range(sgu_cols // cw):
        su_ref[:, c * cw:(c + 1) * cw] = _gelu(proj(qk_cols + v_cols + c * cw)).astype(BF16)

    sv = jnp.concatenate(
        [_gelu(proj(qk_cols + v_cols + sgu_cols + c * cw)) for c in range(sgu_cols // cw)], axis=1)
    mu = jnp.mean(sv, axis=-1, keepdims=True)
    svc = sv - mu
    var = jnp.mean(svc * svc, axis=-1, keepdims=True)
    sv_ref[...] = (svc * lax.rsqrt(var + RMS_EPS) * lng_ref[...] + lnb_ref[...]).astype(BF16)


def _inproj(x2d, norm_g, w_in, gqk, cos_t, sa_t, sb_t, ones_blk, ln_g, ln_b, *, qk_cols, v_cols,
            sgu_cols, head_dim, rope_half):
    t, d = x2d.shape
    tm = ROW_TILE
    row = lambda i: (i, 0)
    fixed = lambda i: (0, 0)
    kern = functools.partial(_inproj_kernel, qk_cols=qk_cols, v_cols=v_cols, sgu_cols=sgu_cols,
                             head_dim=head_dim, rope_half=rope_half)
    return pl.pallas_call(
        kern,
        grid=(t // tm,),
        in_specs=[
            pl.BlockSpec((tm, d), row),
            pl.BlockSpec((1, d), fixed),
            pl.BlockSpec(w_in.shape, fixed),
            pl.BlockSpec((1, qk_cols), fixed),
            pl.BlockSpec((tm, LANES), row),
            pl.BlockSpec((tm, LANES), row),
            pl.BlockSpec((tm, LANES), row),
            pl.BlockSpec((MXU_WIDTH, MXU_WIDTH), fixed),
            pl.BlockSpec((1, sgu_cols), fixed),
            pl.BlockSpec((1, sgu_cols), fixed),
        ],
        out_specs=[
            pl.BlockSpec((tm, qk_cols), row),
            pl.BlockSpec((tm, v_cols), row),
            pl.BlockSpec((tm, sgu_cols), row),
            pl.BlockSpec((tm, sgu_cols), row),
        ],
        out_shape=[
            jax.ShapeDtypeStruct((t, qk_cols), BF16),
            jax.ShapeDtypeStruct((t, v_cols), BF16),
            jax.ShapeDtypeStruct((t, sgu_cols), BF16),
            jax.ShapeDtypeStruct((t, sgu_cols), BF16),
        ],
        compiler_params=_params(1),
        name="inproj",
    )(x2d, norm_g, w_in, gqk, cos_t, sa_t, sb_t, ones_blk, ln_g, ln_b)


def _attn_kernel(shift_ref, lam_ref, gout_ref, q_ref, k_ref, v_ref, o_ref, qm_ref, m_ref, acc_ref, *, tq,
                 n_heads, lambda_init, head_dim):
    qi = pl.program_id(1)
    hw = 2 * head_dim
    vd = v_ref.shape[-1] // n_heads
    lv = lam_ref[...]
    lam = (jnp.exp(jnp.sum(lv[0:1] * lv[1:2], axis=-1, keepdims=True))
           - jnp.exp(jnp.sum(lv[2:3] * lv[3:4], axis=-1, keepdims=True))
           + lambda_init)

    first = lax.broadcasted_iota(jnp.int32, (1, hw), 1) < head_dim
    for h in range(n_heads):
        q = q_ref[0, :, h * hw:(h + 1) * hw]
        zero = jnp.zeros_like(q)
        qm_ref[2 * h] = jnp.where(first, q, zero)
        qm_ref[2 * h + 1] = jnp.where(first, zero, q)
    acc_ref[...] = jnp.zeros(acc_ref.shape, F32)

    ones = jnp.ones((tq, vd), BF16)
    causal = (lax.broadcasted_iota(jnp.int32, (tq, tq), 1)
              <= lax.broadcasted_iota(jnp.int32, (tq, tq), 0))

    def step(j, masked, running_max):
        start = pl.multiple_of(j * tq, tq)
        for h in range(n_heads):
            kc = k_ref[0, pl.ds(start, tq), h * hw:(h + 1) * hw]
            vext = jnp.concatenate([v_ref[0, pl.ds(start, tq), h * vd:(h + 1) * vd], ones], axis=1)
            for mp in range(2):
                c = 2 * h + mp
                s = lax.dot_general(qm_ref[c], kc, (((1,), (1,)), ((), ())), preferred_element_type=F32)
                if masked:
                    s = jnp.where(causal, s, NEG_INF)
                if running_max:
                    m_old = m_ref[c]
                    m_new = jnp.maximum(m_old, jnp.max(s, axis=-1, keepdims=True))
                    p = jnp.exp(s - m_new).astype(BF16)
                    acc_ref[c] = (jnp.exp(m_old - m_new) * acc_ref[c]
                                  + jnp.dot(p, vext, preferred_element_type=F32))
                    m_ref[c] = m_new
                else:
                    acc_ref[c] += jnp.dot(jnp.exp(s).astype(BF16), vext, preferred_element_type=F32)

    def sweep(running_max):
        def body(j, carry):
            step(j, False, running_max)
            return carry

        lax.fori_loop(0, qi, body, 0)
        step(qi, True, running_max)

    @pl.when(shift_ref[0] == 0)
    def _():
        sweep(False)

    @pl.when(shift_ref[0] != 0)
    def _():
        m_ref[...] = jnp.full(m_ref.shape, NEG_INF, F32)
        sweep(True)

    for h in range(n_heads):
        a1 = acc_ref[2 * h]
        a2 = acc_ref[2 * h + 1]
        att = a1[:, :vd] / a1[:, vd:] - lam * (a2[:, :vd] / a2[:, vd:])
        ms = jnp.mean(att * att, axis=-1, keepdims=True)
        o_ref[0, :, h * vd:(h + 1) * vd] = (
            att * lax.rsqrt(ms + RMS_EPS) * gout_ref[...] * (1.0 - lambda_init)).astype(BF16)


def _attention(need_shift, lam_vecs, gout, qk3, v3, *, n_heads, head_dim, lambda_init):
    b, s, _ = qk3.shape
    att_cols = v3.shape[-1]
    vd = att_cols // n_heads
    tq = QBLOCK
    kern = functools.partial(_attn_kernel, tq=tq, n_heads=n_heads, lambda_init=lambda_init,
                             head_dim=head_dim)
    grid_spec = pltpu.PrefetchScalarGridSpec(
        num_scalar_prefetch=1,
        grid=(b, s // tq),
        in_specs=[
            pl.BlockSpec(lam_vecs.shape, lambda bi, qi, f: (0, 0)),
            pl.BlockSpec((1, vd), lambda bi, qi, f: (0, 0)),
            pl.BlockSpec((1, tq, att_cols), lambda bi, qi, f: (bi, qi, 0)),
            pl.BlockSpec((1, s, att_cols), lambda bi, qi, f: (bi, 0, 1)),
            pl.BlockSpec((1, s, att_cols), lambda bi, qi, f: (bi, 0, 0)),
        ],
        out_specs=pl.BlockSpec((1, tq, att_cols), lambda bi, qi, f: (bi, qi, 0)),
        scratch_shapes=[
            pltpu.VMEM((2 * n_heads, tq, 2 * head_dim), BF16),
            pltpu.VMEM((2 * n_heads, tq, 1), F32),
            pltpu.VMEM((2 * n_heads, tq, 2 * vd), F32),
        ],
    )
    return pl.pallas_call(
        kern,
        grid_spec=grid_spec,
        out_shape=jax.ShapeDtypeStruct(v3.shape, BF16),
        compiler_params=_params(2),
        name="diff_attn",
    )(need_shift, lam_vecs, gout, qk3, qk3, v3)


def _mixout_kernel(att_ref, su_ref, sv_ref, x_ref, sw_ref, sbt_ref, sgn_ref, wo_ref, gffn_ref,
                   rwh_ref, rwl_ref, rb_ref, lstrict_ref, h_ref, hn_ref, route_ref, counts_ref, cnt_ref,
                   *, n_sgu_groups, att_cols):
    tm = x_ref.shape[0]
    gw = CHUNK
    r_id = lax.broadcasted_iota(jnp.int32, (gw, gw), 0)
    c_id = lax.broadcasted_iota(jnp.int32, (gw, gw), 1)
    tril = c_id <= r_id

    @pl.when(pl.program_id(0) == 0)
    def _():
        cnt_ref[...] = jnp.zeros_like(cnt_ref)

    acc = jnp.dot(att_ref[...], wo_ref[0:att_cols, :], preferred_element_type=F32)

    sg_cols = []
    for g in range(n_sgu_groups):
        w = jnp.where(tril, sw_ref[g], 0.0).astype(BF16)
        bias = sbt_ref[:, g:g + 1]
        gn = sgn_ref[g:g + 1, :]
        rows = []
        for c in range(tm // gw):
            vblk = sv_ref[c * gw:(c + 1) * gw, g * gw:(g + 1) * gw]
            s = jnp.dot(w, vblk, preferred_element_type=F32) + bias
            sg = su_ref[c * gw:(c + 1) * gw, g * gw:(g + 1) * gw].astype(F32) * s
            ms = jnp.mean(sg * sg, axis=-1, keepdims=True)
            rows.append((sg * lax.rsqrt(ms + RMS_EPS) * gn).astype(BF16))
        sg_cols.append(jnp.concatenate(rows, axis=0))
    sgn = jnp.concatenate(sg_cols, axis=1)
    acc = acc + jnp.dot(sgn, wo_ref[att_cols:, :], preferred_element_type=F32)

    h = x_ref[...] + acc
    h_ref[...] = h
    ms = jnp.mean(h * h, axis=-1, keepdims=True)
    hn = h * lax.rsqrt(ms + RMS_EPS) * gffn_ref[...]
    hn_bf = hn.astype(BF16)
    for c in range(hn.shape[1] // LANES):
        hn_ref[:, c, :] = hn[:, c * LANES:(c + 1) * LANES]

    hn_lo = (hn - hn_bf.astype(F32)).astype(BF16)
    logits = (jnp.dot(hn_bf, rwh_ref[...], preferred_element_type=F32)
              + jnp.dot(hn_lo, rwh_ref[...], preferred_element_type=F32)
              + jnp.dot(hn_bf, rwl_ref[...], preferred_element_type=F32)
              + rb_ref[...])

    lane = lax.broadcasted_iota(jnp.int32, logits.shape, 1).astype(F32)
    big = float(LANES)
    is_grp = lane < N_GROUPS
    gl = jnp.where(is_grp, logits, NEG_INF)
    gmax = jnp.max(gl, axis=-1, keepdims=True)
    gidx = jnp.min(jnp.where(gl == gmax, lane, big), axis=-1, keepdims=True)
    psum = jnp.sum(jnp.where(is_grp, jnp.exp(gl - gmax), 0.0), axis=-1, keepdims=True)
    p_grp = 1.0 / psum
    e_lo = N_GROUPS + gidx * EXPERTS_PER_GROUP
    in_grp = (lane >= e_lo) & (lane < e_lo + EXPERTS_PER_GROUP)
    el = jnp.where(in_grp, logits, NEG_INF)
    t1 = jnp.max(el, axis=-1, keepdims=True)
    i1 = jnp.min(jnp.where(el == t1, lane, big), axis=-1, keepdims=True)
    el2 = jnp.where(lane == i1, NEG_INF, el)
    t2 = jnp.max(el2, axis=-1, keepdims=True)
    i2 = jnp.min(jnp.where(el2 == t2, lane, big), axis=-1, keepdims=True)
    e21 = jnp.exp(t2 - t1)
    den = 1.0 + e21
    g1 = p_grp / den
    g2 = p_grp * (e21 / den)
    e1 = i1 - N_GROUPS
    e2 = i2 - N_GROUPS

    hit1 = lane == e1
    hit2 = lane == e2
    onehot = jnp.where(hit1 | hit2, 1.0, 0.0)
    before = jnp.dot(lstrict_ref[...], onehot.astype(BF16), preferred_element_type=F32) + cnt_ref[...]
    r1 = jnp.sum(jnp.where(hit1, before, 0.0), axis=-1, keepdims=True)
    r2 = jnp.sum(jnp.where(hit2, before, 0.0), axis=-1, keepdims=True)
    cnt_ref[...] = cnt_ref[...] + jnp.sum(onehot, axis=0, keepdims=True)
    counts_ref[...] = cnt_ref[...]

    route = jnp.zeros_like(logits)
    for k, val in enumerate((e1, e2, g1, g2, r1, r2)):
        route = jnp.where(lane == k, val, route)
    route_ref[...] = route


def _mixout(att, su, sv, x2d, sgu_w, sgu_bt, sgu_gn, w_out, g_ffn, rw_hi, rw_lo, rb, lstrict, *,
            att_cols):
    t, d = x2d.shape
    tm = ROW_TILE
    row = lambda i: (i, 0)
    fixed2 = lambda i: (0, 0)
    n_groups = sgu_w.shape[0]
    kern = functools.partial(_mixout_kernel, n_sgu_groups=n_groups, att_cols=att_cols)
    return pl.pallas_call(
        kern,
        grid=(t // tm,),
        in_specs=[
            pl.BlockSpec((tm, att.shape[1]), row),
            pl.BlockSpec((tm, su.shape[1]), row),
            pl.BlockSpec((tm, sv.shape[1]), row),
            pl.BlockSpec((tm, d), row),
            pl.BlockSpec(sgu_w.shape, lambda i: (0, 0, 0)),
            pl.BlockSpec(sgu_bt.shape, fixed2),
            pl.BlockSpec(sgu_gn.shape, fixed2),
            pl.BlockSpec(w_out.shape, fixed2),
            pl.BlockSpec((1, d), fixed2),
            pl.BlockSpec(rw_hi.shape, fixed2),
            pl.BlockSpec(rw_lo.shape, fixed2),
            pl.BlockSpec((1, LANES), fixed2),
            pl.BlockSpec((tm, tm), fixed2),
        ],
        out_specs=[
            pl.BlockSpec((tm, d), row),
            pl.BlockSpec((tm, d // LANES, LANES), lambda i: (i, 0, 0)),
            pl.BlockSpec((tm, LANES), row),
            pl.BlockSpec((1, LANES), fixed2),
        ],
        out_shape=[
            jax.ShapeDtypeStruct((t, d), F32),
            jax.ShapeDtypeStruct((t, d // LANES, LANES), F32),
            jax.ShapeDtypeStruct((t, LANES), F32),
            jax.ShapeDtypeStruct((1, LANES), F32),
        ],
        scratch_shapes=[pltpu.VMEM((1, LANES), F32)],
        compiler_params=_params(1),
        name="mixout",
    )(att, su, sv, x2d, sgu_w, sgu_bt, sgu_gn, w_out, g_ffn, rw_hi, rw_lo, rb, lstrict)


def _dispatch_kernel(dest_ref, hn_ref, xs_ref, sem):
    rows = hn_ref.shape[0]

    def issue(r, c):
        for k in range(TOP_K):
            pltpu.make_async_copy(hn_ref.at[r], xs_ref.at[dest_ref[TOP_K * r + k]], sem).start()
        return c

    lax.fori_loop(0, rows, issue, 0, unroll=4)
    for k in range(TOP_K):
        pltpu.make_async_copy(hn_ref, xs_ref.at[pl.ds(0, rows)], sem).wait()


def _dispatch(hn3, dest):
    t = hn3.shape[0]
    rows = ROW_TILE
    return pl.pallas_call(
        _dispatch_kernel,
        grid=(t // rows,),
        in_specs=[
            pl.BlockSpec((rows * TOP_K,), lambda i: (i,), memory_space=pltpu.SMEM),
            pl.BlockSpec((rows,) + hn3.shape[1:], lambda i: (i, 0, 0)),
        ],
        out_specs=pl.BlockSpec(memory_space=pl.ANY),
        out_shape=jax.ShapeDtypeStruct((t * TOP_K,) + hn3.shape[1:], hn3.dtype),
        scratch_shapes=[pltpu.SemaphoreType.DMA],
        compiler_params=_params(1),
        name="dispatch",
    )(dest, hn3)


def _experts_kernel(vt_ref, ve_ref, vf_ref, vn_ref, lo_ref, hi_ref, xs_ref, wg_ref, wu_ref, wd_ref, ys_ref,
                    wg_bf, wu_bf, wd_bf):
    v = pl.program_id(0)
    flag = vf_ref[v]
    tm, n_chunks, _ = xs_ref.shape

    @pl.when(vn_ref[v] == 1)
    def _():
        wg_bf[...] = wg_ref[0].astype(BF16)
        wu_bf[...] = wu_ref[0].astype(BF16)
        wd_bf[...] = wd_ref[0].astype(BF16)

    @pl.when(flag > 0)
    def _():
        e = ve_ref[v]
        base = vt_ref[v] * tm
        rid = lax.broadcasted_iota(jnp.int32, (tm, 1), 0) + base
        mine = (rid >= lo_ref[e]) & (rid < hi_ref[e])
        x = jnp.concatenate([xs_ref[:, c, :] for c in range(n_chunks)], axis=1).astype(BF16)
        hg = jnp.dot(x, wg_bf[...], preferred_element_type=F32)
        hu = jnp.dot(x, wu_bf[...], preferred_element_type=F32)
        a = (hg * (1.0 / (1.0 + jnp.exp(-hg))) * hu).astype(BF16)
        y = jnp.dot(a, wd_bf[...], preferred_element_type=F32)

        @pl.when(flag == 1)
        def _():
            for c in range(n_chunks):
                ys_ref[:, c, :] = jnp.where(mine, y[:, c * LANES:(c + 1) * LANES], 0.0)

        @pl.when(flag == 2)
        def _():
            for c in range(n_chunks):
                ys_ref[:, c, :] = jnp.where(mine, y[:, c * LANES:(c + 1) * LANES], ys_ref[:, c, :])


def _experts(visit_tile, visit_expert, visit_flag, visit_new, row_lo, row_hi, xs, w_gate, w_up, w_down):
    p, n_chunks, _ = xs.shape
    tm = EXPERT_TILE
    d = n_chunks * LANES
    f = w_gate.shape[-1]
    rows_map = lambda v, vt, ve, vf, vn, lo, hi: (vt[v], 0, 0)
    w_map = lambda v, vt, ve, vf, vn, lo, hi: (ve[v], 0, 0)
    grid_spec = pltpu.PrefetchScalarGridSpec(
        num_scalar_prefetch=6,
        grid=(visit_tile.shape[0],),
        in_specs=[
            pl.BlockSpec((tm, n_chunks, LANES), rows_map),
            pl.BlockSpec((1, d, f), w_map),
            pl.BlockSpec((1, d, f), w_map),
            pl.BlockSpec((1, f, d), w_map),
        ],
        out_specs=pl.BlockSpec((tm, n_chunks, LANES), rows_map),
        scratch_shapes=[pltpu.VMEM((d, f), BF16), pltpu.VMEM((d, f), BF16), pltpu.VMEM((f, d), BF16)],
    )
    return pl.pallas_call(
        _experts_kernel,
        grid_spec=grid_spec,
        out_shape=jax.ShapeDtypeStruct(xs.shape, F32),
        compiler_params=_params(1),
        name="experts",
    )(visit_tile, visit_expert, visit_flag, visit_new, row_lo, row_hi, xs, w_gate, w_up, w_down)


def _combine_kernel(dest_ref, h_ref, route_ref, ys_ref, o_ref, ybuf, sem):
    rows = h_ref.shape[0]
    n_chunks = ybuf.shape[2]

    def issue(r, c):
        for k in range(TOP_K):
            pltpu.make_async_copy(ys_ref.at[dest_ref[TOP_K * r + k]], ybuf.at[k, r], sem).start()
        return c

    lax.fori_loop(0, rows, issue, 0, unroll=4)
    for k in range(TOP_K):
        pltpu.make_async_copy(ys_ref.at[pl.ds(0, rows)], ybuf.at[k], sem).wait()

    route = route_ref[...]
    moe = None
    for k in range(TOP_K):
        gate = route[:, ROUTE_GATE + k:ROUTE_GATE + k + 1]
        yk = jnp.concatenate([ybuf[k, :, c, :] for c in range(n_chunks)], axis=1)
        moe = gate * yk if moe is None else moe + gate * yk
    o_ref[...] = h_ref[...] + moe


def _combine(dest, h, route, ys):
    t, d = h.shape
    rows = ROW_TILE
    return pl.pallas_call(
        _combine_kernel,
        grid=(t // rows,),
        in_specs=[
            pl.BlockSpec((rows * TOP_K,), lambda i: (i,), memory_space=pltpu.SMEM),
            pl.BlockSpec((rows, d), lambda i: (i, 0)),
            pl.BlockSpec((rows, LANES), lambda i: (i, 0)),
            pl.BlockSpec(memory_space=pl.ANY),
        ],
        out_specs=pl.BlockSpec((rows, d), lambda i: (i, 0)),
        out_shape=jax.ShapeDtypeStruct((t, d), F32),
        scratch_shapes=[pltpu.VMEM((TOP_K, rows) + ys.shape[1:], F32), pltpu.SemaphoreType.DMA],
        compiler_params=_params(1),
        name="combine",
    )(dest, h, route, ys)


def _plan(route, counts_row):
    t = route.shape[0]
    counts = counts_row[0, :N_EXPERTS].astype(I32)
    ends = jnp.cumsum(counts)
    offs = ends - counts
    experts = route[:, ROUTE_EXPERT:ROUTE_EXPERT + TOP_K].astype(I32)
    rank = route[:, ROUTE_RANK:ROUTE_RANK + TOP_K].astype(I32)
    dest = (jnp.take(offs, experts) + rank).reshape(-1)

    tm = EXPERT_TILE
    n_visits = (t * TOP_K) // tm + N_EXPERTS - 1
    first_tile = offs // tm
    last_tile = (ends - 1) // tm
    nvis = jnp.where(counts > 0, last_tile - first_tile + 1, 0)
    vend = jnp.cumsum(nvis)
    vstart = vend - nvis
    v = jnp.arange(n_visits, dtype=I32)
    ev = jnp.minimum(jnp.sum((vend[None, :] <= v[:, None]).astype(I32), axis=1), N_EXPERTS - 1)
    tv = jnp.take(first_tile, ev) + (v - jnp.take(vstart, ev))
    valid = v < vend[-1]
    last_v = vend[-1] - 1
    ev = jnp.where(valid, ev, jnp.take(ev, last_v))
    tv = jnp.where(valid, tv, jnp.take(tv, last_v))
    prev_t = jnp.concatenate([jnp.full((1,), -1, I32), tv[:-1]])
    prev_e = jnp.concatenate([jnp.full((1,), -1, I32), ev[:-1]])
    flag = jnp.where(valid, jnp.where(tv != prev_t, 1, 2), 0).astype(I32)
    new_expert = (ev != prev_e).astype(I32)
    return dest.astype(I32), tv.astype(I32), ev.astype(I32), flag, new_expert, offs.astype(I32), ends.astype(I32)


def kernel(x, positions, norm_mix_g, w_in, q_norm_g, k_norm_g, lambda_q1, lambda_k1, lambda_q2,
           lambda_k2, diff_out_norm_g, sgu_ln_g, sgu_ln_b, sgu_w, sgu_b, sgu_out_norm_g, w_out,
           norm_ffn_g, router_group_w, router_group_b, router_expert_w, router_expert_b,
           expert_w_gate, expert_w_up, expert_w_down):
    b, s, d = x.shape
    t = b * s
    depth = w_in.shape[0]
    head_dim = q_norm_g.shape[-1]
    vd = diff_out_norm_g.shape[-1]
    n_sgu_groups = sgu_w.shape[1]
    sgu_cols = n_sgu_groups * sgu_w.shape[-1]
    in_cols = w_in.shape[-1]
    att_cols = (in_cols - 2 * sgu_cols) // 3
    n_heads = att_cols // vd
    qk_cols = 2 * att_cols
    rope_dim = head_dim // ROPE_FRACTION
    rope_half = rope_dim // 2
    scale = head_dim ** -0.5

    inv_freq = ROPE_THETA ** (-jnp.arange(0, rope_dim, 2, dtype=F32) / rope_dim)
    dim = jnp.arange(LANES, dtype=I32) % head_dim
    lane_freq = jnp.where(dim < rope_dim, inv_freq[dim % rope_half], 0.0)
    ang = positions.reshape(t, 1).astype(F32) * lane_freq[None, :]
    cos, sin = jnp.cos(ang), jnp.sin(ang)
    cos_t = cos
    sa_t = jnp.where(dim[None, :] < rope_half, -sin, 0.0)
    sb_t = jnp.where(dim[None, :] >= rope_half, sin, 0.0)

    blk = jnp.arange(MXU_WIDTH, dtype=I32) // head_dim
    ones_blk = (blk[:, None] == blk[None, :]).astype(BF16)
    tri = jnp.arange(ROW_TILE, dtype=I32)
    lstrict = (tri[None, :] < tri[:, None]).astype(BF16)

    h = x.reshape(t, d)
    for l in range(depth):
        lambda_init = 0.8 - 0.6 * math.exp(-0.3 * l)
        gqk = jnp.concatenate([jnp.tile(q_norm_g[l] * scale, att_cols // head_dim),
                               jnp.tile(k_norm_g[l], att_cols // head_dim)]).reshape(1, qk_cols)
        qk, v, su, sv = _inproj(
            h, norm_mix_g[l].reshape(1, d), w_in[l].astype(BF16), gqk, cos_t, sa_t, sb_t, ones_blk,
            sgu_ln_g[l].reshape(1, sgu_cols), sgu_ln_b[l].reshape(1, sgu_cols),
            qk_cols=qk_cols, v_cols=att_cols, sgu_cols=sgu_cols, head_dim=head_dim,
            rope_half=rope_half)

        lam_vecs = jnp.stack([lambda_q1[l], lambda_k1[l], lambda_q2[l], lambda_k2[l]]).astype(F32)
        score_bound = (1.02 * head_dim * scale * jnp.max(jnp.abs(q_norm_g[l]))
                       * jnp.max(jnp.abs(k_norm_g[l])))
        need_shift = (score_bound > MAX_UNSHIFTED_SCORE).astype(I32).reshape(1)
        att = _attention(need_shift, lam_vecs, diff_out_norm_g[l].reshape(1, vd), qk.reshape(b, s, qk_cols),
                         v.reshape(b, s, att_cols), n_heads=n_heads, head_dim=head_dim,
                         lambda_init=lambda_init).reshape(t, att_cols)

        rw = jnp.concatenate([router_group_w[l], router_expert_w[l]], axis=1)
        rw = jnp.pad(rw, ((0, 0), (0, LANES - rw.shape[1])))
        rw_hi = rw.astype(BF16)
        rw_lo = (rw - rw_hi.astype(F32)).astype(BF16)
        rb = jnp.pad(jnp.concatenate([router_group_b[l], router_expert_b[l]]),
                     (0, LANES - N_GROUPS - N_EXPERTS)).reshape(1, LANES)
        h, hn3, route, counts_row = _mixout(
            att, su, sv, h, sgu_w[l], sgu_b[l].T, sgu_out_norm_g[l], w_out[l].astype(BF16),
            norm_ffn_g[l].reshape(1, d), rw_hi, rw_lo, rb, lstrict, att_cols=att_cols)

        dest, visit_tile, visit_expert, visit_flag, visit_new, row_lo, row_hi = _plan(route, counts_row)
        xs = _dispatch(hn3, dest)
        ys = _experts(visit_tile, visit_expert, visit_flag, visit_new, row_lo, row_hi, xs,
                      expert_w_gate[l], expert_w_up[l], expert_w_down[l])
        h = _combine(dest, h, route, ys)
    return h.reshape(b, s, d)
```

```python
import functools
import math

import jax
import jax.numpy as jnp
from jax import lax
from jax.experimental import pallas as pl
from jax.experimental.pallas import tpu as pltpu

F32 = jnp.float32
BF16 = jnp.bfloat16
I32 = jnp.int32

RMS_EPS = 1e-6
NEG_INF = -1e30
ROPE_THETA = 500000.0
MAX_UNSHIFTED_SCORE = 40.0

LANES = 128
SUBLANES = 8
MXU_WIDTH = 256
VMEM_LIMIT_BYTES = 56 * 1024 * 1024

N_GROUPS = 4
EXPERTS_PER_GROUP = 8
N_EXPERTS = N_GROUPS * EXPERTS_PER_GROUP
TOP_K = 2
ROPE_FRACTION = 4
CHUNK = 128
QBLOCK = 256
ROW_TILE = 512
EXPERT_TILE = 512
COMBINE_TILE = 256

ROUTE_EXPERT = 0
ROUTE_GATE = 2
ROUTE_RANK = 4


def _params(n_grid_dims):
    return pltpu.CompilerParams(
        dimension_semantics=("arbitrary",) * n_grid_dims,
        vmem_limit_bytes=VMEM_LIMIT_BYTES,
    )


def _gelu(x):
    return 0.5 * x * (1.0 + lax.erf(x * (1.0 / math.sqrt(2.0))))


def _split_bf16(x):
    hi = x.astype(BF16)
    return hi, (x - hi.astype(F32)).astype(BF16)


def _inproj_kernel(x_ref, g_ref, w_ref, gqk_ref, cs_ref, expand_ref, ones_ref, lng_ref, lnb_ref,
                   qk_ref, v_ref, su_ref, sv_ref, *, qk_cols, v_cols, sgu_cols, head_dim, rope_half):
    x = x_ref[...]
    ms = jnp.mean(x * x, axis=-1, keepdims=True)
    xn = (x * lax.rsqrt(ms + RMS_EPS) * g_ref[...]).astype(BF16)
    cw = MXU_WIDTH

    def proj(col0):
        return jnp.dot(xn, w_ref[:, col0:col0 + cw], preferred_element_type=F32)

    cs_hi, cs_lo = _split_bf16(cs_ref[...])
    tab = (jnp.dot(cs_hi, expand_ref[...], preferred_element_type=F32)
           + jnp.dot(cs_lo, expand_ref[...], preferred_element_type=F32))
    cos = tab[:, 0:LANES]
    sa = tab[:, LANES:2 * LANES]
    sb = tab[:, 2 * LANES:3 * LANES]

    for c in range(qk_cols // cw):
        p = proj(c * cw)
        ss = jnp.dot((p * p).astype(BF16), ones_ref[...], preferred_element_type=F32)
        pn = p * lax.rsqrt(ss * (1.0 / head_dim) + RMS_EPS) * gqk_ref[:, c * cw:(c + 1) * cw]
        for hh in range(cw // LANES):
            blk = pn[:, hh * LANES:(hh + 1) * LANES]
            rot = (blk * cos
                   + pltpu.roll(blk, LANES - rope_half, 1) * sa
                   + pltpu.roll(blk, rope_half, 1) * sb)
            col = c * cw + hh * LANES
            qk_ref[:, col:col + LANES] = rot.astype(BF16)

    for c in range(v_cols // cw):
        v_ref[:, c * cw:(c + 1) * cw] = proj(qk_cols + c * cw).astype(BF16)

    for c in range(sgu_cols // cw):
        su_ref[:, c * cw:(c + 1) * cw] = _gelu(proj(qk_cols + v_cols + c * cw)).astype(BF16)

    sv = jnp.concatenate(
        [_gelu(proj(qk_cols + v_cols + sgu_cols + c * cw)) for c in range(sgu_cols // cw)], axis=1)
    mu = jnp.mean(sv, axis=-1, keepdims=True)
    svc = sv - mu
    var = jnp.mean(svc * svc, axis=-1, keepdims=True)
    sv_ref[...] = (svc * lax.rsqrt(var + RMS_EPS) * lng_ref[...] + lnb_ref[...]).astype(BF16)


def _inproj(x2d, norm_g, w_in, gqk, cs, expand, ones_blk, ln_g, ln_b, *, qk_cols, v_cols,
            sgu_cols, head_dim, rope_half):
    t, d = x2d.shape
    tm = ROW_TILE
    row = lambda i: (i, 0)
    fixed = lambda i: (0, 0)
    kern = functools.partial(_inproj_kernel, qk_cols=qk_cols, v_cols=v_cols, sgu_cols=sgu_cols,
                             head_dim=head_dim, rope_half=rope_half)
    return pl.pallas_call(
        kern,
        grid=(t // tm,),
        in_specs=[
            pl.BlockSpec((tm, d), row),
            pl.BlockSpec((1, d), fixed),
            pl.BlockSpec(w_in.shape, fixed),
            pl.BlockSpec((1, qk_cols), fixed),
            pl.BlockSpec((tm, LANES), row),
            pl.BlockSpec(expand.shape, fixed),
            pl.BlockSpec((MXU_WIDTH, MXU_WIDTH), fixed),
            pl.BlockSpec((1, sgu_cols), fixed),
            pl.BlockSpec((1, sgu_cols), fixed),
        ],
        out_specs=[
            pl.BlockSpec((tm, qk_cols), row),
            pl.BlockSpec((tm, v_cols), row),
            pl.BlockSpec((tm, sgu_cols), row),
            pl.BlockSpec((tm, sgu_cols), row),
        ],
        out_shape=[
            jax.ShapeDtypeStruct((t, qk_cols), BF16),
            jax.ShapeDtypeStruct((t, v_cols), BF16),
            jax.ShapeDtypeStruct((t, sgu_cols), BF16),
            jax.ShapeDtypeStruct((t, sgu_cols), BF16),
        ],
        compiler_params=_params(1),
        name="inproj",
    )(x2d, norm_g, w_in, gqk, cs, expand, ones_blk, ln_g, ln_b)


def _attn_kernel(shift_ref, lam_ref, gout_ref, q_ref, k_ref, v_ref, o_ref, qm_ref, m_ref, acc_ref, *, tq,
                 n_heads, lambda_init, head_dim):
    qi = pl.program_id(1)
    hw = 2 * head_dim
    vd = v_ref.shape[-1] // n_heads
    lv = lam_ref[...]
    lam = (jnp.exp(jnp.sum(lv[0:1] * lv[1:2], axis=-1, keepdims=True))
           - jnp.exp(jnp.sum(lv[2:3] * lv[3:4], axis=-1, keepdims=True))
           + lambda_init)

    first = lax.broadcasted_iota(jnp.int32, (1, hw), 1) < head_dim
    for h in range(n_heads):
        q = q_ref[0, :, h * hw:(h + 1) * hw]
        zero = jnp.zeros_like(q)
        qm_ref[2 * h] = jnp.where(first, q, zero)
        qm_ref[2 * h + 1] = jnp.where(first, zero, q)
    acc_ref[...] = jnp.zeros(acc_ref.shape, F32)

    ones = jnp.ones((tq, vd), BF16)
    causal = (lax.broadcasted_iota(jnp.int32, (tq, tq), 1)
              <= lax.broadcasted_iota(jnp.int32, (tq, tq), 0))

    def step(j, masked, running_max):
        start = pl.multiple_of(j * tq, tq)
        for h in range(n_heads):
            kc = k_ref[0, pl.ds(start, tq), h * hw:(h + 1) * hw]
            vext = jnp.concatenate([v_ref[0, pl.ds(start, tq), h * vd:(h + 1) * vd], ones], axis=1)
            for mp in range(2):
                c = 2 * h + mp
                s = lax.dot_general(qm_ref[c], kc, (((1,), (1,)), ((), ())), preferred_element_type=F32)
                if masked:
                    s = jnp.where(causal, s, NEG_INF)
                if running_max:
                    m_old = m_ref[c]
                    m_new = jnp.maximum(m_old, jnp.max(s, axis=-1, keepdims=True))
                    p = jnp.exp(s - m_new).astype(BF16)
                    acc_ref[c] = (jnp.exp(m_old - m_new) * acc_ref[c]
                                  + jnp.dot(p, vext, preferred_element_type=F32))
                    m_ref[c] = m_new
                else:
                    acc_ref[c] += jnp.dot(jnp.exp(s).astype(BF16), vext, preferred_element_type=F32)

    def sweep(running_max):
        def body(j, carry):
            step(j, False, running_max)
            return carry

        lax.fori_loop(0, qi, body, 0)
        step(qi, True, running_max)

    @pl.when(shift_ref[0] == 0)
    def _():
        sweep(False)

    @pl.when(shift_ref[0] != 0)
    def _():
        m_ref[...] = jnp.full(m_ref.shape, NEG_INF, F32)
        sweep(True)

    for h in range(n_heads):
        a1 = acc_ref[2 * h]
        a2 = acc_ref[2 * h + 1]
        att = a1[:, :vd] / a1[:, vd:] - lam * (a2[:, :vd] / a2[:, vd:])
        ms = jnp.mean(att * att, axis=-1, keepdims=True)
        o_ref[0, :, h * vd:(h + 1) * vd] = (
            att * lax.rsqrt(ms + RMS_EPS) * gout_ref[...] * (1.0 - lambda_init)).astype(BF16)


def _attention(need_shift, lam_vecs, gout, qk3, v3, *, n_heads, head_dim, lambda_init):
    b, s, _ = qk3.shape
    att_cols = v3.shape[-1]
    vd = att_cols // n_heads
    tq = QBLOCK
    kern = functools.partial(_attn_kernel, tq=tq, n_heads=n_heads, lambda_init=lambda_init,
                             head_dim=head_dim)
    grid_spec = pltpu.PrefetchScalarGridSpec(
        num_scalar_prefetch=1,
        grid=(b, s // tq),
        in_specs=[
            pl.BlockSpec(lam_vecs.shape, lambda bi, qi, f: (0, 0)),
            pl.BlockSpec((1, vd), lambda bi, qi, f: (0, 0)),
            pl.BlockSpec((1, tq, att_cols), lambda bi, qi, f: (bi, qi, 0)),
            pl.BlockSpec((1, s, att_cols), lambda bi, qi, f: (bi, 0, 1)),
            pl.BlockSpec((1, s, att_cols), lambda bi, qi, f: (bi, 0, 0)),
        ],
        out_specs=pl.BlockSpec((1, tq, att_cols), lambda bi, qi, f: (bi, qi, 0)),
        scratch_shapes=[
            pltpu.VMEM((2 * n_heads, tq, 2 * head_dim), BF16),
            pltpu.VMEM((2 * n_heads, tq, 1), F32),
            pltpu.VMEM((2 * n_heads, tq, 2 * vd), F32),
        ],
    )
    return pl.pallas_call(
        kern,
        grid_spec=grid_spec,
        out_shape=jax.ShapeDtypeStruct(v3.shape, BF16),
        compiler_params=_params(2),
        name="diff_attn",
    )(need_shift, lam_vecs, gout, qk3, qk3, v3)


def _mixout_kernel(att_ref, su_ref, sv_ref, x_ref, sw_ref, sbt_ref, sgn_ref, wo_ref, gffn_ref,
                   rwh_ref, rwl_ref, rb_ref, lstrict_ref, h_ref, hn_ref, route_ref, counts_ref, cnt_ref,
                   *, n_sgu_groups, att_cols):
    tm = x_ref.shape[0]
    gw = CHUNK
    r_id = lax.broadcasted_iota(jnp.int32, (gw, gw), 0)
    c_id = lax.broadcasted_iota(jnp.int32, (gw, gw), 1)
    tril = c_id <= r_id

    @pl.when(pl.program_id(0) == 0)
    def _():
        cnt_ref[...] = jnp.zeros_like(cnt_ref)

    acc = jnp.dot(att_ref[...], wo_ref[0:att_cols, :], preferred_element_type=F32)

    sg_cols = []
    for g in range(n_sgu_groups):
        w = jnp.where(tril, sw_ref[g], 0.0).astype(BF16)
        bias = sbt_ref[:, g:g + 1]
        gn = sgn_ref[g:g + 1, :]
        rows = []
        for c in range(tm // gw):
            vblk = sv_ref[c * gw:(c + 1) * gw, g * gw:(g + 1) * gw]
            s = jnp.dot(w, vblk, preferred_element_type=F32) + bias
            sg = su_ref[c * gw:(c + 1) * gw, g * gw:(g + 1) * gw].astype(F32) * s
            ms = jnp.mean(sg * sg, axis=-1, keepdims=True)
            rows.append((sg * lax.rsqrt(ms + RMS_EPS) * gn).astype(BF16))
        sg_cols.append(jnp.concatenate(rows, axis=0))
    sgn = jnp.concatenate(sg_cols, axis=1)
    acc = acc + jnp.dot(sgn, wo_ref[att_cols:, :], preferred_element_type=F32)

    h = x_ref[...] + acc
    h_ref[...] = h
    ms = jnp.mean(h * h, axis=-1, keepdims=True)
    hn = h * lax.rsqrt(ms + RMS_EPS) * gffn_ref[...]
    hn_ref[...] = hn

    hn_bf, hn_lo = _split_bf16(hn)
    logits = (jnp.dot(hn_bf, rwh_ref[...], preferred_element_type=F32)
              + jnp.dot(hn_lo, rwh_ref[...], preferred_element_type=F32)
              + jnp.dot(hn_bf, rwl_ref[...], preferred_element_type=F32)
              + rb_ref[...])

    lane = lax.broadcasted_iota(jnp.int32, logits.shape, 1).astype(F32)
    big = float(LANES)
    is_grp = lane < N_GROUPS
    gl = jnp.where(is_grp, logits, NEG_INF)
    gmax = jnp.max(gl, axis=-1, keepdims=True)
    gidx = jnp.min(jnp.where(gl == gmax, lane, big), axis=-1, keepdims=True)
    psum = jnp.sum(jnp.where(is_grp, jnp.exp(gl - gmax), 0.0), axis=-1, keepdims=True)
    p_grp = 1.0 / psum
    e_lo = N_GROUPS + gidx * EXPERTS_PER_GROUP
    in_grp = (lane >= e_lo) & (lane < e_lo + EXPERTS_PER_GROUP)
    el = jnp.where(in_grp, logits, NEG_INF)
    t1 = jnp.max(el, axis=-1, keepdims=True)
    i1 = jnp.min(jnp.where(el == t1, lane, big), axis=-1, keepdims=True)
    el2 = jnp.where(lane == i1, NEG_INF, el)
    t2 = jnp.max(el2, axis=-1, keepdims=True)
    i2 = jnp.min(jnp.where(el2 == t2, lane, big), axis=-1, keepdims=True)
    e21 = jnp.exp(t2 - t1)
    den = 1.0 + e21
    g1 = p_grp / den
    g2 = p_grp * (e21 / den)
    e1 = i1 - N_GROUPS
    e2 = i2 - N_GROUPS

    hit1 = lane == e1
    hit2 = lane == e2
    onehot = jnp.where(hit1 | hit2, 1.0, 0.0)
    before = jnp.dot(lstrict_ref[...], onehot.astype(BF16), preferred_element_type=F32) + cnt_ref[...]
    r1 = jnp.sum(jnp.where(hit1, before, 0.0), axis=-1, keepdims=True)
    r2 = jnp.sum(jnp.where(hit2, before, 0.0), axis=-1, keepdims=True)
    cnt_ref[...] = cnt_ref[...] + jnp.sum(onehot, axis=0, keepdims=True)
    counts_ref[...] = cnt_ref[...]

    route = jnp.zeros_like(logits)
    for k, val in enumerate((e1, e2, g1, g2, r1, r2)):
        route = jnp.where(lane == k, val, route)
    route_ref[...] = route


def _mixout(att, su, sv, x2d, sgu_w, sgu_bt, sgu_gn, w_out, g_ffn, rw_hi, rw_lo, rb, lstrict, *,
            att_cols):
    t, d = x2d.shape
    tm = ROW_TILE
    row = lambda i: (i, 0)
    fixed2 = lambda i: (0, 0)
    n_groups = sgu_w.shape[0]
    kern = functools.partial(_mixout_kernel, n_sgu_groups=n_groups, att_cols=att_cols)
    return pl.pallas_call(
        kern,
        grid=(t // tm,),
        in_specs=[
            pl.BlockSpec((tm, att.shape[1]), row),
            pl.BlockSpec((tm, su.shape[1]), row),
            pl.BlockSpec((tm, sv.shape[1]), row),
            pl.BlockSpec((tm, d), row),
            pl.BlockSpec(sgu_w.shape, lambda i: (0, 0, 0)),
            pl.BlockSpec(sgu_bt.shape, fixed2),
            pl.BlockSpec(sgu_gn.shape, fixed2),
            pl.BlockSpec(w_out.shape, fixed2),
            pl.BlockSpec((1, d), fixed2),
            pl.BlockSpec(rw_hi.shape, fixed2),
            pl.BlockSpec(rw_lo.shape, fixed2),
            pl.BlockSpec((1, LANES), fixed2),
            pl.BlockSpec((tm, tm), fixed2),
        ],
        out_specs=[
            pl.BlockSpec((tm, d), row),
            pl.BlockSpec((tm, d), row),
            pl.BlockSpec((tm, LANES), row),
            pl.BlockSpec((1, LANES), fixed2),
        ],
        out_shape=[
            jax.ShapeDtypeStruct((t, d), F32),
            jax.ShapeDtypeStruct((t, d), F32),
            jax.ShapeDtypeStruct((t, LANES), F32),
            jax.ShapeDtypeStruct((1, LANES), F32),
        ],
        scratch_shapes=[pltpu.VMEM((1, LANES), F32)],
        compiler_params=_params(1),
        name="mixout",
    )(att, su, sv, x2d, sgu_w, sgu_bt, sgu_gn, w_out, g_ffn, rw_hi, rw_lo, rb, lstrict)


def _dispatch_kernel(dest_ref, hn_ref, xs_ref, sem):
    rows = hn_ref.shape[0]

    def issue(r, c):
        for k in range(TOP_K):
            pltpu.make_async_copy(hn_ref.at[pl.ds(r, 1), :],
                                  xs_ref.at[pl.ds(dest_ref[TOP_K * r + k], 1), :], sem).start()
        return c

    lax.fori_loop(0, rows, issue, 0, unroll=4)
    for k in range(TOP_K):
        pltpu.make_async_copy(hn_ref, xs_ref.at[pl.ds(0, rows), :], sem).wait()


def _dispatch(hn, dest):
    t, d = hn.shape
    rows = ROW_TILE
    return pl.pallas_call(
        _dispatch_kernel,
        grid=(t // rows,),
        in_specs=[
            pl.BlockSpec((rows * TOP_K,), lambda i: (i,), memory_space=pltpu.SMEM),
            pl.BlockSpec((rows, d), lambda i: (i, 0)),
        ],
        out_specs=pl.BlockSpec(memory_space=pl.ANY),
        out_shape=jax.ShapeDtypeStruct((t * TOP_K, d), hn.dtype),
        scratch_shapes=[pltpu.SemaphoreType.DMA],
        compiler_params=_params(1),
        name="dispatch",
    )(dest, hn)


def _experts_kernel(vt_ref, ve_ref, vf_ref, vn_ref, lo_ref, hi_ref, xs_ref, wg_ref, wu_ref, wd_ref, ys_ref,
                    wg_bf, wu_bf, wd_bf):
    v = pl.program_id(0)
    flag = vf_ref[v]
    tm = xs_ref.shape[0]

    @pl.when(vn_ref[v] == 1)
    def _():
        wg_bf[...] = wg_ref[0].astype(BF16)
        wu_bf[...] = wu_ref[0].astype(BF16)
        wd_bf[...] = wd_ref[0].astype(BF16)

    @pl.when(flag > 0)
    def _():
        e = ve_ref[v]
        base = vt_ref[v] * tm
        rid = lax.broadcasted_iota(jnp.int32, (tm, 1), 0) + base
        mine = (rid >= lo_ref[e]) & (rid < hi_ref[e])
        x = xs_ref[...].astype(BF16)
        hg = jnp.dot(x, wg_bf[...], preferred_element_type=F32)
        hu = jnp.dot(x, wu_bf[...], preferred_element_type=F32)
        a = (hg * (1.0 / (1.0 + jnp.exp(-hg))) * hu).astype(BF16)
        y = jnp.dot(a, wd_bf[...], preferred_element_type=F32)

        @pl.when(flag == 1)
        def _():
            ys_ref[...] = jnp.where(mine, y, 0.0)

        @pl.when(flag == 2)
        def _():
            ys_ref[...] = jnp.where(mine, y, ys_ref[...])


def _experts(visit_tile, visit_expert, visit_flag, visit_new, row_lo, row_hi, xs, w_gate, w_up, w_down):
    p, d = xs.shape
    tm = EXPERT_TILE
    f = w_gate.shape[-1]
    rows_map = lambda v, vt, ve, vf, vn, lo, hi: (vt[v], 0)
    w_map = lambda v, vt, ve, vf, vn, lo, hi: (ve[v], 0, 0)
    grid_spec = pltpu.PrefetchScalarGridSpec(
        num_scalar_prefetch=6,
        grid=(visit_tile.shape[0],),
        in_specs=[
            pl.BlockSpec((tm, d), rows_map),
            pl.BlockSpec((1, d, f), w_map),
            pl.BlockSpec((1, d, f), w_map),
            pl.BlockSpec((1, f, d), w_map),
        ],
        out_specs=pl.BlockSpec((tm, d), rows_map),
        scratch_shapes=[pltpu.VMEM((d, f), BF16), pltpu.VMEM((d, f), BF16), pltpu.VMEM((f, d), BF16)],
    )
    return pl.pallas_call(
        _experts_kernel,
        grid_spec=grid_spec,
        out_shape=jax.ShapeDtypeStruct(xs.shape, F32),
        compiler_params=_params(1),
        name="experts",
    )(visit_tile, visit_expert, visit_flag, visit_new, row_lo, row_hi, xs, w_gate, w_up, w_down)


def _combine_kernel(dest_ref, dest_next_ref, h_ref, route_ref, ys_ref, o_ref, ybuf, sem):
    i = pl.program_id(0)
    rows = h_ref.shape[0]
    slot = i % 2

    def issue(idx_ref, into):
        def body(r, c):
            for k in range(TOP_K):
                pltpu.make_async_copy(ys_ref.at[pl.ds(idx_ref[TOP_K * r + k], 1), :],
                                      ybuf.at[into, k, pl.ds(r, 1), :], sem.at[into]).start()
            return c

        lax.fori_loop(0, rows, body, 0, unroll=4)

    @pl.when(i == 0)
    def _():
        issue(dest_ref, 0)

    @pl.when(i + 1 < pl.num_programs(0))
    def _():
        issue(dest_next_ref, 1 - slot)

    for k in range(TOP_K):
        pltpu.make_async_copy(ys_ref.at[pl.ds(0, rows), :], ybuf.at[slot, k], sem.at[slot]).wait()

    route = route_ref[...]
    moe = None
    for k in range(TOP_K):
        gated = route[:, ROUTE_GATE + k:ROUTE_GATE + k + 1] * ybuf[slot, k]
        moe = gated if moe is None else moe + gated
    o_ref[...] = h_ref[...] + moe


def _combine(dest, h, route, ys):
    t, d = h.shape
    rows = COMBINE_TILE
    n = t // rows
    return pl.pallas_call(
        _combine_kernel,
        grid=(n,),
        in_specs=[
            pl.BlockSpec((rows * TOP_K,), lambda i: (i,), memory_space=pltpu.SMEM),
            pl.BlockSpec((rows * TOP_K,), lambda i: (jnp.minimum(i + 1, n - 1),), memory_space=pltpu.SMEM),
            pl.BlockSpec((rows, d), lambda i: (i, 0)),
            pl.BlockSpec((rows, LANES), lambda i: (i, 0)),
            pl.BlockSpec(memory_space=pl.ANY),
        ],
        out_specs=pl.BlockSpec((rows, d), lambda i: (i, 0)),
        out_shape=jax.ShapeDtypeStruct((t, d), F32),
        scratch_shapes=[pltpu.VMEM((2, TOP_K, rows, d), F32), pltpu.SemaphoreType.DMA((2,))],
        compiler_params=_params(1),
        name="combine",
    )(dest, dest, h, route, ys)


def _plan(route, counts_row):
    t = route.shape[0]
    counts = counts_row[0, :N_EXPERTS].astype(I32)
    ends = jnp.cumsum(counts)
    offs = ends - counts
    experts = route[:, ROUTE_EXPERT:ROUTE_EXPERT + TOP_K].astype(I32)
    rank = route[:, ROUTE_RANK:ROUTE_RANK + TOP_K].astype(I32)
    dest = (jnp.take(offs, experts) + rank).reshape(-1)

    tm = EXPERT_TILE
    n_visits = (t * TOP_K) // tm + N_EXPERTS - 1
    first_tile = offs // tm
    last_tile = (ends - 1) // tm
    nvis = jnp.where(counts > 0, last_tile - first_tile + 1, 0)
    vend = jnp.cumsum(nvis)
    vstart = vend - nvis
    v = jnp.arange(n_visits, dtype=I32)
    ev = jnp.minimum(jnp.sum((vend[None, :] <= v[:, None]).astype(I32), axis=1), N_EXPERTS - 1)
    tv = jnp.take(first_tile, ev) + (v - jnp.take(vstart, ev))
    valid = v < vend[-1]
    last_v = vend[-1] - 1
    ev = jnp.where(valid, ev, jnp.take(ev, last_v))
    tv = jnp.where(valid, tv, jnp.take(tv, last_v))
    prev_t = jnp.concatenate([jnp.full((1,), -1, I32), tv[:-1]])
    prev_e = jnp.concatenate([jnp.full((1,), -1, I32), ev[:-1]])
    flag = jnp.where(valid, jnp.where(tv != prev_t, 1, 2), 0).astype(I32)
    new_expert = (ev != prev_e).astype(I32)
    return dest.astype(I32), tv.astype(I32), ev.astype(I32), flag, new_expert, offs.astype(I32), ends.astype(I32)


def _rope_inputs(positions, head_dim, rope_dim):
    t = positions.size
    rope_half = rope_dim // 2
    inv_freq = ROPE_THETA ** (-jnp.arange(0, rope_dim, 2, dtype=F32) / rope_dim)
    pos = jnp.repeat(positions.reshape(-1), rope_half).reshape(-1, LANES).astype(F32)
    ang = pos * jnp.tile(inv_freq, LANES // rope_half)[None, :]
    cos = jnp.cos(ang).reshape(t, rope_half)
    sin = jnp.sin(ang).reshape(t, rope_half)
    cs = jnp.concatenate([cos, sin, jnp.ones((t, 1), F32), jnp.zeros((t, LANES - rope_dim - 1), F32)], axis=1)

    dim = jnp.arange(LANES, dtype=I32)[None, :] % head_dim
    k = jnp.arange(LANES, dtype=I32)[:, None]
    c_mat = jnp.where(dim < rope_dim, k == dim % rope_half, k == rope_dim)
    sa_mat = (dim < rope_half) & (k == rope_half + dim)
    sb_mat = (dim >= rope_half) & (dim < rope_dim) & (k == dim)
    expand = jnp.concatenate([c_mat.astype(F32), -sa_mat.astype(F32), sb_mat.astype(F32)], axis=1)
    return cs, expand.astype(BF16)


def kernel(x, positions, norm_mix_g, w_in, q_norm_g, k_norm_g, lambda_q1, lambda_k1, lambda_q2,
           lambda_k2, diff_out_norm_g, sgu_ln_g, sgu_ln_b, sgu_w, sgu_b, sgu_out_norm_g, w_out,
           norm_ffn_g, router_group_w, router_group_b, router_expert_w, router_expert_b,
           expert_w_gate, expert_w_up, expert_w_down):
    b, s, d = x.shape
    t = b * s
    depth = w_in.shape[0]
    head_dim = q_norm_g.shape[-1]
    vd = diff_out_norm_g.shape[-1]
    n_sgu_groups = sgu_w.shape[1]
    sgu_cols = n_sgu_groups * sgu_w.shape[-1]
    in_cols = w_in.shape[-1]
    att_cols = (in_cols - 2 * sgu_cols) // 3
    n_heads = att_cols // vd
    qk_cols = 2 * att_cols
    rope_dim = head_dim // ROPE_FRACTION
    rope_half = rope_dim // 2
    scale = head_dim ** -0.5

    cs, expand = _rope_inputs(positions, head_dim, rope_dim)
    blk = jnp.arange(MXU_WIDTH, dtype=I32) // head_dim
    ones_blk = (blk[:, None] == blk[None, :]).astype(BF16)
    tri = jnp.arange(ROW_TILE, dtype=I32)
    lstrict = (tri[None, :] < tri[:, None]).astype(BF16)

    h = x.reshape(t, d)
    for l in range(depth):
        lambda_init = 0.8 - 0.6 * math.exp(-0.3 * l)
        gqk = jnp.concatenate([jnp.tile(q_norm_g[l] * scale, att_cols // head_dim),
                               jnp.tile(k_norm_g[l], att_cols // head_dim)]).reshape(1, qk_cols)
        qk, v, su, sv = _inproj(
            h, norm_mix_g[l].reshape(1, d), w_in[l].astype(BF16), gqk, cs, expand, ones_blk,
            sgu_ln_g[l].reshape(1, sgu_cols), sgu_ln_b[l].reshape(1, sgu_cols),
            qk_cols=qk_cols, v_cols=att_cols, sgu_cols=sgu_cols, head_dim=head_dim,
            rope_half=rope_half)

        lam_vecs = jnp.stack([lambda_q1[l], lambda_k1[l], lambda_q2[l], lambda_k2[l]]).astype(F32)
        score_bound = (1.02 * head_dim * scale * jnp.max(jnp.abs(q_norm_g[l]))
                       * jnp.max(jnp.abs(k_norm_g[l])))
        need_shift = (score_bound > MAX_UNSHIFTED_SCORE).astype(I32).reshape(1)
        att = _attention(need_shift, lam_vecs, diff_out_norm_g[l].reshape(1, vd), qk.reshape(b, s, qk_cols),
                         v.reshape(b, s, att_cols), n_heads=n_heads, head_dim=head_dim,
                         lambda_init=lambda_init).reshape(t, att_cols)

        rw = jnp.concatenate([router_group_w[l], router_expert_w[l]], axis=1)
        rw = jnp.pad(rw, ((0, 0), (0, LANES - rw.shape[1])))
        rw_hi, rw_lo = _split_bf16(rw)
        rb = jnp.pad(jnp.concatenate([router_group_b[l], router_expert_b[l]]),
                     (0, LANES - N_GROUPS - N_EXPERTS)).reshape(1, LANES)
        h, hn, route, counts_row = _mixout(
            att, su, sv, h, sgu_w[l], sgu_b[l].T, sgu_out_norm_g[l], w_out[l].astype(BF16),
            norm_ffn_g[l].reshape(1, d), rw_hi, rw_lo, rb, lstrict, att_cols=att_cols)

        dest, visit_tile, visit_expert, visit_flag, visit_new, row_lo, row_hi = _plan(route, counts_row)
        xs = _dispatch(hn, dest)
        ys = _experts(visit_tile, visit_expert, visit_flag, visit_new, row_lo, row_hi, xs,
                      expert_w_gate[l], expert_w_up[l], expert_w_down[l])
        h = _combine(dest, h, route, ys)
    return h.reshape(b, s, d)
```

```python
import functools
import math

import jax
import jax.numpy as jnp
from jax import lax
from jax.experimental import pallas as pl
from jax.experimental.pallas import tpu as pltpu

F32 = jnp.float32
BF16 = jnp.bfloat16
I32 = jnp.int32

RMS_EPS = 1e-6
NEG_INF = -1e30
ROPE_THETA = 500000.0
MAX_UNSHIFTED_SCORE = 40.0

LANES = 128
SUBLANES = 8
MXU_WIDTH = 256
VMEM_LIMIT_BYTES = 56 * 1024 * 1024

N_GROUPS = 4
EXPERTS_PER_GROUP = 8
N_EXPERTS = N_GROUPS * EXPERTS_PER_GROUP
TOP_K = 2
ROPE_FRACTION = 4
CHUNK = 128
QBLOCK = 256
ROW_TILE = 512
EXPERT_TILE = 512
COMBINE_TILE = 256

ROUTE_EXPERT = 0
ROUTE_GATE = 2
ROUTE_RANK = 4


def _params(n_grid_dims):
    return pltpu.CompilerParams(
        dimension_semantics=("arbitrary",) * n_grid_dims,
        vmem_limit_bytes=VMEM_LIMIT_BYTES,
    )


def _gelu(x):
    return 0.5 * x * (1.0 + lax.erf(x * (1.0 / math.sqrt(2.0))))


def _split_bf16(x):
    hi = x.astype(BF16)
    return hi, (x - hi.astype(F32)).astype(BF16)


def _inproj_kernel(x_ref, g_ref, w_ref, gqk_ref, cs_ref, expand_ref, ones_ref, lng_ref, lnb_ref,
                   qk_ref, v_ref, su_ref, sv_ref, *, qk_cols, v_cols, sgu_cols, head_dim, rope_half):
    x = x_ref[...]
    ms = jnp.mean(x * x, axis=-1, keepdims=True)
    xn = (x * lax.rsqrt(ms + RMS_EPS) * g_ref[...]).astype(BF16)
    cw = MXU_WIDTH

    def proj(col0):
        return jnp.dot(xn, w_ref[:, col0:col0 + cw], preferred_element_type=F32)

    cs_hi, cs_lo = _split_bf16(cs_ref[...])
    tab = (jnp.dot(cs_hi, expand_ref[...], preferred_element_type=F32)
           + jnp.dot(cs_lo, expand_ref[...], preferred_element_type=F32))
    cos = tab[:, 0:LANES]
    sa = tab[:, LANES:2 * LANES]
    sb = tab[:, 2 * LANES:3 * LANES]

    for c in range(qk_cols // cw):
        p = proj(c * cw)
        ss = jnp.dot((p * p).astype(BF16), ones_ref[...], preferred_element_type=F32)
        pn = p * lax.rsqrt(ss * (1.0 / head_dim) + RMS_EPS) * gqk_ref[:, c * cw:(c + 1) * cw]
        for hh in range(cw // LANES):
            blk = pn[:, hh * LANES:(hh + 1) * LANES]
            rot = (blk * cos
                   + pltpu.roll(blk, LANES - rope_half, 1) * sa
                   + pltpu.roll(blk, rope_half, 1) * sb)
            col = c * cw + hh * LANES
            qk_ref[:, col:col + LANES] = rot.astype(BF16)

    for c in range(v_cols // cw):
        v_ref[:, c * cw:(c + 1) * cw] = proj(qk_cols + c * cw).astype(BF16)

    for c in range(sgu_cols // cw):
        su_ref[:, c * cw:(c + 1) * cw] = _gelu(proj(qk_cols + v_cols + c * cw)).astype(BF16)

    sv = jnp.concatenate(
        [_gelu(proj(qk_cols + v_cols + sgu_cols + c * cw)) for c in range(sgu_cols // cw)], axis=1)
    mu = jnp.mean(sv, axis=-1, keepdims=True)
    svc = sv - mu
    var = jnp.mean(svc * svc, axis=-1, keepdims=True)
    sv_ref[...] = (svc * lax.rsqrt(var + RMS_EPS) * lng_ref[...] + lnb_ref[...]).astype(BF16)


def _inproj(x2d, norm_g, w_in, gqk, cs, expand, ones_blk, ln_g, ln_b, *, qk_cols, v_cols,
            sgu_cols, head_dim, rope_half):
    t, d = x2d.shape
    tm = ROW_TILE
    row = lambda i: (i, 0)
    fixed = lambda i: (0, 0)
    kern = functools.partial(_inproj_kernel, qk_cols=qk_cols, v_cols=v_cols, sgu_cols=sgu_cols,
                             head_dim=head_dim, rope_half=rope_half)
    return pl.pallas_call(
        kern,
        grid=(t // tm,),
        in_specs=[
            pl.BlockSpec((tm, d), row),
            pl.BlockSpec((1, d), fixed),
            pl.BlockSpec(w_in.shape, fixed),
            pl.BlockSpec((1, qk_cols), fixed),
            pl.BlockSpec((tm, LANES), row),
            pl.BlockSpec(expand.shape, fixed),
            pl.BlockSpec((MXU_WIDTH, MXU_WIDTH), fixed),
            pl.BlockSpec((1, sgu_cols), fixed),
            pl.BlockSpec((1, sgu_cols), fixed),
        ],
        out_specs=[
            pl.BlockSpec((tm, qk_cols), row),
            pl.BlockSpec((tm, v_cols), row),
            pl.BlockSpec((tm, sgu_cols), row),
            pl.BlockSpec((tm, sgu_cols), row),
        ],
        out_shape=[
            jax.ShapeDtypeStruct((t, qk_cols), BF16),
            jax.ShapeDtypeStruct((t, v_cols), BF16),
            jax.ShapeDtypeStruct((t, sgu_cols), BF16),
            jax.ShapeDtypeStruct((t, sgu_cols), BF16),
        ],
        compiler_params=_params(1),
        name="inproj",
    )(x2d, norm_g, w_in, gqk, cs, expand, ones_blk, ln_g, ln_b)


def _attn_kernel(shift_ref, lam_ref, gout_ref, q_ref, k_ref, v_ref, o_ref, qm_ref, m_ref, acc_ref, *, tq,
                 n_heads, lambda_init, head_dim):
    qi = pl.program_id(1)
    hw = 2 * head_dim
    vd = v_ref.shape[-1] // n_heads
    lv = lam_ref[...]
    lam = (jnp.exp(jnp.sum(lv[0:1] * lv[1:2], axis=-1, keepdims=True))
           - jnp.exp(jnp.sum(lv[2:3] * lv[3:4], axis=-1, keepdims=True))
           + lambda_init)

    first = lax.broadcasted_iota(jnp.int32, (1, hw), 1) < head_dim
    for h in range(n_heads):
        q = q_ref[0, :, h * hw:(h + 1) * hw]
        zero = jnp.zeros_like(q)
        qm_ref[2 * h] = jnp.where(first, q, zero)
        qm_ref[2 * h + 1] = jnp.where(first, zero, q)
    acc_ref[...] = jnp.zeros(acc_ref.shape, F32)

    ones = jnp.ones((tq, vd), BF16)
    causal = (lax.broadcasted_iota(jnp.int32, (tq, tq), 1)
              <= lax.broadcasted_iota(jnp.int32, (tq, tq), 0))

    def kv_tile(j, h):
        start = pl.multiple_of(j * tq, tq)
        kc = k_ref[0, pl.ds(start, tq), h * hw:(h + 1) * hw]
        vext = jnp.concatenate([v_ref[0, pl.ds(start, tq), h * vd:(h + 1) * vd], ones], axis=1)
        return kc, vext

    def scores(c, kc, masked):
        s = lax.dot_general(qm_ref[c], kc, (((1,), (1,)), ((), ())), preferred_element_type=F32)
        return jnp.where(causal, s, NEG_INF) if masked else s

    def plain_step(tiles):
        for h in range(n_heads):
            kvs = [kv_tile(j, h) for j, _ in tiles]
            for mp in range(2):
                c = 2 * h + mp
                pv = None
                for (kc, vext), (_, masked) in zip(kvs, tiles):
                    part = jnp.dot(jnp.exp2(scores(c, kc, masked)).astype(BF16), vext,
                                   preferred_element_type=F32)
                    pv = part if pv is None else pv + part
                acc_ref[c] += pv

    def running_max_step(j, masked):
        for h in range(n_heads):
            kc, vext = kv_tile(j, h)
            for mp in range(2):
                c = 2 * h + mp
                s = scores(c, kc, masked)
                m_old = m_ref[c]
                m_new = jnp.maximum(m_old, jnp.max(s, axis=-1, keepdims=True))
                p = jnp.exp2(s - m_new).astype(BF16)
                acc_ref[c] = (jnp.exp2(m_old - m_new) * acc_ref[c]
                              + jnp.dot(p, vext, preferred_element_type=F32))
                m_ref[c] = m_new

    @pl.when(shift_ref[0] == 0)
    def _():
        def pair(jj, carry):
            plain_step([(2 * jj, False), (2 * jj + 1, False)])
            return carry

        lax.fori_loop(0, qi // 2, pair, 0)

        @pl.when(qi % 2 == 1)
        def _():
            plain_step([(qi - 1, False), (qi, True)])

        @pl.when(qi % 2 == 0)
        def _():
            plain_step([(qi, True)])

    @pl.when(shift_ref[0] != 0)
    def _():
        m_ref[...] = jnp.full(m_ref.shape, NEG_INF, F32)

        def single(j, carry):
            running_max_step(j, False)
            return carry

        lax.fori_loop(0, qi, single, 0)
        running_max_step(qi, True)

    for h in range(n_heads):
        a1 = acc_ref[2 * h]
        a2 = acc_ref[2 * h + 1]
        att = a1[:, :vd] / a1[:, vd:] - lam * (a2[:, :vd] / a2[:, vd:])
        ms = jnp.mean(att * att, axis=-1, keepdims=True)
        o_ref[0, :, h * vd:(h + 1) * vd] = (
            att * lax.rsqrt(ms + RMS_EPS) * gout_ref[...] * (1.0 - lambda_init)).astype(BF16)


def _attention(need_shift, lam_vecs, gout, qk3, v3, *, n_heads, head_dim, lambda_init):
    b, s, _ = qk3.shape
    att_cols = v3.shape[-1]
    vd = att_cols // n_heads
    tq = QBLOCK
    kern = functools.partial(_attn_kernel, tq=tq, n_heads=n_heads, lambda_init=lambda_init,
                             head_dim=head_dim)
    grid_spec = pltpu.PrefetchScalarGridSpec(
        num_scalar_prefetch=1,
        grid=(b, s // tq),
        in_specs=[
            pl.BlockSpec(lam_vecs.shape, lambda bi, qi, f: (0, 0)),
            pl.BlockSpec((1, vd), lambda bi, qi, f: (0, 0)),
            pl.BlockSpec((1, tq, att_cols), lambda bi, qi, f: (bi, qi, 0)),
            pl.BlockSpec((1, s, att_cols), lambda bi, qi, f: (bi, 0, 1)),
            pl.BlockSpec((1, s, att_cols), lambda bi, qi, f: (bi, 0, 0)),
        ],
        out_specs=pl.BlockSpec((1, tq, att_cols), lambda bi, qi, f: (bi, qi, 0)),
        scratch_shapes=[
            pltpu.VMEM((2 * n_heads, tq, 2 * head_dim), BF16),
            pltpu.VMEM((2 * n_heads, tq, 1), F32),
            pltpu.VMEM((2 * n_heads, tq, 2 * vd), F32),
        ],
    )
    return pl.pallas_call(
        kern,
        grid_spec=grid_spec,
        out_shape=jax.ShapeDtypeStruct(v3.shape, BF16),
        compiler_params=_params(2),
        name="diff_attn",
    )(need_shift, lam_vecs, gout, qk3, qk3, v3)


def _mixout_kernel(att_ref, su_ref, sv_ref, x_ref, sw_ref, sbt_ref, sgn_ref, wo_ref, gffn_ref,
                   rwh_ref, rwl_ref, rb_ref, lstrict_ref, h_ref, hn_ref, route_ref, counts_ref, cnt_ref,
                   *, n_sgu_groups, att_cols):
    tm = x_ref.shape[0]
    gw = CHUNK
    r_id = lax.broadcasted_iota(jnp.int32, (gw, gw), 0)
    c_id = lax.broadcasted_iota(jnp.int32, (gw, gw), 1)
    tril = c_id <= r_id

    @pl.when(pl.program_id(0) == 0)
    def _():
        cnt_ref[...] = jnp.zeros_like(cnt_ref)

    acc = jnp.dot(att_ref[...], wo_ref[0:att_cols, :], preferred_element_type=F32)

    sg_cols = []
    for g in range(n_sgu_groups):
        w = jnp.where(tril, sw_ref[g], 0.0).astype(BF16)
        bias = sbt_ref[:, g:g + 1]
        gn = sgn_ref[g:g + 1, :]
        rows = []
        for c in range(tm // gw):
            vblk = sv_ref[c * gw:(c + 1) * gw, g * gw:(g + 1) * gw]
            s = jnp.dot(w, vblk, preferred_element_type=F32) + bias
            sg = su_ref[c * gw:(c + 1) * gw, g * gw:(g + 1) * gw].astype(F32) * s
            ms = jnp.mean(sg * sg, axis=-1, keepdims=True)
            rows.append((sg * lax.rsqrt(ms + RMS_EPS) * gn).astype(BF16))
        sg_cols.append(jnp.concatenate(rows, axis=0))
    sgn = jnp.concatenate(sg_cols, axis=1)
    acc = acc + jnp.dot(sgn, wo_ref[att_cols:, :], preferred_element_type=F32)

    h = x_ref[...] + acc
    h_ref[...] = h
    ms = jnp.mean(h * h, axis=-1, keepdims=True)
    hn = h * lax.rsqrt(ms + RMS_EPS) * gffn_ref[...]
    hn_ref[...] = hn

    hn_bf, hn_lo = _split_bf16(hn)
    logits = (jnp.dot(hn_bf, rwh_ref[...], preferred_element_type=F32)
              + jnp.dot(hn_lo, rwh_ref[...], preferred_element_type=F32)
              + jnp.dot(hn_bf, rwl_ref[...], preferred_element_type=F32)
              + rb_ref[...])

    lane = lax.broadcasted_iota(jnp.int32, logits.shape, 1).astype(F32)
    big = float(LANES)
    is_grp = lane < N_GROUPS
    gl = jnp.where(is_grp, logits, NEG_INF)
    gmax = jnp.max(gl, axis=-1, keepdims=True)
    gidx = jnp.min(jnp.where(gl == gmax, lane, big), axis=-1, keepdims=True)
    psum = jnp.sum(jnp.where(is_grp, jnp.exp(gl - gmax), 0.0), axis=-1, keepdims=True)
    p_grp = 1.0 / psum
    e_lo = N_GROUPS + gidx * EXPERTS_PER_GROUP
    in_grp = (lane >= e_lo) & (lane < e_lo + EXPERTS_PER_GROUP)
    el = jnp.where(in_grp, logits, NEG_INF)
    t1 = jnp.max(el, axis=-1, keepdims=True)
    i1 = jnp.min(jnp.where(el == t1, lane, big), axis=-1, keepdims=True)
    el2 = jnp.where(lane == i1, NEG_INF, el)
    t2 = jnp.max(el2, axis=-1, keepdims=True)
    i2 = jnp.min(jnp.where(el2 == t2, lane, big), axis=-1, keepdims=True)
    e21 = jnp.exp(t2 - t1)
    den = 1.0 + e21
    g1 = p_grp / den
    g2 = p_grp * (e21 / den)
    e1 = i1 - N_GROUPS
    e2 = i2 - N_GROUPS

    hit1 = lane == e1
    hit2 = lane == e2
    onehot = jnp.where(hit1 | hit2, 1.0, 0.0)
    before = jnp.dot(lstrict_ref[...], onehot.astype(BF16), preferred_element_type=F32) + cnt_ref[...]
    r1 = jnp.sum(jnp.where(hit1, before, 0.0), axis=-1, keepdims=True)
    r2 = jnp.sum(jnp.where(hit2, before, 0.0), axis=-1, keepdims=True)
    cnt_ref[...] = cnt_ref[...] + jnp.sum(onehot, axis=0, keepdims=True)
    counts_ref[...] = cnt_ref[...]

    route = jnp.zeros_like(logits)
    for k, val in enumerate((e1, e2, g1, g2, r1, r2)):
        route = jnp.where(lane == k, val, route)
    route_ref[...] = route


def _mixout(att, su, sv, x2d, sgu_w, sgu_bt, sgu_gn, w_out, g_ffn, rw_hi, rw_lo, rb, lstrict, *,
            att_cols):
    t, d = x2d.shape
    tm = ROW_TILE
    row = lambda i: (i, 0)
    fixed2 = lambda i: (0, 0)
    n_groups = sgu_w.shape[0]
    kern = functools.partial(_mixout_kernel, n_sgu_groups=n_groups, att_cols=att_cols)
    return pl.pallas_call(
        kern,
        grid=(t // tm,),
        in_specs=[
            pl.BlockSpec((tm, att.shape[1]), row),
            pl.BlockSpec((tm, su.shape[1]), row),
            pl.BlockSpec((tm, sv.shape[1]), row),
            pl.BlockSpec((tm, d), row),
            pl.BlockSpec(sgu_w.shape, lambda i: (0, 0, 0)),
            pl.BlockSpec(sgu_bt.shape, fixed2),
            pl.BlockSpec(sgu_gn.shape, fixed2),
            pl.BlockSpec(w_out.shape, fixed2),
            pl.BlockSpec((1, d), fixed2),
            pl.BlockSpec(rw_hi.shape, fixed2),
            pl.BlockSpec(rw_lo.shape, fixed2),
            pl.BlockSpec((1, LANES), fixed2),
            pl.BlockSpec((tm, tm), fixed2),
        ],
        out_specs=[
            pl.BlockSpec((tm, d), row),
            pl.BlockSpec((tm, d), row),
            pl.BlockSpec((tm, LANES), row),
            pl.BlockSpec((1, LANES), fixed2),
        ],
        out_shape=[
            jax.ShapeDtypeStruct((t, d), F32),
            jax.ShapeDtypeStruct((t, d), F32),
            jax.ShapeDtypeStruct((t, LANES), F32),
            jax.ShapeDtypeStruct((1, LANES), F32),
        ],
        scratch_shapes=[pltpu.VMEM((1, LANES), F32)],
        compiler_params=_params(1),
        name="mixout",
    )(att, su, sv, x2d, sgu_w, sgu_bt, sgu_gn, w_out, g_ffn, rw_hi, rw_lo, rb, lstrict)


def _dispatch_kernel(dest_ref, hn_ref, xs_ref, sem):
    rows = hn_ref.shape[0]

    def issue(r, c):
        for k in range(TOP_K):
            pltpu.make_async_copy(hn_ref.at[pl.ds(r, 1), :],
                                  xs_ref.at[pl.ds(dest_ref[TOP_K * r + k], 1), :], sem).start(priority=k)
        return c

    lax.fori_loop(0, rows, issue, 0, unroll=SUBLANES)
    for k in range(TOP_K):
        pltpu.make_async_copy(hn_ref, xs_ref.at[pl.ds(0, rows), :], sem).wait()


def _dispatch(hn, dest):
    t, d = hn.shape
    rows = ROW_TILE
    return pl.pallas_call(
        _dispatch_kernel,
        grid=(t // rows,),
        in_specs=[
            pl.BlockSpec((rows * TOP_K,), lambda i: (i,), memory_space=pltpu.SMEM),
            pl.BlockSpec((rows, d), lambda i: (i, 0)),
        ],
        out_specs=pl.BlockSpec(memory_space=pl.ANY),
        out_shape=jax.ShapeDtypeStruct((t * TOP_K, d), hn.dtype),
        scratch_shapes=[pltpu.SemaphoreType.DMA],
        compiler_params=_params(1),
        name="dispatch",
    )(dest, hn)


def _experts_kernel(vt_ref, ve_ref, vf_ref, vn_ref, lo_ref, hi_ref, xs_ref, wg_ref, wu_ref, wd_ref, ys_ref,
                    wg_bf, wu_bf, wd_bf):
    v = pl.program_id(0)
    flag = vf_ref[v]
    tm = xs_ref.shape[0]

    @pl.when(vn_ref[v] == 1)
    def _():
        wg_bf[...] = wg_ref[0].astype(BF16)
        wu_bf[...] = wu_ref[0].astype(BF16)
        wd_bf[...] = wd_ref[0].astype(BF16)

    @pl.when(flag > 0)
    def _():
        e = ve_ref[v]
        base = vt_ref[v] * tm
        rid = lax.broadcasted_iota(jnp.int32, (tm, 1), 0) + base
        mine = (rid >= lo_ref[e]) & (rid < hi_ref[e])
        x = xs_ref[...].astype(BF16)
        hg = jnp.dot(x, wg_bf[...], preferred_element_type=F32)
        hu = jnp.dot(x, wu_bf[...], preferred_element_type=F32)
        a = (hg * (1.0 / (1.0 + jnp.exp(-hg))) * hu).astype(BF16)
        y = jnp.dot(a, wd_bf[...], preferred_element_type=F32)

        @pl.when(flag == 1)
        def _():
            ys_ref[...] = jnp.where(mine, y, 0.0)

        @pl.when(flag == 2)
        def _():
            ys_ref[...] = jnp.where(mine, y, ys_ref[...])


def _experts(visit_tile, visit_expert, visit_flag, visit_new, row_lo, row_hi, xs, w_gate, w_up, w_down):
    p, d = xs.shape
    tm = EXPERT_TILE
    f = w_gate.shape[-1]
    rows_map = lambda v, vt, ve, vf, vn, lo, hi: (vt[v], 0)
    w_map = lambda v, vt, ve, vf, vn, lo, hi: (ve[v], 0, 0)
    grid_spec = pltpu.PrefetchScalarGridSpec(
        num_scalar_prefetch=6,
        grid=(visit_tile.shape[0],),
        in_specs=[
            pl.BlockSpec((tm, d), rows_map),
            pl.BlockSpec((1, d, f), w_map),
            pl.BlockSpec((1, d, f), w_map),
            pl.BlockSpec((1, f, d), w_map),
        ],
        out_specs=pl.BlockSpec((tm, d), rows_map),
        scratch_shapes=[pltpu.VMEM((d, f), BF16), pltpu.VMEM((d, f), BF16), pltpu.VMEM((f, d), BF16)],
    )
    return pl.pallas_call(
        _experts_kernel,
        grid_spec=grid_spec,
        out_shape=jax.ShapeDtypeStruct(xs.shape, F32),
        compiler_params=_params(1),
        name="experts",
    )(visit_tile, visit_expert, visit_flag, visit_new, row_lo, row_hi, xs, w_gate, w_up, w_down)


def _combine_kernel(dest_ref, dest_next_ref, h_ref, route_ref, ys_ref, o_ref, ybuf, sem):
    i = pl.program_id(0)
    rows = h_ref.shape[0]
    slot = i % 2

    def issue(idx_ref, into):
        def body(r, c):
            for k in range(TOP_K):
                pltpu.make_async_copy(ys_ref.at[pl.ds(idx_ref[TOP_K * r + k], 1), :],
                                      ybuf.at[into, k, pl.ds(r, 1), :], sem.at[into]).start(priority=k)
            return c

        lax.fori_loop(0, rows, body, 0, unroll=SUBLANES)

    @pl.when(i == 0)
    def _():
        issue(dest_ref, 0)

    @pl.when(i + 1 < pl.num_programs(0))
    def _():
        issue(dest_next_ref, 1 - slot)

    for k in range(TOP_K):
        pltpu.make_async_copy(ys_ref.at[pl.ds(0, rows), :], ybuf.at[slot, k], sem.at[slot]).wait()

    route = route_ref[...]
    moe = None
    for k in range(TOP_K):
        gated = route[:, ROUTE_GATE + k:ROUTE_GATE + k + 1] * ybuf[slot, k]
        moe = gated if moe is None else moe + gated
    o_ref[...] = h_ref[...] + moe


def _combine(dest, h, route, ys):
    t, d = h.shape
    rows = COMBINE_TILE
    n = t // rows
    return pl.pallas_call(
        _combine_kernel,
        grid=(n,),
        in_specs=[
            pl.BlockSpec((rows * TOP_K,), lambda i: (i,), memory_space=pltpu.SMEM),
            pl.BlockSpec((rows * TOP_K,), lambda i: (jnp.minimum(i + 1, n - 1),), memory_space=pltpu.SMEM),
            pl.BlockSpec((rows, d), lambda i: (i, 0)),
            pl.BlockSpec((rows, LANES), lambda i: (i, 0)),
            pl.BlockSpec(memory_space=pl.ANY),
        ],
        out_specs=pl.BlockSpec((rows, d), lambda i: (i, 0)),
        out_shape=jax.ShapeDtypeStruct((t, d), F32),
        scratch_shapes=[pltpu.VMEM((2, TOP_K, rows, d), F32), pltpu.SemaphoreType.DMA((2,))],
        compiler_params=_params(1),
        name="combine",
    )(dest, dest, h, route, ys)


def _plan(route, counts_row):
    t = route.shape[0]
    counts = counts_row[0, :N_EXPERTS].astype(I32)
    ends = jnp.cumsum(counts)
    offs = ends - counts
    experts = route[:, ROUTE_EXPERT:ROUTE_EXPERT + TOP_K].astype(I32)
    rank = route[:, ROUTE_RANK:ROUTE_RANK + TOP_K].astype(I32)
    dest = (jnp.take(offs, experts) + rank).reshape(-1)

    tm = EXPERT_TILE
    n_visits = (t * TOP_K) // tm + N_EXPERTS - 1
    first_tile = offs // tm
    last_tile = (ends - 1) // tm
    nvis = jnp.where(counts > 0, last_tile - first_tile + 1, 0)
    vend = jnp.cumsum(nvis)
    vstart = vend - nvis
    v = jnp.arange(n_visits, dtype=I32)
    ev = jnp.minimum(jnp.sum((vend[None, :] <= v[:, None]).astype(I32), axis=1), N_EXPERTS - 1)
    tv = jnp.take(first_tile, ev) + (v - jnp.take(vstart, ev))
    valid = v < vend[-1]
    last_v = vend[-1] - 1
    ev = jnp.where(valid, ev, jnp.take(ev, last_v))
    tv = jnp.where(valid, tv, jnp.take(tv, last_v))
    prev_t = jnp.concatenate([jnp.full((1,), -1, I32), tv[:-1]])
    prev_e = jnp.concatenate([jnp.full((1,), -1, I32), ev[:-1]])
    flag = jnp.where(valid, jnp.where(tv != prev_t, 1, 2), 0).astype(I32)
    new_expert = (ev != prev_e).astype(I32)
    return dest.astype(I32), tv.astype(I32), ev.astype(I32), flag, new_expert, offs.astype(I32), ends.astype(I32)


def _rope_inputs(positions, head_dim, rope_dim):
    t = positions.size
    rope_half = rope_dim // 2
    inv_freq = ROPE_THETA ** (-jnp.arange(0, rope_dim, 2, dtype=F32) / rope_dim)
    pos = jnp.repeat(positions.reshape(-1), rope_half).reshape(-1, LANES).astype(F32)
    ang = pos * jnp.tile(inv_freq, LANES // rope_half)[None, :]
    cos = jnp.cos(ang).reshape(t, rope_half)
    sin = jnp.sin(ang).reshape(t, rope_half)
    cs = jnp.concatenate([cos, sin, jnp.ones((t, 1), F32), jnp.zeros((t, LANES - rope_dim - 1), F32)], axis=1)

    dim = jnp.arange(LANES, dtype=I32)[None, :] % head_dim
    k = jnp.arange(LANES, dtype=I32)[:, None]
    c_mat = jnp.where(dim < rope_dim, k == dim % rope_half, k == rope_dim)
    sa_mat = (dim < rope_half) & (k == rope_half + dim)
    sb_mat = (dim >= rope_half) & (dim < rope_dim) & (k == dim)
    expand = jnp.concatenate([c_mat.astype(F32), -sa_mat.astype(F32), sb_mat.astype(F32)], axis=1)
    return cs, expand.astype(BF16)


def kernel(x, positions, norm_mix_g, w_in, q_norm_g, k_norm_g, lambda_q1, lambda_k1, lambda_q2,
           lambda_k2, diff_out_norm_g, sgu_ln_g, sgu_ln_b, sgu_w, sgu_b, sgu_out_norm_g, w_out,
           norm_ffn_g, router_group_w, router_group_b, router_expert_w, router_expert_b,
           expert_w_gate, expert_w_up, expert_w_down):
    b, s, d = x.shape
    t = b * s
    depth = w_in.shape[0]
    head_dim = q_norm_g.shape[-1]
    vd = diff_out_norm_g.shape[-1]
    n_sgu_groups = sgu_w.shape[1]
    sgu_cols = n_sgu_groups * sgu_w.shape[-1]
    in_cols = w_in.shape[-1]
    att_cols = (in_cols - 2 * sgu_cols) // 3
    n_heads = att_cols // vd
    qk_cols = 2 * att_cols
    rope_dim = head_dim // ROPE_FRACTION
    rope_half = rope_dim // 2
    scale = head_dim ** -0.5

    cs, expand = _rope_inputs(positions, head_dim, rope_dim)
    blk = jnp.arange(MXU_WIDTH, dtype=I32) // head_dim
    ones_blk = (blk[:, None] == blk[None, :]).astype(BF16)
    tri = jnp.arange(ROW_TILE, dtype=I32)
    lstrict = (tri[None, :] < tri[:, None]).astype(BF16)

    h = x.reshape(t, d)
    for l in range(depth):
        lambda_init = 0.8 - 0.6 * math.exp(-0.3 * l)
        gqk = jnp.concatenate([jnp.tile(q_norm_g[l] * (scale * math.log2(math.e)), att_cols // head_dim),
                               jnp.tile(k_norm_g[l], att_cols // head_dim)]).reshape(1, qk_cols)
        qk, v, su, sv = _inproj(
            h, norm_mix_g[l].reshape(1, d), w_in[l].astype(BF16), gqk, cs, expand, ones_blk,
            sgu_ln_g[l].reshape(1, sgu_cols), sgu_ln_b[l].reshape(1, sgu_cols),
            qk_cols=qk_cols, v_cols=att_cols, sgu_cols=sgu_cols, head_dim=head_dim,
            rope_half=rope_half)

        lam_vecs = jnp.stack([lambda_q1[l], lambda_k1[l], lambda_q2[l], lambda_k2[l]]).astype(F32)
        score_bound = (1.02 * head_dim * scale * jnp.max(jnp.abs(q_norm_g[l]))
                       * jnp.max(jnp.abs(k_norm_g[l])))
        need_shift = (score_bound > MAX_UNSHIFTED_SCORE).astype(I32).reshape(1)
        att = _attention(need_shift, lam_vecs, diff_out_norm_g[l].reshape(1, vd), qk.reshape(b, s, qk_cols),
                         v.reshape(b, s, att_cols), n_heads=n_heads, head_dim=head_dim,
                         lambda_init=lambda_init).reshape(t, att_cols)

        rw = jnp.concatenate([router_group_w[l], router_expert_w[l]], axis=1)
        rw = jnp.pad(rw, ((0, 0), (0, LANES - rw.shape[1])))
        rw_hi, rw_lo = _split_bf16(rw)
        rb = jnp.pad(jnp.concatenate([router_group_b[l], router_expert_b[l]]),
                     (0, LANES - N_GROUPS - N_EXPERTS)).reshape(1, LANES)
        h, hn, route, counts_row = _mixout(
            att, su, sv, h, sgu_w[l], sgu_b[l].T, sgu_out_norm_g[l], w_out[l].astype(BF16),
            norm_ffn_g[l].reshape(1, d), rw_hi, rw_lo, rb, lstrict, att_cols=att_cols)

        dest, visit_tile, visit_expert, visit_flag, visit_new, row_lo, row_hi = _plan(route, counts_row)
        xs = _dispatch(hn, dest)
        ys = _experts(visit_tile, visit_expert, visit_flag, visit_new, row_lo, row_hi, xs,
                      expert_w_gate[l], expert_w_up[l], expert_w_down[l])
        h = _combine(dest, h, route, ys)
    return h.reshape(b, s, d)
```

```python
import functools
import math

import jax
import jax.numpy as jnp
from jax import lax
from jax.experimental import pallas as pl
from jax.experimental.pallas import tpu as pltpu

F32 = jnp.float32
BF16 = jnp.bfloat16
I32 = jnp.int32

RMS_EPS = 1e-6
NEG_INF = -1e30
ROPE_THETA = 500000.0
MAX_UNSHIFTED_SCORE = 40.0

LANES = 128
SUBLANES = 8
MXU_WIDTH = 256
VMEM_LIMIT_BYTES = 56 * 1024 * 1024

N_GROUPS = 4
EXPERTS_PER_GROUP = 8
N_EXPERTS = N_GROUPS * EXPERTS_PER_GROUP
TOP_K = 2
ROPE_FRACTION = 4
CHUNK = 128
QBLOCK = 256
ROW_TILE = 512
EXPERT_TILE = 512
COMBINE_TILE = 256

ROUTE_EXPERT = 0
ROUTE_GATE = 2
ROUTE_RANK = 4


def _params(n_grid_dims):
    return pltpu.CompilerParams(
        dimension_semantics=("arbitrary",) * n_grid_dims,
        vmem_limit_bytes=VMEM_LIMIT_BYTES,
    )


def _gelu(x):
    return 0.5 * x * (1.0 + lax.erf(x * (1.0 / math.sqrt(2.0))))


def _split_bf16(x):
    hi = x.astype(BF16)
    return hi, (x - hi.astype(F32)).astype(BF16)


def _inproj_kernel(x_ref, g_ref, w_ref, gqk_ref, pos_ref, freq_ref, expand_ref, ones_ref, lng_ref, lnb_ref,
                   qk_ref, v_ref, su_ref, sv_ref, proj_ref, ss_ref, *, qk_cols, v_cols, sgu_cols, head_dim,
                   rope_half):
    x = x_ref[...]
    ms = jnp.mean(x * x, axis=-1, keepdims=True)
    xn = (x * lax.rsqrt(ms + RMS_EPS) * g_ref[...]).astype(BF16)
    cw = MXU_WIDTH

    ang_t = freq_ref[...] * pos_ref[0]
    cs_t = jnp.concatenate([jnp.cos(ang_t), jnp.sin(ang_t), jnp.ones_like(ang_t),
                            jnp.zeros_like(ang_t)], axis=0)
    contract0 = (((0,), (0,)), ((), ()))
    cs_hi, cs_lo = _split_bf16(cs_t)
    tab = (lax.dot_general(cs_hi, expand_ref[...], contract0, preferred_element_type=F32)
           + lax.dot_general(cs_lo, expand_ref[...], contract0, preferred_element_type=F32))
    cos = tab[:, 0:LANES]
    sa = tab[:, LANES:2 * LANES]
    sb = tab[:, 2 * LANES:3 * LANES]

    def stage(c):
        cols = slice(c * cw, (c + 1) * cw)
        proj_ref[:, cols] = jnp.dot(xn, w_ref[:, cols], preferred_element_type=F32)

    n_qk = qk_cols // cw
    for c in range(n_qk):
        stage(c)
    for c in range(n_qk):
        p = proj_ref[:, c * cw:(c + 1) * cw]
        ss_ref[:, c * cw:(c + 1) * cw] = jnp.dot((p * p).astype(BF16), ones_ref[...],
                                                 preferred_element_type=F32)
    for c in range(n_qk, w_ref.shape[1] // cw):
        stage(c)

    def proj(col0):
        return proj_ref[:, col0:col0 + cw]

    for c in range(qk_cols // cw):
        p = proj(c * cw)
        ss = ss_ref[:, c * cw:(c + 1) * cw]
        pn = p * lax.rsqrt(ss * (1.0 / head_dim) + RMS_EPS) * gqk_ref[:, c * cw:(c + 1) * cw]
        for hh in range(cw // LANES):
            blk = pn[:, hh * LANES:(hh + 1) * LANES]
            rot = (blk * cos
                   + pltpu.roll(blk, LANES - rope_half, 1) * sa
                   + pltpu.roll(blk, rope_half, 1) * sb)
            col = c * cw + hh * LANES
            qk_ref[:, col:col + LANES] = rot.astype(BF16)

    for c in range(v_cols // cw):
        v_ref[:, c * cw:(c + 1) * cw] = proj(qk_cols + c * cw).astype(BF16)

    for c in range(sgu_cols // cw):
        su_ref[:, c * cw:(c + 1) * cw] = _gelu(proj(qk_cols + v_cols + c * cw)).astype(BF16)

    sv = jnp.concatenate(
        [_gelu(proj(qk_cols + v_cols + sgu_cols + c * cw)) for c in range(sgu_cols // cw)], axis=1)
    mu = jnp.mean(sv, axis=-1, keepdims=True)
    svc = sv - mu
    var = jnp.mean(svc * svc, axis=-1, keepdims=True)
    sv_ref[...] = (svc * lax.rsqrt(var + RMS_EPS) * lng_ref[...] + lnb_ref[...]).astype(BF16)


def _inproj(x2d, norm_g, w_in, gqk, pos_rows, freq_col, expand, ones_blk, ln_g, ln_b, *, qk_cols, v_cols,
            sgu_cols, head_dim, rope_half):
    t, d = x2d.shape
    tm = ROW_TILE
    row = lambda i: (i, 0)
    fixed = lambda i: (0, 0)
    kern = functools.partial(_inproj_kernel, qk_cols=qk_cols, v_cols=v_cols, sgu_cols=sgu_cols,
                             head_dim=head_dim, rope_half=rope_half)
    return pl.pallas_call(
        kern,
        grid=(t // tm,),
        in_specs=[
            pl.BlockSpec((tm, d), row),
            pl.BlockSpec((1, d), fixed),
            pl.BlockSpec(w_in.shape, fixed),
            pl.BlockSpec((1, qk_cols), fixed),
            pl.BlockSpec((1, 1, tm), lambda i: (i, 0, 0)),
            pl.BlockSpec(freq_col.shape, fixed),
            pl.BlockSpec(expand.shape, fixed),
            pl.BlockSpec((MXU_WIDTH, MXU_WIDTH), fixed),
            pl.BlockSpec((1, sgu_cols), fixed),
            pl.BlockSpec((1, sgu_cols), fixed),
        ],
        out_specs=[
            pl.BlockSpec((tm, qk_cols), row),
            pl.BlockSpec((tm, v_cols), row),
            pl.BlockSpec((tm, sgu_cols), row),
            pl.BlockSpec((tm, sgu_cols), row),
        ],
        out_shape=[
            jax.ShapeDtypeStruct((t, qk_cols), BF16),
            jax.ShapeDtypeStruct((t, v_cols), BF16),
            jax.ShapeDtypeStruct((t, sgu_cols), BF16),
            jax.ShapeDtypeStruct((t, sgu_cols), BF16),
        ],
        scratch_shapes=[pltpu.VMEM((tm, w_in.shape[1]), F32), pltpu.VMEM((tm, qk_cols), F32)],
        compiler_params=_params(1),
        name="inproj",
    )(x2d, norm_g, w_in, gqk, pos_rows, freq_col, expand, ones_blk, ln_g, ln_b)


def _attn_kernel(shift_ref, lam_ref, gout_ref, q_ref, k_ref, v_ref, o_ref, qm_ref, m_ref, acc_ref, *, tq,
                 n_heads, lambda_init, head_dim):
    qi = pl.program_id(1)
    hw = 2 * head_dim
    vd = v_ref.shape[-1] // n_heads
    lv = lam_ref[...]
    lam = (jnp.exp(jnp.sum(lv[0:1] * lv[1:2], axis=-1, keepdims=True))
           - jnp.exp(jnp.sum(lv[2:3] * lv[3:4], axis=-1, keepdims=True))
           + lambda_init)

    first = lax.broadcasted_iota(jnp.int32, (1, hw), 1) < head_dim
    for h in range(n_heads):
        q = q_ref[0, :, h * hw:(h + 1) * hw]
        zero = jnp.zeros_like(q)
        qm_ref[2 * h] = jnp.where(first, q, zero)
        qm_ref[2 * h + 1] = jnp.where(first, zero, q)
    acc_ref[...] = jnp.zeros(acc_ref.shape, F32)

    ones = jnp.ones((tq, vd), BF16)
    causal = (lax.broadcasted_iota(jnp.int32, (tq, tq), 1)
              <= lax.broadcasted_iota(jnp.int32, (tq, tq), 0))

    def kv_tile(j, h):
        start = pl.multiple_of(j * tq, tq)
        kc = k_ref[0, pl.ds(start, tq), h * hw:(h + 1) * hw]
        vext = jnp.concatenate([v_ref[0, pl.ds(start, tq), h * vd:(h + 1) * vd], ones], axis=1)
        return kc, vext

    def scores(c, kc, masked):
        s = lax.dot_general(qm_ref[c], kc, (((1,), (1,)), ((), ())), preferred_element_type=F32)
        return jnp.where(causal, s, NEG_INF) if masked else s

    def plain_step(tiles):
        for h in range(n_heads):
            kvs = [kv_tile(j, h) for j, _ in tiles]
            for mp in range(2):
                c = 2 * h + mp
                pv = None
                for (kc, vext), (_, masked) in zip(kvs, tiles):
                    part = jnp.dot(jnp.exp2(scores(c, kc, masked)).astype(BF16), vext,
                                   preferred_element_type=F32)
                    pv = part if pv is None else pv + part
                acc_ref[c] += pv

    def running_max_step(j, masked):
        for h in range(n_heads):
            kc, vext = kv_tile(j, h)
            for mp in range(2):
                c = 2 * h + mp
                s = scores(c, kc, masked)
                m_old = m_ref[c]
                m_new = jnp.maximum(m_old, jnp.max(s, axis=-1, keepdims=True))
                p = jnp.exp2(s - m_new).astype(BF16)
                acc_ref[c] = (jnp.exp2(m_old - m_new) * acc_ref[c]
                              + jnp.dot(p, vext, preferred_element_type=F32))
                m_ref[c] = m_new

    @pl.when(shift_ref[0] == 0)
    def _():
        def pair(jj, carry):
            plain_step([(2 * jj, False), (2 * jj + 1, False)])
            return carry

        lax.fori_loop(0, qi // 2, pair, 0)

        @pl.when(qi % 2 == 1)
        def _():
            plain_step([(qi - 1, False), (qi, True)])

        @pl.when(qi % 2 == 0)
        def _():
            plain_step([(qi, True)])

    @pl.when(shift_ref[0] != 0)
    def _():
        m_ref[...] = jnp.full(m_ref.shape, NEG_INF, F32)

        def single(j, carry):
            running_max_step(j, False)
            return carry

        lax.fori_loop(0, qi, single, 0)
        running_max_step(qi, True)

    for h in range(n_heads):
        a1 = acc_ref[2 * h]
        a2 = acc_ref[2 * h + 1]
        att = a1[:, :vd] / a1[:, vd:] - lam * (a2[:, :vd] / a2[:, vd:])
        ms = jnp.mean(att * att, axis=-1, keepdims=True)
        o_ref[0, :, h * vd:(h + 1) * vd] = (
            att * lax.rsqrt(ms + RMS_EPS) * gout_ref[...] * (1.0 - lambda_init)).astype(BF16)


def _attention(need_shift, lam_vecs, gout, qk3, v3, *, n_heads, head_dim, lambda_init):
    b, s, _ = qk3.shape
    att_cols = v3.shape[-1]
    vd = att_cols // n_heads
    tq = QBLOCK
    kern = functools.partial(_attn_kernel, tq=tq, n_heads=n_heads, lambda_init=lambda_init,
                             head_dim=head_dim)
    grid_spec = pltpu.PrefetchScalarGridSpec(
        num_scalar_prefetch=1,
        grid=(b, s // tq),
        in_specs=[
            pl.BlockSpec(lam_vecs.shape, lambda bi, qi, f: (0, 0)),
            pl.BlockSpec((1, vd), lambda bi, qi, f: (0, 0)),
            pl.BlockSpec((1, tq, att_cols), lambda bi, qi, f: (bi, qi, 0)),
            pl.BlockSpec((1, s, att_cols), lambda bi, qi, f: (bi, 0, 1)),
            pl.BlockSpec((1, s, att_cols), lambda bi, qi, f: (bi, 0, 0)),
        ],
        out_specs=pl.BlockSpec((1, tq, att_cols), lambda bi, qi, f: (bi, qi, 0)),
        scratch_shapes=[
            pltpu.VMEM((2 * n_heads, tq, 2 * head_dim), BF16),
            pltpu.VMEM((2 * n_heads, tq, 1), F32),
            pltpu.VMEM((2 * n_heads, tq, 2 * vd), F32),
        ],
    )
    return pl.pallas_call(
        kern,
        grid_spec=grid_spec,
        out_shape=jax.ShapeDtypeStruct(v3.shape, BF16),
        compiler_params=_params(2),
        name="diff_attn",
    )(need_shift, lam_vecs, gout, qk3, qk3, v3)


def _mixout_kernel(att_ref, su_ref, sv_ref, x_ref, sw_ref, sbt_ref, sgn_ref, wo_ref, gffn_ref,
                   rwh_ref, rwl_ref, rb_ref, lstrict_ref, h_ref, hn_ref, route_ref, counts_ref, cnt_ref,
                   *, n_sgu_groups, att_cols):
    tm = x_ref.shape[0]
    gw = CHUNK
    r_id = lax.broadcasted_iota(jnp.int32, (gw, gw), 0)
    c_id = lax.broadcasted_iota(jnp.int32, (gw, gw), 1)
    tril = c_id <= r_id

    @pl.when(pl.program_id(0) == 0)
    def _():
        cnt_ref[...] = jnp.zeros_like(cnt_ref)

    acc = jnp.dot(att_ref[...], wo_ref[0:att_cols, :], preferred_element_type=F32)

    sg_cols = []
    for g in range(n_sgu_groups):
        w = jnp.where(tril, sw_ref[g], 0.0).astype(BF16)
        bias = sbt_ref[:, g:g + 1]
        gn = sgn_ref[g:g + 1, :]
        rows = []
        for c in range(tm // gw):
            vblk = sv_ref[c * gw:(c + 1) * gw, g * gw:(g + 1) * gw]
            s = jnp.dot(w, vblk, preferred_element_type=F32) + bias
            sg = su_ref[c * gw:(c + 1) * gw, g * gw:(g + 1) * gw].astype(F32) * s
            ms = jnp.mean(sg * sg, axis=-1, keepdims=True)
            rows.append((sg * lax.rsqrt(ms + RMS_EPS) * gn).astype(BF16))
        sg_cols.append(jnp.concatenate(rows, axis=0))
    sgn = jnp.concatenate(sg_cols, axis=1)
    acc = acc + jnp.dot(sgn, wo_ref[att_cols:, :], preferred_element_type=F32)

    h = x_ref[...] + acc
    h_ref[...] = h
    ms = jnp.mean(h * h, axis=-1, keepdims=True)
    hn = h * lax.rsqrt(ms + RMS_EPS) * gffn_ref[...]
    hn_ref[...] = hn

    hn_bf, hn_lo = _split_bf16(hn)
    logits = (jnp.dot(hn_bf, rwh_ref[...], preferred_element_type=F32)
              + jnp.dot(hn_lo, rwh_ref[...], preferred_element_type=F32)
              + jnp.dot(hn_bf, rwl_ref[...], preferred_element_type=F32)
              + rb_ref[...])

    lane = lax.broadcasted_iota(jnp.int32, logits.shape, 1).astype(F32)
    big = float(LANES)
    is_grp = lane < N_GROUPS
    gl = jnp.where(is_grp, logits, NEG_INF)
    gmax = jnp.max(gl, axis=-1, keepdims=True)
    gidx = jnp.min(jnp.where(gl == gmax, lane, big), axis=-1, keepdims=True)
    psum = jnp.sum(jnp.where(is_grp, jnp.exp(gl - gmax), 0.0), axis=-1, keepdims=True)
    p_grp = 1.0 / psum
    e_lo = N_GROUPS + gidx * EXPERTS_PER_GROUP
    in_grp = (lane >= e_lo) & (lane < e_lo + EXPERTS_PER_GROUP)
    el = jnp.where(in_grp, logits, NEG_INF)
    t1 = jnp.max(el, axis=-1, keepdims=True)
    i1 = jnp.min(jnp.where(el == t1, lane, big), axis=-1, keepdims=True)
    el2 = jnp.where(lane == i1, NEG_INF, el)
    t2 = jnp.max(el2, axis=-1, keepdims=True)
    i2 = jnp.min(jnp.where(el2 == t2, lane, big), axis=-1, keepdims=True)
    e21 = jnp.exp(t2 - t1)
    den = 1.0 + e21
    g1 = p_grp / den
    g2 = p_grp * (e21 / den)
    e1 = i1 - N_GROUPS
    e2 = i2 - N_GROUPS

    hit1 = lane == e1
    hit2 = lane == e2
    onehot = jnp.where(hit1 | hit2, 1.0, 0.0)
    before = jnp.dot(lstrict_ref[...], onehot.astype(BF16), preferred_element_type=F32) + cnt_ref[...]
    r1 = jnp.sum(jnp.where(hit1, before, 0.0), axis=-1, keepdims=True)
    r2 = jnp.sum(jnp.where(hit2, before, 0.0), axis=-1, keepdims=True)
    cnt_ref[...] = cnt_ref[...] + jnp.sum(onehot, axis=0, keepdims=True)
    counts_ref[...] = cnt_ref[...]

    route = jnp.zeros_like(logits)
    for k, val in enumerate((e1, e2, g1, g2, r1, r2)):
        route = jnp.where(lane == k, val, route)
    route_ref[...] = route


def _mixout(att, su, sv, x2d, sgu_w, sgu_bt, sgu_gn, w_out, g_ffn, rw_hi, rw_lo, rb, lstrict, *,
            att_cols):
    t, d = x2d.shape
    tm = ROW_TILE
    row = lambda i: (i, 0)
    fixed2 = lambda i: (0, 0)
    n_groups = sgu_w.shape[0]
    kern = functools.partial(_mixout_kernel, n_sgu_groups=n_groups, att_cols=att_cols)
    return pl.pallas_call(
        kern,
        grid=(t // tm,),
        in_specs=[
            pl.BlockSpec((tm, att.shape[1]), row),
            pl.BlockSpec((tm, su.shape[1]), row),
            pl.BlockSpec((tm, sv.shape[1]), row),
            pl.BlockSpec((tm, d), row),
            pl.BlockSpec(sgu_w.shape, lambda i: (0, 0, 0)),
            pl.BlockSpec(sgu_bt.shape, fixed2),
            pl.BlockSpec(sgu_gn.shape, fixed2),
            pl.BlockSpec(w_out.shape, fixed2),
            pl.BlockSpec((1, d), fixed2),
            pl.BlockSpec(rw_hi.shape, fixed2),
            pl.BlockSpec(rw_lo.shape, fixed2),
            pl.BlockSpec((1, LANES), fixed2),
            pl.BlockSpec((tm, tm), fixed2),
        ],
        out_specs=[
            pl.BlockSpec((tm, d), row),
            pl.BlockSpec((tm, d), row),
            pl.BlockSpec((tm, LANES), row),
            pl.BlockSpec((1, LANES), fixed2),
        ],
        out_shape=[
            jax.ShapeDtypeStruct((t, d), F32),
            jax.ShapeDtypeStruct((t, d), F32),
            jax.ShapeDtypeStruct((t, LANES), F32),
            jax.ShapeDtypeStruct((1, LANES), F32),
        ],
        scratch_shapes=[pltpu.VMEM((1, LANES), F32)],
        compiler_params=_params(1),
        name="mixout",
    )(att, su, sv, x2d, sgu_w, sgu_bt, sgu_gn, w_out, g_ffn, rw_hi, rw_lo, rb, lstrict)


def _dispatch_kernel(dest_ref, hn_ref, xs_ref, sem):
    rows = hn_ref.shape[0]

    def issue(r, c):
        for k in range(TOP_K):
            pltpu.make_async_copy(hn_ref.at[pl.ds(r, 1), :],
                                  xs_ref.at[pl.ds(dest_ref[TOP_K * r + k], 1), :], sem).start(priority=k)
        return c

    lax.fori_loop(0, rows, issue, 0, unroll=SUBLANES)
    for k in range(TOP_K):
        pltpu.make_async_copy(hn_ref, xs_ref.at[pl.ds(0, rows), :], sem).wait()


def _dispatch(hn, dest):
    t, d = hn.shape
    rows = ROW_TILE
    return pl.pallas_call(
        _dispatch_kernel,
        grid=(t // rows,),
        in_specs=[
            pl.BlockSpec((rows * TOP_K,), lambda i: (i,), memory_space=pltpu.SMEM),
            pl.BlockSpec((rows, d), lambda i: (i, 0)),
        ],
        out_specs=pl.BlockSpec(memory_space=pl.ANY),
        out_shape=jax.ShapeDtypeStruct((t * TOP_K, d), hn.dtype),
        scratch_shapes=[pltpu.SemaphoreType.DMA],
        compiler_params=_params(1),
        name="dispatch",
    )(dest, hn)


def _experts_kernel(vt_ref, ve_ref, vf_ref, vn_ref, lo_ref, hi_ref, xs_ref, wg_ref, wu_ref, wd_ref, ys_ref,
                    wg_bf, wu_bf, wd_bf):
    v = pl.program_id(0)
    flag = vf_ref[v]
    tm = xs_ref.shape[0]

    @pl.when(vn_ref[v] == 1)
    def _():
        wg_bf[...] = wg_ref[0].astype(BF16)
        wu_bf[...] = wu_ref[0].astype(BF16)
        wd_bf[...] = wd_ref[0].astype(BF16)

    @pl.when(flag > 0)
    def _():
        e = ve_ref[v]
        base = vt_ref[v] * tm
        rid = lax.broadcasted_iota(jnp.int32, (tm, 1), 0) + base
        mine = (rid >= lo_ref[e]) & (rid < hi_ref[e])
        x = xs_ref[...].astype(BF16)
        hg = jnp.dot(x, wg_bf[...], preferred_element_type=F32)
        hu = jnp.dot(x, wu_bf[...], preferred_element_type=F32)
        a = (hg * (1.0 / (1.0 + jnp.exp(-hg))) * hu).astype(BF16)

        y = jnp.dot(a, wd_bf[...], preferred_element_type=F32)

        @pl.when(flag == 1)
        def _():
            ys_ref[...] = jnp.where(mine, y, 0.0)

        @pl.when(flag == 2)
        def _():
            ys_ref[...] = jnp.where(mine, y, ys_ref[...])


def _experts(visit_tile, visit_expert, visit_flag, visit_new, row_lo, row_hi, xs, w_gate, w_up, w_down):
    p, d = xs.shape
    tm = EXPERT_TILE
    f = w_gate.shape[-1]
    rows_map = lambda v, vt, ve, vf, vn, lo, hi: (vt[v], 0)
    w_map = lambda v, vt, ve, vf, vn, lo, hi: (ve[v], 0, 0)
    grid_spec = pltpu.PrefetchScalarGridSpec(
        num_scalar_prefetch=6,
        grid=(visit_tile.shape[0],),
        in_specs=[
            pl.BlockSpec((tm, d), rows_map),
            pl.BlockSpec((1, d, f), w_map),
            pl.BlockSpec((1, d, f), w_map),
            pl.BlockSpec((1, f, d), w_map),
        ],
        out_specs=pl.BlockSpec((tm, d), rows_map),
        scratch_shapes=[pltpu.VMEM((d, f), BF16), pltpu.VMEM((d, f), BF16), pltpu.VMEM((f, d), BF16)],
    )
    return pl.pallas_call(
        _experts_kernel,
        grid_spec=grid_spec,
        out_shape=jax.ShapeDtypeStruct(xs.shape, F32),
        compiler_params=_params(1),
        name="experts",
    )(visit_tile, visit_expert, visit_flag, visit_new, row_lo, row_hi, xs, w_gate, w_up, w_down)


def _combine_kernel(dest_ref, dest_next_ref, h_ref, route_ref, ys_ref, o_ref, ybuf, sem):
    i = pl.program_id(0)
    rows = h_ref.shape[0]
    slot = i % 2

    def issue(idx_ref, into):
        def body(r, c):
            for k in range(TOP_K):
                pltpu.make_async_copy(ys_ref.at[pl.ds(idx_ref[TOP_K * r + k], 1), :],
                                      ybuf.at[into, k, pl.ds(r, 1), :], sem.at[into]).start(priority=k)
            return c

        lax.fori_loop(0, rows, body, 0, unroll=SUBLANES)

    @pl.when(i == 0)
    def _():
        issue(dest_ref, 0)

    @pl.when(i + 1 < pl.num_programs(0))
    def _():
        issue(dest_next_ref, 1 - slot)

    for k in range(TOP_K):
        pltpu.make_async_copy(ys_ref.at[pl.ds(0, rows), :], ybuf.at[slot, k], sem.at[slot]).wait()

    route = route_ref[...]
    moe = None
    for k in range(TOP_K):
        gated = route[:, ROUTE_GATE + k:ROUTE_GATE + k + 1] * ybuf[slot, k]
        moe = gated if moe is None else moe + gated
    o_ref[...] = h_ref[...] + moe


def _combine(dest, h, route, ys):
    t, d = h.shape
    rows = COMBINE_TILE
    n = t // rows
    return pl.pallas_call(
        _combine_kernel,
        grid=(n,),
        in_specs=[
            pl.BlockSpec((rows * TOP_K,), lambda i: (i,), memory_space=pltpu.SMEM),
            pl.BlockSpec((rows * TOP_K,), lambda i: (jnp.minimum(i + 1, n - 1),), memory_space=pltpu.SMEM),
            pl.BlockSpec((rows, d), lambda i: (i, 0)),
            pl.BlockSpec((rows, LANES), lambda i: (i, 0)),
            pl.BlockSpec(memory_space=pl.ANY),
        ],
        out_specs=pl.BlockSpec((rows, d), lambda i: (i, 0)),
        out_shape=jax.ShapeDtypeStruct((t, d), F32),
        scratch_shapes=[pltpu.VMEM((2, TOP_K, rows, d), F32), pltpu.SemaphoreType.DMA((2,))],
        compiler_params=_params(1),
        name="combine",
    )(dest, dest, h, route, ys)


def _plan(route, counts_row):
    t = route.shape[0]
    counts = counts_row[0, :N_EXPERTS].astype(I32)
    ends = jnp.cumsum(counts)
    offs = ends - counts
    experts = route[:, ROUTE_EXPERT:ROUTE_EXPERT + TOP_K].astype(I32)
    rank = route[:, ROUTE_RANK:ROUTE_RANK + TOP_K].astype(I32)
    dest = (jnp.take(offs, experts) + rank).reshape(-1)

    tm = EXPERT_TILE
    n_visits = (t * TOP_K) // tm + N_EXPERTS - 1
    first_tile = offs // tm
    last_tile = (ends - 1) // tm
    nvis = jnp.where(counts > 0, last_tile - first_tile + 1, 0)
    vend = jnp.cumsum(nvis)
    vstart = vend - nvis
    v = jnp.arange(n_visits, dtype=I32)
    ev = jnp.minimum(jnp.sum((vend[None, :] <= v[:, None]).astype(I32), axis=1), N_EXPERTS - 1)
    tv = jnp.take(first_tile, ev) + (v - jnp.take(vstart, ev))
    valid = v < vend[-1]
    last_v = vend[-1] - 1
    ev = jnp.where(valid, ev, jnp.take(ev, last_v))
    tv = jnp.where(valid, tv, jnp.take(tv, last_v))
    prev_t = jnp.concatenate([jnp.full((1,), -1, I32), tv[:-1]])
    prev_e = jnp.concatenate([jnp.full((1,), -1, I32), ev[:-1]])
    flag = jnp.where(valid, jnp.where(tv != prev_t, 1, 2), 0).astype(I32)
    new_expert = (ev != prev_e).astype(I32)
    return dest.astype(I32), tv.astype(I32), ev.astype(I32), flag, new_expert, offs.astype(I32), ends.astype(I32)


def _rope_constants(head_dim, rope_dim):
    rope_half = rope_dim // 2
    inv_freq = ROPE_THETA ** (-jnp.arange(0, rope_dim, 2, dtype=F32) / rope_dim)
    dim = jnp.arange(LANES, dtype=I32)[None, :] % head_dim
    k = jnp.arange(4 * rope_half, dtype=I32)[:, None]
    c_mat = jnp.where(dim < rope_dim, k == dim % rope_half, k == rope_dim)
    sa_mat = (dim < rope_half) & (k == rope_half + dim)
    sb_mat = (dim >= rope_half) & (dim < rope_dim) & (k == dim)
    expand = jnp.concatenate([c_mat.astype(F32), -sa_mat.astype(F32), sb_mat.astype(F32)], axis=1)
    return inv_freq.reshape(rope_half, 1), expand.astype(BF16)


def kernel(x, positions, norm_mix_g, w_in, q_norm_g, k_norm_g, lambda_q1, lambda_k1, lambda_q2,
           lambda_k2, diff_out_norm_g, sgu_ln_g, sgu_ln_b, sgu_w, sgu_b, sgu_out_norm_g, w_out,
           norm_ffn_g, router_group_w, router_group_b, router_expert_w, router_expert_b,
           expert_w_gate, expert_w_up, expert_w_down):
    b, s, d = x.shape
    t = b * s
    depth = w_in.shape[0]
    head_dim = q_norm_g.shape[-1]
    vd = diff_out_norm_g.shape[-1]
    n_sgu_groups = sgu_w.shape[1]
    sgu_cols = n_sgu_groups * sgu_w.shape[-1]
    in_cols = w_in.shape[-1]
    att_cols = (in_cols - 2 * sgu_cols) // 3
    n_heads = att_cols // vd
    qk_cols = 2 * att_cols
    rope_dim = head_dim // ROPE_FRACTION
    rope_half = rope_dim // 2
    scale = head_dim ** -0.5

    pos_rows = positions.reshape(t // ROW_TILE, 1, ROW_TILE).astype(F32)
    freq_col, expand = _rope_constants(head_dim, rope_dim)
    blk = jnp.arange(MXU_WIDTH, dtype=I32) // head_dim
    ones_blk = (blk[:, None] == blk[None, :]).astype(BF16)
    tri = jnp.arange(ROW_TILE, dtype=I32)
    lstrict = (tri[None, :] < tri[:, None]).astype(BF16)

    h = x.reshape(t, d)
    for l in range(depth):
        lambda_init = 0.8 - 0.6 * math.exp(-0.3 * l)
        gqk = jnp.concatenate([jnp.tile(q_norm_g[l] * (scale * math.log2(math.e)), att_cols // head_dim),
                               jnp.tile(k_norm_g[l], att_cols // head_dim)]).reshape(1, qk_cols)
        qk, v, su, sv = _inproj(
            h, norm_mix_g[l].reshape(1, d), w_in[l].astype(BF16), gqk, pos_rows, freq_col, expand, ones_blk,
            sgu_ln_g[l].reshape(1, sgu_cols), sgu_ln_b[l].reshape(1, sgu_cols),
            qk_cols=qk_cols, v_cols=att_cols, sgu_cols=sgu_cols, head_dim=head_dim,
            rope_half=rope_half)

        lam_vecs = jnp.stack([lambda_q1[l], lambda_k1[l], lambda_q2[l], lambda_k2[l]]).astype(F32)
        score_bound = (1.02 * head_dim * scale * jnp.max(jnp.abs(q_norm_g[l]))
                       * jnp.max(jnp.abs(k_norm_g[l])))
        need_shift = (score_bound > MAX_UNSHIFTED_SCORE).astype(I32).reshape(1)
        att = _attention(need_shift, lam_vecs, diff_out_norm_g[l].reshape(1, vd), qk.reshape(b, s, qk_cols),
                         v.reshape(b, s, att_cols), n_heads=n_heads, head_dim=head_dim,
                         lambda_init=lambda_init).reshape(t, att_cols)

        rw = jnp.concatenate([router_group_w[l], router_expert_w[l]], axis=1)
        rw = jnp.pad(rw, ((0, 0), (0, LANES - rw.shape[1])))
        rw_hi, rw_lo = _split_bf16(rw)
        rb = jnp.pad(jnp.concatenate([router_group_b[l], router_expert_b[l]]),
                     (0, LANES - N_GROUPS - N_EXPERTS)).reshape(1, LANES)
        h, hn, route, counts_row = _mixout(
            att, su, sv, h, sgu_w[l], sgu_b[l].T, sgu_out_norm_g[l], w_out[l].astype(BF16),
            norm_ffn_g[l].reshape(1, d), rw_hi, rw_lo, rb, lstrict, att_cols=att_cols)

        dest, visit_tile, visit_expert, visit_flag, visit_new, row_lo, row_hi = _plan(route, counts_row)
        xs = _dispatch(hn, dest)
        ys = _experts(visit_tile, visit_expert, visit_flag, visit_new, row_lo, row_hi, xs,
                      expert_w_gate[l], expert_w_up[l], expert_w_down[l])
        h = _combine(dest, h, route, ys)
    return h.reshape(b, s, d)
```

```python
import functools
import math

import jax
import jax.numpy as jnp
from jax import lax
from jax.experimental import pallas as pl
from jax.experimental.pallas import tpu as pltpu

F32 = jnp.float32
BF16 = jnp.bfloat16
I32 = jnp.int32

RMS_EPS = 1e-6
NEG_INF = -1e30
ROPE_THETA = 500000.0
MAX_UNSHIFTED_SCORE = 40.0

LANES = 128
SUBLANES = 8
MXU_WIDTH = 256
VMEM_LIMIT_BYTES = 56 * 1024 * 1024

N_GROUPS = 4
EXPERTS_PER_GROUP = 8
N_EXPERTS = N_GROUPS * EXPERTS_PER_GROUP
TOP_K = 2
ROPE_FRACTION = 4
CHUNK = 128
QBLOCK = 256
ROW_TILE = 512
EXPERT_TILE = 512
COMBINE_TILE = 256

ROUTE_EXPERT = 0
ROUTE_GATE = 2
ROUTE_RANK = 4


def _params(n_grid_dims):
    return pltpu.CompilerParams(
        dimension_semantics=("arbitrary",) * n_grid_dims,
        vmem_limit_bytes=VMEM_LIMIT_BYTES,
    )


def _gelu(x):
    return 0.5 * x * (1.0 + lax.erf(x * (1.0 / math.sqrt(2.0))))


def _split_bf16(x):
    hi = x.astype(BF16)
    return hi, (x - hi.astype(F32)).astype(BF16)


def _inproj_kernel(x_ref, g_ref, w_ref, gqk_ref, pos_ref, freq_ref, expand_ref, ones_ref, lng_ref, lnb_ref,
                   qk_ref, v_ref, su_ref, sv_ref, proj_ref, ss_ref, *, qk_cols, v_cols, sgu_cols, head_dim,
                   rope_half):
    x = x_ref[...]
    ms = jnp.mean(x * x, axis=-1, keepdims=True)
    xn = (x * lax.rsqrt(ms + RMS_EPS) * g_ref[...]).astype(BF16)
    cw = MXU_WIDTH

    ang_t = freq_ref[...] * pos_ref[0]
    cs_t = jnp.concatenate([jnp.cos(ang_t), jnp.sin(ang_t), jnp.ones_like(ang_t),
                            jnp.zeros_like(ang_t)], axis=0)
    contract0 = (((0,), (0,)), ((), ()))
    cs_hi, cs_lo = _split_bf16(cs_t)
    tab = (lax.dot_general(cs_hi, expand_ref[...], contract0, preferred_element_type=F32)
           + lax.dot_general(cs_lo, expand_ref[...], contract0, preferred_element_type=F32))
    cos = tab[:, 0:LANES]
    sa = tab[:, LANES:2 * LANES]
    sb = tab[:, 2 * LANES:3 * LANES]

    def stage(c):
        cols = slice(c * cw, (c + 1) * cw)
        proj_ref[:, cols] = jnp.dot(xn, w_ref[:, cols], preferred_element_type=F32)

    n_qk = qk_cols // cw
    for c in range(n_qk):
        stage(c)
    for c in range(n_qk):
        p = proj_ref[:, c * cw:(c + 1) * cw]
        ss_ref[:, c * cw:(c + 1) * cw] = jnp.dot((p * p).astype(BF16), ones_ref[...],
                                                 preferred_element_type=F32)
    for c in range(n_qk, w_ref.shape[1] // cw):
        stage(c)

    def proj(col0):
        return proj_ref[:, col0:col0 + cw]

    for c in range(qk_cols // cw):
        p = proj(c * cw)
        ss = ss_ref[:, c * cw:(c + 1) * cw]
        pn = p * lax.rsqrt(ss * (1.0 / head_dim) + RMS_EPS) * gqk_ref[:, c * cw:(c + 1) * cw]
        for hh in range(cw // LANES):
            blk = pn[:, hh * LANES:(hh + 1) * LANES]
            rot = (blk * cos
                   + pltpu.roll(blk, LANES - rope_half, 1) * sa
                   + pltpu.roll(blk, rope_half, 1) * sb)
            col = c * cw + hh * LANES
            qk_ref[:, col:col + LANES] = rot.astype(BF16)

    for c in range(v_cols // cw):
        v_ref[:, c * cw:(c + 1) * cw] = proj(qk_cols + c * cw).astype(BF16)

    for c in range(sgu_cols // cw):
        su_ref[:, c * cw:(c + 1) * cw] = _gelu(proj(qk_cols + v_cols + c * cw)).astype(BF16)

    sv = jnp.concatenate(
        [_gelu(proj(qk_cols + v_cols + sgu_cols + c * cw)) for c in range(sgu_cols // cw)], axis=1)
    mu = jnp.mean(sv, axis=-1, keepdims=True)
    svc = sv - mu
    var = jnp.mean(svc * svc, axis=-1, keepdims=True)
    sv_ref[...] = (svc * lax.rsqrt(var + RMS_EPS) * lng_ref[...] + lnb_ref[...]).astype(BF16)


def _inproj(x2d, norm_g, w_in, gqk, pos_rows, freq_col, expand, ones_blk, ln_g, ln_b, *, qk_cols, v_cols,
            sgu_cols, head_dim, rope_half):
    t, d = x2d.shape
    tm = ROW_TILE
    row = lambda i: (i, 0)
    fixed = lambda i: (0, 0)
    kern = functools.partial(_inproj_kernel, qk_cols=qk_cols, v_cols=v_cols, sgu_cols=sgu_cols,
                             head_dim=head_dim, rope_half=rope_half)
    return pl.pallas_call(
        kern,
        grid=(t // tm,),
        in_specs=[
            pl.BlockSpec((tm, d), row),
            pl.BlockSpec((1, d), fixed),
            pl.BlockSpec(w_in.shape, fixed),
            pl.BlockSpec((1, qk_cols), fixed),
            pl.BlockSpec((1, 1, tm), lambda i: (i, 0, 0)),
            pl.BlockSpec(freq_col.shape, fixed),
            pl.BlockSpec(expand.shape, fixed),
            pl.BlockSpec((MXU_WIDTH, MXU_WIDTH), fixed),
            pl.BlockSpec((1, sgu_cols), fixed),
            pl.BlockSpec((1, sgu_cols), fixed),
        ],
        out_specs=[
            pl.BlockSpec((tm, qk_cols), row),
            pl.BlockSpec((tm, v_cols), row),
            pl.BlockSpec((tm, sgu_cols), row),
            pl.BlockSpec((tm, sgu_cols), row),
        ],
        out_shape=[
            jax.ShapeDtypeStruct((t, qk_cols), BF16),
            jax.ShapeDtypeStruct((t, v_cols), BF16),
            jax.ShapeDtypeStruct((t, sgu_cols), BF16),
            jax.ShapeDtypeStruct((t, sgu_cols), BF16),
        ],
        scratch_shapes=[pltpu.VMEM((tm, w_in.shape[1]), F32), pltpu.VMEM((tm, qk_cols), F32)],
        compiler_params=_params(1),
        name="inproj",
    )(x2d, norm_g, w_in, gqk, pos_rows, freq_col, expand, ones_blk, ln_g, ln_b)


def _attn_kernel(shift_ref, lam_ref, gout_ref, q_ref, k_ref, v_ref, o_ref, qm_ref, m_ref, acc_ref, *, tq,
                 n_heads, lambda_init, head_dim):
    qi = pl.program_id(1)
    hw = 2 * head_dim
    vd = v_ref.shape[-1] // n_heads
    lv = lam_ref[...]
    lam = (jnp.exp(jnp.sum(lv[0:1] * lv[1:2], axis=-1, keepdims=True))
           - jnp.exp(jnp.sum(lv[2:3] * lv[3:4], axis=-1, keepdims=True))
           + lambda_init)

    first = lax.broadcasted_iota(jnp.int32, (1, hw), 1) < head_dim
    for h in range(n_heads):
        q = q_ref[0, :, h * hw:(h + 1) * hw]
        zero = jnp.zeros_like(q)
        qm_ref[2 * h] = jnp.where(first, q, zero)
        qm_ref[2 * h + 1] = jnp.where(first, zero, q)
    acc_ref[...] = jnp.zeros(acc_ref.shape, F32)

    ones = jnp.ones((tq, vd), BF16)
    causal = (lax.broadcasted_iota(jnp.int32, (tq, tq), 1)
              <= lax.broadcasted_iota(jnp.int32, (tq, tq), 0))

    def kv_tile(j, h):
        start = pl.multiple_of(j * tq, tq)
        kc = k_ref[0, pl.ds(start, tq), h * hw:(h + 1) * hw]
        vext = jnp.concatenate([v_ref[0, pl.ds(start, tq), h * vd:(h + 1) * vd], ones], axis=1)
        return kc, vext

    def scores(c, kc, masked):
        s = lax.dot_general(qm_ref[c], kc, (((1,), (1,)), ((), ())), preferred_element_type=F32)
        return jnp.where(causal, s, NEG_INF) if masked else s

    def plain_step(tiles):
        for h in range(n_heads):
            kvs = [kv_tile(j, h) for j, _ in tiles]
            for mp in range(2):
                c = 2 * h + mp
                pv = None
                for (kc, vext), (_, masked) in zip(kvs, tiles):
                    part = jnp.dot(jnp.exp2(scores(c, kc, masked)).astype(BF16), vext,
                                   preferred_element_type=F32)
                    pv = part if pv is None else pv + part
                acc_ref[c] += pv

    def running_max_step(j, masked):
        for h in range(n_heads):
            kc, vext = kv_tile(j, h)
            for mp in range(2):
                c = 2 * h + mp
                s = scores(c, kc, masked)
                m_old = m_ref[c]
                m_new = jnp.maximum(m_old, jnp.max(s, axis=-1, keepdims=True))
                p = jnp.exp2(s - m_new).astype(BF16)
                acc_ref[c] = (jnp.exp2(m_old - m_new) * acc_ref[c]
                              + jnp.dot(p, vext, preferred_element_type=F32))
                m_ref[c] = m_new

    @pl.when(shift_ref[0] == 0)
    def _():
        def pair(jj, carry):
            plain_step([(2 * jj, False), (2 * jj + 1, False)])
            return carry

        lax.fori_loop(0, qi // 2, pair, 0)

        @pl.when(qi % 2 == 1)
        def _():
            plain_step([(qi - 1, False), (qi, True)])

        @pl.when(qi % 2 == 0)
        def _():
            plain_step([(qi, True)])

    @pl.when(shift_ref[0] != 0)
    def _():
        m_ref[...] = jnp.full(m_ref.shape, NEG_INF, F32)

        def single(j, carry):
            running_max_step(j, False)
            return carry

        lax.fori_loop(0, qi, single, 0)
        running_max_step(qi, True)

    for h in range(n_heads):
        a1 = acc_ref[2 * h]
        a2 = acc_ref[2 * h + 1]
        att = a1[:, :vd] / a1[:, vd:] - lam * (a2[:, :vd] / a2[:, vd:])
        ms = jnp.mean(att * att, axis=-1, keepdims=True)
        o_ref[0, :, h * vd:(h + 1) * vd] = (
            att * lax.rsqrt(ms + RMS_EPS) * gout_ref[...] * (1.0 - lambda_init)).astype(BF16)


def _attention(need_shift, lam_vecs, gout, qk3, v3, *, n_heads, head_dim, lambda_init):
    b, s, _ = qk3.shape
    att_cols = v3.shape[-1]
    vd = att_cols // n_heads
    tq = QBLOCK
    kern = functools.partial(_attn_kernel, tq=tq, n_heads=n_heads, lambda_init=lambda_init,
                             head_dim=head_dim)
    grid_spec = pltpu.PrefetchScalarGridSpec(
        num_scalar_prefetch=1,
        grid=(b, s // tq),
        in_specs=[
            pl.BlockSpec(lam_vecs.shape, lambda bi, qi, f: (0, 0)),
            pl.BlockSpec((1, vd), lambda bi, qi, f: (0, 0)),
            pl.BlockSpec((1, tq, att_cols), lambda bi, qi, f: (bi, qi, 0)),
            pl.BlockSpec((1, s, att_cols), lambda bi, qi, f: (bi, 0, 1)),
            pl.BlockSpec((1, s, att_cols), lambda bi, qi, f: (bi, 0, 0)),
        ],
        out_specs=pl.BlockSpec((1, tq, att_cols), lambda bi, qi, f: (bi, qi, 0)),
        scratch_shapes=[
            pltpu.VMEM((2 * n_heads, tq, 2 * head_dim), BF16),
            pltpu.VMEM((2 * n_heads, tq, 1), F32),
            pltpu.VMEM((2 * n_heads, tq, 2 * vd), F32),
        ],
    )
    return pl.pallas_call(
        kern,
        grid_spec=grid_spec,
        out_shape=jax.ShapeDtypeStruct(v3.shape, BF16),
        compiler_params=_params(2),
        name="diff_attn",
    )(need_shift, lam_vecs, gout, qk3, qk3, v3)


def _mixout_kernel(att_ref, su_ref, sv_ref, x_ref, sw_ref, sbt_ref, sgn_ref, wo_ref, gffn_ref,
                   rwh_ref, rwl_ref, rb_ref, lstrict_ref, h_ref, hn_ref, route_ref, counts_ref, cnt_ref,
                   hn_prev_ref, *, n_sgu_groups, att_cols):
    step = pl.program_id(0)
    tm = x_ref.shape[0]
    gw = CHUNK
    r_id = lax.broadcasted_iota(jnp.int32, (gw, gw), 0)
    c_id = lax.broadcasted_iota(jnp.int32, (gw, gw), 1)
    tril = c_id <= r_id

    @pl.when(step == 0)
    def _():
        cnt_ref[...] = jnp.zeros_like(cnt_ref)
        hn_prev_ref[...] = jnp.zeros_like(hn_prev_ref)

    _route_tile(hn_prev_ref[...], (step > 0).astype(F32), rwh_ref, rwl_ref, rb_ref, lstrict_ref,
                route_ref, counts_ref, cnt_ref)

    acc = jnp.dot(att_ref[...], wo_ref[0:att_cols, :], preferred_element_type=F32)

    sg_cols = []
    for g in range(n_sgu_groups):
        w = jnp.where(tril, sw_ref[g], 0.0).astype(BF16)
        bias = sbt_ref[:, g:g + 1]
        gn = sgn_ref[g:g + 1, :]
        rows = []
        for c in range(tm // gw):
            vblk = sv_ref[c * gw:(c + 1) * gw, g * gw:(g + 1) * gw]
            s = jnp.dot(w, vblk, preferred_element_type=F32) + bias
            sg = su_ref[c * gw:(c + 1) * gw, g * gw:(g + 1) * gw].astype(F32) * s
            ms = jnp.mean(sg * sg, axis=-1, keepdims=True)
            rows.append((sg * lax.rsqrt(ms + RMS_EPS) * gn).astype(BF16))
        sg_cols.append(jnp.concatenate(rows, axis=0))
    sgn = jnp.concatenate(sg_cols, axis=1)
    acc = acc + jnp.dot(sgn, wo_ref[att_cols:, :], preferred_element_type=F32)

    h = x_ref[...] + acc
    h_ref[...] = h
    ms = jnp.mean(h * h, axis=-1, keepdims=True)
    hn = h * lax.rsqrt(ms + RMS_EPS) * gffn_ref[...]
    hn_ref[...] = hn
    hn_prev_ref[...] = hn


def _route_tile(hn, live, rwh_ref, rwl_ref, rb_ref, lstrict_ref, route_ref, counts_ref, cnt_ref):
    hn_bf, hn_lo = _split_bf16(hn)
    logits = (jnp.dot(hn_bf, rwh_ref[...], preferred_element_type=F32)
              + jnp.dot(hn_lo, rwh_ref[...], preferred_element_type=F32)
              + jnp.dot(hn_bf, rwl_ref[...], preferred_element_type=F32)
              + rb_ref[...])

    lane = lax.broadcasted_iota(jnp.int32, logits.shape, 1).astype(F32)
    big = float(LANES)
    is_grp = lane < N_GROUPS
    gl = jnp.where(is_grp, logits, NEG_INF)
    gmax = jnp.max(gl, axis=-1, keepdims=True)
    gidx = jnp.min(jnp.where(gl == gmax, lane, big), axis=-1, keepdims=True)
    psum = jnp.sum(jnp.where(is_grp, jnp.exp(gl - gmax), 0.0), axis=-1, keepdims=True)
    p_grp = 1.0 / psum
    e_lo = N_GROUPS + gidx * EXPERTS_PER_GROUP
    in_grp = (lane >= e_lo) & (lane < e_lo + EXPERTS_PER_GROUP)
    el = jnp.where(in_grp, logits, NEG_INF)
    t1 = jnp.max(el, axis=-1, keepdims=True)
    i1 = jnp.min(jnp.where(el == t1, lane, big), axis=-1, keepdims=True)
    el2 = jnp.where(lane == i1, NEG_INF, el)
    t2 = jnp.max(el2, axis=-1, keepdims=True)
    i2 = jnp.min(jnp.where(el2 == t2, lane, big), axis=-1, keepdims=True)
    e21 = jnp.exp(t2 - t1)
    den = 1.0 + e21
    g1 = p_grp / den
    g2 = p_grp * (e21 / den)
    e1 = i1 - N_GROUPS
    e2 = i2 - N_GROUPS

    hit1 = lane == e1
    hit2 = lane == e2
    onehot = jnp.where(hit1 | hit2, live, 0.0)
    before = jnp.dot(lstrict_ref[...], onehot.astype(BF16), preferred_element_type=F32) + cnt_ref[...]
    r1 = jnp.sum(jnp.where(hit1, before, 0.0), axis=-1, keepdims=True)
    r2 = jnp.sum(jnp.where(hit2, before, 0.0), axis=-1, keepdims=True)
    cnt_ref[...] = cnt_ref[...] + jnp.sum(onehot, axis=0, keepdims=True)
    counts_ref[...] = cnt_ref[...]

    route = jnp.zeros_like(logits)
    for k, val in enumerate((e1, e2, g1, g2, r1, r2)):
        route = jnp.where(lane == k, val, route)
    route_ref[...] = route


def _mixout(att, su, sv, x2d, sgu_w, sgu_bt, sgu_gn, w_out, g_ffn, rw_hi, rw_lo, rb, lstrict, *,
            att_cols):
    t, d = x2d.shape
    tm = ROW_TILE
    n = t // tm
    row = lambda i: (jnp.minimum(i, n - 1), 0)
    routed = lambda i: (jnp.maximum(i - 1, 0), 0)
    fixed2 = lambda i: (0, 0)
    n_groups = sgu_w.shape[0]
    kern = functools.partial(_mixout_kernel, n_sgu_groups=n_groups, att_cols=att_cols)
    return pl.pallas_call(
        kern,
        grid=(n + 1,),
        in_specs=[
            pl.BlockSpec((tm, att.shape[1]), row),
            pl.BlockSpec((tm, su.shape[1]), row),
            pl.BlockSpec((tm, sv.shape[1]), row),
            pl.BlockSpec((tm, d), row),
            pl.BlockSpec(sgu_w.shape, lambda i: (0, 0, 0)),
            pl.BlockSpec(sgu_bt.shape, fixed2),
            pl.BlockSpec(sgu_gn.shape, fixed2),
            pl.BlockSpec(w_out.shape, fixed2),
            pl.BlockSpec((1, d), fixed2),
            pl.BlockSpec(rw_hi.shape, fixed2),
            pl.BlockSpec(rw_lo.shape, fixed2),
            pl.BlockSpec((1, LANES), fixed2),
            pl.BlockSpec((tm, tm), fixed2),
        ],
        out_specs=[
            pl.BlockSpec((tm, d), row),
            pl.BlockSpec((tm, d), row),
            pl.BlockSpec((tm, LANES), routed),
            pl.BlockSpec((1, LANES), fixed2),
        ],
        out_shape=[
            jax.ShapeDtypeStruct((t, d), F32),
            jax.ShapeDtypeStruct((t, d), F32),
            jax.ShapeDtypeStruct((t, LANES), F32),
            jax.ShapeDtypeStruct((1, LANES), F32),
        ],
        scratch_shapes=[pltpu.VMEM((1, LANES), F32), pltpu.VMEM((tm, d), F32)],
        compiler_params=_params(1),
        name="mixout",
    )(att, su, sv, x2d, sgu_w, sgu_bt, sgu_gn, w_out, g_ffn, rw_hi, rw_lo, rb, lstrict)


def _dispatch_kernel(*refs):
    dest_refs, (hn_ref, xs_ref, sem) = refs[:TOP_K], refs[TOP_K:]
    rows = hn_ref.shape[0]

    def issue(r, c):
        for k in range(TOP_K):
            pltpu.make_async_copy(hn_ref.at[pl.ds(r, 1), :],
                                  xs_ref.at[pl.ds(dest_refs[k][r], 1), :], sem).start(priority=k)
        return c

    lax.fori_loop(0, rows, issue, 0, unroll=SUBLANES)
    for k in range(TOP_K):
        pltpu.make_async_copy(hn_ref, xs_ref.at[pl.ds(0, rows), :], sem).wait()


def _dispatch(hn, dests):
    t, d = hn.shape
    rows = ROW_TILE
    index_spec = pl.BlockSpec((rows,), lambda i: (i,), memory_space=pltpu.SMEM)
    return pl.pallas_call(
        _dispatch_kernel,
        grid=(t // rows,),
        in_specs=[index_spec] * TOP_K + [pl.BlockSpec((rows, d), lambda i: (i, 0))],
        out_specs=pl.BlockSpec(memory_space=pl.ANY),
        out_shape=jax.ShapeDtypeStruct((t * TOP_K, d), hn.dtype),
        scratch_shapes=[pltpu.SemaphoreType.DMA],
        compiler_params=_params(1),
        name="dispatch",
    )(*dests, hn)


def _experts_kernel(vt_ref, ve_ref, vf_ref, vn_ref, lo_ref, hi_ref, xs_ref, wg_ref, wu_ref, wd_ref, ys_ref,
                    wg_bf, wu_bf, wd_bf):
    v = pl.program_id(0)
    flag = vf_ref[v]
    tm = xs_ref.shape[0]

    @pl.when(vn_ref[v] == 1)
    def _():
        wg_bf[...] = wg_ref[0].astype(BF16)
        wu_bf[...] = wu_ref[0].astype(BF16)
        wd_bf[...] = wd_ref[0].astype(BF16)

    @pl.when(flag > 0)
    def _():
        e = ve_ref[v]
        base = vt_ref[v] * tm
        rid = lax.broadcasted_iota(jnp.int32, (tm, 1), 0) + base
        mine = (rid >= lo_ref[e]) & (rid < hi_ref[e])
        x = xs_ref[...].astype(BF16)
        hg = jnp.dot(x, wg_bf[...], preferred_element_type=F32)
        hu = jnp.dot(x, wu_bf[...], preferred_element_type=F32)
        a = (hg * (1.0 / (1.0 + jnp.exp(-hg))) * hu).astype(BF16)

        y = jnp.dot(a, wd_bf[...], preferred_element_type=F32)

        @pl.when(flag == 1)
        def _():
            ys_ref[...] = jnp.where(mine, y, 0.0)

        @pl.when(flag == 2)
        def _():
            ys_ref[...] = jnp.where(mine, y, ys_ref[...])


def _experts(visit_tile, visit_expert, visit_flag, visit_new, row_lo, row_hi, xs, w_gate, w_up, w_down):
    p, d = xs.shape
    tm = EXPERT_TILE
    f = w_gate.shape[-1]
    rows_map = lambda v, vt, ve, vf, vn, lo, hi: (vt[v], 0)
    w_map = lambda v, vt, ve, vf, vn, lo, hi: (ve[v], 0, 0)
    grid_spec = pltpu.PrefetchScalarGridSpec(
        num_scalar_prefetch=6,
        grid=(visit_tile.shape[0],),
        in_specs=[
            pl.BlockSpec((tm, d), rows_map),
            pl.BlockSpec((1, d, f), w_map),
            pl.BlockSpec((1, d, f), w_map),
            pl.BlockSpec((1, f, d), w_map),
        ],
        out_specs=pl.BlockSpec((tm, d), rows_map),
        scratch_shapes=[pltpu.VMEM((d, f), BF16), pltpu.VMEM((d, f), BF16), pltpu.VMEM((f, d), BF16)],
    )
    return pl.pallas_call(
        _experts_kernel,
        grid_spec=grid_spec,
        out_shape=jax.ShapeDtypeStruct(xs.shape, F32),
        compiler_params=_params(1),
        name="experts",
    )(visit_tile, visit_expert, visit_flag, visit_new, row_lo, row_hi, xs, w_gate, w_up, w_down)


def _combine_kernel(*refs):
    dest_refs, dest_next_refs = refs[:TOP_K], refs[TOP_K:2 * TOP_K]
    h_ref, route_ref, ys_ref, o_ref, ybuf, sem = refs[2 * TOP_K:]
    i = pl.program_id(0)
    rows = h_ref.shape[0]
    slot = i % 2

    def issue(idx_refs, into):
        def body(r, c):
            for k in range(TOP_K):
                pltpu.make_async_copy(ys_ref.at[pl.ds(idx_refs[k][r], 1), :],
                                      ybuf.at[into, k, pl.ds(r, 1), :], sem.at[into]).start(priority=k)
            return c

        lax.fori_loop(0, rows, body, 0, unroll=SUBLANES)

    @pl.when(i == 0)
    def _():
        issue(dest_refs, 0)

    @pl.when(i + 1 < pl.num_programs(0))
    def _():
        issue(dest_next_refs, 1 - slot)

    for k in range(TOP_K):
        pltpu.make_async_copy(ys_ref.at[pl.ds(0, rows), :], ybuf.at[slot, k], sem.at[slot]).wait()

    route = route_ref[...]
    moe = None
    for k in range(TOP_K):
        gated = route[:, ROUTE_GATE + k:ROUTE_GATE + k + 1] * ybuf[slot, k]
        moe = gated if moe is None else moe + gated
    o_ref[...] = h_ref[...] + moe


def _combine(dests, h, route, ys):
    t, d = h.shape
    rows = COMBINE_TILE
    n = t // rows
    this_step = pl.BlockSpec((rows,), lambda i: (i,), memory_space=pltpu.SMEM)
    next_step = pl.BlockSpec((rows,), lambda i: (jnp.minimum(i + 1, n - 1),), memory_space=pltpu.SMEM)
    return pl.pallas_call(
        _combine_kernel,
        grid=(n,),
        in_specs=[this_step] * TOP_K + [next_step] * TOP_K + [
            pl.BlockSpec((rows, d), lambda i: (i, 0)),
            pl.BlockSpec((rows, LANES), lambda i: (i, 0)),
            pl.BlockSpec(memory_space=pl.ANY),
        ],
        out_specs=pl.BlockSpec((rows, d), lambda i: (i, 0)),
        out_shape=jax.ShapeDtypeStruct((t, d), F32),
        scratch_shapes=[pltpu.VMEM((2, TOP_K, rows, d), F32), pltpu.SemaphoreType.DMA((2,))],
        compiler_params=_params(1),
        name="combine",
    )(*dests, *dests, h, route, ys)


def _plan(route, counts_row):
    t = route.shape[0]
    counts = counts_row[0, :N_EXPERTS].astype(I32)
    ends = jnp.cumsum(counts)
    offs = ends - counts
    route_t = route.T
    experts = route_t[ROUTE_EXPERT:ROUTE_EXPERT + TOP_K].astype(I32)
    rank = route_t[ROUTE_RANK:ROUTE_RANK + TOP_K].astype(I32)
    dest = jnp.take(offs, experts) + rank
    dests = tuple(dest[k] for k in range(TOP_K))

    tm = EXPERT_TILE
    n_visits = (t * TOP_K) // tm + N_EXPERTS - 1
    first_tile = offs // tm
    last_tile = (ends - 1) // tm
    nvis = jnp.where(counts > 0, last_tile - first_tile + 1, 0)
    vend = jnp.cumsum(nvis)
    vstart = vend - nvis
    v = jnp.arange(n_visits, dtype=I32)
    ev = jnp.minimum(jnp.sum((vend[None, :] <= v[:, None]).astype(I32), axis=1), N_EXPERTS - 1)
    tv = jnp.take(first_tile, ev) + (v - jnp.take(vstart, ev))
    valid = v < vend[-1]
    last_v = vend[-1] - 1
    ev = jnp.where(valid, ev, jnp.take(ev, last_v))
    tv = jnp.where(valid, tv, jnp.take(tv, last_v))
    prev_t = jnp.concatenate([jnp.full((1,), -1, I32), tv[:-1]])
    prev_e = jnp.concatenate([jnp.full((1,), -1, I32), ev[:-1]])
    flag = jnp.where(valid, jnp.where(tv != prev_t, 1, 2), 0).astype(I32)
    new_expert = (ev != prev_e).astype(I32)
    return dests, tv.astype(I32), ev.astype(I32), flag, new_expert, offs.astype(I32), ends.astype(I32)


def _rope_constants(head_dim, rope_dim):
    rope_half = rope_dim // 2
    inv_freq = ROPE_THETA ** (-jnp.arange(0, rope_dim, 2, dtype=F32) / rope_dim)
    dim = jnp.arange(LANES, dtype=I32)[None, :] % head_dim
    k = jnp.arange(4 * rope_half, dtype=I32)[:, None]
    c_mat = jnp.where(dim < rope_dim, k == dim % rope_half, k == rope_dim)
    sa_mat = (dim < rope_half) & (k == rope_half + dim)
    sb_mat = (dim >= rope_half) & (dim < rope_dim) & (k == dim)
    expand = jnp.concatenate([c_mat.astype(F32), -sa_mat.astype(F32), sb_mat.astype(F32)], axis=1)
    return inv_freq.reshape(rope_half, 1), expand.astype(BF16)


def kernel(x, positions, norm_mix_g, w_in, q_norm_g, k_norm_g, lambda_q1, lambda_k1, lambda_q2,
           lambda_k2, diff_out_norm_g, sgu_ln_g, sgu_ln_b, sgu_w, sgu_b, sgu_out_norm_g, w_out,
           norm_ffn_g, router_group_w, router_group_b, router_expert_w, router_expert_b,
           expert_w_gate, expert_w_up, expert_w_down):
    b, s, d = x.shape
    t = b * s
    depth = w_in.shape[0]
    head_dim = q_norm_g.shape[-1]
    vd = diff_out_norm_g.shape[-1]
    n_sgu_groups = sgu_w.shape[1]
    sgu_cols = n_sgu_groups * sgu_w.shape[-1]
    in_cols = w_in.shape[-1]
    att_cols = (in_cols - 2 * sgu_cols) // 3
    n_heads = att_cols // vd
    qk_cols = 2 * att_cols
    rope_dim = head_dim // ROPE_FRACTION
    rope_half = rope_dim // 2
    scale = head_dim ** -0.5

    pos_rows = positions.reshape(t // ROW_TILE, 1, ROW_TILE).astype(F32)
    freq_col, expand = _rope_constants(head_dim, rope_dim)
    blk = jnp.arange(MXU_WIDTH, dtype=I32) // head_dim
    ones_blk = (blk[:, None] == blk[None, :]).astype(BF16)
    tri = jnp.arange(ROW_TILE, dtype=I32)
    lstrict = (tri[None, :] < tri[:, None]).astype(BF16)

    h = x.reshape(t, d)
    for l in range(depth):
        lambda_init = 0.8 - 0.6 * math.exp(-0.3 * l)
        gqk = jnp.concatenate([jnp.tile(q_norm_g[l] * (scale * math.log2(math.e)), att_cols // head_dim),
                               jnp.tile(k_norm_g[l], att_cols // head_dim)]).reshape(1, qk_cols)
        qk, v, su, sv = _inproj(
            h, norm_mix_g[l].reshape(1, d), w_in[l].astype(BF16), gqk, pos_rows, freq_col, expand, ones_blk,
            sgu_ln_g[l].reshape(1, sgu_cols), sgu_ln_b[l].reshape(1, sgu_cols),
            qk_cols=qk_cols, v_cols=att_cols, sgu_cols=sgu_cols, head_dim=head_dim,
            rope_half=rope_half)

        lam_vecs = jnp.stack([lambda_q1[l], lambda_k1[l], lambda_q2[l], lambda_k2[l]]).astype(F32)
        score_bound = (1.02 * head_dim * scale * jnp.max(jnp.abs(q_norm_g[l]))
                       * jnp.max(jnp.abs(k_norm_g[l])))
        need_shift = (score_bound > MAX_UNSHIFTED_SCORE).astype(I32).reshape(1)
        att = _attention(need_shift, lam_vecs, diff_out_norm_g[l].reshape(1, vd), qk.reshape(b, s, qk_cols),
                         v.reshape(b, s, att_cols), n_heads=n_heads, head_dim=head_dim,
                         lambda_init=lambda_init).reshape(t, att_cols)

        rw = jnp.concatenate([router_group_w[l], router_expert_w[l]], axis=1)
        rw = jnp.pad(rw, ((0, 0), (0, LANES - rw.shape[1])))
        rw_hi, rw_lo = _split_bf16(rw)
        rb = jnp.pad(jnp.concatenate([router_group_b[l], router_expert_b[l]]),
                     (0, LANES - N_GROUPS - N_EXPERTS)).reshape(1, LANES)
        h, hn, route, counts_row = _mixout(
            att, su, sv, h, sgu_w[l], sgu_b[l].T, sgu_out_norm_g[l], w_out[l].astype(BF16),
            norm_ffn_g[l].reshape(1, d), rw_hi, rw_lo, rb, lstrict, att_cols=att_cols)

        dests, visit_tile, visit_expert, visit_flag, visit_new, row_lo, row_hi = _plan(route, counts_row)
        xs = _dispatch(hn, dests)
        ys = _experts(visit_tile, visit_expert, visit_flag, visit_new, row_lo, row_hi, xs,
                      expert_w_gate[l], expert_w_up[l], expert_w_down[l])
        h = _combine(dests, h, route, ys)
    return h.reshape(b, s, d)
```

```python
import functools
import math

import jax
import jax.numpy as jnp
from jax import lax
from jax.experimental import pallas as pl
from jax.experimental.pallas import tpu as pltpu

F32 = jnp.float32
BF16 = jnp.bfloat16
I32 = jnp.int32

RMS_EPS = 1e-6
NEG_INF = -1e30
ROPE_THETA = 500000.0
MAX_UNSHIFTED_SCORE = 40.0

LANES = 128
SUBLANES = 8
MXU_WIDTH = 256
VMEM_LIMIT_BYTES = 56 * 1024 * 1024

N_GROUPS = 4
EXPERTS_PER_GROUP = 8
N_EXPERTS = N_GROUPS * EXPERTS_PER_GROUP
TOP_K = 2
ROPE_FRACTION = 4
CHUNK = 128
QBLOCK = 256
ROW_TILE = 512
EXPERT_TILE = 512
COMBINE_TILE = 256

ROUTE_EXPERT = 0
ROUTE_GATE = 2
ROUTE_RANK = 4


def _params(n_grid_dims):
    return pltpu.CompilerParams(
        dimension_semantics=("arbitrary",) * n_grid_dims,
        vmem_limit_bytes=VMEM_LIMIT_BYTES,
    )


def _gelu(x):
    return 0.5 * x * (1.0 + lax.erf(x * (1.0 / math.sqrt(2.0))))


def _split_bf16(x):
    hi = x.astype(BF16)
    return hi, (x - hi.astype(F32)).astype(BF16)


U32 = jnp.uint32
HIGH_HALF = 0xFFFF0000


def _pack_bf16_pairs(x):
    n = x.shape[1] // 2
    hi = lax.bitcast_convert_type(x[:, :n].astype(BF16).astype(F32), U32)
    lo = lax.bitcast_convert_type(x[:, n:].astype(BF16).astype(F32), U32)
    return hi | (lo >> 16)


def _unpack_bf16_pairs(w):
    hi = lax.bitcast_convert_type(w & jnp.uint32(HIGH_HALF), F32)
    lo = lax.bitcast_convert_type(w << 16, F32)
    return jnp.concatenate([hi, lo], axis=1)


def _inproj_kernel(x_ref, g_ref, w_ref, gqk_ref, pos_ref, freq_ref, expand_ref, ones_ref, lng_ref, lnb_ref,
                   qk_ref, v_ref, su_ref, sv_ref, proj_ref, ss_ref, *, qk_cols, v_cols, sgu_cols, head_dim,
                   rope_half):
    x = x_ref[...]
    ms = jnp.mean(x * x, axis=-1, keepdims=True)
    xn = (x * lax.rsqrt(ms + RMS_EPS) * g_ref[...]).astype(BF16)
    cw = MXU_WIDTH

    ang_t = freq_ref[...] * pos_ref[0]
    cs_t = jnp.concatenate([jnp.cos(ang_t), jnp.sin(ang_t), jnp.ones_like(ang_t),
                            jnp.zeros_like(ang_t)], axis=0)
    contract0 = (((0,), (0,)), ((), ()))
    cs_hi, cs_lo = _split_bf16(cs_t)
    tab = (lax.dot_general(cs_hi, expand_ref[...], contract0, preferred_element_type=F32)
           + lax.dot_general(cs_lo, expand_ref[...], contract0, preferred_element_type=F32))
    cos = tab[:, 0:LANES]
    sa = tab[:, LANES:2 * LANES]
    sb = tab[:, 2 * LANES:3 * LANES]

    def stage(c):
        cols = slice(c * cw, (c + 1) * cw)
        proj_ref[:, cols] = jnp.dot(xn, w_ref[:, cols], preferred_element_type=F32)

    n_qk = qk_cols // cw
    for c in range(n_qk):
        stage(c)
    for c in range(n_qk):
        p = proj_ref[:, c * cw:(c + 1) * cw]
        ss_ref[:, c * cw:(c + 1) * cw] = jnp.dot((p * p).astype(BF16), ones_ref[...],
                                                 preferred_element_type=F32)
    for c in range(n_qk, w_ref.shape[1] // cw):
        stage(c)

    def proj(col0):
        return proj_ref[:, col0:col0 + cw]

    for c in range(qk_cols // cw):
        p = proj(c * cw)
        ss = ss_ref[:, c * cw:(c + 1) * cw]
        pn = p * lax.rsqrt(ss * (1.0 / head_dim) + RMS_EPS) * gqk_ref[:, c * cw:(c + 1) * cw]
        for hh in range(cw // LANES):
            blk = pn[:, hh * LANES:(hh + 1) * LANES]
            rot = (blk * cos
                   + pltpu.roll(blk, LANES - rope_half, 1) * sa
                   + pltpu.roll(blk, rope_half, 1) * sb)
            col = c * cw + hh * LANES
            qk_ref[:, col:col + LANES] = rot.astype(BF16)

    for c in range(v_cols // cw):
        v_ref[:, c * cw:(c + 1) * cw] = proj(qk_cols + c * cw).astype(BF16)

    for c in range(sgu_cols // cw):
        su_ref[:, c * cw:(c + 1) * cw] = _gelu(proj(qk_cols + v_cols + c * cw)).astype(BF16)

    sv = jnp.concatenate(
        [_gelu(proj(qk_cols + v_cols + sgu_cols + c * cw)) for c in range(sgu_cols // cw)], axis=1)
    mu = jnp.mean(sv, axis=-1, keepdims=True)
    svc = sv - mu
    var = jnp.mean(svc * svc, axis=-1, keepdims=True)
    sv_ref[...] = (svc * lax.rsqrt(var + RMS_EPS) * lng_ref[...] + lnb_ref[...]).astype(BF16)


def _inproj(x2d, norm_g, w_in, gqk, pos_rows, freq_col, expand, ones_blk, ln_g, ln_b, *, qk_cols, v_cols,
            sgu_cols, head_dim, rope_half):
    t, d = x2d.shape
    tm = ROW_TILE
    row = lambda i: (i, 0)
    fixed = lambda i: (0, 0)
    kern = functools.partial(_inproj_kernel, qk_cols=qk_cols, v_cols=v_cols, sgu_cols=sgu_cols,
                             head_dim=head_dim, rope_half=rope_half)
    return pl.pallas_call(
        kern,
        grid=(t // tm,),
        in_specs=[
            pl.BlockSpec((tm, d), row),
            pl.BlockSpec((1, d), fixed),
            pl.BlockSpec(w_in.shape, fixed),
            pl.BlockSpec((1, qk_cols), fixed),
            pl.BlockSpec((1, 1, tm), lambda i: (i, 0, 0)),
            pl.BlockSpec(freq_col.shape, fixed),
            pl.BlockSpec(expand.shape, fixed),
            pl.BlockSpec((MXU_WIDTH, MXU_WIDTH), fixed),
            pl.BlockSpec((1, sgu_cols), fixed),
            pl.BlockSpec((1, sgu_cols), fixed),
        ],
        out_specs=[
            pl.BlockSpec((tm, qk_cols), row),
            pl.BlockSpec((tm, v_cols), row),
            pl.BlockSpec((tm, sgu_cols), row),
            pl.BlockSpec((tm, sgu_cols), row),
        ],
        out_shape=[
            jax.ShapeDtypeStruct((t, qk_cols), BF16),
            jax.ShapeDtypeStruct((t, v_cols), BF16),
            jax.ShapeDtypeStruct((t, sgu_cols), BF16),
            jax.ShapeDtypeStruct((t, sgu_cols), BF16),
        ],
        scratch_shapes=[pltpu.VMEM((tm, w_in.shape[1]), F32), pltpu.VMEM((tm, qk_cols), F32)],
        compiler_params=_params(1),
        name="inproj",
    )(x2d, norm_g, w_in, gqk, pos_rows, freq_col, expand, ones_blk, ln_g, ln_b)


def _attn_kernel(shift_ref, lam_ref, gout_ref, q_ref, k_ref, v_ref, o_ref, qm_ref, m_ref, acc_ref, *, tq,
                 n_heads, lambda_init, head_dim):
    qi = pl.program_id(1)
    hw = 2 * head_dim
    vd = v_ref.shape[-1] // n_heads
    lv = lam_ref[...]
    lam = (jnp.exp(jnp.sum(lv[0:1] * lv[1:2], axis=-1, keepdims=True))
           - jnp.exp(jnp.sum(lv[2:3] * lv[3:4], axis=-1, keepdims=True))
           + lambda_init)

    first = lax.broadcasted_iota(jnp.int32, (1, hw), 1) < head_dim
    for h in range(n_heads):
        q = q_ref[0, :, h * hw:(h + 1) * hw]
        zero = jnp.zeros_like(q)
        qm_ref[2 * h] = jnp.where(first, q, zero)
        qm_ref[2 * h + 1] = jnp.where(first, zero, q)
    acc_ref[...] = jnp.zeros(acc_ref.shape, F32)

    ones = jnp.ones((tq, vd), BF16)
    causal = (lax.broadcasted_iota(jnp.int32, (tq, tq), 1)
              <= lax.broadcasted_iota(jnp.int32, (tq, tq), 0))

    def kv_tile(j, h):
        start = pl.multiple_of(j * tq, tq)
        kc = k_ref[0, pl.ds(start, tq), h * hw:(h + 1) * hw]
        vext = jnp.concatenate([v_ref[0, pl.ds(start, tq), h * vd:(h + 1) * vd], ones], axis=1)
        return kc, vext

    def scores(c, kc, masked):
        s = lax.dot_general(qm_ref[c], kc, (((1,), (1,)), ((), ())), preferred_element_type=F32)
        return jnp.where(causal, s, NEG_INF) if masked else s

    def plain_step(tiles):
        for h in range(n_heads):
            kvs = [kv_tile(j, h) for j, _ in tiles]
            for mp in range(2):
                c = 2 * h + mp
                pv = None
                for (kc, vext), (_, masked) in zip(kvs, tiles):
                    part = jnp.dot(jnp.exp2(scores(c, kc, masked)).astype(BF16), vext,
                                   preferred_element_type=F32)
                    pv = part if pv is None else pv + part
                acc_ref[c] += pv

    def running_max_step(j, masked):
        for h in range(n_heads):
            kc, vext = kv_tile(j, h)
            for mp in range(2):
                c = 2 * h + mp
                s = scores(c, kc, masked)
                m_old = m_ref[c]
                m_new = jnp.maximum(m_old, jnp.max(s, axis=-1, keepdims=True))
                p = jnp.exp2(s - m_new).astype(BF16)
                acc_ref[c] = (jnp.exp2(m_old - m_new) * acc_ref[c]
                              + jnp.dot(p, vext, preferred_element_type=F32))
                m_ref[c] = m_new

    @pl.when(shift_ref[0] == 0)
    def _():
        def pair(jj, carry):
            plain_step([(2 * jj, False), (2 * jj + 1, False)])
            return carry

        lax.fori_loop(0, qi // 2, pair, 0)

        @pl.when(qi % 2 == 1)
        def _():
            plain_step([(qi - 1, False), (qi, True)])

        @pl.when(qi % 2 == 0)
        def _():
            plain_step([(qi, True)])

    @pl.when(shift_ref[0] != 0)
    def _():
        m_ref[...] = jnp.full(m_ref.shape, NEG_INF, F32)

        def single(j, carry):
            running_max_step(j, False)
            return carry

        lax.fori_loop(0, qi, single, 0)
        running_max_step(qi, True)

    for h in range(n_heads):
        a1 = acc_ref[2 * h]
        a2 = acc_ref[2 * h + 1]
        att = a1[:, :vd] / a1[:, vd:] - lam * (a2[:, :vd] / a2[:, vd:])
        ms = jnp.mean(att * att, axis=-1, keepdims=True)
        o_ref[0, :, h * vd:(h + 1) * vd] = (
            att * lax.rsqrt(ms + RMS_EPS) * gout_ref[...] * (1.0 - lambda_init)).astype(BF16)


def _attention(need_shift, lam_vecs, gout, qk3, v3, *, n_heads, head_dim, lambda_init):
    b, s, _ = qk3.shape
    att_cols = v3.shape[-1]
    vd = att_cols // n_heads
    tq = QBLOCK
    kern = functools.partial(_attn_kernel, tq=tq, n_heads=n_heads, lambda_init=lambda_init,
                             head_dim=head_dim)
    grid_spec = pltpu.PrefetchScalarGridSpec(
        num_scalar_prefetch=1,
        grid=(b, s // tq),
        in_specs=[
            pl.BlockSpec(lam_vecs.shape, lambda bi, qi, f: (0, 0)),
            pl.BlockSpec((1, vd), lambda bi, qi, f: (0, 0)),
            pl.BlockSpec((1, tq, att_cols), lambda bi, qi, f: (bi, qi, 0)),
            pl.BlockSpec((1, s, att_cols), lambda bi, qi, f: (bi, 0, 1)),
            pl.BlockSpec((1, s, att_cols), lambda bi, qi, f: (bi, 0, 0)),
        ],
        out_specs=pl.BlockSpec((1, tq, att_cols), lambda bi, qi, f: (bi, qi, 0)),
        scratch_shapes=[
            pltpu.VMEM((2 * n_heads, tq, 2 * head_dim), BF16),
            pltpu.VMEM((2 * n_heads, tq, 1), F32),
            pltpu.VMEM((2 * n_heads, tq, 2 * vd), F32),
        ],
    )
    return pl.pallas_call(
        kern,
        grid_spec=grid_spec,
        out_shape=jax.ShapeDtypeStruct(v3.shape, BF16),
        compiler_params=_params(2),
        name="diff_attn",
    )(need_shift, lam_vecs, gout, qk3, qk3, v3)


def _mixout_kernel(att_ref, su_ref, sv_ref, x_ref, sw_ref, sbt_ref, sgn_ref, wo_ref, gffn_ref,
                   rwh_ref, rwl_ref, rb_ref, lstrict_ref, h_ref, hn_ref, route_ref, counts_ref, cnt_ref,
                   hn_prev_ref, *, n_sgu_groups, att_cols):
    step = pl.program_id(0)
    tm = x_ref.shape[0]
    gw = CHUNK
    r_id = lax.broadcasted_iota(jnp.int32, (gw, gw), 0)
    c_id = lax.broadcasted_iota(jnp.int32, (gw, gw), 1)
    tril = c_id <= r_id

    @pl.when(step == 0)
    def _():
        cnt_ref[...] = jnp.zeros_like(cnt_ref)
        hn_prev_ref[...] = jnp.zeros_like(hn_prev_ref)

    _route_tile(hn_prev_ref[...], (step > 0).astype(F32), rwh_ref, rwl_ref, rb_ref, lstrict_ref,
                route_ref, counts_ref, cnt_ref)

    acc = jnp.dot(att_ref[...], wo_ref[0:att_cols, :], preferred_element_type=F32)

    sg_cols = []
    for g in range(n_sgu_groups):
        w = jnp.where(tril, sw_ref[g], 0.0).astype(BF16)
        bias = sbt_ref[:, g:g + 1]
        gn = sgn_ref[g:g + 1, :]
        rows = []
        for c in range(tm // gw):
            vblk = sv_ref[c * gw:(c + 1) * gw, g * gw:(g + 1) * gw]
            s = jnp.dot(w, vblk, preferred_element_type=F32) + bias
            sg = su_ref[c * gw:(c + 1) * gw, g * gw:(g + 1) * gw].astype(F32) * s
            ms = jnp.mean(sg * sg, axis=-1, keepdims=True)
            rows.append((sg * lax.rsqrt(ms + RMS_EPS) * gn).astype(BF16))
        sg_cols.append(jnp.concatenate(rows, axis=0))
    sgn = jnp.concatenate(sg_cols, axis=1)
    acc = acc + jnp.dot(sgn, wo_ref[att_cols:, :], preferred_element_type=F32)

    h = x_ref[...] + acc
    h_ref[...] = h
    ms = jnp.mean(h * h, axis=-1, keepdims=True)
    hn = h * lax.rsqrt(ms + RMS_EPS) * gffn_ref[...]
    hn_ref[...] = _pack_bf16_pairs(hn)
    hn_prev_ref[...] = hn


def _route_tile(hn, live, rwh_ref, rwl_ref, rb_ref, lstrict_ref, route_ref, counts_ref, cnt_ref):
    hn_bf, hn_lo = _split_bf16(hn)
    logits = (jnp.dot(hn_bf, rwh_ref[...], preferred_element_type=F32)
              + jnp.dot(hn_lo, rwh_ref[...], preferred_element_type=F32)
              + jnp.dot(hn_bf, rwl_ref[...], preferred_element_type=F32)
              + rb_ref[...])

    lane = lax.broadcasted_iota(jnp.int32, logits.shape, 1).astype(F32)
    big = float(LANES)
    is_grp = lane < N_GROUPS
    gl = jnp.where(is_grp, logits, NEG_INF)
    gmax = jnp.max(gl, axis=-1, keepdims=True)
    gidx = jnp.min(jnp.where(gl == gmax, lane, big), axis=-1, keepdims=True)
    psum = jnp.sum(jnp.where(is_grp, jnp.exp(gl - gmax), 0.0), axis=-1, keepdims=True)
    p_grp = 1.0 / psum
    e_lo = N_GROUPS + gidx * EXPERTS_PER_GROUP
    in_grp = (lane >= e_lo) & (lane < e_lo + EXPERTS_PER_GROUP)
    el = jnp.where(in_grp, logits, NEG_INF)
    t1 = jnp.max(el, axis=-1, keepdims=True)
    i1 = jnp.min(jnp.where(el == t1, lane, big), axis=-1, keepdims=True)
    el2 = jnp.where(lane == i1, NEG_INF, el)
    t2 = jnp.max(el2, axis=-1, keepdims=True)
    i2 = jnp.min(jnp.where(el2 == t2, lane, big), axis=-1, keepdims=True)
    e21 = jnp.exp(t2 - t1)
    den = 1.0 + e21
    g1 = p_grp / den
    g2 = p_grp * (e21 / den)
    e1 = i1 - N_GROUPS
    e2 = i2 - N_GROUPS

    hit1 = lane == e1
    hit2 = lane == e2
    onehot = jnp.where(hit1 | hit2, live, 0.0)
    before = jnp.dot(lstrict_ref[...], onehot.astype(BF16), preferred_element_type=F32) + cnt_ref[...]
    r1 = jnp.sum(jnp.where(hit1, before, 0.0), axis=-1, keepdims=True)
    r2 = jnp.sum(jnp.where(hit2, before, 0.0), axis=-1, keepdims=True)
    cnt_ref[...] = cnt_ref[...] + jnp.sum(onehot, axis=0, keepdims=True)
    counts_ref[...] = cnt_ref[...]

    route = jnp.zeros_like(logits)
    for k, val in enumerate((e1, e2, g1, g2, r1, r2)):
        route = jnp.where(lane == k, val, route)
    route_ref[...] = route


def _mixout(att, su, sv, x2d, sgu_w, sgu_bt, sgu_gn, w_out, g_ffn, rw_hi, rw_lo, rb, lstrict, *,
            att_cols):
    t, d = x2d.shape
    tm = ROW_TILE
    n = t // tm
    row = lambda i: (jnp.minimum(i, n - 1), 0)
    routed = lambda i: (jnp.maximum(i - 1, 0), 0)
    fixed2 = lambda i: (0, 0)
    n_groups = sgu_w.shape[0]
    kern = functools.partial(_mixout_kernel, n_sgu_groups=n_groups, att_cols=att_cols)
    return pl.pallas_call(
        kern,
        grid=(n + 1,),
        in_specs=[
            pl.BlockSpec((tm, att.shape[1]), row),
            pl.BlockSpec((tm, su.shape[1]), row),
            pl.BlockSpec((tm, sv.shape[1]), row),
            pl.BlockSpec((tm, d), row),
            pl.BlockSpec(sgu_w.shape, lambda i: (0, 0, 0)),
            pl.BlockSpec(sgu_bt.shape, fixed2),
            pl.BlockSpec(sgu_gn.shape, fixed2),
            pl.BlockSpec(w_out.shape, fixed2),
            pl.BlockSpec((1, d), fixed2),
            pl.BlockSpec(rw_hi.shape, fixed2),
            pl.BlockSpec(rw_lo.shape, fixed2),
            pl.BlockSpec((1, LANES), fixed2),
            pl.BlockSpec((tm, tm), fixed2),
        ],
        out_specs=[
            pl.BlockSpec((tm, d), row),
            pl.BlockSpec((tm, d // 2), row),
            pl.BlockSpec((tm, LANES), routed),
            pl.BlockSpec((1, LANES), fixed2),
        ],
        out_shape=[
            jax.ShapeDtypeStruct((t, d), F32),
            jax.ShapeDtypeStruct((t, d // 2), U32),
            jax.ShapeDtypeStruct((t, LANES), F32),
            jax.ShapeDtypeStruct((1, LANES), F32),
        ],
        scratch_shapes=[pltpu.VMEM((1, LANES), F32), pltpu.VMEM((tm, d), F32)],
        compiler_params=_params(1),
        name="mixout",
    )(att, su, sv, x2d, sgu_w, sgu_bt, sgu_gn, w_out, g_ffn, rw_hi, rw_lo, rb, lstrict)


def _dispatch_kernel(*refs):
    dest_refs, (hn_ref, xs_ref, sem) = refs[:TOP_K], refs[TOP_K:]
    rows = hn_ref.shape[0]

    def issue(r, c):
        for k in range(TOP_K):
            pltpu.make_async_copy(hn_ref.at[pl.ds(r, 1), :],
                                  xs_ref.at[pl.ds(dest_refs[k][r], 1), :], sem).start(priority=k)
        return c

    lax.fori_loop(0, rows, issue, 0, unroll=SUBLANES)
    for k in range(TOP_K):
        pltpu.make_async_copy(hn_ref, xs_ref.at[pl.ds(0, rows), :], sem).wait()


def _dispatch(hn, dests):
    t, d = hn.shape
    rows = ROW_TILE
    index_spec = pl.BlockSpec((rows,), lambda i: (i,), memory_space=pltpu.SMEM)
    return pl.pallas_call(
        _dispatch_kernel,
        grid=(t // rows,),
        in_specs=[index_spec] * TOP_K + [pl.BlockSpec((rows, d), lambda i: (i, 0))],
        out_specs=pl.BlockSpec(memory_space=pl.ANY),
        out_shape=jax.ShapeDtypeStruct((t * TOP_K, d), hn.dtype),
        scratch_shapes=[pltpu.SemaphoreType.DMA],
        compiler_params=_params(1),
        name="dispatch",
    )(*dests, hn)


def _experts_kernel(vt_ref, ve_ref, vf_ref, vn_ref, lo_ref, hi_ref, xs_ref, wg_ref, wu_ref, wd_ref, ys_ref,
                    wg_bf, wu_bf, wd_bf):
    v = pl.program_id(0)
    flag = vf_ref[v]
    tm = xs_ref.shape[0]

    @pl.when(vn_ref[v] == 1)
    def _():
        wg_bf[...] = wg_ref[0].astype(BF16)
        wu_bf[...] = wu_ref[0].astype(BF16)
        wd_bf[...] = wd_ref[0].astype(BF16)

    @pl.when(flag > 0)
    def _():
        e = ve_ref[v]
        base = vt_ref[v] * tm
        rid = lax.broadcasted_iota(jnp.int32, (tm, 1), 0) + base
        mine = (rid >= lo_ref[e]) & (rid < hi_ref[e])
        x = _unpack_bf16_pairs(xs_ref[...]).astype(BF16)
        hg = jnp.dot(x, wg_bf[...], preferred_element_type=F32)
        hu = jnp.dot(x, wu_bf[...], preferred_element_type=F32)
        a = (hg * (1.0 / (1.0 + jnp.exp(-hg))) * hu).astype(BF16)
        y = _pack_bf16_pairs(jnp.dot(a, wd_bf[...], preferred_element_type=F32))

        @pl.when(flag == 1)
        def _():
            ys_ref[...] = jnp.where(mine, y, jnp.zeros_like(y))

        @pl.when(flag == 2)
        def _():
            ys_ref[...] = jnp.where(mine, y, ys_ref[...])


def _experts(visit_tile, visit_expert, visit_flag, visit_new, row_lo, row_hi, xs, w_gate, w_up, w_down):
    p, words = xs.shape
    d = 2 * words
    tm = EXPERT_TILE
    f = w_gate.shape[-1]
    rows_map = lambda v, vt, ve, vf, vn, lo, hi: (vt[v], 0)
    w_map = lambda v, vt, ve, vf, vn, lo, hi: (ve[v], 0, 0)
    grid_spec = pltpu.PrefetchScalarGridSpec(
        num_scalar_prefetch=6,
        grid=(visit_tile.shape[0],),
        in_specs=[
            pl.BlockSpec((tm, words), rows_map),
            pl.BlockSpec((1, d, f), w_map),
            pl.BlockSpec((1, d, f), w_map),
            pl.BlockSpec((1, f, d), w_map),
        ],
        out_specs=pl.BlockSpec((tm, words), rows_map),
        scratch_shapes=[pltpu.VMEM((d, f), BF16), pltpu.VMEM((d, f), BF16), pltpu.VMEM((f, d), BF16)],
    )
    return pl.pallas_call(
        _experts_kernel,
        grid_spec=grid_spec,
        out_shape=jax.ShapeDtypeStruct(xs.shape, xs.dtype),
        compiler_params=_params(1),
        name="experts",
    )(visit_tile, visit_expert, visit_flag, visit_new, row_lo, row_hi, xs, w_gate, w_up, w_down)


def _combine_kernel(*refs):
    dest_refs, dest_next_refs = refs[:TOP_K], refs[TOP_K:2 * TOP_K]
    h_ref, route_ref, ys_ref, o_ref, ybuf, sem = refs[2 * TOP_K:]
    i = pl.program_id(0)
    rows = h_ref.shape[0]
    slot = i % 2

    def issue(idx_refs, into):
        def body(r, c):
            for k in range(TOP_K):
                pltpu.make_async_copy(ys_ref.at[pl.ds(idx_refs[k][r], 1), :],
                                      ybuf.at[into, k, pl.ds(r, 1), :], sem.at[into]).start(priority=k)
            return c

        lax.fori_loop(0, rows, body, 0, unroll=SUBLANES)

    @pl.when(i == 0)
    def _():
        issue(dest_refs, 0)

    @pl.when(i + 1 < pl.num_programs(0))
    def _():
        issue(dest_next_refs, 1 - slot)

    for k in range(TOP_K):
        pltpu.make_async_copy(ys_ref.at[pl.ds(0, rows), :], ybuf.at[slot, k], sem.at[slot]).wait()

    route = route_ref[...]
    moe = None
    for k in range(TOP_K):
        gated = route[:, ROUTE_GATE + k:ROUTE_GATE + k + 1] * _unpack_bf16_pairs(ybuf[slot, k])
        moe = gated if moe is None else moe + gated
    o_ref[...] = h_ref[...] + moe


def _combine(dests, h, route, ys):
    t, d = h.shape
    rows = COMBINE_TILE
    n = t // rows
    this_step = pl.BlockSpec((rows,), lambda i: (i,), memory_space=pltpu.SMEM)
    next_step = pl.BlockSpec((rows,), lambda i: (jnp.minimum(i + 1, n - 1),), memory_space=pltpu.SMEM)
    return pl.pallas_call(
        _combine_kernel,
        grid=(n,),
        in_specs=[this_step] * TOP_K + [next_step] * TOP_K + [
            pl.BlockSpec((rows, d), lambda i: (i, 0)),
            pl.BlockSpec((rows, LANES), lambda i: (i, 0)),
            pl.BlockSpec(memory_space=pl.ANY),
        ],
        out_specs=pl.BlockSpec((rows, d), lambda i: (i, 0)),
        out_shape=jax.ShapeDtypeStruct((t, d), F32),
        scratch_shapes=[pltpu.VMEM((2, TOP_K, rows, ys.shape[1]), ys.dtype), pltpu.SemaphoreType.DMA((2,))],
        compiler_params=_params(1),
        name="combine",
    )(*dests, *dests, h, route, ys)


def _plan(route, counts_row):
    t = route.shape[0]
    counts = counts_row[0, :N_EXPERTS].astype(I32)
    ends = jnp.cumsum(counts)
    offs = ends - counts
    route_t = route.T
    experts = route_t[ROUTE_EXPERT:ROUTE_EXPERT + TOP_K].astype(I32)
    rank = route_t[ROUTE_RANK:ROUTE_RANK + TOP_K].astype(I32)
    expert_ids = jnp.arange(N_EXPERTS, dtype=I32)[:, None, None]
    first_row = jnp.sum(jnp.where(experts[None] == expert_ids, offs[:, None, None], 0), axis=0)
    dest = first_row + rank
    dests = tuple(dest[k] for k in range(TOP_K))

    tm = EXPERT_TILE
    n_visits = (t * TOP_K) // tm + N_EXPERTS - 1
    first_tile = offs // tm
    last_tile = (ends - 1) // tm
    nvis = jnp.where(counts > 0, last_tile - first_tile + 1, 0)
    vend = jnp.cumsum(nvis)
    vstart = vend - nvis
    v = jnp.arange(n_visits, dtype=I32)
    ev = jnp.minimum(jnp.sum((vend[None, :] <= v[:, None]).astype(I32), axis=1), N_EXPERTS - 1)
    tv = jnp.take(first_tile, ev) + (v - jnp.take(vstart, ev))
    valid = v < vend[-1]
    last_v = vend[-1] - 1
    ev = jnp.where(valid, ev, jnp.take(ev, last_v))
    tv = jnp.where(valid, tv, jnp.take(tv, last_v))
    prev_t = jnp.concatenate([jnp.full((1,), -1, I32), tv[:-1]])
    prev_e = jnp.concatenate([jnp.full((1,), -1, I32), ev[:-1]])
    flag = jnp.where(valid, jnp.where(tv != prev_t, 1, 2), 0).astype(I32)
    new_expert = (ev != prev_e).astype(I32)
    return dests, tv.astype(I32), ev.astype(I32), flag, new_expert, offs.astype(I32), ends.astype(I32)


def _rope_constants(head_dim, rope_dim):
    rope_half = rope_dim // 2
    inv_freq = ROPE_THETA ** (-jnp.arange(0, rope_dim, 2, dtype=F32) / rope_dim)
    dim = jnp.arange(LANES, dtype=I32)[None, :] % head_dim
    k = jnp.arange(4 * rope_half, dtype=I32)[:, None]
    c_mat = jnp.where(dim < rope_dim, k == dim % rope_half, k == rope_dim)
    sa_mat = (dim < rope_half) & (k == rope_half + dim)
    sb_mat = (dim >= rope_half) & (dim < rope_dim) & (k == dim)
    expand = jnp.concatenate([c_mat.astype(F32), -sa_mat.astype(F32), sb_mat.astype(F32)], axis=1)
    return inv_freq.reshape(rope_half, 1), expand.astype(BF16)


def kernel(x, positions, norm_mix_g, w_in, q_norm_g, k_norm_g, lambda_q1, lambda_k1, lambda_q2,
           lambda_k2, diff_out_norm_g, sgu_ln_g, sgu_ln_b, sgu_w, sgu_b, sgu_out_norm_g, w_out,
           norm_ffn_g, router_group_w, router_group_b, router_expert_w, router_expert_b,
           expert_w_gate, expert_w_up, expert_w_down):
    b, s, d = x.shape
    t = b * s
    depth = w_in.shape[0]
    head_dim = q_norm_g.shape[-1]
    vd = diff_out_norm_g.shape[-1]
    n_sgu_groups = sgu_w.shape[1]
    sgu_cols = n_sgu_groups * sgu_w.shape[-1]
    in_cols = w_in.shape[-1]
    att_cols = (in_cols - 2 * sgu_cols) // 3
    n_heads = att_cols // vd
    qk_cols = 2 * att_cols
    rope_dim = head_dim // ROPE_FRACTION
    rope_half = rope_dim // 2
    scale = head_dim ** -0.5

    pos_rows = positions.reshape(t // ROW_TILE, 1, ROW_TILE).astype(F32)
    freq_col, expand = _rope_constants(head_dim, rope_dim)
    blk = jnp.arange(MXU_WIDTH, dtype=I32) // head_dim
    ones_blk = (blk[:, None] == blk[None, :]).astype(BF16)
    tri = jnp.arange(ROW_TILE, dtype=I32)
    lstrict = (tri[None, :] < tri[:, None]).astype(BF16)

    h = x.reshape(t, d)
    for l in range(depth):
        lambda_init = 0.8 - 0.6 * math.exp(-0.3 * l)
        gqk = jnp.concatenate([jnp.tile(q_norm_g[l] * (scale * math.log2(math.e)), att_cols // head_dim),
                               jnp.tile(k_norm_g[l], att_cols // head_dim)]).reshape(1, qk_cols)
        qk, v, su, sv = _inproj(
            h, norm_mix_g[l].reshape(1, d), w_in[l].astype(BF16), gqk, pos_rows, freq_col, expand, ones_blk,
            sgu_ln_g[l].reshape(1, sgu_cols), sgu_ln_b[l].reshape(1, sgu_cols),
            qk_cols=qk_cols, v_cols=att_cols, sgu_cols=sgu_cols, head_dim=head_dim,
            rope_half=rope_half)

        lam_vecs = jnp.stack([lambda_q1[l], lambda_k1[l], lambda_q2[l], lambda_k2[l]]).astype(F32)
        score_bound = (1.02 * head_dim * scale * jnp.max(jnp.abs(q_norm_g[l]))
                       * jnp.max(jnp.abs(k_norm_g[l])))
        need_shift = (score_bound > MAX_UNSHIFTED_SCORE).astype(I32).reshape(1)
        att = _attention(need_shift, lam_vecs, diff_out_norm_g[l].reshape(1, vd), qk.reshape(b, s, qk_cols),
                         v.reshape(b, s, att_cols), n_heads=n_heads, head_dim=head_dim,
                         lambda_init=lambda_init).reshape(t, att_cols)

        rw = jnp.concatenate([router_group_w[l], router_expert_w[l]], axis=1)
        rw = jnp.pad(rw, ((0, 0), (0, LANES - rw.shape[1])))
        rw_hi, rw_lo = _split_bf16(rw)
        rb = jnp.pad(jnp.concatenate([router_group_b[l], router_expert_b[l]]),
                     (0, LANES - N_GROUPS - N_EXPERTS)).reshape(1, LANES)
        h, hn, route, counts_row = _mixout(
            att, su, sv, h, sgu_w[l], sgu_b[l].T, sgu_out_norm_g[l], w_out[l].astype(BF16),
            norm_ffn_g[l].reshape(1, d), rw_hi, rw_lo, rb, lstrict, att_cols=att_cols)

        dests, visit_tile, visit_expert, visit_flag, visit_new, row_lo, row_hi = _plan(route, counts_row)
        xs = _dispatch(hn, dests)
        ys = _experts(visit_tile, visit_expert, visit_flag, visit_new, row_lo, row_hi, xs,
                      expert_w_gate[l], expert_w_up[l], expert_w_down[l])
        h = _combine(dests, h, route, ys)
    return h.reshape(b, s, d)
```

```python
import functools
import math

import jax
import jax.numpy as jnp
from jax import lax
from jax.experimental import pallas as pl
from jax.experimental.pallas import tpu as pltpu

F32 = jnp.float32
BF16 = jnp.bfloat16
I32 = jnp.int32

RMS_EPS = 1e-6
NEG_INF = -1e30
ROPE_THETA = 500000.0
MAX_UNSHIFTED_SCORE = 40.0

LANES = 128
SUBLANES = 8
MXU_WIDTH = 256
VMEM_LIMIT_BYTES = 56 * 1024 * 1024

N_GROUPS = 4
EXPERTS_PER_GROUP = 8
N_EXPERTS = N_GROUPS * EXPERTS_PER_GROUP
TOP_K = 2
ROPE_FRACTION = 4
CHUNK = 128
QBLOCK = 256
ROW_TILE = 512
EXPERT_TILE = 512
COMBINE_TILE = 256

ROUTE_EXPERT = 0
ROUTE_GATE = 2
ROUTE_RANK = 4


def _params(n_grid_dims):
    return pltpu.CompilerParams(
        dimension_semantics=("arbitrary",) * n_grid_dims,
        vmem_limit_bytes=VMEM_LIMIT_BYTES,
    )


def _gelu(x):
    return 0.5 * x * (1.0 + lax.erf(x * (1.0 / math.sqrt(2.0))))


def _split_bf16(x):
    hi = x.astype(BF16)
    return hi, (x - hi.astype(F32)).astype(BF16)


U32 = jnp.uint32
HIGH_HALF = 0xFFFF0000


def _pack_bf16_pairs(x):
    n = x.shape[1] // 2
    hi = lax.bitcast_convert_type(x[:, :n].astype(BF16).astype(F32), U32)
    lo = lax.bitcast_convert_type(x[:, n:].astype(BF16).astype(F32), U32)
    return hi | (lo >> 16)


def _unpack_bf16_pairs(w):
    hi = lax.bitcast_convert_type(w & jnp.uint32(HIGH_HALF), F32)
    lo = lax.bitcast_convert_type(w << 16, F32)
    return jnp.concatenate([hi, lo], axis=1)


def _inproj_kernel(x_ref, g_ref, w_ref, gqk_ref, pos_ref, freq_ref, expand_ref, ones_ref, lng_ref, lnb_ref,
                   qk_ref, v_ref, su_ref, sv_ref, proj_ref, ss_ref, *, qk_cols, v_cols, sgu_cols, head_dim,
                   rope_half):
    x = x_ref[...]
    ms = jnp.mean(x * x, axis=-1, keepdims=True)
    xn = (x * lax.rsqrt(ms + RMS_EPS) * g_ref[...]).astype(BF16)
    cw = MXU_WIDTH

    ang_t = freq_ref[...] * pos_ref[0]
    cs_t = jnp.concatenate([jnp.cos(ang_t), jnp.sin(ang_t), jnp.ones_like(ang_t),
                            jnp.zeros_like(ang_t)], axis=0)
    contract0 = (((0,), (0,)), ((), ()))
    cs_hi, cs_lo = _split_bf16(cs_t)
    tab = (lax.dot_general(cs_hi, expand_ref[...], contract0, preferred_element_type=F32)
           + lax.dot_general(cs_lo, expand_ref[...], contract0, preferred_element_type=F32))
    cos = tab[:, 0:LANES]
    sa = tab[:, LANES:2 * LANES]
    sb = tab[:, 2 * LANES:3 * LANES]

    def stage(c):
        cols = slice(c * cw, (c + 1) * cw)
        proj_ref[:, cols] = jnp.dot(xn, w_ref[:, cols], preferred_element_type=F32)

    n_qk = qk_cols // cw
    for c in range(n_qk):
        stage(c)
    for c in range(n_qk):
        p = proj_ref[:, c * cw:(c + 1) * cw]
        ss_ref[:, c * cw:(c + 1) * cw] = jnp.dot((p * p).astype(BF16), ones_ref[...],
                                                 preferred_element_type=F32)
    for c in range(n_qk, w_ref.shape[1] // cw):
        stage(c)

    def proj(col0):
        return proj_ref[:, col0:col0 + cw]

    for c in range(qk_cols // cw):
        p = proj(c * cw)
        ss = ss_ref[:, c * cw:(c + 1) * cw]
        pn = p * lax.rsqrt(ss * (1.0 / head_dim) + RMS_EPS) * gqk_ref[:, c * cw:(c + 1) * cw]
        for hh in range(cw // LANES):
            blk = pn[:, hh * LANES:(hh + 1) * LANES]
            rot = (blk * cos
                   + pltpu.roll(blk, LANES - rope_half, 1) * sa
                   + pltpu.roll(blk, rope_half, 1) * sb)
            col = c * cw + hh * LANES
            qk_ref[:, col:col + LANES] = rot.astype(BF16)

    for c in range(v_cols // cw):
        v_ref[:, c * cw:(c + 1) * cw] = proj(qk_cols + c * cw).astype(BF16)

    for c in range(sgu_cols // cw):
        su_ref[:, c * cw:(c + 1) * cw] = _gelu(proj(qk_cols + v_cols + c * cw)).astype(BF16)

    sv = jnp.concatenate(
        [_gelu(proj(qk_cols + v_cols + sgu_cols + c * cw)) for c in range(sgu_cols // cw)], axis=1)
    mu = jnp.mean(sv, axis=-1, keepdims=True)
    svc = sv - mu
    var = jnp.mean(svc * svc, axis=-1, keepdims=True)
    sv_ref[...] = (svc * lax.rsqrt(var + RMS_EPS) * lng_ref[...] + lnb_ref[...]).astype(BF16)


def _inproj(x2d, norm_g, w_in, gqk, pos_rows, freq_col, expand, ones_blk, ln_g, ln_b, *, qk_cols, v_cols,
            sgu_cols, head_dim, rope_half):
    t, d = x2d.shape
    tm = ROW_TILE
    row = lambda i: (i, 0)
    fixed = lambda i: (0, 0)
    kern = functools.partial(_inproj_kernel, qk_cols=qk_cols, v_cols=v_cols, sgu_cols=sgu_cols,
                             head_dim=head_dim, rope_half=rope_half)
    return pl.pallas_call(
        kern,
        grid=(t // tm,),
        in_specs=[
            pl.BlockSpec((tm, d), row),
            pl.BlockSpec((1, d), fixed),
            pl.BlockSpec(w_in.shape, fixed),
            pl.BlockSpec((1, qk_cols), fixed),
            pl.BlockSpec((1, 1, tm), lambda i: (i, 0, 0)),
            pl.BlockSpec(freq_col.shape, fixed),
            pl.BlockSpec(expand.shape, fixed),
            pl.BlockSpec((MXU_WIDTH, MXU_WIDTH), fixed),
            pl.BlockSpec((1, sgu_cols), fixed),
            pl.BlockSpec((1, sgu_cols), fixed),
        ],
        out_specs=[
            pl.BlockSpec((tm, qk_cols), row),
            pl.BlockSpec((tm, v_cols), row),
            pl.BlockSpec((tm, sgu_cols), row),
            pl.BlockSpec((tm, sgu_cols), row),
        ],
        out_shape=[
            jax.ShapeDtypeStruct((t, qk_cols), BF16),
            jax.ShapeDtypeStruct((t, v_cols), BF16),
            jax.ShapeDtypeStruct((t, sgu_cols), BF16),
            jax.ShapeDtypeStruct((t, sgu_cols), BF16),
        ],
        scratch_shapes=[pltpu.VMEM((tm, w_in.shape[1]), F32), pltpu.VMEM((tm, qk_cols), F32)],
        compiler_params=_params(1),
        name="inproj",
    )(x2d, norm_g, w_in, gqk, pos_rows, freq_col, expand, ones_blk, ln_g, ln_b)


def _attn_kernel(shift_ref, lam_ref, gout_ref, q_ref, k_ref, v_ref, o_ref, qm_ref, m_ref, acc_ref, *, tq,
                 n_heads, lambda_init, head_dim):
    qi = pl.program_id(1)
    hw = 2 * head_dim
    vd = v_ref.shape[-1] // n_heads
    lv = lam_ref[...]
    lam = (jnp.exp(jnp.sum(lv[0:1] * lv[1:2], axis=-1, keepdims=True))
           - jnp.exp(jnp.sum(lv[2:3] * lv[3:4], axis=-1, keepdims=True))
           + lambda_init)

    first = lax.broadcasted_iota(jnp.int32, (1, hw), 1) < head_dim
    for h in range(n_heads):
        q = q_ref[0, :, h * hw:(h + 1) * hw]
        zero = jnp.zeros_like(q)
        qm_ref[2 * h] = jnp.where(first, q, zero)
        qm_ref[2 * h + 1] = jnp.where(first, zero, q)
    acc_ref[...] = jnp.zeros(acc_ref.shape, F32)

    ones = jnp.ones((tq, vd), BF16)
    causal = (lax.broadcasted_iota(jnp.int32, (tq, tq), 1)
              <= lax.broadcasted_iota(jnp.int32, (tq, tq), 0))

    def kv_tile(j, h):
        start = pl.multiple_of(j * tq, tq)
        kc = k_ref[0, pl.ds(start, tq), h * hw:(h + 1) * hw]
        vext = jnp.concatenate([v_ref[0, pl.ds(start, tq), h * vd:(h + 1) * vd], ones], axis=1)
        return kc, vext

    def scores(c, kc, masked):
        s = lax.dot_general(qm_ref[c], kc, (((1,), (1,)), ((), ())), preferred_element_type=F32)
        return jnp.where(causal, s, NEG_INF) if masked else s

    def plain_step(tiles):
        for h in range(n_heads):
            kvs = [kv_tile(j, h) for j, _ in tiles]
            for mp in range(2):
                c = 2 * h + mp
                pv = None
                for (kc, vext), (_, masked) in zip(kvs, tiles):
                    part = jnp.dot(jnp.exp2(scores(c, kc, masked)).astype(BF16), vext,
                                   preferred_element_type=F32)
                    pv = part if pv is None else pv + part
                acc_ref[c] += pv

    def running_max_step(j, masked):
        for h in range(n_heads):
            kc, vext = kv_tile(j, h)
            for mp in range(2):
                c = 2 * h + mp
                s = scores(c, kc, masked)
                m_old = m_ref[c]
                m_new = jnp.maximum(m_old, jnp.max(s, axis=-1, keepdims=True))
                p = jnp.exp2(s - m_new).astype(BF16)
                acc_ref[c] = (jnp.exp2(m_old - m_new) * acc_ref[c]
                              + jnp.dot(p, vext, preferred_element_type=F32))
                m_ref[c] = m_new

    @pl.when(shift_ref[0] == 0)
    def _():
        def pair(jj, carry):
            plain_step([(2 * jj, False), (2 * jj + 1, False)])
            return carry

        lax.fori_loop(0, qi // 2, pair, 0)

        @pl.when(qi % 2 == 1)
        def _():
            plain_step([(qi - 1, False), (qi, True)])

        @pl.when(qi % 2 == 0)
        def _():
            plain_step([(qi, True)])

    @pl.when(shift_ref[0] != 0)
    def _():
        m_ref[...] = jnp.full(m_ref.shape, NEG_INF, F32)

        def single(j, carry):
            running_max_step(j, False)
            return carry

        lax.fori_loop(0, qi, single, 0)
        running_max_step(qi, True)

    for h in range(n_heads):
        a1 = acc_ref[2 * h]
        a2 = acc_ref[2 * h + 1]
        att = a1[:, :vd] / a1[:, vd:] - lam * (a2[:, :vd] / a2[:, vd:])
        ms = jnp.mean(att * att, axis=-1, keepdims=True)
        o_ref[0, :, h * vd:(h + 1) * vd] = (
            att * lax.rsqrt(ms + RMS_EPS) * gout_ref[...] * (1.0 - lambda_init)).astype(BF16)


def _attention(need_shift, lam_vecs, gout, qk3, v3, *, n_heads, head_dim, lambda_init):
    b, s, _ = qk3.shape
    att_cols = v3.shape[-1]
    vd = att_cols // n_heads
    tq = QBLOCK
    kern = functools.partial(_attn_kernel, tq=tq, n_heads=n_heads, lambda_init=lambda_init,
                             head_dim=head_dim)
    grid_spec = pltpu.PrefetchScalarGridSpec(
        num_scalar_prefetch=1,
        grid=(b, s // tq),
        in_specs=[
            pl.BlockSpec(lam_vecs.shape, lambda bi, qi, f: (0, 0)),
            pl.BlockSpec((1, vd), lambda bi, qi, f: (0, 0)),
            pl.BlockSpec((1, tq, att_cols), lambda bi, qi, f: (bi, qi, 0)),
            pl.BlockSpec((1, s, att_cols), lambda bi, qi, f: (bi, 0, 1)),
            pl.BlockSpec((1, s, att_cols), lambda bi, qi, f: (bi, 0, 0)),
        ],
        out_specs=pl.BlockSpec((1, tq, att_cols), lambda bi, qi, f: (bi, qi, 0)),
        scratch_shapes=[
            pltpu.VMEM((2 * n_heads, tq, 2 * head_dim), BF16),
            pltpu.VMEM((2 * n_heads, tq, 1), F32),
            pltpu.VMEM((2 * n_heads, tq, 2 * vd), F32),
        ],
    )
    return pl.pallas_call(
        kern,
        grid_spec=grid_spec,
        out_shape=jax.ShapeDtypeStruct(v3.shape, BF16),
        compiler_params=_params(2),
        name="diff_attn",
    )(need_shift, lam_vecs, gout, qk3, qk3, v3)


def _mixout_kernel(att_ref, su_ref, sv_ref, x_ref, sw_ref, sbt_ref, sgn_ref, wo_ref, gffn_ref,
                   rwh_ref, rwl_ref, rb_ref, lstrict_ref, h_ref, hn_ref, route_ref, counts_ref, cnt_ref,
                   hn_prev_ref, *, n_sgu_groups, att_cols):
    step = pl.program_id(0)
    tm = x_ref.shape[0]
    gw = CHUNK
    r_id = lax.broadcasted_iota(jnp.int32, (gw, gw), 0)
    c_id = lax.broadcasted_iota(jnp.int32, (gw, gw), 1)
    tril = c_id <= r_id

    @pl.when(step == 0)
    def _():
        cnt_ref[...] = jnp.zeros_like(cnt_ref)
        hn_prev_ref[...] = jnp.zeros_like(hn_prev_ref)

    _route_tile(hn_prev_ref[...], (step > 0).astype(F32), rwh_ref, rwl_ref, rb_ref, lstrict_ref,
                route_ref, counts_ref, cnt_ref)

    acc = jnp.dot(att_ref[...], wo_ref[0:att_cols, :], preferred_element_type=F32)

    sg_cols = []
    for g in range(n_sgu_groups):
        w = jnp.where(tril, sw_ref[g], 0.0).astype(BF16)
        bias = sbt_ref[:, g:g + 1]
        gn = sgn_ref[g:g + 1, :]
        rows = []
        for c in range(tm // gw):
            vblk = sv_ref[c * gw:(c + 1) * gw, g * gw:(g + 1) * gw]
            s = jnp.dot(w, vblk, preferred_element_type=F32) + bias
            sg = su_ref[c * gw:(c + 1) * gw, g * gw:(g + 1) * gw].astype(F32) * s
            ms = jnp.mean(sg * sg, axis=-1, keepdims=True)
            rows.append((sg * lax.rsqrt(ms + RMS_EPS) * gn).astype(BF16))
        sg_cols.append(jnp.concatenate(rows, axis=0))
    sgn = jnp.concatenate(sg_cols, axis=1)
    acc = acc + jnp.dot(sgn, wo_ref[att_cols:, :], preferred_element_type=F32)

    h = x_ref[...] + acc
    h_ref[...] = h
    ms = jnp.mean(h * h, axis=-1, keepdims=True)
    hn = h * lax.rsqrt(ms + RMS_EPS) * gffn_ref[...]
    hn_ref[...] = _pack_bf16_pairs(hn)
    hn_prev_ref[...] = hn


def _route_tile(hn, live, rwh_ref, rwl_ref, rb_ref, lstrict_ref, route_ref, counts_ref, cnt_ref):
    hn_bf, hn_lo = _split_bf16(hn)
    logits = (jnp.dot(hn_bf, rwh_ref[...], preferred_element_type=F32)
              + jnp.dot(hn_lo, rwh_ref[...], preferred_element_type=F32)
              + jnp.dot(hn_bf, rwl_ref[...], preferred_element_type=F32)
              + rb_ref[...])

    lane = lax.broadcasted_iota(jnp.int32, logits.shape, 1).astype(F32)
    big = float(LANES)
    is_grp = lane < N_GROUPS
    gl = jnp.where(is_grp, logits, NEG_INF)
    gmax = jnp.max(gl, axis=-1, keepdims=True)
    gidx = jnp.min(jnp.where(gl == gmax, lane, big), axis=-1, keepdims=True)
    psum = jnp.sum(jnp.where(is_grp, jnp.exp(gl - gmax), 0.0), axis=-1, keepdims=True)
    p_grp = 1.0 / psum
    e_lo = N_GROUPS + gidx * EXPERTS_PER_GROUP
    in_grp = (lane >= e_lo) & (lane < e_lo + EXPERTS_PER_GROUP)
    el = jnp.where(in_grp, logits, NEG_INF)
    t1 = jnp.max(el, axis=-1, keepdims=True)
    i1 = jnp.min(jnp.where(el == t1, lane, big), axis=-1, keepdims=True)
    el2 = jnp.where(lane == i1, NEG_INF, el)
    t2 = jnp.max(el2, axis=-1, keepdims=True)
    i2 = jnp.min(jnp.where(el2 == t2, lane, big), axis=-1, keepdims=True)
    e21 = jnp.exp(t2 - t1)
    den = 1.0 + e21
    g1 = p_grp / den
    g2 = p_grp * (e21 / den)
    e1 = i1 - N_GROUPS
    e2 = i2 - N_GROUPS

    hit1 = lane == e1
    hit2 = lane == e2
    onehot = jnp.where(hit1 | hit2, live, 0.0)
    before = jnp.dot(lstrict_ref[...], onehot.astype(BF16), preferred_element_type=F32) + cnt_ref[...]
    r1 = jnp.sum(jnp.where(hit1, before, 0.0), axis=-1, keepdims=True)
    r2 = jnp.sum(jnp.where(hit2, before, 0.0), axis=-1, keepdims=True)
    cnt_ref[...] = cnt_ref[...] + jnp.sum(onehot, axis=0, keepdims=True)
    counts_ref[...] = cnt_ref[...]

    route = jnp.zeros_like(logits)
    for k, val in enumerate((e1, e2, g1, g2, r1, r2)):
        route = jnp.where(lane == k, val, route)
    route_ref[...] = route


def _mixout(att, su, sv, x2d, sgu_w, sgu_bt, sgu_gn, w_out, g_ffn, rw_hi, rw_lo, rb, lstrict, *,
            att_cols):
    t, d = x2d.shape
    tm = ROW_TILE
    n = t // tm
    row = lambda i: (jnp.minimum(i, n - 1), 0)
    routed = lambda i: (jnp.maximum(i - 1, 0), 0)
    fixed2 = lambda i: (0, 0)
    n_groups = sgu_w.shape[0]
    kern = functools.partial(_mixout_kernel, n_sgu_groups=n_groups, att_cols=att_cols)
    return pl.pallas_call(
        kern,
        grid=(n + 1,),
        in_specs=[
            pl.BlockSpec((tm, att.shape[1]), row),
            pl.BlockSpec((tm, su.shape[1]), row),
            pl.BlockSpec((tm, sv.shape[1]), row),
            pl.BlockSpec((tm, d), row),
            pl.BlockSpec(sgu_w.shape, lambda i: (0, 0, 0)),
            pl.BlockSpec(sgu_bt.shape, fixed2),
            pl.BlockSpec(sgu_gn.shape, fixed2),
            pl.BlockSpec(w_out.shape, fixed2),
            pl.BlockSpec((1, d), fixed2),
            pl.BlockSpec(rw_hi.shape, fixed2),
            pl.BlockSpec(rw_lo.shape, fixed2),
            pl.BlockSpec((1, LANES), fixed2),
            pl.BlockSpec((tm, tm), fixed2),
        ],
        out_specs=[
            pl.BlockSpec((tm, d), row),
            pl.BlockSpec((tm, d // 2), row),
            pl.BlockSpec((tm, LANES), routed),
            pl.BlockSpec((1, LANES), fixed2),
        ],
        out_shape=[
            jax.ShapeDtypeStruct((t, d), F32),
            jax.ShapeDtypeStruct((t, d // 2), U32),
            jax.ShapeDtypeStruct((t, LANES), F32),
            jax.ShapeDtypeStruct((1, LANES), F32),
        ],
        scratch_shapes=[pltpu.VMEM((1, LANES), F32), pltpu.VMEM((tm, d), F32)],
        compiler_params=_params(1),
        name="mixout",
    )(att, su, sv, x2d, sgu_w, sgu_bt, sgu_gn, w_out, g_ffn, rw_hi, rw_lo, rb, lstrict)


def _dispatch_kernel(*refs):
    dest_refs, (hn_ref, xs_ref, sem) = refs[:TOP_K], refs[TOP_K:]
    rows = hn_ref.shape[0]

    for r in range(rows):
        for k in range(TOP_K):
            pltpu.make_async_copy(hn_ref.at[pl.ds(r, 1), :],
                                  xs_ref.at[pl.ds(dest_refs[k][r], 1), :], sem).start(priority=k)
    for k in range(TOP_K):
        pltpu.make_async_copy(hn_ref, xs_ref.at[pl.ds(0, rows), :], sem).wait()


def _dispatch(hn, dests):
    t, d = hn.shape
    rows = ROW_TILE
    index_spec = pl.BlockSpec((rows,), lambda i: (i,), memory_space=pltpu.SMEM)
    return pl.pallas_call(
        _dispatch_kernel,
        grid=(t // rows,),
        in_specs=[index_spec] * TOP_K + [pl.BlockSpec((rows, d), lambda i: (i, 0))],
        out_specs=pl.BlockSpec(memory_space=pl.ANY),
        out_shape=jax.ShapeDtypeStruct((t * TOP_K, d), hn.dtype),
        scratch_shapes=[pltpu.SemaphoreType.DMA],
        compiler_params=_params(1),
        name="dispatch",
    )(*dests, hn)


def _experts_kernel(vt_ref, ve_ref, vf_ref, vn_ref, lo_ref, hi_ref, xs_ref, wg_ref, wu_ref, wd_ref, ys_ref,
                    wg_bf, wu_bf, wd_bf):
    v = pl.program_id(0)
    flag = vf_ref[v]
    tm = xs_ref.shape[0]

    @pl.when(vn_ref[v] == 1)
    def _():
        wg_bf[...] = wg_ref[0].astype(BF16)
        wu_bf[...] = wu_ref[0].astype(BF16)
        wd_bf[...] = wd_ref[0].astype(BF16)

    @pl.when(flag > 0)
    def _():
        e = ve_ref[v]
        base = vt_ref[v] * tm
        rid = lax.broadcasted_iota(jnp.int32, (tm, 1), 0) + base
        mine = (rid >= lo_ref[e]) & (rid < hi_ref[e])
        x = _unpack_bf16_pairs(xs_ref[...]).astype(BF16)
        hg = jnp.dot(x, wg_bf[...], preferred_element_type=F32)
        hu = jnp.dot(x, wu_bf[...], preferred_element_type=F32)
        a = (hg * (1.0 / (1.0 + jnp.exp(-hg))) * hu).astype(BF16)
        y = _pack_bf16_pairs(jnp.dot(a, wd_bf[...], preferred_element_type=F32))

        @pl.when(flag == 1)
        def _():
            ys_ref[...] = jnp.where(mine, y, jnp.zeros_like(y))

        @pl.when(flag == 2)
        def _():
            ys_ref[...] = jnp.where(mine, y, ys_ref[...])


def _experts(visit_tile, visit_expert, visit_flag, visit_new, row_lo, row_hi, xs, w_gate, w_up, w_down):
    p, words = xs.shape
    d = 2 * words
    tm = EXPERT_TILE
    f = w_gate.shape[-1]
    rows_map = lambda v, vt, ve, vf, vn, lo, hi: (vt[v], 0)
    w_map = lambda v, vt, ve, vf, vn, lo, hi: (ve[v], 0, 0)
    grid_spec = pltpu.PrefetchScalarGridSpec(
        num_scalar_prefetch=6,
        grid=(visit_tile.shape[0],),
        in_specs=[
            pl.BlockSpec((tm, words), rows_map),
            pl.BlockSpec((1, d, f), w_map),
            pl.BlockSpec((1, d, f), w_map),
            pl.BlockSpec((1, f, d), w_map),
        ],
        out_specs=pl.BlockSpec((tm, words), rows_map),
        scratch_shapes=[pltpu.VMEM((d, f), BF16), pltpu.VMEM((d, f), BF16), pltpu.VMEM((f, d), BF16)],
    )
    return pl.pallas_call(
        _experts_kernel,
        grid_spec=grid_spec,
        out_shape=jax.ShapeDtypeStruct(xs.shape, xs.dtype),
        compiler_params=_params(1),
        name="experts",
    )(visit_tile, visit_expert, visit_flag, visit_new, row_lo, row_hi, xs, w_gate, w_up, w_down)


def _combine_kernel(*refs):
    dest_refs, dest_next_refs = refs[:TOP_K], refs[TOP_K:2 * TOP_K]
    h_ref, route_ref, ys_ref, o_ref, ybuf, sem = refs[2 * TOP_K:]
    i = pl.program_id(0)
    rows = h_ref.shape[0]
    slot = i % 2

    def issue(idx_refs, into):
        for r in range(rows):
            for k in range(TOP_K):
                pltpu.make_async_copy(ys_ref.at[pl.ds(idx_refs[k][r], 1), :],
                                      ybuf.at[into, k, pl.ds(r, 1), :], sem.at[into]).start(priority=k)

    @pl.when(i == 0)
    def _():
        issue(dest_refs, 0)

    @pl.when(i + 1 < pl.num_programs(0))
    def _():
        issue(dest_next_refs, 1 - slot)

    for k in range(TOP_K):
        pltpu.make_async_copy(ys_ref.at[pl.ds(0, rows), :], ybuf.at[slot, k], sem.at[slot]).wait()

    route = route_ref[...]
    moe = None
    for k in range(TOP_K):
        gated = route[:, ROUTE_GATE + k:ROUTE_GATE + k + 1] * _unpack_bf16_pairs(ybuf[slot, k])
        moe = gated if moe is None else moe + gated
    o_ref[...] = h_ref[...] + moe


def _combine(dests, h, route, ys):
    t, d = h.shape
    rows = COMBINE_TILE
    n = t // rows
    this_step = pl.BlockSpec((rows,), lambda i: (i,), memory_space=pltpu.SMEM)
    next_step = pl.BlockSpec((rows,), lambda i: (jnp.minimum(i + 1, n - 1),), memory_space=pltpu.SMEM)
    return pl.pallas_call(
        _combine_kernel,
        grid=(n,),
        in_specs=[this_step] * TOP_K + [next_step] * TOP_K + [
            pl.BlockSpec((rows, d), lambda i: (i, 0)),
            pl.BlockSpec((rows, LANES), lambda i: (i, 0)),
            pl.BlockSpec(memory_space=pl.ANY),
        ],
        out_specs=pl.BlockSpec((rows, d), lambda i: (i, 0)),
        out_shape=jax.ShapeDtypeStruct((t, d), F32),
        scratch_shapes=[pltpu.VMEM((2, TOP_K, rows, ys.shape[1]), ys.dtype), pltpu.SemaphoreType.DMA((2,))],
        compiler_params=_params(1),
        name="combine",
    )(*dests, *dests, h, route, ys)


def _plan(route, counts_row):
    t = route.shape[0]
    counts = counts_row[0, :N_EXPERTS].astype(I32)
    ends = jnp.cumsum(counts)
    offs = ends - counts
    route_t = route.T
    experts = route_t[ROUTE_EXPERT:ROUTE_EXPERT + TOP_K].astype(I32)
    rank = route_t[ROUTE_RANK:ROUTE_RANK + TOP_K].astype(I32)
    expert_ids = jnp.arange(N_EXPERTS, dtype=I32)[:, None, None]
    first_row = jnp.sum(jnp.where(experts[None] == expert_ids, offs[:, None, None], 0), axis=0)
    dest = first_row + rank
    dests = tuple(dest[k] for k in range(TOP_K))

    tm = EXPERT_TILE
    n_visits = (t * TOP_K) // tm + N_EXPERTS - 1
    first_tile = offs // tm
    last_tile = (ends - 1) // tm
    nvis = jnp.where(counts > 0, last_tile - first_tile + 1, 0)
    vend = jnp.cumsum(nvis)
    vstart = vend - nvis
    v = jnp.arange(n_visits, dtype=I32)
    ev = jnp.minimum(jnp.sum((vend[None, :] <= v[:, None]).astype(I32), axis=1), N_EXPERTS - 1)
    tv = jnp.take(first_tile, ev) + (v - jnp.take(vstart, ev))
    valid = v < vend[-1]
    last_v = vend[-1] - 1
    ev = jnp.where(valid, ev, jnp.take(ev, last_v))
    tv = jnp.where(valid, tv, jnp.take(tv, last_v))
    prev_t = jnp.concatenate([jnp.full((1,), -1, I32), tv[:-1]])
    prev_e = jnp.concatenate([jnp.full((1,), -1, I32), ev[:-1]])
    flag = jnp.where(valid, jnp.where(tv != prev_t, 1, 2), 0).astype(I32)
    new_expert = (ev != prev_e).astype(I32)
    return dests, tv.astype(I32), ev.astype(I32), flag, new_expert, offs.astype(I32), ends.astype(I32)


def _rope_constants(head_dim, rope_dim):
    rope_half = rope_dim // 2
    inv_freq = ROPE_THETA ** (-jnp.arange(0, rope_dim, 2, dtype=F32) / rope_dim)
    dim = jnp.arange(LANES, dtype=I32)[None, :] % head_dim
    k = jnp.arange(4 * rope_half, dtype=I32)[:, None]
    c_mat = jnp.where(dim < rope_dim, k == dim % rope_half, k == rope_dim)
    sa_mat = (dim < rope_half) & (k == rope_half + dim)
    sb_mat = (dim >= rope_half) & (dim < rope_dim) & (k == dim)
    expand = jnp.concatenate([c_mat.astype(F32), -sa_mat.astype(F32), sb_mat.astype(F32)], axis=1)
    return inv_freq.reshape(rope_half, 1), expand.astype(BF16)


def kernel(x, positions, norm_mix_g, w_in, q_norm_g, k_norm_g, lambda_q1, lambda_k1, lambda_q2,
           lambda_k2, diff_out_norm_g, sgu_ln_g, sgu_ln_b, sgu_w, sgu_b, sgu_out_norm_g, w_out,
           norm_ffn_g, router_group_w, router_group_b, router_expert_w, router_expert_b,
           expert_w_gate, expert_w_up, expert_w_down):
    b, s, d = x.shape
    t = b * s
    depth = w_in.shape[0]
    head_dim = q_norm_g.shape[-1]
    vd = diff_out_norm_g.shape[-1]
    n_sgu_groups = sgu_w.shape[1]
    sgu_cols = n_sgu_groups * sgu_w.shape[-1]
    in_cols = w_in.shape[-1]
    att_cols = (in_cols - 2 * sgu_cols) // 3
    n_heads = att_cols // vd
    qk_cols = 2 * att_cols
    rope_dim = head_dim // ROPE_FRACTION
    rope_half = rope_dim // 2
    scale = head_dim ** -0.5

    pos_rows = positions.reshape(t // ROW_TILE, 1, ROW_TILE).astype(F32)
    freq_col, expand = _rope_constants(head_dim, rope_dim)
    blk = jnp.arange(MXU_WIDTH, dtype=I32) // head_dim
    ones_blk = (blk[:, None] == blk[None, :]).astype(BF16)
    tri = jnp.arange(ROW_TILE, dtype=I32)
    lstrict = (tri[None, :] < tri[:, None]).astype(BF16)

    h = x.reshape(t, d)
    for l in range(depth):
        lambda_init = 0.8 - 0.6 * math.exp(-0.3 * l)
        gqk = jnp.concatenate([jnp.tile(q_norm_g[l] * (scale * math.log2(math.e)), att_cols // head_dim),
                               jnp.tile(k_norm_g[l], att_cols // head_dim)]).reshape(1, qk_cols)
        qk, v, su, sv = _inproj(
            h, norm_mix_g[l].reshape(1, d), w_in[l].astype(BF16), gqk, pos_rows, freq_col, expand, ones_blk,
            sgu_ln_g[l].reshape(1, sgu_cols), sgu_ln_b[l].reshape(1, sgu_cols),
            qk_cols=qk_cols, v_cols=att_cols, sgu_cols=sgu_cols, head_dim=head_dim,
            rope_half=rope_half)

        lam_vecs = jnp.stack([lambda_q1[l], lambda_k1[l], lambda_q2[l], lambda_k2[l]]).astype(F32)
        score_bound = (1.02 * head_dim * scale * jnp.max(jnp.abs(q_norm_g[l]))
                       * jnp.max(jnp.abs(k_norm_g[l])))
        need_shift = (score_bound > MAX_UNSHIFTED_SCORE).astype(I32).reshape(1)
        att = _attention(need_shift, lam_vecs, diff_out_norm_g[l].reshape(1, vd), qk.reshape(b, s, qk_cols),
                         v.reshape(b, s, att_cols), n_heads=n_heads, head_dim=head_dim,
                         lambda_init=lambda_init).reshape(t, att_cols)

        rw = jnp.concatenate([router_group_w[l], router_expert_w[l]], axis=1)
        rw = jnp.pad(rw, ((0, 0), (0, LANES - rw.shape[1])))
        rw_hi, rw_lo = _split_bf16(rw)
        rb = jnp.pad(jnp.concatenate([router_group_b[l], router_expert_b[l]]),
                     (0, LANES - N_GROUPS - N_EXPERTS)).reshape(1, LANES)
        h, hn, route, counts_row = _mixout(
            att, su, sv, h, sgu_w[l], sgu_b[l].T, sgu_out_norm_g[l], w_out[l].astype(BF16),
            norm_ffn_g[l].reshape(1, d), rw_hi, rw_lo, rb, lstrict, att_cols=att_cols)

        dests, visit_tile, visit_expert, visit_flag, visit_new, row_lo, row_hi = _plan(route, counts_row)
        xs = _dispatch(hn, dests)
        ys = _experts(visit_tile, visit_expert, visit_flag, visit_new, row_lo, row_hi, xs,
                      expert_w_gate[l], expert_w_up[l], expert_w_down[l])
        h = _combine(dests, h, route, ys)
    return h.reshape(b, s, d)
```

```python
import functools
import math

import jax
import jax.numpy as jnp
from jax import lax
from jax.experimental import pallas as pl
from jax.experimental.pallas import tpu as pltpu

F32 = jnp.float32
BF16 = jnp.bfloat16
I32 = jnp.int32

RMS_EPS = 1e-6
NEG_INF = -1e30
ROPE_THETA = 500000.0
MAX_UNSHIFTED_SCORE = 40.0

LANES = 128
SUBLANES = 8
MXU_WIDTH = 256
VMEM_LIMIT_BYTES = 56 * 1024 * 1024

N_GROUPS = 4
EXPERTS_PER_GROUP = 8
N_EXPERTS = N_GROUPS * EXPERTS_PER_GROUP
TOP_K = 2
ROPE_FRACTION = 4
CHUNK = 128
QBLOCK = 256
ROW_TILE = 512
EXPERT_TILE = 512
COMBINE_TILE = 256

ROUTE_EXPERT = 0
ROUTE_GATE = 2
ROUTE_RANK = 4


def _params(n_grid_dims):
    return pltpu.CompilerParams(
        dimension_semantics=("arbitrary",) * n_grid_dims,
        vmem_limit_bytes=VMEM_LIMIT_BYTES,
    )


def _gelu(x):
    return 0.5 * x * (1.0 + lax.erf(x * (1.0 / math.sqrt(2.0))))


def _split_bf16(x):
    hi = x.astype(BF16)
    return hi, (x - hi.astype(F32)).astype(BF16)


U32 = jnp.uint32
HIGH_HALF = 0xFFFF0000


def _pack_bf16_pairs(x):
    n = x.shape[1] // 2
    hi = lax.bitcast_convert_type(x[:, :n].astype(BF16).astype(F32), U32)
    lo = lax.bitcast_convert_type(x[:, n:].astype(BF16).astype(F32), U32)
    return hi | (lo >> 16)


def _unpack_bf16_pairs(w):
    hi = lax.bitcast_convert_type(w & jnp.uint32(HIGH_HALF), F32)
    lo = lax.bitcast_convert_type(w << 16, F32)
    return jnp.concatenate([hi, lo], axis=1)


def _inproj_kernel(x_ref, g_ref, w_ref, gqk_ref, pos_ref, freq_ref, expand_ref, ones_ref, lng_ref, lnb_ref,
                   qk_ref, v_ref, su_ref, sv_ref, proj_ref, ss_ref, *, qk_cols, v_cols, sgu_cols, head_dim,
                   rope_half):
    x = x_ref[...]
    ms = jnp.mean(x * x, axis=-1, keepdims=True)
    xn = (x * lax.rsqrt(ms + RMS_EPS) * g_ref[...]).astype(BF16)
    cw = MXU_WIDTH

    ang_t = freq_ref[...] * pos_ref[0]
    cs_t = jnp.concatenate([jnp.cos(ang_t), jnp.sin(ang_t), jnp.ones_like(ang_t),
                            jnp.zeros_like(ang_t)], axis=0)
    contract0 = (((0,), (0,)), ((), ()))
    cs_hi, cs_lo = _split_bf16(cs_t)
    tab = (lax.dot_general(cs_hi, expand_ref[...], contract0, preferred_element_type=F32)
           + lax.dot_general(cs_lo, expand_ref[...], contract0, preferred_element_type=F32))
    cos = tab[:, 0:LANES]
    sa = tab[:, LANES:2 * LANES]
    sb = tab[:, 2 * LANES:3 * LANES]

    def stage(c):
        cols = slice(c * cw, (c + 1) * cw)
        proj_ref[:, cols] = jnp.dot(xn, w_ref[:, cols], preferred_element_type=F32)

    n_qk = qk_cols // cw
    for c in range(n_qk):
        stage(c)
    for c in range(n_qk):
        p = proj_ref[:, c * cw:(c + 1) * cw]
        ss_ref[:, c * cw:(c + 1) * cw] = jnp.dot((p * p).astype(BF16), ones_ref[...],
                                                 preferred_element_type=F32)
    for c in range(n_qk, w_ref.shape[1] // cw):
        stage(c)

    def proj(col0):
        return proj_ref[:, col0:col0 + cw]

    for c in range(qk_cols // cw):
        p = proj(c * cw)
        ss = ss_ref[:, c * cw:(c + 1) * cw]
        pn = p * lax.rsqrt(ss * (1.0 / head_dim) + RMS_EPS) * gqk_ref[:, c * cw:(c + 1) * cw]
        for hh in range(cw // LANES):
            blk = pn[:, hh * LANES:(hh + 1) * LANES]
            rot = (blk * cos
                   + pltpu.roll(blk, LANES - rope_half, 1) * sa
                   + pltpu.roll(blk, rope_half, 1) * sb)
            col = c * cw + hh * LANES
            qk_ref[:, col:col + LANES] = rot.astype(BF16)

    for c in range(v_cols // cw):
        v_ref[:, c * cw:(c + 1) * cw] = proj(qk_cols + c * cw).astype(BF16)

    for c in range(sgu_cols // cw):
        su_ref[:, c * cw:(c + 1) * cw] = _gelu(proj(qk_cols + v_cols + c * cw)).astype(BF16)

    sv = jnp.concatenate(
        [_gelu(proj(qk_cols + v_cols + sgu_cols + c * cw)) for c in range(sgu_cols // cw)], axis=1)
    mu = jnp.mean(sv, axis=-1, keepdims=True)
    svc = sv - mu
    var = jnp.mean(svc * svc, axis=-1, keepdims=True)
    sv_ref[...] = (svc * lax.rsqrt(var + RMS_EPS) * lng_ref[...] + lnb_ref[...]).astype(BF16)


def _inproj(x2d, norm_g, w_in, gqk, pos_rows, freq_col, expand, ones_blk, ln_g, ln_b, *, qk_cols, v_cols,
            sgu_cols, head_dim, rope_half):
    t, d = x2d.shape
    tm = ROW_TILE
    row = lambda i: (i, 0)
    fixed = lambda i: (0, 0)
    kern = functools.partial(_inproj_kernel, qk_cols=qk_cols, v_cols=v_cols, sgu_cols=sgu_cols,
                             head_dim=head_dim, rope_half=rope_half)
    return pl.pallas_call(
        kern,
        grid=(t // tm,),
        in_specs=[
            pl.BlockSpec((tm, d), row),
            pl.BlockSpec((1, d), fixed),
            pl.BlockSpec(w_in.shape, fixed),
            pl.BlockSpec((1, qk_cols), fixed),
            pl.BlockSpec((1, 1, tm), lambda i: (i, 0, 0)),
            pl.BlockSpec(freq_col.shape, fixed),
            pl.BlockSpec(expand.shape, fixed),
            pl.BlockSpec((MXU_WIDTH, MXU_WIDTH), fixed),
            pl.BlockSpec((1, sgu_cols), fixed),
            pl.BlockSpec((1, sgu_cols), fixed),
        ],
        out_specs=[
            pl.BlockSpec((tm, qk_cols), row),
            pl.BlockSpec((tm, v_cols), row),
            pl.BlockSpec((tm, sgu_cols), row),
            pl.BlockSpec((tm, sgu_cols), row),
        ],
        out_shape=[
            jax.ShapeDtypeStruct((t, qk_cols), BF16),
            jax.ShapeDtypeStruct((t, v_cols), BF16),
            jax.ShapeDtypeStruct((t, sgu_cols), BF16),
            jax.ShapeDtypeStruct((t, sgu_cols), BF16),
        ],
        scratch_shapes=[pltpu.VMEM((tm, w_in.shape[1]), F32), pltpu.VMEM((tm, qk_cols), F32)],
        compiler_params=_params(1),
        name="inproj",
    )(x2d, norm_g, w_in, gqk, pos_rows, freq_col, expand, ones_blk, ln_g, ln_b)


def _attn_kernel(shift_ref, lam_ref, gout_ref, q_ref, k_ref, v_ref, o_ref, qm_ref, m_ref, acc_ref, *, tq,
                 n_heads, lambda_init, head_dim):
    qi = pl.program_id(1)
    hw = 2 * head_dim
    vd = v_ref.shape[-1] // n_heads
    lv = lam_ref[...]
    lam = (jnp.exp(jnp.sum(lv[0:1] * lv[1:2], axis=-1, keepdims=True))
           - jnp.exp(jnp.sum(lv[2:3] * lv[3:4], axis=-1, keepdims=True))
           + lambda_init)

    first = lax.broadcasted_iota(jnp.int32, (1, hw), 1) < head_dim
    for h in range(n_heads):
        q = q_ref[0, :, h * hw:(h + 1) * hw]
        zero = jnp.zeros_like(q)
        qm_ref[2 * h] = jnp.where(first, q, zero)
        qm_ref[2 * h + 1] = jnp.where(first, zero, q)
    acc_ref[...] = jnp.zeros(acc_ref.shape, F32)

    ones = jnp.ones((tq, vd), BF16)
    causal = (lax.broadcasted_iota(jnp.int32, (tq, tq), 1)
              <= lax.broadcasted_iota(jnp.int32, (tq, tq), 0))

    def kv_tile(j, h):
        start = pl.multiple_of(j * tq, tq)
        kc = k_ref[0, pl.ds(start, tq), h * hw:(h + 1) * hw]
        vext = jnp.concatenate([v_ref[0, pl.ds(start, tq), h * vd:(h + 1) * vd], ones], axis=1)
        return kc, vext

    def scores(c, kc, masked):
        s = lax.dot_general(qm_ref[c], kc, (((1,), (1,)), ((), ())), preferred_element_type=F32)
        return jnp.where(causal, s, NEG_INF) if masked else s

    def plain_step(tiles):
        for h in range(n_heads):
            kvs = [kv_tile(j, h) for j, _ in tiles]
            for mp in range(2):
                c = 2 * h + mp
                pv = None
                for (kc, vext), (_, masked) in zip(kvs, tiles):
                    part = jnp.dot(jnp.exp2(scores(c, kc, masked)).astype(BF16), vext,
                                   preferred_element_type=F32)
                    pv = part if pv is None else pv + part
                acc_ref[c] += pv

    def running_max_step(j, masked):
        for h in range(n_heads):
            kc, vext = kv_tile(j, h)
            for mp in range(2):
                c = 2 * h + mp
                s = scores(c, kc, masked)
                m_old = m_ref[c]
                m_new = jnp.maximum(m_old, jnp.max(s, axis=-1, keepdims=True))
                p = jnp.exp2(s - m_new).astype(BF16)
                acc_ref[c] = (jnp.exp2(m_old - m_new) * acc_ref[c]
                              + jnp.dot(p, vext, preferred_element_type=F32))
                m_ref[c] = m_new

    @pl.when(shift_ref[0] == 0)
    def _():
        def pair(jj, carry):
            plain_step([(2 * jj, False), (2 * jj + 1, False)])
            return carry

        lax.fori_loop(0, qi // 2, pair, 0)

        @pl.when(qi % 2 == 1)
        def _():
            plain_step([(qi - 1, False), (qi, True)])

        @pl.when(qi % 2 == 0)
        def _():
            plain_step([(qi, True)])

    @pl.when(shift_ref[0] != 0)
    def _():
        m_ref[...] = jnp.full(m_ref.shape, NEG_INF, F32)

        def single(j, carry):
            running_max_step(j, False)
            return carry

        lax.fori_loop(0, qi, single, 0)
        running_max_step(qi, True)

    for h in range(n_heads):
        a1 = acc_ref[2 * h]
        a2 = acc_ref[2 * h + 1]
        att = a1[:, :vd] / a1[:, vd:] - lam * (a2[:, :vd] / a2[:, vd:])
        ms = jnp.mean(att * att, axis=-1, keepdims=True)
        o_ref[0, :, h * vd:(h + 1) * vd] = (
            att * lax.rsqrt(ms + RMS_EPS) * gout_ref[...] * (1.0 - lambda_init)).astype(BF16)


def _attn_static_kernel(lam_ref, gout_ref, q_ref, k_ref, v_ref, o_ref, vext_ref, *, tq, lambda_init, head_dim):
    s_len = q_ref.shape[1]
    hw = 2 * head_dim
    vd = v_ref.shape[-1]
    lv = lam_ref[...]
    lam = (jnp.exp(jnp.sum(lv[0:1] * lv[1:2], axis=-1, keepdims=True))
           - jnp.exp(jnp.sum(lv[2:3] * lv[3:4], axis=-1, keepdims=True))
           + lambda_init)
    first = lax.broadcasted_iota(jnp.int32, (1, hw), 1) < head_dim
    causal = (lax.broadcasted_iota(jnp.int32, (tq, tq), 1)
              <= lax.broadcasted_iota(jnp.int32, (tq, tq), 0))
    nt = (((1,), (1,)), ((), ()))

    vext_ref[:, 0:vd] = v_ref[0]
    vext_ref[:, vd:2 * vd] = jnp.ones((s_len, vd), BF16)

    for qi in range(s_len // tq):
        lo = qi * tq
        q = q_ref[0, lo:lo + tq, :]
        zero = jnp.zeros_like(q)
        acc = []
        for mp in range(2):
            qm = jnp.where(first, q, zero) if mp == 0 else jnp.where(first, zero, q)
            sd = lax.dot_general(qm, k_ref[0, lo:lo + tq, :], nt, preferred_element_type=F32)
            pd = jnp.exp2(jnp.where(causal, sd, NEG_INF)).astype(BF16)
            pv = jnp.dot(pd, vext_ref[lo:lo + tq, :], preferred_element_type=F32)
            if qi > 0:
                s = lax.dot_general(qm, k_ref[0, 0:lo, :], nt, preferred_element_type=F32)
                pv = pv + jnp.dot(jnp.exp2(s).astype(BF16), vext_ref[0:lo, :], preferred_element_type=F32)
            acc.append(pv)
        a1, a2 = acc
        att = a1[:, :vd] / a1[:, vd:] - lam * (a2[:, :vd] / a2[:, vd:])
        ms = jnp.mean(att * att, axis=-1, keepdims=True)
        o_ref[0, lo:lo + tq, :] = (
            att * lax.rsqrt(ms + RMS_EPS) * gout_ref[...] * (1.0 - lambda_init)).astype(BF16)


def _attention_static(lam_vecs, gout, qk3, v3, *, n_heads, head_dim, lambda_init):
    b, s, _ = qk3.shape
    vd = v3.shape[-1] // n_heads
    hw = 2 * head_dim
    kern = functools.partial(_attn_static_kernel, tq=QBLOCK, lambda_init=lambda_init, head_dim=head_dim)
    return pl.pallas_call(
        kern,
        grid=(b, n_heads),
        in_specs=[
            pl.BlockSpec(lam_vecs.shape, lambda bi, h: (0, 0)),
            pl.BlockSpec((1, vd), lambda bi, h: (0, 0)),
            pl.BlockSpec((1, s, hw), lambda bi, h: (bi, 0, h)),
            pl.BlockSpec((1, s, hw), lambda bi, h: (bi, 0, n_heads + h)),
            pl.BlockSpec((1, s, vd), lambda bi, h: (bi, 0, h)),
        ],
        out_specs=pl.BlockSpec((1, s, vd), lambda bi, h: (bi, 0, h)),
        out_shape=jax.ShapeDtypeStruct(v3.shape, BF16),
        scratch_shapes=[pltpu.VMEM((s, 2 * vd), BF16)],
        compiler_params=_params(2),
        name="diff_attn_static",
    )(lam_vecs, gout, qk3, qk3, v3)


def _attention(need_shift, lam_vecs, gout, qk3, v3, *, n_heads, head_dim, lambda_init):
    b, s, _ = qk3.shape
    att_cols = v3.shape[-1]
    vd = att_cols // n_heads
    tq = QBLOCK
    kern = functools.partial(_attn_kernel, tq=tq, n_heads=n_heads, lambda_init=lambda_init,
                             head_dim=head_dim)
    grid_spec = pltpu.PrefetchScalarGridSpec(
        num_scalar_prefetch=1,
        grid=(b, s // tq),
        in_specs=[
            pl.BlockSpec(lam_vecs.shape, lambda bi, qi, f: (0, 0)),
            pl.BlockSpec((1, vd), lambda bi, qi, f: (0, 0)),
            pl.BlockSpec((1, tq, att_cols), lambda bi, qi, f: (bi, qi, 0)),
            pl.BlockSpec((1, s, att_cols), lambda bi, qi, f: (bi, 0, 1)),
            pl.BlockSpec((1, s, att_cols), lambda bi, qi, f: (bi, 0, 0)),
        ],
        out_specs=pl.BlockSpec((1, tq, att_cols), lambda bi, qi, f: (bi, qi, 0)),
        scratch_shapes=[
            pltpu.VMEM((2 * n_heads, tq, 2 * head_dim), BF16),
            pltpu.VMEM((2 * n_heads, tq, 1), F32),
            pltpu.VMEM((2 * n_heads, tq, 2 * vd), F32),
        ],
    )
    return pl.pallas_call(
        kern,
        grid_spec=grid_spec,
        out_shape=jax.ShapeDtypeStruct(v3.shape, BF16),
        compiler_params=_params(2),
        name="diff_attn",
    )(need_shift, lam_vecs, gout, qk3, qk3, v3)


def _mixout_kernel(att_ref, su_ref, sv_ref, x_ref, sw_ref, sbt_ref, sgn_ref, wo_ref, gffn_ref,
                   rwh_ref, rwl_ref, rb_ref, lstrict_ref, h_ref, hn_ref, route_ref, counts_ref, cnt_ref,
                   hn_prev_ref, *, n_sgu_groups, att_cols):
    step = pl.program_id(0)
    tm = x_ref.shape[0]
    gw = CHUNK
    r_id = lax.broadcasted_iota(jnp.int32, (gw, gw), 0)
    c_id = lax.broadcasted_iota(jnp.int32, (gw, gw), 1)
    tril = c_id <= r_id

    @pl.when(step == 0)
    def _():
        cnt_ref[...] = jnp.zeros_like(cnt_ref)
        hn_prev_ref[...] = jnp.zeros_like(hn_prev_ref)

    _route_tile(hn_prev_ref[...], (step > 0).astype(F32), rwh_ref, rwl_ref, rb_ref, lstrict_ref,
                route_ref, counts_ref, cnt_ref)

    acc = jnp.dot(att_ref[...], wo_ref[0:att_cols, :], preferred_element_type=F32)

    sg_cols = []
    for g in range(n_sgu_groups):
        w = jnp.where(tril, sw_ref[g], 0.0).astype(BF16)
        bias = sbt_ref[:, g:g + 1]
        gn = sgn_ref[g:g + 1, :]
        rows = []
        for c in range(tm // gw):
            vblk = sv_ref[c * gw:(c + 1) * gw, g * gw:(g + 1) * gw]
            s = jnp.dot(w, vblk, preferred_element_type=F32) + bias
            sg = su_ref[c * gw:(c + 1) * gw, g * gw:(g + 1) * gw].astype(F32) * s
            ms = jnp.mean(sg * sg, axis=-1, keepdims=True)
            rows.append((sg * lax.rsqrt(ms + RMS_EPS) * gn).astype(BF16))
        sg_cols.append(jnp.concatenate(rows, axis=0))
    sgn = jnp.concatenate(sg_cols, axis=1)
    acc = acc + jnp.dot(sgn, wo_ref[att_cols:, :], preferred_element_type=F32)

    h = x_ref[...] + acc
    h_ref[...] = h
    ms = jnp.mean(h * h, axis=-1, keepdims=True)
    hn = h * lax.rsqrt(ms + RMS_EPS) * gffn_ref[...]
    hn_ref[...] = _pack_bf16_pairs(hn)
    hn_prev_ref[...] = hn


def _route_tile(hn, live, rwh_ref, rwl_ref, rb_ref, lstrict_ref, route_ref, counts_ref, cnt_ref):
    hn_bf, hn_lo = _split_bf16(hn)
    logits = (jnp.dot(hn_bf, rwh_ref[...], preferred_element_type=F32)
              + jnp.dot(hn_lo, rwh_ref[...], preferred_element_type=F32)
              + jnp.dot(hn_bf, rwl_ref[...], preferred_element_type=F32)
              + rb_ref[...])

    lane = lax.broadcasted_iota(jnp.int32, logits.shape, 1).astype(F32)
    big = float(LANES)
    is_grp = lane < N_GROUPS
    gl = jnp.where(is_grp, logits, NEG_INF)
    gmax = jnp.max(gl, axis=-1, keepdims=True)
    gidx = jnp.min(jnp.where(gl == gmax, lane, big), axis=-1, keepdims=True)
    psum = jnp.sum(jnp.where(is_grp, jnp.exp(gl - gmax), 0.0), axis=-1, keepdims=True)
    p_grp = 1.0 / psum
    e_lo = N_GROUPS + gidx * EXPERTS_PER_GROUP
    in_grp = (lane >= e_lo) & (lane < e_lo + EXPERTS_PER_GROUP)
    el = jnp.where(in_grp, logits, NEG_INF)
    t1 = jnp.max(el, axis=-1, keepdims=True)
    i1 = jnp.min(jnp.where(el == t1, lane, big), axis=-1, keepdims=True)
    el2 = jnp.where(lane == i1, NEG_INF, el)
    t2 = jnp.max(el2, axis=-1, keepdims=True)
    i2 = jnp.min(jnp.where(el2 == t2, lane, big), axis=-1, keepdims=True)
    e21 = jnp.exp(t2 - t1)
    den = 1.0 + e21
    g1 = p_grp / den
    g2 = p_grp * (e21 / den)
    e1 = i1 - N_GROUPS
    e2 = i2 - N_GROUPS

    hit1 = lane == e1
    hit2 = lane == e2
    onehot = jnp.where(hit1 | hit2, live, 0.0)
    before = jnp.dot(lstrict_ref[...], onehot.astype(BF16), preferred_element_type=F32) + cnt_ref[...]
    r1 = jnp.sum(jnp.where(hit1, before, 0.0), axis=-1, keepdims=True)
    r2 = jnp.sum(jnp.where(hit2, before, 0.0), axis=-1, keepdims=True)
    cnt_ref[...] = cnt_ref[...] + jnp.sum(onehot, axis=0, keepdims=True)
    counts_ref[...] = cnt_ref[...]

    route = jnp.zeros_like(logits)
    for k, val in enumerate((e1, e2, g1, g2, r1, r2)):
        route = jnp.where(lane == k, val, route)
    route_ref[...] = route


def _mixout(att, su, sv, x2d, sgu_w, sgu_bt, sgu_gn, w_out, g_ffn, rw_hi, rw_lo, rb, lstrict, *,
            att_cols):
    t, d = x2d.shape
    tm = ROW_TILE
    n = t // tm
    row = lambda i: (jnp.minimum(i, n - 1), 0)
    routed = lambda i: (jnp.maximum(i - 1, 0), 0)
    fixed2 = lambda i: (0, 0)
    n_groups = sgu_w.shape[0]
    kern = functools.partial(_mixout_kernel, n_sgu_groups=n_groups, att_cols=att_cols)
    return pl.pallas_call(
        kern,
        grid=(n + 1,),
        in_specs=[
            pl.BlockSpec((tm, att.shape[1]), row),
            pl.BlockSpec((tm, su.shape[1]), row),
            pl.BlockSpec((tm, sv.shape[1]), row),
            pl.BlockSpec((tm, d), row),
            pl.BlockSpec(sgu_w.shape, lambda i: (0, 0, 0)),
            pl.BlockSpec(sgu_bt.shape, fixed2),
            pl.BlockSpec(sgu_gn.shape, fixed2),
            pl.BlockSpec(w_out.shape, fixed2),
            pl.BlockSpec((1, d), fixed2),
            pl.BlockSpec(rw_hi.shape, fixed2),
            pl.BlockSpec(rw_lo.shape, fixed2),
            pl.BlockSpec((1, LANES), fixed2),
            pl.BlockSpec((tm, tm), fixed2),
        ],
        out_specs=[
            pl.BlockSpec((tm, d), row),
            pl.BlockSpec((tm, d // 2), row),
            pl.BlockSpec((tm, LANES), routed),
            pl.BlockSpec((1, LANES), fixed2),
        ],
        out_shape=[
            jax.ShapeDtypeStruct((t, d), F32),
            jax.ShapeDtypeStruct((t, d // 2), U32),
            jax.ShapeDtypeStruct((t, LANES), F32),
            jax.ShapeDtypeStruct((1, LANES), F32),
        ],
        scratch_shapes=[pltpu.VMEM((1, LANES), F32), pltpu.VMEM((tm, d), F32)],
        compiler_params=_params(1),
        name="mixout",
    )(att, su, sv, x2d, sgu_w, sgu_bt, sgu_gn, w_out, g_ffn, rw_hi, rw_lo, rb, lstrict)


def _dispatch_kernel(*refs):
    dest_refs, (hn_ref, xs_ref, sem) = refs[:TOP_K], refs[TOP_K:]
    rows = hn_ref.shape[0]

    for r in range(rows):
        for k in range(TOP_K):
            pltpu.make_async_copy(hn_ref.at[pl.ds(r, 1), :],
                                  xs_ref.at[pl.ds(dest_refs[k][r], 1), :], sem).start(priority=k)
    for k in range(TOP_K):
        pltpu.make_async_copy(hn_ref, xs_ref.at[pl.ds(0, rows), :], sem).wait()


def _dispatch(hn, dests):
    t, d = hn.shape
    rows = ROW_TILE
    index_spec = pl.BlockSpec((rows,), lambda i: (i,), memory_space=pltpu.SMEM)
    return pl.pallas_call(
        _dispatch_kernel,
        grid=(t // rows,),
        in_specs=[index_spec] * TOP_K + [pl.BlockSpec((rows, d), lambda i: (i, 0))],
        out_specs=pl.BlockSpec(memory_space=pl.ANY),
        out_shape=jax.ShapeDtypeStruct((t * TOP_K, d), hn.dtype),
        scratch_shapes=[pltpu.SemaphoreType.DMA],
        compiler_params=_params(1),
        name="dispatch",
    )(*dests, hn)


def _experts_kernel(vt_ref, ve_ref, vf_ref, vn_ref, lo_ref, hi_ref, xs_ref, wg_ref, wu_ref, wd_ref, ys_ref,
                    wg_bf, wu_bf, wd_bf):
    v = pl.program_id(0)
    flag = vf_ref[v]
    tm = xs_ref.shape[0]

    @pl.when(vn_ref[v] == 1)
    def _():
        wg_bf[...] = wg_ref[0].astype(BF16)
        wu_bf[...] = wu_ref[0].astype(BF16)
        wd_bf[...] = wd_ref[0].astype(BF16)

    @pl.when(flag > 0)
    def _():
        e = ve_ref[v]
        base = vt_ref[v] * tm
        rid = lax.broadcasted_iota(jnp.int32, (tm, 1), 0) + base
        mine = (rid >= lo_ref[e]) & (rid < hi_ref[e])
        x = _unpack_bf16_pairs(xs_ref[...]).astype(BF16)
        hg = jnp.dot(x, wg_bf[...], preferred_element_type=F32)
        hu = jnp.dot(x, wu_bf[...], preferred_element_type=F32)
        a = (hg * (1.0 / (1.0 + jnp.exp(-hg))) * hu).astype(BF16)
        y = _pack_bf16_pairs(jnp.dot(a, wd_bf[...], preferred_element_type=F32))

        @pl.when(flag == 1)
        def _():
            ys_ref[...] = jnp.where(mine, y, jnp.zeros_like(y))

        @pl.when(flag == 2)
        def _():
            ys_ref[...] = jnp.where(mine, y, ys_ref[...])


def _experts(visit_tile, visit_expert, visit_flag, visit_new, row_lo, row_hi, xs, w_gate, w_up, w_down):
    p, words = xs.shape
    d = 2 * words
    tm = EXPERT_TILE
    f = w_gate.shape[-1]
    rows_map = lambda v, vt, ve, vf, vn, lo, hi: (vt[v], 0)
    w_map = lambda v, vt, ve, vf, vn, lo, hi: (ve[v], 0, 0)
    grid_spec = pltpu.PrefetchScalarGridSpec(
        num_scalar_prefetch=6,
        grid=(visit_tile.shape[0],),
        in_specs=[
            pl.BlockSpec((tm, words), rows_map),
            pl.BlockSpec((1, d, f), w_map),
            pl.BlockSpec((1, d, f), w_map),
            pl.BlockSpec((1, f, d), w_map),
        ],
        out_specs=pl.BlockSpec((tm, words), rows_map),
        scratch_shapes=[pltpu.VMEM((d, f), BF16), pltpu.VMEM((d, f), BF16), pltpu.VMEM((f, d), BF16)],
    )
    return pl.pallas_call(
        _experts_kernel,
        grid_spec=grid_spec,
        out_shape=jax.ShapeDtypeStruct(xs.shape, xs.dtype),
        compiler_params=_params(1),
        name="experts",
    )(visit_tile, visit_expert, visit_flag, visit_new, row_lo, row_hi, xs, w_gate, w_up, w_down)


def _combine_kernel(*refs):
    dest_refs, dest_next_refs = refs[:TOP_K], refs[TOP_K:2 * TOP_K]
    h_ref, route_ref, ys_ref, o_ref, ybuf, sem = refs[2 * TOP_K:]
    i = pl.program_id(0)
    rows = h_ref.shape[0]
    slot = i % 2

    def issue(idx_refs, into):
        for r in range(rows):
            for k in range(TOP_K):
                pltpu.make_async_copy(ys_ref.at[pl.ds(idx_refs[k][r], 1), :],
                                      ybuf.at[into, k, pl.ds(r, 1), :], sem.at[into]).start(priority=k)

    @pl.when(i == 0)
    def _():
        issue(dest_refs, 0)

    @pl.when(i + 1 < pl.num_programs(0))
    def _():
        issue(dest_next_refs, 1 - slot)

    for k in range(TOP_K):
        pltpu.make_async_copy(ys_ref.at[pl.ds(0, rows), :], ybuf.at[slot, k], sem.at[slot]).wait()

    route = route_ref[...]
    moe = None
    for k in range(TOP_K):
        gated = route[:, ROUTE_GATE + k:ROUTE_GATE + k + 1] * _unpack_bf16_pairs(ybuf[slot, k])
        moe = gated if moe is None else moe + gated
    o_ref[...] = h_ref[...] + moe


def _combine(dests, h, route, ys):
    t, d = h.shape
    rows = COMBINE_TILE
    n = t // rows
    this_step = pl.BlockSpec((rows,), lambda i: (i,), memory_space=pltpu.SMEM)
    next_step = pl.BlockSpec((rows,), lambda i: (jnp.minimum(i + 1, n - 1),), memory_space=pltpu.SMEM)
    return pl.pallas_call(
        _combine_kernel,
        grid=(n,),
        in_specs=[this_step] * TOP_K + [next_step] * TOP_K + [
            pl.BlockSpec((rows, d), lambda i: (i, 0)),
            pl.BlockSpec((rows, LANES), lambda i: (i, 0)),
            pl.BlockSpec(memory_space=pl.ANY),
        ],
        out_specs=pl.BlockSpec((rows, d), lambda i: (i, 0)),
        out_shape=jax.ShapeDtypeStruct((t, d), F32),
        scratch_shapes=[pltpu.VMEM((2, TOP_K, rows, ys.shape[1]), ys.dtype), pltpu.SemaphoreType.DMA((2,))],
        compiler_params=_params(1),
        name="combine",
    )(*dests, *dests, h, route, ys)


def _plan(route, counts_row):
    t = route.shape[0]
    counts = counts_row[0, :N_EXPERTS].astype(I32)
    ends = jnp.cumsum(counts)
    offs = ends - counts
    route_t = route.T
    experts = route_t[ROUTE_EXPERT:ROUTE_EXPERT + TOP_K].astype(I32)
    rank = route_t[ROUTE_RANK:ROUTE_RANK + TOP_K].astype(I32)
    expert_ids = jnp.arange(N_EXPERTS, dtype=I32)[:, None, None]
    first_row = jnp.sum(jnp.where(experts[None] == expert_ids, offs[:, None, None], 0), axis=0)
    dest = first_row + rank
    dests = tuple(dest[k] for k in range(TOP_K))

    tm = EXPERT_TILE
    n_visits = (t * TOP_K) // tm + N_EXPERTS - 1
    first_tile = offs // tm
    last_tile = (ends - 1) // tm
    nvis = jnp.where(counts > 0, last_tile - first_tile + 1, 0)
    vend = jnp.cumsum(nvis)
    vstart = vend - nvis
    v = jnp.arange(n_visits, dtype=I32)
    ev = jnp.minimum(jnp.sum((vend[None, :] <= v[:, None]).astype(I32), axis=1), N_EXPERTS - 1)
    tv = jnp.take(first_tile, ev) + (v - jnp.take(vstart, ev))
    valid = v < vend[-1]
    last_v = vend[-1] - 1
    ev = jnp.where(valid, ev, jnp.take(ev, last_v))
    tv = jnp.where(valid, tv, jnp.take(tv, last_v))
    prev_t = jnp.concatenate([jnp.full((1,), -1, I32), tv[:-1]])
    prev_e = jnp.concatenate([jnp.full((1,), -1, I32), ev[:-1]])
    flag = jnp.where(valid, jnp.where(tv != prev_t, 1, 2), 0).astype(I32)
    new_expert = (ev != prev_e).astype(I32)
    return dests, tv.astype(I32), ev.astype(I32), flag, new_expert, offs.astype(I32), ends.astype(I32)


def _rope_constants(head_dim, rope_dim):
    rope_half = rope_dim // 2
    inv_freq = ROPE_THETA ** (-jnp.arange(0, rope_dim, 2, dtype=F32) / rope_dim)
    dim = jnp.arange(LANES, dtype=I32)[None, :] % head_dim
    k = jnp.arange(4 * rope_half, dtype=I32)[:, None]
    c_mat = jnp.where(dim < rope_dim, k == dim % rope_half, k == rope_dim)
    sa_mat = (dim < rope_half) & (k == rope_half + dim)
    sb_mat = (dim >= rope_half) & (dim < rope_dim) & (k == dim)
    expand = jnp.concatenate([c_mat.astype(F32), -sa_mat.astype(F32), sb_mat.astype(F32)], axis=1)
    return inv_freq.reshape(rope_half, 1), expand.astype(BF16)


def kernel(x, positions, norm_mix_g, w_in, q_norm_g, k_norm_g, lambda_q1, lambda_k1, lambda_q2,
           lambda_k2, diff_out_norm_g, sgu_ln_g, sgu_ln_b, sgu_w, sgu_b, sgu_out_norm_g, w_out,
           norm_ffn_g, router_group_w, router_group_b, router_expert_w, router_expert_b,
           expert_w_gate, expert_w_up, expert_w_down):
    b, s, d = x.shape
    t = b * s
    depth = w_in.shape[0]
    head_dim = q_norm_g.shape[-1]
    vd = diff_out_norm_g.shape[-1]
    n_sgu_groups = sgu_w.shape[1]
    sgu_cols = n_sgu_groups * sgu_w.shape[-1]
    in_cols = w_in.shape[-1]
    att_cols = (in_cols - 2 * sgu_cols) // 3
    n_heads = att_cols // vd
    qk_cols = 2 * att_cols
    rope_dim = head_dim // ROPE_FRACTION
    rope_half = rope_dim // 2
    scale = head_dim ** -0.5

    pos_rows = positions.reshape(t // ROW_TILE, 1, ROW_TILE).astype(F32)
    freq_col, expand = _rope_constants(head_dim, rope_dim)
    blk = jnp.arange(MXU_WIDTH, dtype=I32) // head_dim
    ones_blk = (blk[:, None] == blk[None, :]).astype(BF16)
    tri = jnp.arange(ROW_TILE, dtype=I32)
    lstrict = (tri[None, :] < tri[:, None]).astype(BF16)

    h = x.reshape(t, d)
    for l in range(depth):
        lambda_init = 0.8 - 0.6 * math.exp(-0.3 * l)
        gqk = jnp.concatenate([jnp.tile(q_norm_g[l] * (scale * math.log2(math.e)), att_cols // head_dim),
                               jnp.tile(k_norm_g[l], att_cols // head_dim)]).reshape(1, qk_cols)
        qk, v, su, sv = _inproj(
            h, norm_mix_g[l].reshape(1, d), w_in[l].astype(BF16), gqk, pos_rows, freq_col, expand, ones_blk,
            sgu_ln_g[l].reshape(1, sgu_cols), sgu_ln_b[l].reshape(1, sgu_cols),
            qk_cols=qk_cols, v_cols=att_cols, sgu_cols=sgu_cols, head_dim=head_dim,
            rope_half=rope_half)

        lam_vecs = jnp.stack([lambda_q1[l], lambda_k1[l], lambda_q2[l], lambda_k2[l]]).astype(F32)
        score_bound = (1.02 * head_dim * scale * jnp.max(jnp.abs(q_norm_g[l]))
                       * jnp.max(jnp.abs(k_norm_g[l])))
        need_shift = (score_bound > MAX_UNSHIFTED_SCORE).astype(I32).reshape(1)
        attn_args = (lam_vecs, diff_out_norm_g[l].reshape(1, vd), qk.reshape(b, s, qk_cols),
                     v.reshape(b, s, att_cols))
        attn_kw = dict(n_heads=n_heads, head_dim=head_dim, lambda_init=lambda_init)
        att = lax.cond(need_shift[0] != 0,
                       lambda *a: _attention(need_shift, *a, **attn_kw),
                       lambda *a: _attention_static(*a, **attn_kw),
                       *attn_args).reshape(t, att_cols)

        rw = jnp.concatenate([router_group_w[l], router_expert_w[l]], axis=1)
        rw = jnp.pad(rw, ((0, 0), (0, LANES - rw.shape[1])))
        rw_hi, rw_lo = _split_bf16(rw)
        rb = jnp.pad(jnp.concatenate([router_group_b[l], router_expert_b[l]]),
                     (0, LANES - N_GROUPS - N_EXPERTS)).reshape(1, LANES)
        h, hn, route, counts_row = _mixout(
            att, su, sv, h, sgu_w[l], sgu_b[l].T, sgu_out_norm_g[l], w_out[l].astype(BF16),
            norm_ffn_g[l].reshape(1, d), rw_hi, rw_lo, rb, lstrict, att_cols=att_cols)

        dests, visit_tile, visit_expert, visit_flag, visit_new, row_lo, row_hi = _plan(route, counts_row)
        xs = _dispatch(hn, dests)
        ys = _experts(visit_tile, visit_expert, visit_flag, visit_new, row_lo, row_hi, xs,
                      expert_w_gate[l], expert_w_up[l], expert_w_down[l])
        h = _combine(dests, h, route, ys)
    return h.reshape(b, s, d)
```

```python
import functools
import math

import jax
import jax.numpy as jnp
from jax import lax
from jax.experimental import pallas as pl
from jax.experimental.pallas import tpu as pltpu

F32 = jnp.float32
BF16 = jnp.bfloat16
I32 = jnp.int32

RMS_EPS = 1e-6
NEG_INF = -1e30
ROPE_THETA = 500000.0
MAX_UNSHIFTED_SCORE = 40.0

LANES = 128
SUBLANES = 8
MXU_WIDTH = 256
VMEM_LIMIT_BYTES = 56 * 1024 * 1024

N_GROUPS = 4
EXPERTS_PER_GROUP = 8
N_EXPERTS = N_GROUPS * EXPERTS_PER_GROUP
TOP_K = 2
ROPE_FRACTION = 4
CHUNK = 128
QBLOCK = 256
ROW_TILE = 512
EXPERT_TILE = 512
COMBINE_TILE = 256

ROUTE_EXPERT = 0
ROUTE_GATE = 2
ROUTE_RANK = 4


def _params(n_grid_dims):
    return pltpu.CompilerParams(
        dimension_semantics=("arbitrary",) * n_grid_dims,
        vmem_limit_bytes=VMEM_LIMIT_BYTES,
    )


def _gelu(x):
    return 0.5 * x * (1.0 + lax.erf(x * (1.0 / math.sqrt(2.0))))


def _split_bf16(x):
    hi = x.astype(BF16)
    return hi, (x - hi.astype(F32)).astype(BF16)


U32 = jnp.uint32
HIGH_HALF = 0xFFFF0000


def _pack_bf16_pairs(x):
    n = x.shape[1] // 2
    hi = lax.bitcast_convert_type(x[:, :n].astype(BF16).astype(F32), U32)
    lo = lax.bitcast_convert_type(x[:, n:].astype(BF16).astype(F32), U32)
    return hi | (lo >> 16)


def _unpack_bf16_pairs(w):
    hi = lax.bitcast_convert_type(w & jnp.uint32(HIGH_HALF), F32)
    lo = lax.bitcast_convert_type(w << 16, F32)
    return jnp.concatenate([hi, lo], axis=1)


def _inproj_kernel(x_ref, g_ref, w_ref, gqk_ref, pos_ref, freq_ref, expand_ref, ones_ref, lng_ref, lnb_ref,
                   qk_ref, v_ref, su_ref, sv_ref, proj_ref, ss_ref, *, qk_cols, v_cols, sgu_cols, head_dim,
                   rope_half):
    x = x_ref[...]
    ms = jnp.mean(x * x, axis=-1, keepdims=True)
    xn = (x * lax.rsqrt(ms + RMS_EPS) * g_ref[...]).astype(BF16)
    cw = MXU_WIDTH

    ang_t = freq_ref[...] * pos_ref[0]
    cs_t = jnp.concatenate([jnp.cos(ang_t), jnp.sin(ang_t), jnp.ones_like(ang_t),
                            jnp.zeros_like(ang_t)], axis=0)
    contract0 = (((0,), (0,)), ((), ()))
    cs_hi, cs_lo = _split_bf16(cs_t)
    tab = (lax.dot_general(cs_hi, expand_ref[...], contract0, preferred_element_type=F32)
           + lax.dot_general(cs_lo, expand_ref[...], contract0, preferred_element_type=F32))
    cos = tab[:, 0:LANES]
    sa = tab[:, LANES:2 * LANES]
    sb = tab[:, 2 * LANES:3 * LANES]

    def stage(c):
        cols = slice(c * cw, (c + 1) * cw)
        proj_ref[:, cols] = jnp.dot(xn, w_ref[:, cols], preferred_element_type=F32)

    n_qk = qk_cols // cw
    for c in range(n_qk):
        stage(c)
    for c in range(n_qk):
        p = proj_ref[:, c * cw:(c + 1) * cw]
        ss_ref[:, c * cw:(c + 1) * cw] = jnp.dot((p * p).astype(BF16), ones_ref[...],
                                                 preferred_element_type=F32)
    for c in range(n_qk, w_ref.shape[1] // cw):
        stage(c)

    def proj(col0):
        return proj_ref[:, col0:col0 + cw]

    for c in range(qk_cols // cw):
        p = proj(c * cw)
        ss = ss_ref[:, c * cw:(c + 1) * cw]
        pn = p * lax.rsqrt(ss * (1.0 / head_dim) + RMS_EPS) * gqk_ref[:, c * cw:(c + 1) * cw]
        for hh in range(cw // LANES):
            blk = pn[:, hh * LANES:(hh + 1) * LANES]
            rot = (blk * cos
                   + pltpu.roll(blk, LANES - rope_half, 1) * sa
                   + pltpu.roll(blk, rope_half, 1) * sb)
            col = c * cw + hh * LANES
            qk_ref[:, col:col + LANES] = rot.astype(BF16)

    for c in range(v_cols // cw):
        v_ref[:, c * cw:(c + 1) * cw] = proj(qk_cols + c * cw).astype(BF16)

    for c in range(sgu_cols // cw):
        su_ref[:, c * cw:(c + 1) * cw] = _gelu(proj(qk_cols + v_cols + c * cw)).astype(BF16)

    sv = jnp.concatenate(
        [_gelu(proj(qk_cols + v_cols + sgu_cols + c * cw)) for c in range(sgu_cols // cw)], axis=1)
    mu = jnp.mean(sv, axis=-1, keepdims=True)
    svc = sv - mu
    var = jnp.mean(svc * svc, axis=-1, keepdims=True)
    sv_ref[...] = (svc * lax.rsqrt(var + RMS_EPS) * lng_ref[...] + lnb_ref[...]).astype(BF16)


def _inproj(x2d, norm_g, w_in, gqk, pos_rows, freq_col, expand, ones_blk, ln_g, ln_b, *, qk_cols, v_cols,
            sgu_cols, head_dim, rope_half):
    t, d = x2d.shape
    tm = ROW_TILE
    row = lambda i: (i, 0)
    fixed = lambda i: (0, 0)
    kern = functools.partial(_inproj_kernel, qk_cols=qk_cols, v_cols=v_cols, sgu_cols=sgu_cols,
                             head_dim=head_dim, rope_half=rope_half)
    return pl.pallas_call(
        kern,
        grid=(t // tm,),
        in_specs=[
            pl.BlockSpec((tm, d), row),
            pl.BlockSpec((1, d), fixed),
            pl.BlockSpec(w_in.shape, fixed),
            pl.BlockSpec((1, qk_cols), fixed),
            pl.BlockSpec((1, 1, tm), lambda i: (i, 0, 0)),
            pl.BlockSpec(freq_col.shape, fixed),
            pl.BlockSpec(expand.shape, fixed),
            pl.BlockSpec((MXU_WIDTH, MXU_WIDTH), fixed),
            pl.BlockSpec((1, sgu_cols), fixed),
            pl.BlockSpec((1, sgu_cols), fixed),
        ],
        out_specs=[
            pl.BlockSpec((tm, qk_cols), row),
            pl.BlockSpec((tm, v_cols), row),
            pl.BlockSpec((tm, sgu_cols), row),
            pl.BlockSpec((tm, sgu_cols), row),
        ],
        out_shape=[
            jax.ShapeDtypeStruct((t, qk_cols), BF16),
            jax.ShapeDtypeStruct((t, v_cols), BF16),
            jax.ShapeDtypeStruct((t, sgu_cols), BF16),
            jax.ShapeDtypeStruct((t, sgu_cols), BF16),
        ],
        scratch_shapes=[pltpu.VMEM((tm, w_in.shape[1]), F32), pltpu.VMEM((tm, qk_cols), F32)],
        compiler_params=_params(1),
        name="inproj",
    )(x2d, norm_g, w_in, gqk, pos_rows, freq_col, expand, ones_blk, ln_g, ln_b)


def _attn_kernel(shift_ref, lam_ref, gout_ref, q_ref, k_ref, v_ref, o_ref, qm_ref, m_ref, acc_ref, *, tq,
                 n_heads, lambda_init, head_dim):
    qi = pl.program_id(1)
    hw = 2 * head_dim
    vd = v_ref.shape[-1] // n_heads
    lv = lam_ref[...]
    lam = (jnp.exp(jnp.sum(lv[0:1] * lv[1:2], axis=-1, keepdims=True))
           - jnp.exp(jnp.sum(lv[2:3] * lv[3:4], axis=-1, keepdims=True))
           + lambda_init)

    first = lax.broadcasted_iota(jnp.int32, (1, hw), 1) < head_dim
    for h in range(n_heads):
        q = q_ref[0, :, h * hw:(h + 1) * hw]
        zero = jnp.zeros_like(q)
        qm_ref[2 * h] = jnp.where(first, q, zero)
        qm_ref[2 * h + 1] = jnp.where(first, zero, q)
    acc_ref[...] = jnp.zeros(acc_ref.shape, F32)

    ones = jnp.ones((tq, vd), BF16)
    causal = (lax.broadcasted_iota(jnp.int32, (tq, tq), 1)
              <= lax.broadcasted_iota(jnp.int32, (tq, tq), 0))

    def kv_tile(j, h):
        start = pl.multiple_of(j * tq, tq)
        kc = k_ref[0, pl.ds(start, tq), h * hw:(h + 1) * hw]
        vext = jnp.concatenate([v_ref[0, pl.ds(start, tq), h * vd:(h + 1) * vd], ones], axis=1)
        return kc, vext

    def scores(c, kc, masked):
        s = lax.dot_general(qm_ref[c], kc, (((1,), (1,)), ((), ())), preferred_element_type=F32)
        return jnp.where(causal, s, NEG_INF) if masked else s

    def plain_step(tiles):
        for h in range(n_heads):
            kvs = [kv_tile(j, h) for j, _ in tiles]
            for mp in range(2):
                c = 2 * h + mp
                pv = None
                for (kc, vext), (_, masked) in zip(kvs, tiles):
                    part = jnp.dot(jnp.exp2(scores(c, kc, masked)).astype(BF16), vext,
                                   preferred_element_type=F32)
                    pv = part if pv is None else pv + part
                acc_ref[c] += pv

    def running_max_step(j, masked):
        for h in range(n_heads):
            kc, vext = kv_tile(j, h)
            for mp in range(2):
                c = 2 * h + mp
                s = scores(c, kc, masked)
                m_old = m_ref[c]
                m_new = jnp.maximum(m_old, jnp.max(s, axis=-1, keepdims=True))
                p = jnp.exp2(s - m_new).astype(BF16)
                acc_ref[c] = (jnp.exp2(m_old - m_new) * acc_ref[c]
                              + jnp.dot(p, vext, preferred_element_type=F32))
                m_ref[c] = m_new

    @pl.when(shift_ref[0] == 0)
    def _():
        def pair(jj, carry):
            plain_step([(2 * jj, False), (2 * jj + 1, False)])
            return carry

        lax.fori_loop(0, qi // 2, pair, 0)

        @pl.when(qi % 2 == 1)
        def _():
            plain_step([(qi - 1, False), (qi, True)])

        @pl.when(qi % 2 == 0)
        def _():
            plain_step([(qi, True)])

    @pl.when(shift_ref[0] != 0)
    def _():
        m_ref[...] = jnp.full(m_ref.shape, NEG_INF, F32)

        def single(j, carry):
            running_max_step(j, False)
            return carry

        lax.fori_loop(0, qi, single, 0)
        running_max_step(qi, True)

    for h in range(n_heads):
        a1 = acc_ref[2 * h]
        a2 = acc_ref[2 * h + 1]
        att = a1[:, :vd] / a1[:, vd:] - lam * (a2[:, :vd] / a2[:, vd:])
        ms = jnp.mean(att * att, axis=-1, keepdims=True)
        o_ref[0, :, h * vd:(h + 1) * vd] = (
            att * lax.rsqrt(ms + RMS_EPS) * gout_ref[...] * (1.0 - lambda_init)).astype(BF16)


def _attn_static_kernel(lam_ref, gout_ref, q_ref, k_ref, v_ref, o_ref, vext_ref, *, tq, lambda_init, head_dim):
    s_len = q_ref.shape[1]
    hw = 2 * head_dim
    vd = v_ref.shape[-1]
    lv = lam_ref[...]
    lam = (jnp.exp(jnp.sum(lv[0:1] * lv[1:2], axis=-1, keepdims=True))
           - jnp.exp(jnp.sum(lv[2:3] * lv[3:4], axis=-1, keepdims=True))
           + lambda_init)
    first = lax.broadcasted_iota(jnp.int32, (1, hw), 1) < head_dim
    causal = (lax.broadcasted_iota(jnp.int32, (tq, tq), 1)
              <= lax.broadcasted_iota(jnp.int32, (tq, tq), 0))
    nt = (((1,), (1,)), ((), ()))

    vext_ref[:, 0:vd] = v_ref[0]
    vext_ref[:, vd:2 * vd] = jnp.ones((s_len, vd), BF16)

    for qi in range(s_len // tq):
        lo = qi * tq
        q = q_ref[0, lo:lo + tq, :]
        zero = jnp.zeros_like(q)
        acc = []
        for mp in range(2):
            qm = jnp.where(first, q, zero) if mp == 0 else jnp.where(first, zero, q)
            sd = lax.dot_general(qm, k_ref[0, lo:lo + tq, :], nt, preferred_element_type=F32)
            pd = jnp.exp2(jnp.where(causal, sd, NEG_INF)).astype(BF16)
            pv = jnp.dot(pd, vext_ref[lo:lo + tq, :], preferred_element_type=F32)
            if qi > 0:
                s = lax.dot_general(qm, k_ref[0, 0:lo, :], nt, preferred_element_type=F32)
                pv = pv + jnp.dot(jnp.exp2(s).astype(BF16), vext_ref[0:lo, :], preferred_element_type=F32)
            acc.append(pv)
        a1, a2 = acc
        att = a1[:, :vd] / a1[:, vd:] - lam * (a2[:, :vd] / a2[:, vd:])
        ms = jnp.mean(att * att, axis=-1, keepdims=True)
        o_ref[0, lo:lo + tq, :] = (
            att * lax.rsqrt(ms + RMS_EPS) * gout_ref[...] * (1.0 - lambda_init)).astype(BF16)


def _attention_static(lam_vecs, gout, qk3, v3, *, n_heads, head_dim, lambda_init):
    b, s, _ = qk3.shape
    vd = v3.shape[-1] // n_heads
    hw = 2 * head_dim
    kern = functools.partial(_attn_static_kernel, tq=QBLOCK, lambda_init=lambda_init, head_dim=head_dim)
    return pl.pallas_call(
        kern,
        grid=(b, n_heads),
        in_specs=[
            pl.BlockSpec(lam_vecs.shape, lambda bi, h: (0, 0)),
            pl.BlockSpec((1, vd), lambda bi, h: (0, 0)),
            pl.BlockSpec((1, s, hw), lambda bi, h: (bi, 0, h)),
            pl.BlockSpec((1, s, hw), lambda bi, h: (bi, 0, n_heads + h)),
            pl.BlockSpec((1, s, vd), lambda bi, h: (bi, 0, h)),
        ],
        out_specs=pl.BlockSpec((1, s, vd), lambda bi, h: (bi, 0, h)),
        out_shape=jax.ShapeDtypeStruct(v3.shape, BF16),
        scratch_shapes=[pltpu.VMEM((s, 2 * vd), BF16)],
        compiler_params=_params(2),
        name="diff_attn_static",
    )(lam_vecs, gout, qk3, qk3, v3)


def _attention(need_shift, lam_vecs, gout, qk3, v3, *, n_heads, head_dim, lambda_init):
    b, s, _ = qk3.shape
    att_cols = v3.shape[-1]
    vd = att_cols // n_heads
    tq = QBLOCK
    kern = functools.partial(_attn_kernel, tq=tq, n_heads=n_heads, lambda_init=lambda_init,
                             head_dim=head_dim)
    grid_spec = pltpu.PrefetchScalarGridSpec(
        num_scalar_prefetch=1,
        grid=(b, s // tq),
        in_specs=[
            pl.BlockSpec(lam_vecs.shape, lambda bi, qi, f: (0, 0)),
            pl.BlockSpec((1, vd), lambda bi, qi, f: (0, 0)),
            pl.BlockSpec((1, tq, att_cols), lambda bi, qi, f: (bi, qi, 0)),
            pl.BlockSpec((1, s, att_cols), lambda bi, qi, f: (bi, 0, 1)),
            pl.BlockSpec((1, s, att_cols), lambda bi, qi, f: (bi, 0, 0)),
        ],
        out_specs=pl.BlockSpec((1, tq, att_cols), lambda bi, qi, f: (bi, qi, 0)),
        scratch_shapes=[
            pltpu.VMEM((2 * n_heads, tq, 2 * head_dim), BF16),
            pltpu.VMEM((2 * n_heads, tq, 1), F32),
            pltpu.VMEM((2 * n_heads, tq, 2 * vd), F32),
        ],
    )
    return pl.pallas_call(
        kern,
        grid_spec=grid_spec,
        out_shape=jax.ShapeDtypeStruct(v3.shape, BF16),
        compiler_params=_params(2),
        name="diff_attn",
    )(need_shift, lam_vecs, gout, qk3, qk3, v3)


def _mixout_kernel(att_ref, su_ref, sv_ref, x_ref, sw_ref, sbt_ref, sgn_ref, wo_ref, gffn_ref,
                   rwc_ref, rwh_ref, rb_ref, lstrict_ref, h_ref, hn_ref, route_ref, counts_ref, cnt_ref,
                   h_prev_ref, *, n_sgu_groups, att_cols):
    step = pl.program_id(0)
    tm = x_ref.shape[0]
    gw = CHUNK
    r_id = lax.broadcasted_iota(jnp.int32, (gw, gw), 0)
    c_id = lax.broadcasted_iota(jnp.int32, (gw, gw), 1)
    tril = c_id <= r_id

    @pl.when(step == 0)
    def _():
        cnt_ref[...] = jnp.zeros_like(cnt_ref)
        h_prev_ref[...] = jnp.zeros_like(h_prev_ref)

    h_prev = h_prev_ref[...]
    ms_prev = jnp.mean(h_prev * h_prev, axis=-1, keepdims=True)
    hn = h_prev * lax.rsqrt(ms_prev + RMS_EPS) * gffn_ref[...]
    hn_ref[...] = _pack_bf16_pairs(hn)
    _route_tile(hn, (step > 0).astype(F32), rwc_ref, rwh_ref, rb_ref, lstrict_ref, route_ref, counts_ref,
                cnt_ref)

    acc = jnp.dot(att_ref[...], wo_ref[0:att_cols, :], preferred_element_type=F32)

    sg_cols = []
    for g in range(n_sgu_groups):
        w = jnp.where(tril, sw_ref[g], 0.0).astype(BF16)
        bias = sbt_ref[:, g:g + 1]
        gn = sgn_ref[g:g + 1, :]
        rows = []
        for c in range(tm // gw):
            vblk = sv_ref[c * gw:(c + 1) * gw, g * gw:(g + 1) * gw]
            s = jnp.dot(w, vblk, preferred_element_type=F32) + bias
            sg = su_ref[c * gw:(c + 1) * gw, g * gw:(g + 1) * gw].astype(F32) * s
            ms = jnp.mean(sg * sg, axis=-1, keepdims=True)
            rows.append((sg * lax.rsqrt(ms + RMS_EPS) * gn).astype(BF16))
        sg_cols.append(jnp.concatenate(rows, axis=0))
    sgn = jnp.concatenate(sg_cols, axis=1)
    acc = acc + jnp.dot(sgn, wo_ref[att_cols:, :], preferred_element_type=F32)

    h = x_ref[...] + acc
    h_ref[...] = h
    h_prev_ref[...] = h


def _route_tile(hn, live, rwc_ref, rwh_ref, rb_ref, lstrict_ref, route_ref, counts_ref, cnt_ref):
    hn_bf, hn_lo = _split_bf16(hn)
    both = jnp.dot(hn_bf, rwc_ref[...], preferred_element_type=F32)
    logits = (both[:, :LANES] + both[:, LANES:]
              + jnp.dot(hn_lo, rwh_ref[...], preferred_element_type=F32)
              + rb_ref[...])

    lane = lax.broadcasted_iota(jnp.int32, logits.shape, 1).astype(F32)
    big = float(LANES)
    is_grp = lane < N_GROUPS
    gl = jnp.where(is_grp, logits, NEG_INF)
    gmax = jnp.max(gl, axis=-1, keepdims=True)
    gidx = jnp.min(jnp.where(gl == gmax, lane, big), axis=-1, keepdims=True)
    psum = jnp.sum(jnp.where(is_grp, jnp.exp(gl - gmax), 0.0), axis=-1, keepdims=True)
    p_grp = 1.0 / psum
    e_lo = N_GROUPS + gidx * EXPERTS_PER_GROUP
    in_grp = (lane >= e_lo) & (lane < e_lo + EXPERTS_PER_GROUP)
    el = jnp.where(in_grp, logits, NEG_INF)
    t1 = jnp.max(el, axis=-1, keepdims=True)
    i1 = jnp.min(jnp.where(el == t1, lane, big), axis=-1, keepdims=True)
    el2 = jnp.where(lane == i1, NEG_INF, el)
    t2 = jnp.max(el2, axis=-1, keepdims=True)
    i2 = jnp.min(jnp.where(el2 == t2, lane, big), axis=-1, keepdims=True)
    e21 = jnp.exp(t2 - t1)
    den = 1.0 + e21
    g1 = p_grp / den
    g2 = p_grp * (e21 / den)
    e1 = i1 - N_GROUPS
    e2 = i2 - N_GROUPS

    hit1 = lane == e1
    hit2 = lane == e2
    onehot = jnp.where(hit1 | hit2, live, 0.0)
    before = jnp.dot(lstrict_ref[...], onehot.astype(BF16), preferred_element_type=F32) + cnt_ref[...]
    r1 = jnp.sum(jnp.where(hit1, before, 0.0), axis=-1, keepdims=True)
    r2 = jnp.sum(jnp.where(hit2, before, 0.0), axis=-1, keepdims=True)
    cnt_ref[...] = cnt_ref[...] + jnp.sum(onehot, axis=0, keepdims=True)
    counts_ref[...] = cnt_ref[...]

    route = jnp.zeros_like(logits)
    for k, val in enumerate((e1, e2, g1, g2, r1, r2)):
        route = jnp.where(lane == k, val, route)
    route_ref[...] = route


def _mixout(att, su, sv, x2d, sgu_w, sgu_bt, sgu_gn, w_out, g_ffn, rw_cat, rw_hi, rb, lstrict, *,
            att_cols):
    t, d = x2d.shape
    tm = ROW_TILE
    n = t // tm
    row = lambda i: (jnp.minimum(i, n - 1), 0)
    routed = lambda i: (jnp.maximum(i - 1, 0), 0)
    fixed2 = lambda i: (0, 0)
    n_groups = sgu_w.shape[0]
    kern = functools.partial(_mixout_kernel, n_sgu_groups=n_groups, att_cols=att_cols)
    return pl.pallas_call(
        kern,
        grid=(n + 1,),
        in_specs=[
            pl.BlockSpec((tm, att.shape[1]), row),
            pl.BlockSpec((tm, su.shape[1]), row),
            pl.BlockSpec((tm, sv.shape[1]), row),
            pl.BlockSpec((tm, d), row),
            pl.BlockSpec(sgu_w.shape, lambda i: (0, 0, 0)),
            pl.BlockSpec(sgu_bt.shape, fixed2),
            pl.BlockSpec(sgu_gn.shape, fixed2),
            pl.BlockSpec(w_out.shape, fixed2),
            pl.BlockSpec((1, d), fixed2),
            pl.BlockSpec(rw_cat.shape, fixed2),
            pl.BlockSpec(rw_hi.shape, fixed2),
            pl.BlockSpec((1, LANES), fixed2),
            pl.BlockSpec((tm, tm), fixed2),
        ],
        out_specs=[
            pl.BlockSpec((tm, d), row),
            pl.BlockSpec((tm, d // 2), routed),
            pl.BlockSpec((tm, LANES), routed),
            pl.BlockSpec((1, LANES), fixed2),
        ],
        out_shape=[
            jax.ShapeDtypeStruct((t, d), F32),
            jax.ShapeDtypeStruct((t, d // 2), U32),
            jax.ShapeDtypeStruct((t, LANES), F32),
            jax.ShapeDtypeStruct((1, LANES), F32),
        ],
        scratch_shapes=[pltpu.VMEM((1, LANES), F32), pltpu.VMEM((tm, d), F32)],
        compiler_params=_params(1),
        name="mixout",
    )(att, su, sv, x2d, sgu_w, sgu_bt, sgu_gn, w_out, g_ffn, rw_cat, rw_hi, rb, lstrict)


def _dispatch_kernel(*refs):
    dest_refs, (hn_ref, xs_ref, sem) = refs[:TOP_K], refs[TOP_K:]
    rows = hn_ref.shape[0]

    for r in range(rows):
        for k in range(TOP_K):
            pltpu.make_async_copy(hn_ref.at[pl.ds(r, 1), :],
                                  xs_ref.at[pl.ds(dest_refs[k][r], 1), :], sem).start(priority=k)
    for k in range(TOP_K):
        pltpu.make_async_copy(hn_ref, xs_ref.at[pl.ds(0, rows), :], sem).wait()


def _dispatch(hn, dests):
    t, d = hn.shape
    rows = ROW_TILE
    index_spec = pl.BlockSpec((rows,), lambda i: (i,), memory_space=pltpu.SMEM)
    return pl.pallas_call(
        _dispatch_kernel,
        grid=(t // rows,),
        in_specs=[index_spec] * TOP_K + [pl.BlockSpec((rows, d), lambda i: (i, 0))],
        out_specs=pl.BlockSpec(memory_space=pl.ANY),
        out_shape=jax.ShapeDtypeStruct((t * TOP_K, d), hn.dtype),
        scratch_shapes=[pltpu.SemaphoreType.DMA],
        compiler_params=_params(1),
        name="dispatch",
    )(*dests, hn)


def _experts_kernel(vt_ref, ve_ref, vf_ref, vn_ref, lo_ref, hi_ref, xs_ref, wg_ref, wu_ref, wd_ref, ys_ref,
                    wg_bf, wu_bf, wd_bf, act_ref):
    v = pl.program_id(0)
    flag = vf_ref[v]
    tm = xs_ref.shape[0]

    @pl.when(vn_ref[v] == 1)
    def _():
        wg_bf[...] = wg_ref[0].astype(BF16)
        wu_bf[...] = wu_ref[0].astype(BF16)
        wd_bf[...] = wd_ref[0].astype(BF16)

    @pl.when(flag > 0)
    def _():
        e = ve_ref[v]
        base = vt_ref[v] * tm
        rid = lax.broadcasted_iota(jnp.int32, (tm, 1), 0) + base
        mine = (rid >= lo_ref[e]) & (rid < hi_ref[e])
        x = _unpack_bf16_pairs(xs_ref[...]).astype(BF16)
        for n in range(act_ref.shape[1] // MXU_WIDTH):
            cols = slice(n * MXU_WIDTH, (n + 1) * MXU_WIDTH)
            hg = jnp.dot(x, wg_bf[:, cols], preferred_element_type=F32)
            hu = jnp.dot(x, wu_bf[:, cols], preferred_element_type=F32)
            act_ref[:, cols] = (hg * (1.0 / (1.0 + jnp.exp(-hg))) * hu).astype(BF16)
        y = _pack_bf16_pairs(jnp.dot(act_ref[...], wd_bf[...], preferred_element_type=F32))

        @pl.when(flag == 1)
        def _():
            ys_ref[...] = jnp.where(mine, y, jnp.zeros_like(y))

        @pl.when(flag == 2)
        def _():
            ys_ref[...] = jnp.where(mine, y, ys_ref[...])


def _experts(visit_tile, visit_expert, visit_flag, visit_new, row_lo, row_hi, xs, w_gate, w_up, w_down):
    p, words = xs.shape
    d = 2 * words
    tm = EXPERT_TILE
    f = w_gate.shape[-1]
    rows_map = lambda v, vt, ve, vf, vn, lo, hi: (vt[v], 0)
    w_map = lambda v, vt, ve, vf, vn, lo, hi: (ve[v], 0, 0)
    grid_spec = pltpu.PrefetchScalarGridSpec(
        num_scalar_prefetch=6,
        grid=(visit_tile.shape[0],),
        in_specs=[
            pl.BlockSpec((tm, words), rows_map),
            pl.BlockSpec((1, d, f), w_map),
            pl.BlockSpec((1, d, f), w_map),
            pl.BlockSpec((1, f, d), w_map),
        ],
        out_specs=pl.BlockSpec((tm, words), rows_map),
        scratch_shapes=[pltpu.VMEM((d, f), BF16), pltpu.VMEM((d, f), BF16), pltpu.VMEM((f, d), BF16),
                        pltpu.VMEM((tm, f), BF16)],
    )
    return pl.pallas_call(
        _experts_kernel,
        grid_spec=grid_spec,
        out_shape=jax.ShapeDtypeStruct(xs.shape, xs.dtype),
        compiler_params=_params(1),
        name="experts",
    )(visit_tile, visit_expert, visit_flag, visit_new, row_lo, row_hi, xs, w_gate, w_up, w_down)


def _combine_kernel(*refs):
    dest_refs, dest_next_refs = refs[:TOP_K], refs[TOP_K:2 * TOP_K]
    h_ref, route_ref, ys_ref, o_ref, ybuf, sem = refs[2 * TOP_K:]
    i = pl.program_id(0)
    rows = h_ref.shape[0]
    slot = i % 2

    def issue(idx_refs, into):
        for r in range(rows):
            for k in range(TOP_K):
                pltpu.make_async_copy(ys_ref.at[pl.ds(idx_refs[k][r], 1), :],
                                      ybuf.at[into, k, pl.ds(r, 1), :], sem.at[into]).start(priority=k)

    @pl.when(i == 0)
    def _():
        issue(dest_refs, 0)

    @pl.when(i + 1 < pl.num_programs(0))
    def _():
        issue(dest_next_refs, 1 - slot)

    for k in range(TOP_K):
        pltpu.make_async_copy(ys_ref.at[pl.ds(0, rows), :], ybuf.at[slot, k], sem.at[slot]).wait()

    route = route_ref[...]
    moe = None
    for k in range(TOP_K):
        gated = route[:, ROUTE_GATE + k:ROUTE_GATE + k + 1] * _unpack_bf16_pairs(ybuf[slot, k])
        moe = gated if moe is None else moe + gated
    o_ref[...] = h_ref[...] + moe


def _combine(dests, h, route, ys):
    t, d = h.shape
    rows = COMBINE_TILE
    n = t // rows
    this_step = pl.BlockSpec((rows,), lambda i: (i,), memory_space=pltpu.SMEM)
    next_step = pl.BlockSpec((rows,), lambda i: (jnp.minimum(i + 1, n - 1),), memory_space=pltpu.SMEM)
    return pl.pallas_call(
        _combine_kernel,
        grid=(n,),
        in_specs=[this_step] * TOP_K + [next_step] * TOP_K + [
            pl.BlockSpec((rows, d), lambda i: (i, 0)),
            pl.BlockSpec((rows, LANES), lambda i: (i, 0)),
            pl.BlockSpec(memory_space=pl.ANY),
        ],
        out_specs=pl.BlockSpec((rows, d), lambda i: (i, 0)),
        out_shape=jax.ShapeDtypeStruct((t, d), F32),
        scratch_shapes=[pltpu.VMEM((2, TOP_K, rows, ys.shape[1]), ys.dtype), pltpu.SemaphoreType.DMA((2,))],
        compiler_params=_params(1),
        name="combine",
    )(*dests, *dests, h, route, ys)


def _plan(route, counts_row):
    t = route.shape[0]
    counts = counts_row[0, :N_EXPERTS].astype(I32)
    ends = jnp.cumsum(counts)
    offs = ends - counts
    route_t = route.T
    experts = route_t[ROUTE_EXPERT:ROUTE_EXPERT + TOP_K].astype(I32)
    rank = route_t[ROUTE_RANK:ROUTE_RANK + TOP_K].astype(I32)
    expert_ids = jnp.arange(N_EXPERTS, dtype=I32)[:, None, None]
    first_row = jnp.sum(jnp.where(experts[None] == expert_ids, offs[:, None, None], 0), axis=0)
    dest = first_row + rank
    dests = tuple(dest[k] for k in range(TOP_K))

    tm = EXPERT_TILE
    n_visits = (t * TOP_K) // tm + N_EXPERTS - 1
    first_tile = offs // tm
    last_tile = (ends - 1) // tm
    nvis = jnp.where(counts > 0, last_tile - first_tile + 1, 0)
    vend = jnp.cumsum(nvis)
    vstart = vend - nvis
    v = jnp.arange(n_visits, dtype=I32)
    ev = jnp.minimum(jnp.sum((vend[None, :] <= v[:, None]).astype(I32), axis=1), N_EXPERTS - 1)
    tv = jnp.take(first_tile, ev) + (v - jnp.take(vstart, ev))
    valid = v < vend[-1]
    last_v = vend[-1] - 1
    ev = jnp.where(valid, ev, jnp.take(ev, last_v))
    tv = jnp.where(valid, tv, jnp.take(tv, last_v))
    prev_t = jnp.concatenate([jnp.full((1,), -1, I32), tv[:-1]])
    prev_e = jnp.concatenate([jnp.full((1,), -1, I32), ev[:-1]])
    flag = jnp.where(valid, jnp.where(tv != prev_t, 1, 2), 0).astype(I32)
    new_expert = (ev != prev_e).astype(I32)
    return dests, tv.astype(I32), ev.astype(I32), flag, new_expert, offs.astype(I32), ends.astype(I32)


def _rope_constants(head_dim, rope_dim):
    rope_half = rope_dim // 2
    inv_freq = ROPE_THETA ** (-jnp.arange(0, rope_dim, 2, dtype=F32) / rope_dim)
    dim = jnp.arange(LANES, dtype=I32)[None, :] % head_dim
    k = jnp.arange(4 * rope_half, dtype=I32)[:, None]
    c_mat = jnp.where(dim < rope_dim, k == dim % rope_half, k == rope_dim)
    sa_mat = (dim < rope_half) & (k == rope_half + dim)
    sb_mat = (dim >= rope_half) & (dim < rope_dim) & (k == dim)
    expand = jnp.concatenate([c_mat.astype(F32), -sa_mat.astype(F32), sb_mat.astype(F32)], axis=1)
    return inv_freq.reshape(rope_half, 1), expand.astype(BF16)


def kernel(x, positions, norm_mix_g, w_in, q_norm_g, k_norm_g, lambda_q1, lambda_k1, lambda_q2,
           lambda_k2, diff_out_norm_g, sgu_ln_g, sgu_ln_b, sgu_w, sgu_b, sgu_out_norm_g, w_out,
           norm_ffn_g, router_group_w, router_group_b, router_expert_w, router_expert_b,
           expert_w_gate, expert_w_up, expert_w_down):
    b, s, d = x.shape
    t = b * s
    depth = w_in.shape[0]
    head_dim = q_norm_g.shape[-1]
    vd = diff_out_norm_g.shape[-1]
    n_sgu_groups = sgu_w.shape[1]
    sgu_cols = n_sgu_groups * sgu_w.shape[-1]
    in_cols = w_in.shape[-1]
    att_cols = (in_cols - 2 * sgu_cols) // 3
    n_heads = att_cols // vd
    qk_cols = 2 * att_cols
    rope_dim = head_dim // ROPE_FRACTION
    rope_half = rope_dim // 2
    scale = head_dim ** -0.5

    pos_rows = positions.reshape(t // ROW_TILE, 1, ROW_TILE).astype(F32)
    freq_col, expand = _rope_constants(head_dim, rope_dim)
    blk = jnp.arange(MXU_WIDTH, dtype=I32) // head_dim
    ones_blk = (blk[:, None] == blk[None, :]).astype(BF16)
    tri = jnp.arange(ROW_TILE, dtype=I32)
    lstrict = (tri[None, :] < tri[:, None]).astype(BF16)

    h = x.reshape(t, d)
    for l in range(depth):
        lambda_init = 0.8 - 0.6 * math.exp(-0.3 * l)
        gqk = jnp.concatenate([jnp.tile(q_norm_g[l] * (scale * math.log2(math.e)), att_cols // head_dim),
                               jnp.tile(k_norm_g[l], att_cols // head_dim)]).reshape(1, qk_cols)
        qk, v, su, sv = _inproj(
            h, norm_mix_g[l].reshape(1, d), w_in[l].astype(BF16), gqk, pos_rows, freq_col, expand, ones_blk,
            sgu_ln_g[l].reshape(1, sgu_cols), sgu_ln_b[l].reshape(1, sgu_cols),
            qk_cols=qk_cols, v_cols=att_cols, sgu_cols=sgu_cols, head_dim=head_dim,
            rope_half=rope_half)

        lam_vecs = jnp.stack([lambda_q1[l], lambda_k1[l], lambda_q2[l], lambda_k2[l]]).astype(F32)
        score_bound = (1.02 * head_dim * scale * jnp.max(jnp.abs(q_norm_g[l]))
                       * jnp.max(jnp.abs(k_norm_g[l])))
        need_shift = (score_bound > MAX_UNSHIFTED_SCORE).astype(I32).reshape(1)
        attn_args = (lam_vecs, diff_out_norm_g[l].reshape(1, vd), qk.reshape(b, s, qk_cols),
                     v.reshape(b, s, att_cols))
        attn_kw = dict(n_heads=n_heads, head_dim=head_dim, lambda_init=lambda_init)
        att = lax.cond(need_shift[0] != 0,
                       lambda *a: _attention(need_shift, *a, **attn_kw),
                       lambda *a: _attention_static(*a, **attn_kw),
                       *attn_args).reshape(t, att_cols)

        rw = jnp.concatenate([router_group_w[l], router_expert_w[l]], axis=1)
        rw = jnp.pad(rw, ((0, 0), (0, LANES - rw.shape[1])))
        rw_hi, rw_lo = _split_bf16(rw)
        rb = jnp.pad(jnp.concatenate([router_group_b[l], router_expert_b[l]]),
                     (0, LANES - N_GROUPS - N_EXPERTS)).reshape(1, LANES)
        h, hn, route, counts_row = _mixout(
            att, su, sv, h, sgu_w[l], sgu_b[l].T, sgu_out_norm_g[l], w_out[l].astype(BF16),
            norm_ffn_g[l].reshape(1, d), jnp.concatenate([rw_hi, rw_lo], axis=1), rw_hi, rb, lstrict,
            att_cols=att_cols)

        dests, visit_tile, visit_expert, visit_flag, visit_new, row_lo, row_hi = _plan(route, counts_row)
        xs = _dispatch(hn, dests)
        ys = _experts(visit_tile, visit_expert, visit_flag, visit_new, row_lo, row_hi, xs,
                      expert_w_gate[l], expert_w_up[l], expert_w_down[l])
        h = _combine(dests, h, route, ys)
    return h.reshape(b, s, d)
```

```python
import functools
import math

import jax
import jax.numpy as jnp
from jax import lax
from jax.experimental import pallas as pl
from jax.experimental.pallas import tpu as pltpu

F32 = jnp.float32
BF16 = jnp.bfloat16
I32 = jnp.int32

RMS_EPS = 1e-6
NEG_INF = -1e30
ROPE_THETA = 500000.0
MAX_UNSHIFTED_SCORE = 40.0

LANES = 128
SUBLANES = 8
MXU_WIDTH = 256
VMEM_LIMIT_BYTES = 56 * 1024 * 1024

N_GROUPS = 4
EXPERTS_PER_GROUP = 8
N_EXPERTS = N_GROUPS * EXPERTS_PER_GROUP
TOP_K = 2
ROPE_FRACTION = 4
CHUNK = 128
QBLOCK = 256
ROW_TILE = 512
EXPERT_TILE = 512
COMBINE_TILE = 256

ROUTE_EXPERT = 0
ROUTE_GATE = 2
ROUTE_RANK = 4


def _params(n_grid_dims):
    return pltpu.CompilerParams(
        dimension_semantics=("arbitrary",) * n_grid_dims,
        vmem_limit_bytes=VMEM_LIMIT_BYTES,
    )


def _gelu(x):
    return 0.5 * x * (1.0 + lax.erf(x * (1.0 / math.sqrt(2.0))))


def _split_bf16(x):
    hi = x.astype(BF16)
    return hi, (x - hi.astype(F32)).astype(BF16)


U32 = jnp.uint32
HIGH_HALF = 0xFFFF0000


def _pack_bf16_pairs(x):
    n = x.shape[1] // 2
    hi = lax.bitcast_convert_type(x[:, :n].astype(BF16).astype(F32), U32)
    lo = lax.bitcast_convert_type(x[:, n:].astype(BF16).astype(F32), U32)
    return hi | (lo >> 16)


def _unpack_bf16_pairs(w):
    hi = lax.bitcast_convert_type(w & jnp.uint32(HIGH_HALF), F32)
    lo = lax.bitcast_convert_type(w << 16, F32)
    return jnp.concatenate([hi, lo], axis=1)


def _inproj_kernel(x_ref, g_ref, w_ref, gqk_ref, pos_ref, freq_ref, expand_ref, ones_ref, lng_ref, lnb_ref,
                   qk_ref, v_ref, su_ref, sv_ref, proj_ref, ss_ref, *, qk_cols, v_cols, sgu_cols, head_dim,
                   rope_half):
    x = x_ref[...]
    ms = jnp.mean(x * x, axis=-1, keepdims=True)
    xn = (x * lax.rsqrt(ms + RMS_EPS) * g_ref[...]).astype(BF16)
    cw = MXU_WIDTH

    ang_t = freq_ref[...] * pos_ref[0]
    cs_t = jnp.concatenate([jnp.cos(ang_t), jnp.sin(ang_t), jnp.ones_like(ang_t),
                            jnp.zeros_like(ang_t)], axis=0)
    contract0 = (((0,), (0,)), ((), ()))
    cs_hi, cs_lo = _split_bf16(cs_t)
    tab = (lax.dot_general(cs_hi, expand_ref[...], contract0, preferred_element_type=F32)
           + lax.dot_general(cs_lo, expand_ref[...], contract0, preferred_element_type=F32))
    cos = tab[:, 0:LANES]
    sa = tab[:, LANES:2 * LANES]
    sb = tab[:, 2 * LANES:3 * LANES]

    def stage(c):
        cols = slice(c * cw, (c + 1) * cw)
        proj_ref[:, cols] = jnp.dot(xn, w_ref[:, cols], preferred_element_type=F32)

    n_qk = qk_cols // cw
    for c in range(n_qk):
        stage(c)
    for c in range(n_qk):
        p = proj_ref[:, c * cw:(c + 1) * cw]
        ss_ref[:, c * cw:(c + 1) * cw] = jnp.dot((p * p).astype(BF16), ones_ref[...],
                                                 preferred_element_type=F32)
    for c in range(n_qk, w_ref.shape[1] // cw):
        stage(c)

    def proj(col0):
        return proj_ref[:, col0:col0 + cw]

    for c in range(qk_cols // cw):
        p = proj(c * cw)
        ss = ss_ref[:, c * cw:(c + 1) * cw]
        pn = p * lax.rsqrt(ss * (1.0 / head_dim) + RMS_EPS) * gqk_ref[:, c * cw:(c + 1) * cw]
        for hh in range(cw // LANES):
            blk = pn[:, hh * LANES:(hh + 1) * LANES]
            rot = (blk * cos
                   + pltpu.roll(blk, LANES - rope_half, 1) * sa
                   + pltpu.roll(blk, rope_half, 1) * sb)
            col = c * cw + hh * LANES
            qk_ref[:, col:col + LANES] = rot.astype(BF16)

    for c in range(v_cols // cw):
        v_ref[:, c * cw:(c + 1) * cw] = proj(qk_cols + c * cw).astype(BF16)

    for c in range(sgu_cols // cw):
        su_ref[:, c * cw:(c + 1) * cw] = _gelu(proj(qk_cols + v_cols + c * cw)).astype(BF16)

    sv = jnp.concatenate(
        [_gelu(proj(qk_cols + v_cols + sgu_cols + c * cw)) for c in range(sgu_cols // cw)], axis=1)
    mu = jnp.mean(sv, axis=-1, keepdims=True)
    svc = sv - mu
    var = jnp.mean(svc * svc, axis=-1, keepdims=True)
    sv_ref[...] = (svc * lax.rsqrt(var + RMS_EPS) * lng_ref[...] + lnb_ref[...]).astype(BF16)


def _inproj(x2d, norm_g, w_in, gqk, pos_rows, freq_col, expand, ones_blk, ln_g, ln_b, *, qk_cols, v_cols,
            sgu_cols, head_dim, rope_half):
    t, d = x2d.shape
    tm = ROW_TILE
    row = lambda i: (i, 0)
    fixed = lambda i: (0, 0)
    kern = functools.partial(_inproj_kernel, qk_cols=qk_cols, v_cols=v_cols, sgu_cols=sgu_cols,
                             head_dim=head_dim, rope_half=rope_half)
    return pl.pallas_call(
        kern,
        grid=(t // tm,),
        in_specs=[
            pl.BlockSpec((tm, d), row),
            pl.BlockSpec((1, d), fixed),
            pl.BlockSpec(w_in.shape, fixed),
            pl.BlockSpec((1, qk_cols), fixed),
            pl.BlockSpec((1, 1, tm), lambda i: (i, 0, 0)),
            pl.BlockSpec(freq_col.shape, fixed),
            pl.BlockSpec(expand.shape, fixed),
            pl.BlockSpec((MXU_WIDTH, MXU_WIDTH), fixed),
            pl.BlockSpec((1, sgu_cols), fixed),
            pl.BlockSpec((1, sgu_cols), fixed),
        ],
        out_specs=[
            pl.BlockSpec((tm, qk_cols), row),
            pl.BlockSpec((tm, v_cols), row),
            pl.BlockSpec((tm, sgu_cols), row),
            pl.BlockSpec((tm, sgu_cols), row),
        ],
        out_shape=[
            jax.ShapeDtypeStruct((t, qk_cols), BF16),
            jax.ShapeDtypeStruct((t, v_cols), BF16),
            jax.ShapeDtypeStruct((t, sgu_cols), BF16),
            jax.ShapeDtypeStruct((t, sgu_cols), BF16),
        ],
        scratch_shapes=[pltpu.VMEM((tm, w_in.shape[1]), F32), pltpu.VMEM((tm, qk_cols), F32)],
        compiler_params=_params(1),
        name="inproj",
    )(x2d, norm_g, w_in, gqk, pos_rows, freq_col, expand, ones_blk, ln_g, ln_b)


def _attn_kernel(shift_ref, lam_ref, gout_ref, q_ref, k_ref, v_ref, o_ref, qm_ref, m_ref, acc_ref, *, tq,
                 n_heads, lambda_init, head_dim):
    qi = pl.program_id(1)
    hw = 2 * head_dim
    vd = v_ref.shape[-1] // n_heads
    lv = lam_ref[...]
    lam = (jnp.exp(jnp.sum(lv[0:1] * lv[1:2], axis=-1, keepdims=True))
           - jnp.exp(jnp.sum(lv[2:3] * lv[3:4], axis=-1, keepdims=True))
           + lambda_init)

    first = lax.broadcasted_iota(jnp.int32, (1, hw), 1) < head_dim
    for h in range(n_heads):
        q = q_ref[0, :, h * hw:(h + 1) * hw]
        zero = jnp.zeros_like(q)
        qm_ref[2 * h] = jnp.where(first, q, zero)
        qm_ref[2 * h + 1] = jnp.where(first, zero, q)
    acc_ref[...] = jnp.zeros(acc_ref.shape, F32)

    ones = jnp.ones((tq, vd), BF16)
    causal = (lax.broadcasted_iota(jnp.int32, (tq, tq), 1)
              <= lax.broadcasted_iota(jnp.int32, (tq, tq), 0))

    def kv_tile(j, h):
        start = pl.multiple_of(j * tq, tq)
        kc = k_ref[0, pl.ds(start, tq), h * hw:(h + 1) * hw]
        vext = jnp.concatenate([v_ref[0, pl.ds(start, tq), h * vd:(h + 1) * vd], ones], axis=1)
        return kc, vext

    def scores(c, kc, masked):
        s = lax.dot_general(qm_ref[c], kc, (((1,), (1,)), ((), ())), preferred_element_type=F32)
        return jnp.where(causal, s, NEG_INF) if masked else s

    def plain_step(tiles):
        for h in range(n_heads):
            kvs = [kv_tile(j, h) for j, _ in tiles]
            for mp in range(2):
                c = 2 * h + mp
                pv = None
                for (kc, vext), (_, masked) in zip(kvs, tiles):
                    part = jnp.dot(jnp.exp2(scores(c, kc, masked)).astype(BF16), vext,
                                   preferred_element_type=F32)
                    pv = part if pv is None else pv + part
                acc_ref[c] += pv

    def running_max_step(j, masked):
        for h in range(n_heads):
            kc, vext = kv_tile(j, h)
            for mp in range(2):
                c = 2 * h + mp
                s = scores(c, kc, masked)
                m_old = m_ref[c]
                m_new = jnp.maximum(m_old, jnp.max(s, axis=-1, keepdims=True))
                p = jnp.exp2(s - m_new).astype(BF16)
                acc_ref[c] = (jnp.exp2(m_old - m_new) * acc_ref[c]
                              + jnp.dot(p, vext, preferred_element_type=F32))
                m_ref[c] = m_new

    @pl.when(shift_ref[0] == 0)
    def _():
        def pair(jj, carry):
            plain_step([(2 * jj, False), (2 * jj + 1, False)])
            return carry

        lax.fori_loop(0, qi // 2, pair, 0)

        @pl.when(qi % 2 == 1)
        def _():
            plain_step([(qi - 1, False), (qi, True)])

        @pl.when(qi % 2 == 0)
        def _():
            plain_step([(qi, True)])

    @pl.when(shift_ref[0] != 0)
    def _():
        m_ref[...] = jnp.full(m_ref.shape, NEG_INF, F32)

        def single(j, carry):
            running_max_step(j, False)
            return carry

        lax.fori_loop(0, qi, single, 0)
        running_max_step(qi, True)

    for h in range(n_heads):
        a1 = acc_ref[2 * h]
        a2 = acc_ref[2 * h + 1]
        att = a1[:, :vd] / a1[:, vd:] - lam * (a2[:, :vd] / a2[:, vd:])
        ms = jnp.mean(att * att, axis=-1, keepdims=True)
        o_ref[0, :, h * vd:(h + 1) * vd] = (
            att * lax.rsqrt(ms + RMS_EPS) * gout_ref[...] * (1.0 - lambda_init)).astype(BF16)


def _attn_static_kernel(lam_ref, gout_ref, q_ref, k_ref, v_ref, o_ref, vext_ref, *, tq, lambda_init, head_dim):
    s_len = q_ref.shape[1]
    hw = 2 * head_dim
    vd = v_ref.shape[-1]
    lv = lam_ref[...]
    lam = (jnp.exp(jnp.sum(lv[0:1] * lv[1:2], axis=-1, keepdims=True))
           - jnp.exp(jnp.sum(lv[2:3] * lv[3:4], axis=-1, keepdims=True))
           + lambda_init)
    first = lax.broadcasted_iota(jnp.int32, (1, hw), 1) < head_dim
    causal = (lax.broadcasted_iota(jnp.int32, (tq, tq), 1)
              <= lax.broadcasted_iota(jnp.int32, (tq, tq), 0))
    nt = (((1,), (1,)), ((), ()))

    vext_ref[:, 0:vd] = v_ref[0]
    vext_ref[:, vd:2 * vd] = jnp.ones((s_len, vd), BF16)

    for qi in range(s_len // tq):
        lo = qi * tq
        q = q_ref[0, lo:lo + tq, :]
        zero = jnp.zeros_like(q)
        acc = []
        for mp in range(2):
            qm = jnp.where(first, q, zero) if mp == 0 else jnp.where(first, zero, q)
            sd = lax.dot_general(qm, k_ref[0, lo:lo + tq, :], nt, preferred_element_type=F32)
            pd = jnp.exp2(jnp.where(causal, sd, NEG_INF)).astype(BF16)
            pv = jnp.dot(pd, vext_ref[lo:lo + tq, :], preferred_element_type=F32)
            if qi > 0:
                s = lax.dot_general(qm, k_ref[0, 0:lo, :], nt, preferred_element_type=F32)
                pv = pv + jnp.dot(jnp.exp2(s).astype(BF16), vext_ref[0:lo, :], preferred_element_type=F32)
            acc.append(pv)
        a1, a2 = acc
        att = a1[:, :vd] / a1[:, vd:] - lam * (a2[:, :vd] / a2[:, vd:])
        ms = jnp.mean(att * att, axis=-1, keepdims=True)
        o_ref[0, lo:lo + tq, :] = (
            att * lax.rsqrt(ms + RMS_EPS) * gout_ref[...] * (1.0 - lambda_init)).astype(BF16)


def _attention_static(lam_vecs, gout, qk3, v3, *, n_heads, head_dim, lambda_init):
    b, s, _ = qk3.shape
    vd = v3.shape[-1] // n_heads
    hw = 2 * head_dim
    kern = functools.partial(_attn_static_kernel, tq=QBLOCK, lambda_init=lambda_init, head_dim=head_dim)
    return pl.pallas_call(
        kern,
        grid=(b, n_heads),
        in_specs=[
            pl.BlockSpec(lam_vecs.shape, lambda bi, h: (0, 0)),
            pl.BlockSpec((1, vd), lambda bi, h: (0, 0)),
            pl.BlockSpec((1, s, hw), lambda bi, h: (bi, 0, h)),
            pl.BlockSpec((1, s, hw), lambda bi, h: (bi, 0, n_heads + h)),
            pl.BlockSpec((1, s, vd), lambda bi, h: (bi, 0, h)),
        ],
        out_specs=pl.BlockSpec((1, s, vd), lambda bi, h: (bi, 0, h)),
        out_shape=jax.ShapeDtypeStruct(v3.shape, BF16),
        scratch_shapes=[pltpu.VMEM((s, 2 * vd), BF16)],
        compiler_params=_params(2),
        name="diff_attn_static",
    )(lam_vecs, gout, qk3, qk3, v3)


def _attention(need_shift, lam_vecs, gout, qk3, v3, *, n_heads, head_dim, lambda_init):
    b, s, _ = qk3.shape
    att_cols = v3.shape[-1]
    vd = att_cols // n_heads
    tq = QBLOCK
    kern = functools.partial(_attn_kernel, tq=tq, n_heads=n_heads, lambda_init=lambda_init,
                             head_dim=head_dim)
    grid_spec = pltpu.PrefetchScalarGridSpec(
        num_scalar_prefetch=1,
        grid=(b, s // tq),
        in_specs=[
            pl.BlockSpec(lam_vecs.shape, lambda bi, qi, f: (0, 0)),
            pl.BlockSpec((1, vd), lambda bi, qi, f: (0, 0)),
            pl.BlockSpec((1, tq, att_cols), lambda bi, qi, f: (bi, qi, 0)),
            pl.BlockSpec((1, s, att_cols), lambda bi, qi, f: (bi, 0, 1)),
            pl.BlockSpec((1, s, att_cols), lambda bi, qi, f: (bi, 0, 0)),
        ],
        out_specs=pl.BlockSpec((1, tq, att_cols), lambda bi, qi, f: (bi, qi, 0)),
        scratch_shapes=[
            pltpu.VMEM((2 * n_heads, tq, 2 * head_dim), BF16),
            pltpu.VMEM((2 * n_heads, tq, 1), F32),
            pltpu.VMEM((2 * n_heads, tq, 2 * vd), F32),
        ],
    )
    return pl.pallas_call(
        kern,
        grid_spec=grid_spec,
        out_shape=jax.ShapeDtypeStruct(v3.shape, BF16),
        compiler_params=_params(2),
        name="diff_attn",
    )(need_shift, lam_vecs, gout, qk3, qk3, v3)


def _mixout_kernel(att_ref, su_ref, sv_ref, x_ref, sw_ref, sbt_ref, sgn_ref, wo_ref, gffn_ref,
                   rwc_ref, rwh_ref, rb_ref, lstrict_ref, h_ref, hn_ref, route_ref, counts_ref, cnt_ref,
                   h_prev_ref, *, n_sgu_groups, att_cols):
    step = pl.program_id(0)
    tm = x_ref.shape[0]
    gw = CHUNK
    r_id = lax.broadcasted_iota(jnp.int32, (gw, gw), 0)
    c_id = lax.broadcasted_iota(jnp.int32, (gw, gw), 1)
    tril = c_id <= r_id

    @pl.when(step == 0)
    def _():
        cnt_ref[...] = jnp.zeros_like(cnt_ref)
        h_prev_ref[...] = jnp.zeros_like(h_prev_ref)

    h_prev = h_prev_ref[...]
    ms_prev = jnp.mean(h_prev * h_prev, axis=-1, keepdims=True)
    hn = h_prev * lax.rsqrt(ms_prev + RMS_EPS) * gffn_ref[...]
    hn_ref[...] = _pack_bf16_pairs(hn)
    _route_tile(hn, (step > 0).astype(F32), rwc_ref, rwh_ref, rb_ref, lstrict_ref, route_ref, counts_ref,
                cnt_ref)

    acc = jnp.dot(att_ref[...], wo_ref[0:att_cols, :], preferred_element_type=F32)

    sg_cols = []
    for g in range(n_sgu_groups):
        w = jnp.where(tril, sw_ref[g], 0.0).astype(BF16)
        bias = sbt_ref[:, g:g + 1]
        gn = sgn_ref[g:g + 1, :]
        rows = []
        for c in range(tm // gw):
            vblk = sv_ref[c * gw:(c + 1) * gw, g * gw:(g + 1) * gw]
            s = jnp.dot(w, vblk, preferred_element_type=F32) + bias
            sg = su_ref[c * gw:(c + 1) * gw, g * gw:(g + 1) * gw].astype(F32) * s
            ms = jnp.mean(sg * sg, axis=-1, keepdims=True)
            rows.append((sg * lax.rsqrt(ms + RMS_EPS) * gn).astype(BF16))
        sg_cols.append(jnp.concatenate(rows, axis=0))
    sgn = jnp.concatenate(sg_cols, axis=1)
    acc = acc + jnp.dot(sgn, wo_ref[att_cols:, :], preferred_element_type=F32)

    h = x_ref[...] + acc
    h_ref[...] = h
    h_prev_ref[...] = h


def _route_tile(hn, live, rwc_ref, rwh_ref, rb_ref, lstrict_ref, route_ref, counts_ref, cnt_ref):
    hn_bf, hn_lo = _split_bf16(hn)
    both = jnp.dot(hn_bf, rwc_ref[...], preferred_element_type=F32)
    logits = (both[:, :LANES] + both[:, LANES:]
              + jnp.dot(hn_lo, rwh_ref[...], preferred_element_type=F32)
              + rb_ref[...])

    lane = lax.broadcasted_iota(jnp.int32, logits.shape, 1).astype(F32)
    big = float(LANES)
    is_grp = lane < N_GROUPS
    gl = jnp.where(is_grp, logits, NEG_INF)
    gmax = jnp.max(gl, axis=-1, keepdims=True)
    gidx = jnp.min(jnp.where(gl == gmax, lane, big), axis=-1, keepdims=True)
    psum = jnp.sum(jnp.where(is_grp, jnp.exp(gl - gmax), 0.0), axis=-1, keepdims=True)
    p_grp = 1.0 / psum
    e_lo = N_GROUPS + gidx * EXPERTS_PER_GROUP
    in_grp = (lane >= e_lo) & (lane < e_lo + EXPERTS_PER_GROUP)
    el = jnp.where(in_grp, logits, NEG_INF)
    t1 = jnp.max(el, axis=-1, keepdims=True)
    i1 = jnp.min(jnp.where(el == t1, lane, big), axis=-1, keepdims=True)
    el2 = jnp.where(lane == i1, NEG_INF, el)
    t2 = jnp.max(el2, axis=-1, keepdims=True)
    i2 = jnp.min(jnp.where(el2 == t2, lane, big), axis=-1, keepdims=True)
    e21 = jnp.exp(t2 - t1)
    den = 1.0 + e21
    g1 = p_grp / den
    g2 = p_grp * (e21 / den)
    e1 = i1 - N_GROUPS
    e2 = i2 - N_GROUPS

    hit1 = lane == e1
    hit2 = lane == e2
    onehot = jnp.where(hit1 | hit2, live, 0.0)
    before = jnp.dot(lstrict_ref[...], onehot.astype(BF16), preferred_element_type=F32) + cnt_ref[...]
    r1 = jnp.sum(jnp.where(hit1, before, 0.0), axis=-1, keepdims=True)
    r2 = jnp.sum(jnp.where(hit2, before, 0.0), axis=-1, keepdims=True)
    cnt_ref[...] = cnt_ref[...] + jnp.sum(onehot, axis=0, keepdims=True)
    counts_ref[...] = cnt_ref[...]

    route = jnp.zeros_like(logits)
    for k, val in enumerate((e1, e2, g1, g2, r1, r2)):
        route = jnp.where(lane == k, val, route)
    route_ref[...] = route


def _mixout(att, su, sv, x2d, sgu_w, sgu_bt, sgu_gn, w_out, g_ffn, rw_cat, rw_hi, rb, lstrict, *,
            att_cols):
    t, d = x2d.shape
    tm = ROW_TILE
    n = t // tm
    row = lambda i: (jnp.minimum(i, n - 1), 0)
    routed = lambda i: (jnp.maximum(i - 1, 0), 0)
    fixed2 = lambda i: (0, 0)
    n_groups = sgu_w.shape[0]
    kern = functools.partial(_mixout_kernel, n_sgu_groups=n_groups, att_cols=att_cols)
    return pl.pallas_call(
        kern,
        grid=(n + 1,),
        in_specs=[
            pl.BlockSpec((tm, att.shape[1]), row),
            pl.BlockSpec((tm, su.shape[1]), row),
            pl.BlockSpec((tm, sv.shape[1]), row),
            pl.BlockSpec((tm, d), row),
            pl.BlockSpec(sgu_w.shape, lambda i: (0, 0, 0)),
            pl.BlockSpec(sgu_bt.shape, fixed2),
            pl.BlockSpec(sgu_gn.shape, fixed2),
            pl.BlockSpec(w_out.shape, fixed2),
            pl.BlockSpec((1, d), fixed2),
            pl.BlockSpec(rw_cat.shape, fixed2),
            pl.BlockSpec(rw_hi.shape, fixed2),
            pl.BlockSpec((1, LANES), fixed2),
            pl.BlockSpec((tm, tm), fixed2),
        ],
        out_specs=[
            pl.BlockSpec((tm, d), row),
            pl.BlockSpec((tm, d // 2), routed),
            pl.BlockSpec((tm, LANES), routed),
            pl.BlockSpec((1, LANES), fixed2),
        ],
        out_shape=[
            jax.ShapeDtypeStruct((t, d), F32),
            jax.ShapeDtypeStruct((t, d // 2), U32),
            jax.ShapeDtypeStruct((t, LANES), F32),
            jax.ShapeDtypeStruct((1, LANES), F32),
        ],
        scratch_shapes=[pltpu.VMEM((1, LANES), F32), pltpu.VMEM((tm, d), F32)],
        compiler_params=_params(1),
        name="mixout",
    )(att, su, sv, x2d, sgu_w, sgu_bt, sgu_gn, w_out, g_ffn, rw_cat, rw_hi, rb, lstrict)


def _dispatch_kernel(*refs):
    dest_refs, (hn_ref, xs_ref, sem) = refs[:TOP_K], refs[TOP_K:]
    rows = hn_ref.shape[0]

    for r in range(rows):
        for k in range(TOP_K):
            pltpu.make_async_copy(hn_ref.at[pl.ds(r, 1), :],
                                  xs_ref.at[pl.ds(dest_refs[k][r], 1), :], sem).start(priority=k)
    for k in range(TOP_K):
        pltpu.make_async_copy(hn_ref, xs_ref.at[pl.ds(0, rows), :], sem).wait()


def _dispatch(hn, dests):
    t, d = hn.shape
    rows = ROW_TILE
    index_spec = pl.BlockSpec((rows,), lambda i: (i,), memory_space=pltpu.SMEM)
    return pl.pallas_call(
        _dispatch_kernel,
        grid=(t // rows,),
        in_specs=[index_spec] * TOP_K + [pl.BlockSpec((rows, d), lambda i: (i, 0))],
        out_specs=pl.BlockSpec(memory_space=pl.ANY),
        out_shape=jax.ShapeDtypeStruct((t * TOP_K, d), hn.dtype),
        scratch_shapes=[pltpu.SemaphoreType.DMA],
        compiler_params=_params(1),
        name="dispatch",
    )(*dests, hn)


def _experts_kernel(vt_ref, ve_ref, vf_ref, vn_ref, vs_ref, vx_ref, lo_ref, hi_ref, xs_ref, wg_hbm, wu_hbm,
                    wd_hbm, ys_ref, wg_f32, wu_f32, wd_f32, wg_bf, wu_bf, wd_bf, act_ref, wsem):
    v = pl.program_id(0)
    flag = vf_ref[v]
    tm = xs_ref.shape[0]

    def weight_copies(expert, slot):
        return [pltpu.make_async_copy(src.at[expert], dst.at[slot], wsem.at[slot])
                for src, dst in ((wg_hbm, wg_f32), (wu_hbm, wu_f32), (wd_hbm, wd_f32))]

    @pl.when(v == 0)
    def _():
        for cp in weight_copies(ve_ref[0], 0):
            cp.start()

    @pl.when(vn_ref[v] == 1)
    def _():
        slot = vs_ref[v]
        for cp in weight_copies(ve_ref[v], slot):
            cp.wait()

        @pl.when(vx_ref[v] >= 0)
        def _():
            for cp in weight_copies(vx_ref[v], 1 - slot):
                cp.start()

        wg_bf[...] = wg_f32[slot].astype(BF16)
        wu_bf[...] = wu_f32[slot].astype(BF16)
        wd_bf[...] = wd_f32[slot].astype(BF16)

    @pl.when(flag > 0)
    def _():
        e = ve_ref[v]
        base = vt_ref[v] * tm
        rid = lax.broadcasted_iota(jnp.int32, (tm, 1), 0) + base
        mine = (rid >= lo_ref[e]) & (rid < hi_ref[e])
        x = _unpack_bf16_pairs(xs_ref[...]).astype(BF16)
        for n in range(act_ref.shape[1] // MXU_WIDTH):
            cols = slice(n * MXU_WIDTH, (n + 1) * MXU_WIDTH)
            hg = jnp.dot(x, wg_bf[:, cols], preferred_element_type=F32)
            hu = jnp.dot(x, wu_bf[:, cols], preferred_element_type=F32)
            act_ref[:, cols] = (hg * (1.0 / (1.0 + jnp.exp(-hg))) * hu).astype(BF16)
        y = _pack_bf16_pairs(jnp.dot(act_ref[...], wd_bf[...], preferred_element_type=F32))

        @pl.when(flag == 1)
        def _():
            ys_ref[...] = jnp.where(mine, y, jnp.zeros_like(y))

        @pl.when(flag == 2)
        def _():
            ys_ref[...] = jnp.where(mine, y, ys_ref[...])


def _experts(visits, row_lo, row_hi, xs, w_gate, w_up, w_down):
    p, words = xs.shape
    d = 2 * words
    tm = EXPERT_TILE
    f = w_gate.shape[-1]
    rows_map = lambda v, vt, *_: (vt[v], 0)
    whole = pl.BlockSpec(memory_space=pl.ANY)
    grid_spec = pltpu.PrefetchScalarGridSpec(
        num_scalar_prefetch=len(visits) + 2,
        grid=(visits[0].shape[0],),
        in_specs=[pl.BlockSpec((tm, words), rows_map), whole, whole, whole],
        out_specs=pl.BlockSpec((tm, words), rows_map),
        scratch_shapes=[pltpu.VMEM((2, d, f), F32), pltpu.VMEM((2, d, f), F32), pltpu.VMEM((2, f, d), F32),
                        pltpu.VMEM((d, f), BF16), pltpu.VMEM((d, f), BF16), pltpu.VMEM((f, d), BF16),
                        pltpu.VMEM((tm, f), BF16), pltpu.SemaphoreType.DMA((2,))],
    )
    return pl.pallas_call(
        _experts_kernel,
        grid_spec=grid_spec,
        out_shape=jax.ShapeDtypeStruct(xs.shape, xs.dtype),
        compiler_params=_params(1),
        name="experts",
    )(*visits, row_lo, row_hi, xs, w_gate, w_up, w_down)


def _combine_kernel(*refs):
    dest_refs, dest_next_refs = refs[:TOP_K], refs[TOP_K:2 * TOP_K]
    h_ref, route_ref, ys_ref, o_ref, ybuf, sem = refs[2 * TOP_K:]
    i = pl.program_id(0)
    rows = h_ref.shape[0]
    slot = i % 2

    def issue(idx_refs, into):
        for r in range(rows):
            for k in range(TOP_K):
                pltpu.make_async_copy(ys_ref.at[pl.ds(idx_refs[k][r], 1), :],
                                      ybuf.at[into, k, pl.ds(r, 1), :], sem.at[into]).start(priority=k)

    @pl.when(i == 0)
    def _():
        issue(dest_refs, 0)

    @pl.when(i + 1 < pl.num_programs(0))
    def _():
        issue(dest_next_refs, 1 - slot)

    for k in range(TOP_K):
        pltpu.make_async_copy(ys_ref.at[pl.ds(0, rows), :], ybuf.at[slot, k], sem.at[slot]).wait()

    route = route_ref[...]
    moe = None
    for k in range(TOP_K):
        gated = route[:, ROUTE_GATE + k:ROUTE_GATE + k + 1] * _unpack_bf16_pairs(ybuf[slot, k])
        moe = gated if moe is None else moe + gated
    o_ref[...] = h_ref[...] + moe


def _combine(dests, h, route, ys):
    t, d = h.shape
    rows = COMBINE_TILE
    n = t // rows
    this_step = pl.BlockSpec((rows,), lambda i: (i,), memory_space=pltpu.SMEM)
    next_step = pl.BlockSpec((rows,), lambda i: (jnp.minimum(i + 1, n - 1),), memory_space=pltpu.SMEM)
    return pl.pallas_call(
        _combine_kernel,
        grid=(n,),
        in_specs=[this_step] * TOP_K + [next_step] * TOP_K + [
            pl.BlockSpec((rows, d), lambda i: (i, 0)),
            pl.BlockSpec((rows, LANES), lambda i: (i, 0)),
            pl.BlockSpec(memory_space=pl.ANY),
        ],
        out_specs=pl.BlockSpec((rows, d), lambda i: (i, 0)),
        out_shape=jax.ShapeDtypeStruct((t, d), F32),
        scratch_shapes=[pltpu.VMEM((2, TOP_K, rows, ys.shape[1]), ys.dtype), pltpu.SemaphoreType.DMA((2,))],
        compiler_params=_params(1),
        name="combine",
    )(*dests, *dests, h, route, ys)


def _plan(route, counts_row):
    t = route.shape[0]
    counts = counts_row[0, :N_EXPERTS].astype(I32)
    ends = jnp.cumsum(counts)
    offs = ends - counts
    route_t = route.T
    experts = route_t[ROUTE_EXPERT:ROUTE_EXPERT + TOP_K].astype(I32)
    rank = route_t[ROUTE_RANK:ROUTE_RANK + TOP_K].astype(I32)
    expert_ids = jnp.arange(N_EXPERTS, dtype=I32)[:, None, None]
    first_row = jnp.sum(jnp.where(experts[None] == expert_ids, offs[:, None, None], 0), axis=0)
    dest = first_row + rank
    dests = tuple(dest[k] for k in range(TOP_K))

    tm = EXPERT_TILE
    n_visits = (t * TOP_K) // tm + N_EXPERTS - 1
    first_tile = offs // tm
    last_tile = (ends - 1) // tm
    nvis = jnp.where(counts > 0, last_tile - first_tile + 1, 0)
    vend = jnp.cumsum(nvis)
    vstart = vend - nvis
    v = jnp.arange(n_visits, dtype=I32)
    ev = jnp.minimum(jnp.sum((vend[None, :] <= v[:, None]).astype(I32), axis=1), N_EXPERTS - 1)
    tv = jnp.take(first_tile, ev) + (v - jnp.take(vstart, ev))
    valid = v < vend[-1]
    last_v = vend[-1] - 1
    ev = jnp.where(valid, ev, jnp.take(ev, last_v))
    tv = jnp.where(valid, tv, jnp.take(tv, last_v))
    prev_t = jnp.concatenate([jnp.full((1,), -1, I32), tv[:-1]])
    prev_e = jnp.concatenate([jnp.full((1,), -1, I32), ev[:-1]])
    flag = jnp.where(valid, jnp.where(tv != prev_t, 1, 2), 0).astype(I32)
    new_expert = (ev != prev_e).astype(I32)
    slot = (jnp.cumsum(new_expert) - 1) % 2
    ids = jnp.arange(N_EXPERTS, dtype=I32)
    later = (ids[None, :] > ids[:, None]) & (counts[None, :] > 0)
    next_active = jnp.min(jnp.where(later, ids[None, :], N_EXPERTS), axis=1)
    next_active = jnp.where(next_active < N_EXPERTS, next_active, -1)
    visits = (tv.astype(I32), ev.astype(I32), flag, new_expert, slot.astype(I32),
              jnp.take(next_active, ev).astype(I32))
    return dests, visits, offs.astype(I32), ends.astype(I32)


def _rope_constants(head_dim, rope_dim):
    rope_half = rope_dim // 2
    inv_freq = ROPE_THETA ** (-jnp.arange(0, rope_dim, 2, dtype=F32) / rope_dim)
    dim = jnp.arange(LANES, dtype=I32)[None, :] % head_dim
    k = jnp.arange(4 * rope_half, dtype=I32)[:, None]
    c_mat = jnp.where(dim < rope_dim, k == dim % rope_half, k == rope_dim)
    sa_mat = (dim < rope_half) & (k == rope_half + dim)
    sb_mat = (dim >= rope_half) & (dim < rope_dim) & (k == dim)
    expand = jnp.concatenate([c_mat.astype(F32), -sa_mat.astype(F32), sb_mat.astype(F32)], axis=1)
    return inv_freq.reshape(rope_half, 1), expand.astype(BF16)


def kernel(x, positions, norm_mix_g, w_in, q_norm_g, k_norm_g, lambda_q1, lambda_k1, lambda_q2,
           lambda_k2, diff_out_norm_g, sgu_ln_g, sgu_ln_b, sgu_w, sgu_b, sgu_out_norm_g, w_out,
           norm_ffn_g, router_group_w, router_group_b, router_expert_w, router_expert_b,
           expert_w_gate, expert_w_up, expert_w_down):
    b, s, d = x.shape
    t = b * s
    depth = w_in.shape[0]
    head_dim = q_norm_g.shape[-1]
    vd = diff_out_norm_g.shape[-1]
    n_sgu_groups = sgu_w.shape[1]
    sgu_cols = n_sgu_groups * sgu_w.shape[-1]
    in_cols = w_in.shape[-1]
    att_cols = (in_cols - 2 * sgu_cols) // 3
    n_heads = att_cols // vd
    qk_cols = 2 * att_cols
    rope_dim = head_dim // ROPE_FRACTION
    rope_half = rope_dim // 2
    scale = head_dim ** -0.5

    pos_rows = positions.reshape(t // ROW_TILE, 1, ROW_TILE).astype(F32)
    freq_col, expand = _rope_constants(head_dim, rope_dim)
    blk = jnp.arange(MXU_WIDTH, dtype=I32) // head_dim
    ones_blk = (blk[:, None] == blk[None, :]).astype(BF16)
    tri = jnp.arange(ROW_TILE, dtype=I32)
    lstrict = (tri[None, :] < tri[:, None]).astype(BF16)

    h = x.reshape(t, d)
    for l in range(depth):
        lambda_init = 0.8 - 0.6 * math.exp(-0.3 * l)
        gqk = jnp.concatenate([jnp.tile(q_norm_g[l] * (scale * math.log2(math.e)), att_cols // head_dim),
                               jnp.tile(k_norm_g[l], att_cols // head_dim)]).reshape(1, qk_cols)
        qk, v, su, sv = _inproj(
            h, norm_mix_g[l].reshape(1, d), w_in[l].astype(BF16), gqk, pos_rows, freq_col, expand, ones_blk,
            sgu_ln_g[l].reshape(1, sgu_cols), sgu_ln_b[l].reshape(1, sgu_cols),
            qk_cols=qk_cols, v_cols=att_cols, sgu_cols=sgu_cols, head_dim=head_dim,
            rope_half=rope_half)

        lam_vecs = jnp.stack([lambda_q1[l], lambda_k1[l], lambda_q2[l], lambda_k2[l]]).astype(F32)
        score_bound = (1.02 * head_dim * scale * jnp.max(jnp.abs(q_norm_g[l]))
                       * jnp.max(jnp.abs(k_norm_g[l])))
        need_shift = (score_bound > MAX_UNSHIFTED_SCORE).astype(I32).reshape(1)
        attn_args = (lam_vecs, diff_out_norm_g[l].reshape(1, vd), qk.reshape(b, s, qk_cols),
                     v.reshape(b, s, att_cols))
        attn_kw = dict(n_heads=n_heads, head_dim=head_dim, lambda_init=lambda_init)
        att = lax.cond(need_shift[0] != 0,
                       lambda *a: _attention(need_shift, *a, **attn_kw),
                       lambda *a: _attention_static(*a, **attn_kw),
                       *attn_args).reshape(t, att_cols)

        rw = jnp.concatenate([router_group_w[l], router_expert_w[l]], axis=1)
        rw = jnp.pad(rw, ((0, 0), (0, LANES - rw.shape[1])))
        rw_hi, rw_lo = _split_bf16(rw)
        rb = jnp.pad(jnp.concatenate([router_group_b[l], router_expert_b[l]]),
                     (0, LANES - N_GROUPS - N_EXPERTS)).reshape(1, LANES)
        h, hn, route, counts_row = _mixout(
            att, su, sv, h, sgu_w[l], sgu_b[l].T, sgu_out_norm_g[l], w_out[l].astype(BF16),
            norm_ffn_g[l].reshape(1, d), jnp.concatenate([rw_hi, rw_lo], axis=1), rw_hi, rb, lstrict,
            att_cols=att_cols)

        dests, visits, row_lo, row_hi = _plan(route, counts_row)
        xs = _dispatch(hn, dests)
        ys = _experts(visits, row_lo, row_hi, xs, expert_w_gate[l], expert_w_up[l], expert_w_down[l])
        h = _combine(dests, h, route, ys)
    return h.reshape(b, s, d)
```

```python
import functools
import math

import jax
import jax.numpy as jnp
from jax import lax
from jax.experimental import pallas as pl
from jax.experimental.pallas import tpu as pltpu

F32 = jnp.float32
BF16 = jnp.bfloat16
I32 = jnp.int32

RMS_EPS = 1e-6
NEG_INF = -1e30
ROPE_THETA = 500000.0
MAX_UNSHIFTED_SCORE = 40.0

LANES = 128
SUBLANES = 8
MXU_WIDTH = 256
VMEM_LIMIT_BYTES = 56 * 1024 * 1024

N_GROUPS = 4
EXPERTS_PER_GROUP = 8
N_EXPERTS = N_GROUPS * EXPERTS_PER_GROUP
TOP_K = 2
ROPE_FRACTION = 4
CHUNK = 128
QBLOCK = 256
ROW_TILE = 512
EXPERT_TILE = 512
DISPATCH_TILE = 1024
COMBINE_TILE = 512

ROUTE_EXPERT = 0
ROUTE_GATE = 2
ROUTE_RANK = 4


def _params(n_grid_dims):
    return pltpu.CompilerParams(
        dimension_semantics=("arbitrary",) * n_grid_dims,
        vmem_limit_bytes=VMEM_LIMIT_BYTES,
    )


def _gelu(x):
    return 0.5 * x * (1.0 + lax.erf(x * (1.0 / math.sqrt(2.0))))


def _split_bf16(x):
    hi = x.astype(BF16)
    return hi, (x - hi.astype(F32)).astype(BF16)


U32 = jnp.uint32
HIGH_HALF = 0xFFFF0000


def _pack_bf16_pairs(x):
    n = x.shape[1] // 2
    hi = lax.bitcast_convert_type(x[:, :n].astype(BF16).astype(F32), U32)
    lo = lax.bitcast_convert_type(x[:, n:].astype(BF16).astype(F32), U32)
    return hi | (lo >> 16)


def _unpack_bf16_pairs(w):
    hi = lax.bitcast_convert_type(w & jnp.uint32(HIGH_HALF), F32)
    lo = lax.bitcast_convert_type(w << 16, F32)
    return jnp.concatenate([hi, lo], axis=1)


def _inproj_kernel(x_ref, g_ref, w_ref, gqk_ref, pos_ref, freq_ref, expand_ref, ones_ref, lng_ref, lnb_ref,
                   qk_ref, v_ref, su_ref, sv_ref, proj_ref, ss_ref, *, qk_cols, v_cols, sgu_cols, head_dim,
                   rope_half):
    x = x_ref[...]
    ms = jnp.mean(x * x, axis=-1, keepdims=True)
    xn = (x * lax.rsqrt(ms + RMS_EPS) * g_ref[...]).astype(BF16)
    cw = MXU_WIDTH

    ang_t = freq_ref[...] * pos_ref[0]
    cs_t = jnp.concatenate([jnp.cos(ang_t), jnp.sin(ang_t), jnp.ones_like(ang_t),
                            jnp.zeros_like(ang_t)], axis=0)
    contract0 = (((0,), (0,)), ((), ()))
    cs_hi, cs_lo = _split_bf16(cs_t)
    tab = (lax.dot_general(cs_hi, expand_ref[...], contract0, preferred_element_type=F32)
           + lax.dot_general(cs_lo, expand_ref[...], contract0, preferred_element_type=F32))
    cos = tab[:, 0:LANES]
    sa = tab[:, LANES:2 * LANES]
    sb = tab[:, 2 * LANES:3 * LANES]

    def stage(c):
        cols = slice(c * cw, (c + 1) * cw)
        proj_ref[:, cols] = jnp.dot(xn, w_ref[:, cols], preferred_element_type=F32)

    n_qk = qk_cols // cw
    for c in range(n_qk):
        stage(c)
    for c in range(n_qk):
        p = proj_ref[:, c * cw:(c + 1) * cw]
        ss_ref[:, c * cw:(c + 1) * cw] = jnp.dot((p * p).astype(BF16), ones_ref[...],
                                                 preferred_element_type=F32)
    n_v = v_cols // cw
    for c in range(n_qk + n_v, w_ref.shape[1] // cw):
        stage(c)
    for c in range(n_qk, n_qk + n_v):
        stage(c)

    def proj(col0):
        return proj_ref[:, col0:col0 + cw]

    for c in range(qk_cols // cw):
        p = proj(c * cw)
        ss = ss_ref[:, c * cw:(c + 1) * cw]
        pn = p * lax.rsqrt(ss * (1.0 / head_dim) + RMS_EPS) * gqk_ref[:, c * cw:(c + 1) * cw]
        for hh in range(cw // LANES):
            blk = pn[:, hh * LANES:(hh + 1) * LANES]
            rot = (blk * cos
                   + pltpu.roll(blk, LANES - rope_half, 1) * sa
                   + pltpu.roll(blk, rope_half, 1) * sb)
            col = c * cw + hh * LANES
            qk_ref[:, col:col + LANES] = rot.astype(BF16)

    for c in range(v_cols // cw):
        v_ref[:, c * cw:(c + 1) * cw] = proj(qk_cols + c * cw).astype(BF16)

    for c in range(sgu_cols // cw):
        su_ref[:, c * cw:(c + 1) * cw] = _gelu(proj(qk_cols + v_cols + c * cw)).astype(BF16)

    sv = jnp.concatenate(
        [_gelu(proj(qk_cols + v_cols + sgu_cols + c * cw)) for c in range(sgu_cols // cw)], axis=1)
    mu = jnp.mean(sv, axis=-1, keepdims=True)
    svc = sv - mu
    var = jnp.mean(svc * svc, axis=-1, keepdims=True)
    sv_ref[...] = (svc * lax.rsqrt(var + RMS_EPS) * lng_ref[...] + lnb_ref[...]).astype(BF16)


def _inproj(x2d, norm_g, w_in, gqk, pos_rows, freq_col, expand, ones_blk, ln_g, ln_b, *, qk_cols, v_cols,
            sgu_cols, head_dim, rope_half):
    t, d = x2d.shape
    tm = ROW_TILE
    row = lambda i: (i, 0)
    fixed = lambda i: (0, 0)
    kern = functools.partial(_inproj_kernel, qk_cols=qk_cols, v_cols=v_cols, sgu_cols=sgu_cols,
                             head_dim=head_dim, rope_half=rope_half)
    return pl.pallas_call(
        kern,
        grid=(t // tm,),
        in_specs=[
            pl.BlockSpec((tm, d), row),
            pl.BlockSpec((1, d), fixed),
            pl.BlockSpec(w_in.shape, fixed),
            pl.BlockSpec((1, qk_cols), fixed),
            pl.BlockSpec((1, 1, tm), lambda i: (i, 0, 0)),
            pl.BlockSpec(freq_col.shape, fixed),
            pl.BlockSpec(expand.shape, fixed),
            pl.BlockSpec((MXU_WIDTH, MXU_WIDTH), fixed),
            pl.BlockSpec((1, sgu_cols), fixed),
            pl.BlockSpec((1, sgu_cols), fixed),
        ],
        out_specs=[
            pl.BlockSpec((tm, qk_cols), row),
            pl.BlockSpec((tm, v_cols), row),
            pl.BlockSpec((tm, sgu_cols), row),
            pl.BlockSpec((tm, sgu_cols), row),
        ],
        out_shape=[
            jax.ShapeDtypeStruct((t, qk_cols), BF16),
            jax.ShapeDtypeStruct((t, v_cols), BF16),
            jax.ShapeDtypeStruct((t, sgu_cols), BF16),
            jax.ShapeDtypeStruct((t, sgu_cols), BF16),
        ],
        scratch_shapes=[pltpu.VMEM((tm, w_in.shape[1]), F32), pltpu.VMEM((tm, qk_cols), F32)],
        compiler_params=_params(1),
        name="inproj",
    )(x2d, norm_g, w_in, gqk, pos_rows, freq_col, expand, ones_blk, ln_g, ln_b)


def _attn_kernel(shift_ref, lam_ref, gout_ref, q_ref, k_ref, v_ref, o_ref, qm_ref, m_ref, acc_ref, *, tq,
                 n_heads, lambda_init, head_dim):
    qi = pl.program_id(1)
    hw = 2 * head_dim
    vd = v_ref.shape[-1] // n_heads
    lv = lam_ref[...]
    lam = (jnp.exp(jnp.sum(lv[0:1] * lv[1:2], axis=-1, keepdims=True))
           - jnp.exp(jnp.sum(lv[2:3] * lv[3:4], axis=-1, keepdims=True))
           + lambda_init)

    first = lax.broadcasted_iota(jnp.int32, (1, hw), 1) < head_dim
    for h in range(n_heads):
        q = q_ref[0, :, h * hw:(h + 1) * hw]
        zero = jnp.zeros_like(q)
        qm_ref[2 * h] = jnp.where(first, q, zero)
        qm_ref[2 * h + 1] = jnp.where(first, zero, q)
    acc_ref[...] = jnp.zeros(acc_ref.shape, F32)

    ones = jnp.ones((tq, vd), BF16)
    causal = (lax.broadcasted_iota(jnp.int32, (tq, tq), 1)
              <= lax.broadcasted_iota(jnp.int32, (tq, tq), 0))

    def kv_tile(j, h):
        start = pl.multiple_of(j * tq, tq)
        kc = k_ref[0, pl.ds(start, tq), h * hw:(h + 1) * hw]
        vext = jnp.concatenate([v_ref[0, pl.ds(start, tq), h * vd:(h + 1) * vd], ones], axis=1)
        return kc, vext

    def scores(c, kc, masked):
        s = lax.dot_general(qm_ref[c], kc, (((1,), (1,)), ((), ())), preferred_element_type=F32)
        return jnp.where(causal, s, NEG_INF) if masked else s

    def plain_step(tiles):
        for h in range(n_heads):
            kvs = [kv_tile(j, h) for j, _ in tiles]
            for mp in range(2):
                c = 2 * h + mp
                pv = None
                for (kc, vext), (_, masked) in zip(kvs, tiles):
                    part = jnp.dot(jnp.exp2(scores(c, kc, masked)).astype(BF16), vext,
                                   preferred_element_type=F32)
                    pv = part if pv is None else pv + part
                acc_ref[c] += pv

    def running_max_step(j, masked):
        for h in range(n_heads):
            kc, vext = kv_tile(j, h)
            for mp in range(2):
                c = 2 * h + mp
                s = scores(c, kc, masked)
                m_old = m_ref[c]
                m_new = jnp.maximum(m_old, jnp.max(s, axis=-1, keepdims=True))
                p = jnp.exp2(s - m_new).astype(BF16)
                acc_ref[c] = (jnp.exp2(m_old - m_new) * acc_ref[c]
                              + jnp.dot(p, vext, preferred_element_type=F32))
                m_ref[c] = m_new

    @pl.when(shift_ref[0] == 0)
    def _():
        def pair(jj, carry):
            plain_step([(2 * jj, False), (2 * jj + 1, False)])
            return carry

        lax.fori_loop(0, qi // 2, pair, 0)

        @pl.when(qi % 2 == 1)
        def _():
            plain_step([(qi - 1, False), (qi, True)])

        @pl.when(qi % 2 == 0)
        def _():
            plain_step([(qi, True)])

    @pl.when(shift_ref[0] != 0)
    def _():
        m_ref[...] = jnp.full(m_ref.shape, NEG_INF, F32)

        def single(j, carry):
            running_max_step(j, False)
            return carry

        lax.fori_loop(0, qi, single, 0)
        running_max_step(qi, True)

    for h in range(n_heads):
        a1 = acc_ref[2 * h]
        a2 = acc_ref[2 * h + 1]
        att = a1[:, :vd] / a1[:, vd:] - lam * (a2[:, :vd] / a2[:, vd:])
        ms = jnp.mean(att * att, axis=-1, keepdims=True)
        o_ref[0, :, h * vd:(h + 1) * vd] = (
            att * lax.rsqrt(ms + RMS_EPS) * gout_ref[...] * (1.0 - lambda_init)).astype(BF16)


def _attn_static_kernel(lam_ref, gout_ref, q_ref, k_ref, v_ref, o_ref, vext_ref, *, tq, lambda_init, head_dim):
    s_len = q_ref.shape[1]
    hw = 2 * head_dim
    vd = v_ref.shape[-1]
    lv = lam_ref[...]
    lam = (jnp.exp(jnp.sum(lv[0:1] * lv[1:2], axis=-1, keepdims=True))
           - jnp.exp(jnp.sum(lv[2:3] * lv[3:4], axis=-1, keepdims=True))
           + lambda_init)
    first = lax.broadcasted_iota(jnp.int32, (1, hw), 1) < head_dim
    causal = (lax.broadcasted_iota(jnp.int32, (tq, tq), 1)
              <= lax.broadcasted_iota(jnp.int32, (tq, tq), 0))
    nt = (((1,), (1,)), ((), ()))

    vext_ref[:, 0:vd] = v_ref[0]
    vext_ref[:, vd:2 * vd] = jnp.ones((s_len, vd), BF16)

    for qi in range(s_len // tq):
        lo = qi * tq
        q = q_ref[0, lo:lo + tq, :]
        zero = jnp.zeros_like(q)
        acc = []
        for mp in range(2):
            qm = jnp.where(first, q, zero) if mp == 0 else jnp.where(first, zero, q)
            sd = lax.dot_general(qm, k_ref[0, lo:lo + tq, :], nt, preferred_element_type=F32)
            pd = jnp.exp2(jnp.where(causal, sd, NEG_INF)).astype(BF16)
            pv = jnp.dot(pd, vext_ref[lo:lo + tq, :], preferred_element_type=F32)
            if qi > 0:
                s = lax.dot_general(qm, k_ref[0, 0:lo, :], nt, preferred_element_type=F32)
                pv = pv + jnp.dot(jnp.exp2(s).astype(BF16), vext_ref[0:lo, :], preferred_element_type=F32)
            acc.append(pv)
        a1, a2 = acc
        att = a1[:, :vd] / a1[:, vd:] - lam * (a2[:, :vd] / a2[:, vd:])
        ms = jnp.mean(att * att, axis=-1, keepdims=True)
        o_ref[0, lo:lo + tq, :] = (
            att * lax.rsqrt(ms + RMS_EPS) * gout_ref[...] * (1.0 - lambda_init)).astype(BF16)


def _attention_static(lam_vecs, gout, qk3, v3, *, n_heads, head_dim, lambda_init):
    b, s, _ = qk3.shape
    vd = v3.shape[-1] // n_heads
    hw = 2 * head_dim
    kern = functools.partial(_attn_static_kernel, tq=QBLOCK, lambda_init=lambda_init, head_dim=head_dim)
    return pl.pallas_call(
        kern,
        grid=(b, n_heads),
        in_specs=[
            pl.BlockSpec(lam_vecs.shape, lambda bi, h: (0, 0)),
            pl.BlockSpec((1, vd), lambda bi, h: (0, 0)),
            pl.BlockSpec((1, s, hw), lambda bi, h: (bi, 0, h)),
            pl.BlockSpec((1, s, hw), lambda bi, h: (bi, 0, n_heads + h)),
            pl.BlockSpec((1, s, vd), lambda bi, h: (bi, 0, h)),
        ],
        out_specs=pl.BlockSpec((1, s, vd), lambda bi, h: (bi, 0, h)),
        out_shape=jax.ShapeDtypeStruct(v3.shape, BF16),
        scratch_shapes=[pltpu.VMEM((s, 2 * vd), BF16)],
        compiler_params=_params(2),
        name="diff_attn_static",
    )(lam_vecs, gout, qk3, qk3, v3)


def _attention(need_shift, lam_vecs, gout, qk3, v3, *, n_heads, head_dim, lambda_init):
    b, s, _ = qk3.shape
    att_cols = v3.shape[-1]
    vd = att_cols // n_heads
    tq = QBLOCK
    kern = functools.partial(_attn_kernel, tq=tq, n_heads=n_heads, lambda_init=lambda_init,
                             head_dim=head_dim)
    grid_spec = pltpu.PrefetchScalarGridSpec(
        num_scalar_prefetch=1,
        grid=(b, s // tq),
        in_specs=[
            pl.BlockSpec(lam_vecs.shape, lambda bi, qi, f: (0, 0)),
            pl.BlockSpec((1, vd), lambda bi, qi, f: (0, 0)),
            pl.BlockSpec((1, tq, att_cols), lambda bi, qi, f: (bi, qi, 0)),
            pl.BlockSpec((1, s, att_cols), lambda bi, qi, f: (bi, 0, 1)),
            pl.BlockSpec((1, s, att_cols), lambda bi, qi, f: (bi, 0, 0)),
        ],
        out_specs=pl.BlockSpec((1, tq, att_cols), lambda bi, qi, f: (bi, qi, 0)),
        scratch_shapes=[
            pltpu.VMEM((2 * n_heads, tq, 2 * head_dim), BF16),
            pltpu.VMEM((2 * n_heads, tq, 1), F32),
            pltpu.VMEM((2 * n_heads, tq, 2 * vd), F32),
        ],
    )
    return pl.pallas_call(
        kern,
        grid_spec=grid_spec,
        out_shape=jax.ShapeDtypeStruct(v3.shape, BF16),
        compiler_params=_params(2),
        name="diff_attn",
    )(need_shift, lam_vecs, gout, qk3, qk3, v3)


def _mixout_kernel(att_ref, su_ref, sv_ref, x_ref, sw_ref, sbt_ref, sgn_ref, wo_ref, gffn_ref,
                   rwc_ref, rwh_ref, rb_ref, lstrict_ref, h_ref, hn_ref, route_ref, route_t_ref, counts_ref,
                   cnt_ref, h_prev_ref, acc_ref, *, n_sgu_groups, att_cols):
    step = pl.program_id(0)
    tm = x_ref.shape[0]
    gw = CHUNK
    r_id = lax.broadcasted_iota(jnp.int32, (gw, gw), 0)
    c_id = lax.broadcasted_iota(jnp.int32, (gw, gw), 1)
    tril = c_id <= r_id

    @pl.when(step == 0)
    def _():
        cnt_ref[...] = jnp.zeros_like(cnt_ref)
        h_prev_ref[...] = jnp.zeros_like(h_prev_ref)

    h_prev = h_prev_ref[...]
    ms_prev = jnp.mean(h_prev * h_prev, axis=-1, keepdims=True)
    hn = h_prev * lax.rsqrt(ms_prev + RMS_EPS) * gffn_ref[...]
    hn_ref[...] = _pack_bf16_pairs(hn)
    _route_tile(hn, (step > 0).astype(F32), rwc_ref, rwh_ref, rb_ref, lstrict_ref, route_ref, route_t_ref,
                counts_ref, cnt_ref)

    acc_ref[...] = jnp.dot(att_ref[...], wo_ref[0:att_cols, :], preferred_element_type=F32)

    sg_cols = []
    for g in range(n_sgu_groups):
        w = jnp.where(tril, sw_ref[g], 0.0).astype(BF16)
        bias = sbt_ref[:, g:g + 1]
        gn = sgn_ref[g:g + 1, :]
        rows = []
        for c in range(tm // gw):
            vblk = sv_ref[c * gw:(c + 1) * gw, g * gw:(g + 1) * gw]
            s = jnp.dot(w, vblk, preferred_element_type=F32) + bias
            sg = su_ref[c * gw:(c + 1) * gw, g * gw:(g + 1) * gw].astype(F32) * s
            ms = jnp.mean(sg * sg, axis=-1, keepdims=True)
            rows.append((sg * lax.rsqrt(ms + RMS_EPS) * gn).astype(BF16))
        sg_cols.append(jnp.concatenate(rows, axis=0))
    sgn = jnp.concatenate(sg_cols, axis=1)
    acc = acc_ref[...] + jnp.dot(sgn, wo_ref[att_cols:, :], preferred_element_type=F32)

    h = x_ref[...] + acc
    h_ref[...] = h
    h_prev_ref[...] = h


def _route_tile(hn, live, rwc_ref, rwh_ref, rb_ref, lstrict_ref, route_ref, route_t_ref, counts_ref, cnt_ref):
    hn_bf, hn_lo = _split_bf16(hn)
    both = jnp.dot(hn_bf, rwc_ref[...], preferred_element_type=F32)
    logits = (both[:, :LANES] + both[:, LANES:]
              + jnp.dot(hn_lo, rwh_ref[...], preferred_element_type=F32)
              + rb_ref[...])

    lane = lax.broadcasted_iota(jnp.int32, logits.shape, 1).astype(F32)
    big = float(LANES)
    is_grp = lane < N_GROUPS
    gl = jnp.where(is_grp, logits, NEG_INF)
    gmax = jnp.max(gl, axis=-1, keepdims=True)
    gidx = jnp.min(jnp.where(gl == gmax, lane, big), axis=-1, keepdims=True)
    psum = jnp.sum(jnp.where(is_grp, jnp.exp(gl - gmax), 0.0), axis=-1, keepdims=True)
    p_grp = 1.0 / psum
    e_lo = N_GROUPS + gidx * EXPERTS_PER_GROUP
    in_grp = (lane >= e_lo) & (lane < e_lo + EXPERTS_PER_GROUP)
    el = jnp.where(in_grp, logits, NEG_INF)
    t1 = jnp.max(el, axis=-1, keepdims=True)
    i1 = jnp.min(jnp.where(el == t1, lane, big), axis=-1, keepdims=True)
    el2 = jnp.where(lane == i1, NEG_INF, el)
    t2 = jnp.max(el2, axis=-1, keepdims=True)
    i2 = jnp.min(jnp.where(el2 == t2, lane, big), axis=-1, keepdims=True)
    e21 = jnp.exp(t2 - t1)
    den = 1.0 + e21
    g1 = p_grp / den
    g2 = p_grp * (e21 / den)
    e1 = i1 - N_GROUPS
    e2 = i2 - N_GROUPS

    hit1 = lane == e1
    hit2 = lane == e2
    onehot = jnp.where(hit1 | hit2, live, 0.0)
    before = jnp.dot(lstrict_ref[...], onehot.astype(BF16), preferred_element_type=F32) + cnt_ref[...]
    r1 = jnp.sum(jnp.where(hit1, before, 0.0), axis=-1, keepdims=True)
    r2 = jnp.sum(jnp.where(hit2, before, 0.0), axis=-1, keepdims=True)
    cnt_ref[...] = cnt_ref[...] + jnp.sum(onehot, axis=0, keepdims=True)
    counts_ref[...] = cnt_ref[...]

    route = jnp.zeros_like(logits)
    for k, val in enumerate((e1, e2, g1, g2, r1, r2)):
        route = jnp.where(lane == k, val, route)
    route_ref[...] = route
    route_t_ref[...] = route.T[0:SUBLANES, :]


def _mixout(att, su, sv, x2d, sgu_w, sgu_bt, sgu_gn, w_out, g_ffn, rw_cat, rw_hi, rb, lstrict, *,
            att_cols):
    t, d = x2d.shape
    tm = ROW_TILE
    n = t // tm
    row = lambda i: (jnp.minimum(i, n - 1), 0)
    routed = lambda i: (jnp.maximum(i - 1, 0), 0)
    fixed2 = lambda i: (0, 0)
    n_groups = sgu_w.shape[0]
    kern = functools.partial(_mixout_kernel, n_sgu_groups=n_groups, att_cols=att_cols)
    return pl.pallas_call(
        kern,
        grid=(n + 1,),
        in_specs=[
            pl.BlockSpec((tm, att.shape[1]), row),
            pl.BlockSpec((tm, su.shape[1]), row),
            pl.BlockSpec((tm, sv.shape[1]), row),
            pl.BlockSpec((tm, d), row),
            pl.BlockSpec(sgu_w.shape, lambda i: (0, 0, 0)),
            pl.BlockSpec(sgu_bt.shape, fixed2),
            pl.BlockSpec(sgu_gn.shape, fixed2),
            pl.BlockSpec(w_out.shape, fixed2),
            pl.BlockSpec((1, d), fixed2),
            pl.BlockSpec(rw_cat.shape, fixed2),
            pl.BlockSpec(rw_hi.shape, fixed2),
            pl.BlockSpec((1, LANES), fixed2),
            pl.BlockSpec((tm, tm), fixed2),
        ],
        out_specs=[
            pl.BlockSpec((tm, d), row),
            pl.BlockSpec((tm, d // 2), routed),
            pl.BlockSpec((tm, LANES), routed),
            pl.BlockSpec((SUBLANES, tm), lambda i: (0, jnp.maximum(i - 1, 0))),
            pl.BlockSpec((1, LANES), fixed2),
        ],
        out_shape=[
            jax.ShapeDtypeStruct((t, d), F32),
            jax.ShapeDtypeStruct((t, d // 2), U32),
            jax.ShapeDtypeStruct((t, LANES), F32),
            jax.ShapeDtypeStruct((SUBLANES, t), F32),
            jax.ShapeDtypeStruct((1, LANES), F32),
        ],
        scratch_shapes=[pltpu.VMEM((1, LANES), F32), pltpu.VMEM((tm, d), F32), pltpu.VMEM((tm, d), F32)],
        compiler_params=_params(1),
        name="mixout",
    )(att, su, sv, x2d, sgu_w, sgu_bt, sgu_gn, w_out, g_ffn, rw_cat, rw_hi, rb, lstrict)


def _dispatch_kernel(*refs):
    dest_refs, (hn_ref, xs_ref, sem) = refs[:TOP_K], refs[TOP_K:]
    rows = hn_ref.shape[0]

    for r in range(rows):
        for k in range(TOP_K):
            pltpu.make_async_copy(hn_ref.at[pl.ds(r, 1), :],
                                  xs_ref.at[pl.ds(dest_refs[k][r], 1), :], sem).start(priority=k)
    for k in range(TOP_K):
        pltpu.make_async_copy(hn_ref, xs_ref.at[pl.ds(0, rows), :], sem).wait()


def _dispatch(hn, dests):
    t, d = hn.shape
    rows = DISPATCH_TILE
    index_spec = pl.BlockSpec((rows,), lambda i: (i,), memory_space=pltpu.SMEM)
    return pl.pallas_call(
        _dispatch_kernel,
        grid=(t // rows,),
        in_specs=[index_spec] * TOP_K + [pl.BlockSpec((rows, d), lambda i: (i, 0))],
        out_specs=pl.BlockSpec(memory_space=pl.ANY),
        out_shape=jax.ShapeDtypeStruct((t * TOP_K, d), hn.dtype),
        scratch_shapes=[pltpu.SemaphoreType.DMA],
        compiler_params=_params(1),
        name="dispatch",
    )(*dests, hn)


def _experts_kernel(vt_ref, ve_ref, vf_ref, vn_ref, vs_ref, vx_ref, lo_ref, hi_ref, xs_ref, wg_hbm, wu_hbm,
                    wd_hbm, ys_ref, wg_f32, wu_f32, wd_f32, wg_bf, wu_bf, wd_bf, act_ref, wsem):
    v = pl.program_id(0)
    flag = vf_ref[v]
    tm = xs_ref.shape[0]

    def weight_copies(expert, slot):
        return [pltpu.make_async_copy(src.at[expert], dst.at[slot], wsem.at[slot])
                for src, dst in ((wg_hbm, wg_f32), (wu_hbm, wu_f32), (wd_hbm, wd_f32))]

    @pl.when(v == 0)
    def _():
        for cp in weight_copies(ve_ref[0], 0):
            cp.start()

    @pl.when(vn_ref[v] == 1)
    def _():
        slot = vs_ref[v]
        for cp in weight_copies(ve_ref[v], slot):
            cp.wait()

        @pl.when(vx_ref[v] >= 0)
        def _():
            for cp in weight_copies(vx_ref[v], 1 - slot):
                cp.start()

        wg_bf[...] = wg_f32[slot].astype(BF16)
        wu_bf[...] = wu_f32[slot].astype(BF16)
        wd_bf[...] = wd_f32[slot].astype(BF16)

    @pl.when(flag > 0)
    def _():
        e = ve_ref[v]
        base = vt_ref[v] * tm
        rid = lax.broadcasted_iota(jnp.int32, (tm, 1), 0) + base
        mine = (rid >= lo_ref[e]) & (rid < hi_ref[e])
        x = _unpack_bf16_pairs(xs_ref[...]).astype(BF16)
        for n in range(act_ref.shape[1] // MXU_WIDTH):
            cols = slice(n * MXU_WIDTH, (n + 1) * MXU_WIDTH)
            hg = jnp.dot(x, wg_bf[:, cols], preferred_element_type=F32)
            hu = jnp.dot(x, wu_bf[:, cols], preferred_element_type=F32)
            act_ref[:, cols] = (hg * (1.0 / (1.0 + jnp.exp(-hg))) * hu).astype(BF16)
        y = _pack_bf16_pairs(jnp.dot(act_ref[...], wd_bf[...], preferred_element_type=F32))

        @pl.when(flag == 1)
        def _():
            ys_ref[...] = jnp.where(mine, y, jnp.zeros_like(y))

        @pl.when(flag == 2)
        def _():
            ys_ref[...] = jnp.where(mine, y, ys_ref[...])


def _experts(visits, row_lo, row_hi, xs, w_gate, w_up, w_down):
    p, words = xs.shape
    d = 2 * words
    tm = EXPERT_TILE
    f = w_gate.shape[-1]
    rows_map = lambda v, vt, *_: (vt[v], 0)
    whole = pl.BlockSpec(memory_space=pl.ANY)
    grid_spec = pltpu.PrefetchScalarGridSpec(
        num_scalar_prefetch=len(visits) + 2,
        grid=(visits[0].shape[0],),
        in_specs=[pl.BlockSpec((tm, words), rows_map), whole, whole, whole],
        out_specs=pl.BlockSpec((tm, words), rows_map),
        scratch_shapes=[pltpu.VMEM((2, d, f), F32), pltpu.VMEM((2, d, f), F32), pltpu.VMEM((2, f, d), F32),
                        pltpu.VMEM((d, f), BF16), pltpu.VMEM((d, f), BF16), pltpu.VMEM((f, d), BF16),
                        pltpu.VMEM((tm, f), BF16), pltpu.SemaphoreType.DMA((2,))],
    )
    return pl.pallas_call(
        _experts_kernel,
        grid_spec=grid_spec,
        out_shape=jax.ShapeDtypeStruct(xs.shape, xs.dtype),
        compiler_params=_params(1),
        name="experts",
    )(*visits, row_lo, row_hi, xs, w_gate, w_up, w_down)


def _combine_kernel(*refs):
    dest_refs, dest_next_refs = refs[:TOP_K], refs[TOP_K:2 * TOP_K]
    h_ref, route_ref, ys_ref, o_ref, ybuf, sem = refs[2 * TOP_K:]
    i = pl.program_id(0)
    rows = h_ref.shape[0]
    slot = i % 2

    def issue(idx_refs, into):
        for r in range(rows):
            for k in range(TOP_K):
                pltpu.make_async_copy(ys_ref.at[pl.ds(idx_refs[k][r], 1), :],
                                      ybuf.at[into, k, pl.ds(r, 1), :], sem.at[into]).start(priority=k)

    @pl.when(i == 0)
    def _():
        issue(dest_refs, 0)

    @pl.when(i + 1 < pl.num_programs(0))
    def _():
        issue(dest_next_refs, 1 - slot)

    for k in range(TOP_K):
        pltpu.make_async_copy(ys_ref.at[pl.ds(0, rows), :], ybuf.at[slot, k], sem.at[slot]).wait()

    route = route_ref[...]
    moe = None
    for k in range(TOP_K):
        gated = route[:, ROUTE_GATE + k:ROUTE_GATE + k + 1] * _unpack_bf16_pairs(ybuf[slot, k])
        moe = gated if moe is None else moe + gated
    o_ref[...] = h_ref[...] + moe


def _combine(dests, h, route, ys):
    t, d = h.shape
    rows = COMBINE_TILE
    n = t // rows
    this_step = pl.BlockSpec((rows,), lambda i: (i,), memory_space=pltpu.SMEM)
    next_step = pl.BlockSpec((rows,), lambda i: (jnp.minimum(i + 1, n - 1),), memory_space=pltpu.SMEM)
    return pl.pallas_call(
        _combine_kernel,
        grid=(n,),
        in_specs=[this_step] * TOP_K + [next_step] * TOP_K + [
            pl.BlockSpec((rows, d), lambda i: (i, 0)),
            pl.BlockSpec((rows, LANES), lambda i: (i, 0)),
            pl.BlockSpec(memory_space=pl.ANY),
        ],
        out_specs=pl.BlockSpec((rows, d), lambda i: (i, 0)),
        out_shape=jax.ShapeDtypeStruct((t, d), F32),
        scratch_shapes=[pltpu.VMEM((2, TOP_K, rows, ys.shape[1]), ys.dtype), pltpu.SemaphoreType.DMA((2,))],
        compiler_params=_params(1),
        name="combine",
    )(*dests, *dests, h, route, ys)


def _plan(route_t, counts_row):
    t = route_t.shape[1]
    counts = counts_row[0, :N_EXPERTS].astype(I32)
    ends = jnp.cumsum(counts)
    offs = ends - counts
    experts = route_t[ROUTE_EXPERT:ROUTE_EXPERT + TOP_K].astype(I32)
    rank = route_t[ROUTE_RANK:ROUTE_RANK + TOP_K].astype(I32)
    expert_ids = jnp.arange(N_EXPERTS, dtype=I32)[:, None, None]
    first_row = jnp.sum(jnp.where(experts[None] == expert_ids, offs[:, None, None], 0), axis=0)
    dest = first_row + rank
    dests = tuple(dest[k] for k in range(TOP_K))

    tm = EXPERT_TILE
    n_visits = (t * TOP_K) // tm + N_EXPERTS - 1
    first_tile = offs // tm
    last_tile = (ends - 1) // tm
    nvis = jnp.where(counts > 0, last_tile - first_tile + 1, 0)
    vend = jnp.cumsum(nvis)
    vstart = vend - nvis
    v = jnp.arange(n_visits, dtype=I32)
    ev = jnp.minimum(jnp.sum((vend[None, :] <= v[:, None]).astype(I32), axis=1), N_EXPERTS - 1)
    tv = jnp.take(first_tile, ev) + (v - jnp.take(vstart, ev))
    valid = v < vend[-1]
    last_v = vend[-1] - 1
    ev = jnp.where(valid, ev, jnp.take(ev, last_v))
    tv = jnp.where(valid, tv, jnp.take(tv, last_v))
    prev_t = jnp.concatenate([jnp.full((1,), -1, I32), tv[:-1]])
    prev_e = jnp.concatenate([jnp.full((1,), -1, I32), ev[:-1]])
    flag = jnp.where(valid, jnp.where(tv != prev_t, 1, 2), 0).astype(I32)
    new_expert = (ev != prev_e).astype(I32)
    slot = (jnp.cumsum(new_expert) - 1) % 2
    ids = jnp.arange(N_EXPERTS, dtype=I32)
    later = (ids[None, :] > ids[:, None]) & (counts[None, :] > 0)
    next_active = jnp.min(jnp.where(later, ids[None, :], N_EXPERTS), axis=1)
    next_active = jnp.where(next_active < N_EXPERTS, next_active, -1)
    visits = (tv.astype(I32), ev.astype(I32), flag, new_expert, slot.astype(I32),
              jnp.take(next_active, ev).astype(I32))
    return dests, visits, offs.astype(I32), ends.astype(I32)


def _rope_constants(head_dim, rope_dim):
    rope_half = rope_dim // 2
    inv_freq = ROPE_THETA ** (-jnp.arange(0, rope_dim, 2, dtype=F32) / rope_dim)
    dim = jnp.arange(LANES, dtype=I32)[None, :] % head_dim
    k = jnp.arange(4 * rope_half, dtype=I32)[:, None]
    c_mat = jnp.where(dim < rope_dim, k == dim % rope_half, k == rope_dim)
    sa_mat = (dim < rope_half) & (k == rope_half + dim)
    sb_mat = (dim >= rope_half) & (dim < rope_dim) & (k == dim)
    expand = jnp.concatenate([c_mat.astype(F32), -sa_mat.astype(F32), sb_mat.astype(F32)], axis=1)
    return inv_freq.reshape(rope_half, 1), expand.astype(BF16)


def kernel(x, positions, norm_mix_g, w_in, q_norm_g, k_norm_g, lambda_q1, lambda_k1, lambda_q2,
           lambda_k2, diff_out_norm_g, sgu_ln_g, sgu_ln_b, sgu_w, sgu_b, sgu_out_norm_g, w_out,
           norm_ffn_g, router_group_w, router_group_b, router_expert_w, router_expert_b,
           expert_w_gate, expert_w_up, expert_w_down):
    b, s, d = x.shape
    t = b * s
    depth = w_in.shape[0]
    head_dim = q_norm_g.shape[-1]
    vd = diff_out_norm_g.shape[-1]
    n_sgu_groups = sgu_w.shape[1]
    sgu_cols = n_sgu_groups * sgu_w.shape[-1]
    in_cols = w_in.shape[-1]
    att_cols = (in_cols - 2 * sgu_cols) // 3
    n_heads = att_cols // vd
    qk_cols = 2 * att_cols
    rope_dim = head_dim // ROPE_FRACTION
    rope_half = rope_dim // 2
    scale = head_dim ** -0.5

    pos_rows = positions.reshape(t // ROW_TILE, 1, ROW_TILE).astype(F32)
    freq_col, expand = _rope_constants(head_dim, rope_dim)
    blk = jnp.arange(MXU_WIDTH, dtype=I32) // head_dim
    ones_blk = (blk[:, None] == blk[None, :]).astype(BF16)
    tri = jnp.arange(ROW_TILE, dtype=I32)
    lstrict = (tri[None, :] < tri[:, None]).astype(BF16)

    h = x.reshape(t, d)
    for l in range(depth):
        lambda_init = 0.8 - 0.6 * math.exp(-0.3 * l)
        gqk = jnp.concatenate([jnp.tile(q_norm_g[l] * (scale * math.log2(math.e)), att_cols // head_dim),
                               jnp.tile(k_norm_g[l], att_cols // head_dim)]).reshape(1, qk_cols)
        qk, v, su, sv = _inproj(
            h, norm_mix_g[l].reshape(1, d), w_in[l].astype(BF16), gqk, pos_rows, freq_col, expand, ones_blk,
            sgu_ln_g[l].reshape(1, sgu_cols), sgu_ln_b[l].reshape(1, sgu_cols),
            qk_cols=qk_cols, v_cols=att_cols, sgu_cols=sgu_cols, head_dim=head_dim,
            rope_half=rope_half)

        lam_vecs = jnp.stack([lambda_q1[l], lambda_k1[l], lambda_q2[l], lambda_k2[l]]).astype(F32)
        score_bound = (1.02 * head_dim * scale * jnp.max(jnp.abs(q_norm_g[l]))
                       * jnp.max(jnp.abs(k_norm_g[l])))
        need_shift = (score_bound > MAX_UNSHIFTED_SCORE).astype(I32).reshape(1)
        attn_args = (lam_vecs, diff_out_norm_g[l].reshape(1, vd), qk.reshape(b, s, qk_cols),
                     v.reshape(b, s, att_cols))
        attn_kw = dict(n_heads=n_heads, head_dim=head_dim, lambda_init=lambda_init)
        att = lax.cond(need_shift[0] != 0,
                       lambda *a: _attention(need_shift, *a, **attn_kw),
                       lambda *a: _attention_static(*a, **attn_kw),
                       *attn_args).reshape(t, att_cols)

        rw = jnp.concatenate([router_group_w[l], router_expert_w[l]], axis=1)
        rw = jnp.pad(rw, ((0, 0), (0, LANES - rw.shape[1])))
        rw_hi, rw_lo = _split_bf16(rw)
        rb = jnp.pad(jnp.concatenate([router_group_b[l], router_expert_b[l]]),
                     (0, LANES - N_GROUPS - N_EXPERTS)).reshape(1, LANES)
        h, hn, route, route_t, counts_row = _mixout(
            att, su, sv, h, sgu_w[l], sgu_b[l].T, sgu_out_norm_g[l], w_out[l].astype(BF16),
            norm_ffn_g[l].reshape(1, d), jnp.concatenate([rw_hi, rw_lo], axis=1), rw_hi, rb, lstrict,
            att_cols=att_cols)

        dests, visits, row_lo, row_hi = _plan(route_t, counts_row)
        xs = _dispatch(hn, dests)
        ys = _experts(visits, row_lo, row_hi, xs, expert_w_gate[l], expert_w_up[l], expert_w_down[l])
        h = _combine(dests, h, route, ys)
    return h.reshape(b, s, d)
```

```python
import functools
import math

import jax
import jax.numpy as jnp
from jax import lax
from jax.experimental import pallas as pl
from jax.experimental.pallas import tpu as pltpu

F32 = jnp.float32
BF16 = jnp.bfloat16
I32 = jnp.int32

RMS_EPS = 1e-6
NEG_INF = -1e30
ROPE_THETA = 500000.0
MAX_UNSHIFTED_SCORE = 40.0

LANES = 128
SUBLANES = 8
MXU_WIDTH = 256
VMEM_LIMIT_BYTES = 56 * 1024 * 1024

N_GROUPS = 4
EXPERTS_PER_GROUP = 8
N_EXPERTS = N_GROUPS * EXPERTS_PER_GROUP
TOP_K = 2
ROPE_FRACTION = 4
CHUNK = 128
QBLOCK = 256
ATTN_HEADS_PER_STEP = 2
ROW_TILE = 512
EXPERT_TILE = 512
DISPATCH_TILE = 1024
COMBINE_TILE = 512

ROUTE_EXPERT = 0
ROUTE_GATE = 2
ROUTE_RANK = 4


def _params(n_grid_dims):
    return pltpu.CompilerParams(
        dimension_semantics=("arbitrary",) * n_grid_dims,
        vmem_limit_bytes=VMEM_LIMIT_BYTES,
    )


def _gelu(x):
    return 0.5 * x * (1.0 + lax.erf(x * (1.0 / math.sqrt(2.0))))


def _split_bf16(x):
    hi = x.astype(BF16)
    return hi, (x - hi.astype(F32)).astype(BF16)


U32 = jnp.uint32
HIGH_HALF = 0xFFFF0000


def _pack_bf16_pairs(x):
    n = x.shape[1] // 2
    hi = lax.bitcast_convert_type(x[:, :n].astype(BF16).astype(F32), U32)
    lo = lax.bitcast_convert_type(x[:, n:].astype(BF16).astype(F32), U32)
    return hi | (lo >> 16)


def _unpack_bf16_pairs(w):
    hi = lax.bitcast_convert_type(w & jnp.uint32(HIGH_HALF), F32)
    lo = lax.bitcast_convert_type(w << 16, F32)
    return jnp.concatenate([hi, lo], axis=1)


def _inproj_kernel(x_ref, g_ref, w_ref, gqk_ref, pos_ref, freq_ref, expand_ref, ones_ref, lng_ref, lnb_ref,
                   qk_ref, v_ref, su_ref, sv_ref, proj_ref, ss_ref, *, qk_cols, v_cols, sgu_cols, head_dim,
                   rope_half):
    x = x_ref[...]
    ms = jnp.mean(x * x, axis=-1, keepdims=True)
    xn = (x * lax.rsqrt(ms + RMS_EPS) * g_ref[...]).astype(BF16)
    cw = MXU_WIDTH

    ang_t = freq_ref[...] * pos_ref[0]
    cs_t = jnp.concatenate([jnp.cos(ang_t), jnp.sin(ang_t), jnp.ones_like(ang_t),
                            jnp.zeros_like(ang_t)], axis=0)
    contract0 = (((0,), (0,)), ((), ()))
    cs_hi, cs_lo = _split_bf16(cs_t)
    tab = (lax.dot_general(cs_hi, expand_ref[...], contract0, preferred_element_type=F32)
           + lax.dot_general(cs_lo, expand_ref[...], contract0, preferred_element_type=F32))
    cos = tab[:, 0:LANES]
    sa = tab[:, LANES:2 * LANES]
    sb = tab[:, 2 * LANES:3 * LANES]

    def stage(c):
        cols = slice(c * cw, (c + 1) * cw)
        proj_ref[:, cols] = jnp.dot(xn, w_ref[:, cols], preferred_element_type=F32)

    n_qk = qk_cols // cw
    for c in range(n_qk):
        stage(c)
    for c in range(n_qk):
        p = proj_ref[:, c * cw:(c + 1) * cw]
        ss_ref[:, c * cw:(c + 1) * cw] = jnp.dot((p * p).astype(BF16), ones_ref[...],
                                                 preferred_element_type=F32)
    n_v = v_cols // cw
    for c in range(n_qk + n_v, w_ref.shape[1] // cw):
        stage(c)
    for c in range(n_qk, n_qk + n_v):
        stage(c)

    def proj(col0):
        return proj_ref[:, col0:col0 + cw]

    for c in range(qk_cols // cw):
        p = proj(c * cw)
        ss = ss_ref[:, c * cw:(c + 1) * cw]
        pn = p * lax.rsqrt(ss * (1.0 / head_dim) + RMS_EPS) * gqk_ref[:, c * cw:(c + 1) * cw]
        for hh in range(cw // LANES):
            blk = pn[:, hh * LANES:(hh + 1) * LANES]
            rot = (blk * cos
                   + pltpu.roll(blk, LANES - rope_half, 1) * sa
                   + pltpu.roll(blk, rope_half, 1) * sb)
            col = c * cw + hh * LANES
            qk_ref[:, col:col + LANES] = rot.astype(BF16)

    for c in range(v_cols // cw):
        v_ref[:, c * cw:(c + 1) * cw] = proj(qk_cols + c * cw).astype(BF16)

    for c in range(sgu_cols // cw):
        su_ref[:, c * cw:(c + 1) * cw] = _gelu(proj(qk_cols + v_cols + c * cw)).astype(BF16)

    sv = jnp.concatenate(
        [_gelu(proj(qk_cols + v_cols + sgu_cols + c * cw)) for c in range(sgu_cols // cw)], axis=1)
    mu = jnp.mean(sv, axis=-1, keepdims=True)
    svc = sv - mu
    var = jnp.mean(svc * svc, axis=-1, keepdims=True)
    sv_ref[...] = (svc * lax.rsqrt(var + RMS_EPS) * lng_ref[...] + lnb_ref[...]).astype(BF16)


def _inproj(x2d, norm_g, w_in, gqk, pos_rows, freq_col, expand, ones_blk, ln_g, ln_b, *, qk_cols, v_cols,
            sgu_cols, head_dim, rope_half):
    t, d = x2d.shape
    tm = ROW_TILE
    row = lambda i: (i, 0)
    fixed = lambda i: (0, 0)
    kern = functools.partial(_inproj_kernel, qk_cols=qk_cols, v_cols=v_cols, sgu_cols=sgu_cols,
                             head_dim=head_dim, rope_half=rope_half)
    return pl.pallas_call(
        kern,
        grid=(t // tm,),
        in_specs=[
            pl.BlockSpec((tm, d), row),
            pl.BlockSpec((1, d), fixed),
            pl.BlockSpec(w_in.shape, fixed),
            pl.BlockSpec((1, qk_cols), fixed),
            pl.BlockSpec((1, 1, tm), lambda i: (i, 0, 0)),
            pl.BlockSpec(freq_col.shape, fixed),
            pl.BlockSpec(expand.shape, fixed),
            pl.BlockSpec((MXU_WIDTH, MXU_WIDTH), fixed),
            pl.BlockSpec((1, sgu_cols), fixed),
            pl.BlockSpec((1, sgu_cols), fixed),
        ],
        out_specs=[
            pl.BlockSpec((tm, qk_cols), row),
            pl.BlockSpec((tm, v_cols), row),
            pl.BlockSpec((tm, sgu_cols), row),
            pl.BlockSpec((tm, sgu_cols), row),
        ],
        out_shape=[
            jax.ShapeDtypeStruct((t, qk_cols), BF16),
            jax.ShapeDtypeStruct((t, v_cols), BF16),
            jax.ShapeDtypeStruct((t, sgu_cols), BF16),
            jax.ShapeDtypeStruct((t, sgu_cols), BF16),
        ],
        scratch_shapes=[pltpu.VMEM((tm, w_in.shape[1]), F32), pltpu.VMEM((tm, qk_cols), F32)],
        compiler_params=_params(1),
        name="inproj",
    )(x2d, norm_g, w_in, gqk, pos_rows, freq_col, expand, ones_blk, ln_g, ln_b)


def _attn_kernel(shift_ref, lam_ref, gout_ref, q_ref, k_ref, v_ref, o_ref, qm_ref, m_ref, acc_ref, *, tq,
                 n_heads, lambda_init, head_dim):
    qi = pl.program_id(1)
    hw = 2 * head_dim
    vd = v_ref.shape[-1] // n_heads
    lv = lam_ref[...]
    lam = (jnp.exp(jnp.sum(lv[0:1] * lv[1:2], axis=-1, keepdims=True))
           - jnp.exp(jnp.sum(lv[2:3] * lv[3:4], axis=-1, keepdims=True))
           + lambda_init)

    first = lax.broadcasted_iota(jnp.int32, (1, hw), 1) < head_dim
    for h in range(n_heads):
        q = q_ref[0, :, h * hw:(h + 1) * hw]
        zero = jnp.zeros_like(q)
        qm_ref[2 * h] = jnp.where(first, q, zero)
        qm_ref[2 * h + 1] = jnp.where(first, zero, q)
    acc_ref[...] = jnp.zeros(acc_ref.shape, F32)

    ones = jnp.ones((tq, vd), BF16)
    causal = (lax.broadcasted_iota(jnp.int32, (tq, tq), 1)
              <= lax.broadcasted_iota(jnp.int32, (tq, tq), 0))

    def kv_tile(j, h):
        start = pl.multiple_of(j * tq, tq)
        kc = k_ref[0, pl.ds(start, tq), h * hw:(h + 1) * hw]
        vext = jnp.concatenate([v_ref[0, pl.ds(start, tq), h * vd:(h + 1) * vd], ones], axis=1)
        return kc, vext

    def scores(c, kc, masked):
        s = lax.dot_general(qm_ref[c], kc, (((1,), (1,)), ((), ())), preferred_element_type=F32)
        return jnp.where(causal, s, NEG_INF) if masked else s

    def plain_step(tiles):
        for h in range(n_heads):
            kvs = [kv_tile(j, h) for j, _ in tiles]
            for mp in range(2):
                c = 2 * h + mp
                pv = None
                for (kc, vext), (_, masked) in zip(kvs, tiles):
                    part = jnp.dot(jnp.exp2(scores(c, kc, masked)).astype(BF16), vext,
                                   preferred_element_type=F32)
                    pv = part if pv is None else pv + part
                acc_ref[c] += pv

    def running_max_step(j, masked):
        for h in range(n_heads):
            kc, vext = kv_tile(j, h)
            for mp in range(2):
                c = 2 * h + mp
                s = scores(c, kc, masked)
                m_old = m_ref[c]
                m_new = jnp.maximum(m_old, jnp.max(s, axis=-1, keepdims=True))
                p = jnp.exp2(s - m_new).astype(BF16)
                acc_ref[c] = (jnp.exp2(m_old - m_new) * acc_ref[c]
                              + jnp.dot(p, vext, preferred_element_type=F32))
                m_ref[c] = m_new

    @pl.when(shift_ref[0] == 0)
    def _():
        def pair(jj, carry):
            plain_step([(2 * jj, False), (2 * jj + 1, False)])
            return carry

        lax.fori_loop(0, qi // 2, pair, 0)

        @pl.when(qi % 2 == 1)
        def _():
            plain_step([(qi - 1, False), (qi, True)])

        @pl.when(qi % 2 == 0)
        def _():
            plain_step([(qi, True)])

    @pl.when(shift_ref[0] != 0)
    def _():
        m_ref[...] = jnp.full(m_ref.shape, NEG_INF, F32)

        def single(j, carry):
            running_max_step(j, False)
            return carry

        lax.fori_loop(0, qi, single, 0)
        running_max_step(qi, True)

    for h in range(n_heads):
        a1 = acc_ref[2 * h]
        a2 = acc_ref[2 * h + 1]
        att = a1[:, :vd] / a1[:, vd:] - lam * (a2[:, :vd] / a2[:, vd:])
        ms = jnp.mean(att * att, axis=-1, keepdims=True)
        o_ref[0, :, h * vd:(h + 1) * vd] = (
            att * lax.rsqrt(ms + RMS_EPS) * gout_ref[...] * (1.0 - lambda_init)).astype(BF16)


def _attn_static_kernel(lam_ref, gout_ref, q_ref, k_ref, v_ref, o_ref, vext_ref, *, tq, heads, lambda_init,
                        head_dim):
    s_len = q_ref.shape[1]
    hw = 2 * head_dim
    vd = v_ref.shape[-1] // heads
    lv = lam_ref[...]
    lam = (jnp.exp(jnp.sum(lv[0:1] * lv[1:2], axis=-1, keepdims=True))
           - jnp.exp(jnp.sum(lv[2:3] * lv[3:4], axis=-1, keepdims=True))
           + lambda_init)
    first = lax.broadcasted_iota(jnp.int32, (1, hw), 1) < head_dim
    causal = (lax.broadcasted_iota(jnp.int32, (tq, tq), 1)
              <= lax.broadcasted_iota(jnp.int32, (tq, tq), 0))
    nt = (((1,), (1,)), ((), ()))

    for h in range(heads):
        vext_ref[h, :, 0:vd] = v_ref[0, :, h * vd:(h + 1) * vd]
        vext_ref[h, :, vd:2 * vd] = jnp.ones((s_len, vd), BF16)

    for h in range(heads):
        kcols = slice(h * hw, (h + 1) * hw)
        for qi in range(s_len // tq):
            lo = qi * tq
            q = q_ref[0, lo:lo + tq, kcols]
            zero = jnp.zeros_like(q)
            acc = []
            for mp in range(2):
                qm = jnp.where(first, q, zero) if mp == 0 else jnp.where(first, zero, q)
                sd = lax.dot_general(qm, k_ref[0, lo:lo + tq, kcols], nt, preferred_element_type=F32)
                pd = jnp.exp2(jnp.where(causal, sd, NEG_INF)).astype(BF16)
                pv = jnp.dot(pd, vext_ref[h, lo:lo + tq, :], preferred_element_type=F32)
                if qi > 0:
                    s = lax.dot_general(qm, k_ref[0, 0:lo, kcols], nt, preferred_element_type=F32)
                    pv = pv + jnp.dot(jnp.exp2(s).astype(BF16), vext_ref[h, 0:lo, :],
                                      preferred_element_type=F32)
                acc.append(pv)
            a1, a2 = acc
            att = a1[:, :vd] / a1[:, vd:] - lam * (a2[:, :vd] / a2[:, vd:])
            ms = jnp.mean(att * att, axis=-1, keepdims=True)
            o_ref[0, lo:lo + tq, h * vd:(h + 1) * vd] = (
                att * lax.rsqrt(ms + RMS_EPS) * gout_ref[...] * (1.0 - lambda_init)).astype(BF16)


def _attention_static(lam_vecs, gout, qk3, v3, *, n_heads, head_dim, lambda_init):
    b, s, _ = qk3.shape
    vd = v3.shape[-1] // n_heads
    heads = ATTN_HEADS_PER_STEP
    groups = n_heads // heads
    qw = heads * 2 * head_dim
    kern = functools.partial(_attn_static_kernel, tq=QBLOCK, heads=heads, lambda_init=lambda_init,
                             head_dim=head_dim)
    return pl.pallas_call(
        kern,
        grid=(b, groups),
        in_specs=[
            pl.BlockSpec(lam_vecs.shape, lambda bi, g: (0, 0)),
            pl.BlockSpec((1, vd), lambda bi, g: (0, 0)),
            pl.BlockSpec((1, s, qw), lambda bi, g: (bi, 0, g)),
            pl.BlockSpec((1, s, qw), lambda bi, g: (bi, 0, groups + g)),
            pl.BlockSpec((1, s, heads * vd), lambda bi, g: (bi, 0, g)),
        ],
        out_specs=pl.BlockSpec((1, s, heads * vd), lambda bi, g: (bi, 0, g)),
        out_shape=jax.ShapeDtypeStruct(v3.shape, BF16),
        scratch_shapes=[pltpu.VMEM((heads, s, 2 * vd), BF16)],
        compiler_params=_params(2),
        name="diff_attn_static",
    )(lam_vecs, gout, qk3, qk3, v3)


def _attention(need_shift, lam_vecs, gout, qk3, v3, *, n_heads, head_dim, lambda_init):
    b, s, _ = qk3.shape
    att_cols = v3.shape[-1]
    vd = att_cols // n_heads
    tq = QBLOCK
    kern = functools.partial(_attn_kernel, tq=tq, n_heads=n_heads, lambda_init=lambda_init,
                             head_dim=head_dim)
    grid_spec = pltpu.PrefetchScalarGridSpec(
        num_scalar_prefetch=1,
        grid=(b, s // tq),
        in_specs=[
            pl.BlockSpec(lam_vecs.shape, lambda bi, qi, f: (0, 0)),
            pl.BlockSpec((1, vd), lambda bi, qi, f: (0, 0)),
            pl.BlockSpec((1, tq, att_cols), lambda bi, qi, f: (bi, qi, 0)),
            pl.BlockSpec((1, s, att_cols), lambda bi, qi, f: (bi, 0, 1)),
            pl.BlockSpec((1, s, att_cols), lambda bi, qi, f: (bi, 0, 0)),
        ],
        out_specs=pl.BlockSpec((1, tq, att_cols), lambda bi, qi, f: (bi, qi, 0)),
        scratch_shapes=[
            pltpu.VMEM((2 * n_heads, tq, 2 * head_dim), BF16),
            pltpu.VMEM((2 * n_heads, tq, 1), F32),
            pltpu.VMEM((2 * n_heads, tq, 2 * vd), F32),
        ],
    )
    return pl.pallas_call(
        kern,
        grid_spec=grid_spec,
        out_shape=jax.ShapeDtypeStruct(v3.shape, BF16),
        compiler_params=_params(2),
        name="diff_attn",
    )(need_shift, lam_vecs, gout, qk3, qk3, v3)


def _mixout_kernel(att_ref, su_ref, sv_ref, x_ref, sw_ref, sbt_ref, sgn_ref, wo_ref, gffn_ref,
                   rwc_ref, rwh_ref, rb_ref, lstrict_ref, h_ref, hn_ref, route_ref, route_t_ref, counts_ref,
                   cnt_ref, h_prev_ref, acc_ref, *, n_sgu_groups, att_cols):
    step = pl.program_id(0)
    tm = x_ref.shape[0]
    gw = CHUNK
    r_id = lax.broadcasted_iota(jnp.int32, (gw, gw), 0)
    c_id = lax.broadcasted_iota(jnp.int32, (gw, gw), 1)
    tril = c_id <= r_id

    @pl.when(step == 0)
    def _():
        cnt_ref[...] = jnp.zeros_like(cnt_ref)
        h_prev_ref[...] = jnp.zeros_like(h_prev_ref)

    h_prev = h_prev_ref[...]
    ms_prev = jnp.mean(h_prev * h_prev, axis=-1, keepdims=True)
    hn = h_prev * lax.rsqrt(ms_prev + RMS_EPS) * gffn_ref[...]
    hn_ref[...] = _pack_bf16_pairs(hn)
    _route_tile(hn, (step > 0).astype(F32), rwc_ref, rwh_ref, rb_ref, lstrict_ref, route_ref, route_t_ref,
                counts_ref, cnt_ref)

    acc_ref[...] = jnp.dot(att_ref[...], wo_ref[0:att_cols, :], preferred_element_type=F32)

    sg_cols = []
    for g in range(n_sgu_groups):
        w = jnp.where(tril, sw_ref[g], 0.0).astype(BF16)
        bias = sbt_ref[:, g:g + 1]
        gn = sgn_ref[g:g + 1, :]
        rows = []
        for c in range(tm // gw):
            vblk = sv_ref[c * gw:(c + 1) * gw, g * gw:(g + 1) * gw]
            s = jnp.dot(w, vblk, preferred_element_type=F32) + bias
            sg = su_ref[c * gw:(c + 1) * gw, g * gw:(g + 1) * gw].astype(F32) * s
            ms = jnp.mean(sg * sg, axis=-1, keepdims=True)
            rows.append((sg * lax.rsqrt(ms + RMS_EPS) * gn).astype(BF16))
        sg_cols.append(jnp.concatenate(rows, axis=0))
    sgn = jnp.concatenate(sg_cols, axis=1)
    acc = acc_ref[...] + jnp.dot(sgn, wo_ref[att_cols:, :], preferred_element_type=F32)

    h = x_ref[...] + acc
    h_ref[...] = h
    h_prev_ref[...] = h


def _route_tile(hn, live, rwc_ref, rwh_ref, rb_ref, lstrict_ref, route_ref, route_t_ref, counts_ref, cnt_ref):
    hn_bf, hn_lo = _split_bf16(hn)
    both = jnp.dot(hn_bf, rwc_ref[...], preferred_element_type=F32)
    logits = (both[:, :LANES] + both[:, LANES:]
              + jnp.dot(hn_lo, rwh_ref[...], preferred_element_type=F32)
              + rb_ref[...])

    lane = lax.broadcasted_iota(jnp.int32, logits.shape, 1).astype(F32)
    big = float(LANES)
    is_grp = lane < N_GROUPS
    gl = jnp.where(is_grp, logits, NEG_INF)
    gmax = jnp.max(gl, axis=-1, keepdims=True)
    gidx = jnp.min(jnp.where(gl == gmax, lane, big), axis=-1, keepdims=True)
    psum = jnp.sum(jnp.where(is_grp, jnp.exp(gl - gmax), 0.0), axis=-1, keepdims=True)
    p_grp = 1.0 / psum
    e_lo = N_GROUPS + gidx * EXPERTS_PER_GROUP
    in_grp = (lane >= e_lo) & (lane < e_lo + EXPERTS_PER_GROUP)
    el = jnp.where(in_grp, logits, NEG_INF)
    t1 = jnp.max(el, axis=-1, keepdims=True)
    i1 = jnp.min(jnp.where(el == t1, lane, big), axis=-1, keepdims=True)
    el2 = jnp.where(lane == i1, NEG_INF, el)
    t2 = jnp.max(el2, axis=-1, keepdims=True)
    i2 = jnp.min(jnp.where(el2 == t2, lane, big), axis=-1, keepdims=True)
    e21 = jnp.exp(t2 - t1)
    den = 1.0 + e21
    g1 = p_grp / den
    g2 = p_grp * (e21 / den)
    e1 = i1 - N_GROUPS
    e2 = i2 - N_GROUPS

    hit1 = lane == e1
    hit2 = lane == e2
    onehot = jnp.where(hit1 | hit2, live, 0.0)
    before = jnp.dot(lstrict_ref[...], onehot.astype(BF16), preferred_element_type=F32) + cnt_ref[...]
    r1 = jnp.sum(jnp.where(hit1, before, 0.0), axis=-1, keepdims=True)
    r2 = jnp.sum(jnp.where(hit2, before, 0.0), axis=-1, keepdims=True)
    cnt_ref[...] = cnt_ref[...] + jnp.sum(onehot, axis=0, keepdims=True)
    counts_ref[...] = cnt_ref[...]

    route = jnp.zeros_like(logits)
    for k, val in enumerate((e1, e2, g1, g2, r1, r2)):
        route = jnp.where(lane == k, val, route)
    route_ref[...] = route
    route_t_ref[...] = route.T[0:SUBLANES, :]


def _mixout(att, su, sv, x2d, sgu_w, sgu_bt, sgu_gn, w_out, g_ffn, rw_cat, rw_hi, rb, lstrict, *,
            att_cols):
    t, d = x2d.shape
    tm = ROW_TILE
    n = t // tm
    row = lambda i: (jnp.minimum(i, n - 1), 0)
    routed = lambda i: (jnp.maximum(i - 1, 0), 0)
    fixed2 = lambda i: (0, 0)
    n_groups = sgu_w.shape[0]
    kern = functools.partial(_mixout_kernel, n_sgu_groups=n_groups, att_cols=att_cols)
    return pl.pallas_call(
        kern,
        grid=(n + 1,),
        in_specs=[
            pl.BlockSpec((tm, att.shape[1]), row),
            pl.BlockSpec((tm, su.shape[1]), row),
            pl.BlockSpec((tm, sv.shape[1]), row),
            pl.BlockSpec((tm, d), row),
            pl.BlockSpec(sgu_w.shape, lambda i: (0, 0, 0)),
            pl.BlockSpec(sgu_bt.shape, fixed2),
            pl.BlockSpec(sgu_gn.shape, fixed2),
            pl.BlockSpec(w_out.shape, fixed2),
            pl.BlockSpec((1, d), fixed2),
            pl.BlockSpec(rw_cat.shape, fixed2),
            pl.BlockSpec(rw_hi.shape, fixed2),
            pl.BlockSpec((1, LANES), fixed2),
            pl.BlockSpec((tm, tm), fixed2),
        ],
        out_specs=[
            pl.BlockSpec((tm, d), row),
            pl.BlockSpec((tm, d // 2), routed),
            pl.BlockSpec((tm, LANES), routed),
            pl.BlockSpec((SUBLANES, tm), lambda i: (0, jnp.maximum(i - 1, 0))),
            pl.BlockSpec((1, LANES), fixed2),
        ],
        out_shape=[
            jax.ShapeDtypeStruct((t, d), F32),
            jax.ShapeDtypeStruct((t, d // 2), U32),
            jax.ShapeDtypeStruct((t, LANES), F32),
            jax.ShapeDtypeStruct((SUBLANES, t), F32),
            jax.ShapeDtypeStruct((1, LANES), F32),
        ],
        scratch_shapes=[pltpu.VMEM((1, LANES), F32), pltpu.VMEM((tm, d), F32), pltpu.VMEM((tm, d), F32)],
        compiler_params=_params(1),
        name="mixout",
    )(att, su, sv, x2d, sgu_w, sgu_bt, sgu_gn, w_out, g_ffn, rw_cat, rw_hi, rb, lstrict)


def _dispatch_kernel(*refs):
    dest_refs, (hn_ref, xs_ref, sem) = refs[:TOP_K], refs[TOP_K:]
    rows = hn_ref.shape[0]

    for r in range(rows):
        for k in range(TOP_K):
            pltpu.make_async_copy(hn_ref.at[pl.ds(r, 1), :],
                                  xs_ref.at[pl.ds(dest_refs[k][r], 1), :], sem).start(priority=k)
    for k in range(TOP_K):
        pltpu.make_async_copy(hn_ref, xs_ref.at[pl.ds(0, rows), :], sem).wait()


def _dispatch(hn, dests):
    t, d = hn.shape
    rows = DISPATCH_TILE
    index_spec = pl.BlockSpec((rows,), lambda i: (i,), memory_space=pltpu.SMEM)
    return pl.pallas_call(
        _dispatch_kernel,
        grid=(t // rows,),
        in_specs=[index_spec] * TOP_K + [pl.BlockSpec((rows, d), lambda i: (i, 0))],
        out_specs=pl.BlockSpec(memory_space=pl.ANY),
        out_shape=jax.ShapeDtypeStruct((t * TOP_K, d), hn.dtype),
        scratch_shapes=[pltpu.SemaphoreType.DMA],
        compiler_params=_params(1),
        name="dispatch",
    )(*dests, hn)


def _experts_kernel(vt_ref, ve_ref, vf_ref, vn_ref, vs_ref, vx_ref, lo_ref, hi_ref, xs_ref, wg_hbm, wu_hbm,
                    wd_hbm, ys_ref, wg_f32, wu_f32, wd_f32, wg_bf, wu_bf, wd_bf, act_ref, wsem):
    v = pl.program_id(0)
    flag = vf_ref[v]
    tm = xs_ref.shape[0]

    def weight_copies(expert, slot):
        return [pltpu.make_async_copy(src.at[expert], dst.at[slot], wsem.at[slot])
                for src, dst in ((wg_hbm, wg_f32), (wu_hbm, wu_f32), (wd_hbm, wd_f32))]

    @pl.when(v == 0)
    def _():
        for cp in weight_copies(ve_ref[0], 0):
            cp.start()

    @pl.when(vn_ref[v] == 1)
    def _():
        slot = vs_ref[v]
        for cp in weight_copies(ve_ref[v], slot):
            cp.wait()

        @pl.when(vx_ref[v] >= 0)
        def _():
            for cp in weight_copies(vx_ref[v], 1 - slot):
                cp.start()

        wg_bf[...] = wg_f32[slot].astype(BF16)
        wu_bf[...] = wu_f32[slot].astype(BF16)
        wd_bf[...] = wd_f32[slot].astype(BF16)

    @pl.when(flag > 0)
    def _():
        e = ve_ref[v]
        base = vt_ref[v] * tm
        rid = lax.broadcasted_iota(jnp.int32, (tm, 1), 0) + base
        mine = (rid >= lo_ref[e]) & (rid < hi_ref[e])
        x = _unpack_bf16_pairs(xs_ref[...]).astype(BF16)
        for n in range(act_ref.shape[1] // MXU_WIDTH):
            cols = slice(n * MXU_WIDTH, (n + 1) * MXU_WIDTH)
            hg = jnp.dot(x, wg_bf[:, cols], preferred_element_type=F32)
            hu = jnp.dot(x, wu_bf[:, cols], preferred_element_type=F32)
            act_ref[:, cols] = (hg * (1.0 / (1.0 + jnp.exp(-hg))) * hu).astype(BF16)
        y = _pack_bf16_pairs(jnp.dot(act_ref[...], wd_bf[...], preferred_element_type=F32))

        @pl.when(flag == 1)
        def _():
            ys_ref[...] = jnp.where(mine, y, jnp.zeros_like(y))

        @pl.when(flag == 2)
        def _():
            ys_ref[...] = jnp.where(mine, y, ys_ref[...])


def _experts(visits, row_lo, row_hi, xs, w_gate, w_up, w_down):
    p, words = xs.shape
    d = 2 * words
    tm = EXPERT_TILE
    f = w_gate.shape[-1]
    rows_map = lambda v, vt, *_: (vt[v], 0)
    whole = pl.BlockSpec(memory_space=pl.ANY)
    grid_spec = pltpu.PrefetchScalarGridSpec(
        num_scalar_prefetch=len(visits) + 2,
        grid=(visits[0].shape[0],),
        in_specs=[pl.BlockSpec((tm, words), rows_map), whole, whole, whole],
        out_specs=pl.BlockSpec((tm, words), rows_map),
        scratch_shapes=[pltpu.VMEM((2, d, f), F32), pltpu.VMEM((2, d, f), F32), pltpu.VMEM((2, f, d), F32),
                        pltpu.VMEM((d, f), BF16), pltpu.VMEM((d, f), BF16), pltpu.VMEM((f, d), BF16),
                        pltpu.VMEM((tm, f), BF16), pltpu.SemaphoreType.DMA((2,))],
    )
    return pl.pallas_call(
        _experts_kernel,
        grid_spec=grid_spec,
        out_shape=jax.ShapeDtypeStruct(xs.shape, xs.dtype),
        compiler_params=_params(1),
        name="experts",
    )(*visits, row_lo, row_hi, xs, w_gate, w_up, w_down)


def _combine_kernel(*refs):
    dest_refs, dest_next_refs = refs[:TOP_K], refs[TOP_K:2 * TOP_K]
    h_ref, route_ref, ys_ref, o_ref, ybuf, sem = refs[2 * TOP_K:]
    i = pl.program_id(0)
    rows = h_ref.shape[0]
    slot = i % 2

    def issue(idx_refs, into):
        for r in range(rows):
            for k in range(TOP_K):
                pltpu.make_async_copy(ys_ref.at[pl.ds(idx_refs[k][r], 1), :],
                                      ybuf.at[into, k, pl.ds(r, 1), :], sem.at[into]).start(priority=k)

    @pl.when(i == 0)
    def _():
        issue(dest_refs, 0)

    @pl.when(i + 1 < pl.num_programs(0))
    def _():
        issue(dest_next_refs, 1 - slot)

    for k in range(TOP_K):
        pltpu.make_async_copy(ys_ref.at[pl.ds(0, rows), :], ybuf.at[slot, k], sem.at[slot]).wait()

    route = route_ref[...]
    moe = None
    for k in range(TOP_K):
        gated = route[:, ROUTE_GATE + k:ROUTE_GATE + k + 1] * _unpack_bf16_pairs(ybuf[slot, k])
        moe = gated if moe is None else moe + gated
    o_ref[...] = h_ref[...] + moe


def _combine(dests, h, route, ys):
    t, d = h.shape
    rows = COMBINE_TILE
    n = t // rows
    this_step = pl.BlockSpec((rows,), lambda i: (i,), memory_space=pltpu.SMEM)
    next_step = pl.BlockSpec((rows,), lambda i: (jnp.minimum(i + 1, n - 1),), memory_space=pltpu.SMEM)
    return pl.pallas_call(
        _combine_kernel,
        grid=(n,),
        in_specs=[this_step] * TOP_K + [next_step] * TOP_K + [
            pl.BlockSpec((rows, d), lambda i: (i, 0)),
            pl.BlockSpec((rows, LANES), lambda i: (i, 0)),
            pl.BlockSpec(memory_space=pl.ANY),
        ],
        out_specs=pl.BlockSpec((rows, d), lambda i: (i, 0)),
        out_shape=jax.ShapeDtypeStruct((t, d), F32),
        scratch_shapes=[pltpu.VMEM((2, TOP_K, rows, ys.shape[1]), ys.dtype), pltpu.SemaphoreType.DMA((2,))],
        compiler_params=_params(1),
        name="combine",
    )(*dests, *dests, h, route, ys)


def _plan(route_t, counts_row):
    t = route_t.shape[1]
    counts = counts_row[0, :N_EXPERTS].astype(I32)
    ends = jnp.cumsum(counts)
    offs = ends - counts
    experts = route_t[ROUTE_EXPERT:ROUTE_EXPERT + TOP_K].astype(I32)
    rank = route_t[ROUTE_RANK:ROUTE_RANK + TOP_K].astype(I32)
    expert_ids = jnp.arange(N_EXPERTS, dtype=I32)[:, None, None]
    first_row = jnp.sum(jnp.where(experts[None] == expert_ids, offs[:, None, None], 0), axis=0)
    dest = first_row + rank
    dests = tuple(dest[k] for k in range(TOP_K))

    tm = EXPERT_TILE
    n_visits = (t * TOP_K) // tm + N_EXPERTS - 1
    first_tile = offs // tm
    last_tile = (ends - 1) // tm
    nvis = jnp.where(counts > 0, last_tile - first_tile + 1, 0)
    vend = jnp.cumsum(nvis)
    vstart = vend - nvis
    v = jnp.arange(n_visits, dtype=I32)
    ev = jnp.minimum(jnp.sum((vend[None, :] <= v[:, None]).astype(I32), axis=1), N_EXPERTS - 1)
    tv = jnp.take(first_tile, ev) + (v - jnp.take(vstart, ev))
    valid = v < vend[-1]
    last_v = vend[-1] - 1
    ev = jnp.where(valid, ev, jnp.take(ev, last_v))
    tv = jnp.where(valid, tv, jnp.take(tv, last_v))
    prev_t = jnp.concatenate([jnp.full((1,), -1, I32), tv[:-1]])
    prev_e = jnp.concatenate([jnp.full((1,), -1, I32), ev[:-1]])
    flag = jnp.where(valid, jnp.where(tv != prev_t, 1, 2), 0).astype(I32)
    new_expert = (ev != prev_e).astype(I32)
    slot = (jnp.cumsum(new_expert) - 1) % 2
    ids = jnp.arange(N_EXPERTS, dtype=I32)
    later = (ids[None, :] > ids[:, None]) & (counts[None, :] > 0)
    next_active = jnp.min(jnp.where(later, ids[None, :], N_EXPERTS), axis=1)
    next_active = jnp.where(next_active < N_EXPERTS, next_active, -1)
    visits = (tv.astype(I32), ev.astype(I32), flag, new_expert, slot.astype(I32),
              jnp.take(next_active, ev).astype(I32))
    return dests, visits, offs.astype(I32), ends.astype(I32)


def _rope_constants(head_dim, rope_dim):
    rope_half = rope_dim // 2
    inv_freq = ROPE_THETA ** (-jnp.arange(0, rope_dim, 2, dtype=F32) / rope_dim)
    dim = jnp.arange(LANES, dtype=I32)[None, :] % head_dim
    k = jnp.arange(4 * rope_half, dtype=I32)[:, None]
    c_mat = jnp.where(dim < rope_dim, k == dim % rope_half, k == rope_dim)
    sa_mat = (dim < rope_half) & (k == rope_half + dim)
    sb_mat = (dim >= rope_half) & (dim < rope_dim) & (k == dim)
    expand = jnp.concatenate([c_mat.astype(F32), -sa_mat.astype(F32), sb_mat.astype(F32)], axis=1)
    return inv_freq.reshape(rope_half, 1), expand.astype(BF16)


def kernel(x, positions, norm_mix_g, w_in, q_norm_g, k_norm_g, lambda_q1, lambda_k1, lambda_q2,
           lambda_k2, diff_out_norm_g, sgu_ln_g, sgu_ln_b, sgu_w, sgu_b, sgu_out_norm_g, w_out,
           norm_ffn_g, router_group_w, router_group_b, router_expert_w, router_expert_b,
           expert_w_gate, expert_w_up, expert_w_down):
    b, s, d = x.shape
    t = b * s
    depth = w_in.shape[0]
    head_dim = q_norm_g.shape[-1]
    vd = diff_out_norm_g.shape[-1]
    n_sgu_groups = sgu_w.shape[1]
    sgu_cols = n_sgu_groups * sgu_w.shape[-1]
    in_cols = w_in.shape[-1]
    att_cols = (in_cols - 2 * sgu_cols) // 3
    n_heads = att_cols // vd
    qk_cols = 2 * att_cols
    rope_dim = head_dim // ROPE_FRACTION
    rope_half = rope_dim // 2
    scale = head_dim ** -0.5

    pos_rows = positions.reshape(t // ROW_TILE, 1, ROW_TILE).astype(F32)
    freq_col, expand = _rope_constants(head_dim, rope_dim)
    blk = jnp.arange(MXU_WIDTH, dtype=I32) // head_dim
    ones_blk = (blk[:, None] == blk[None, :]).astype(BF16)
    tri = jnp.arange(ROW_TILE, dtype=I32)
    lstrict = (tri[None, :] < tri[:, None]).astype(BF16)

    h = x.reshape(t, d)
    for l in range(depth):
        lambda_init = 0.8 - 0.6 * math.exp(-0.3 * l)
        gqk = jnp.concatenate([jnp.tile(q_norm_g[l] * (scale * math.log2(math.e)), att_cols // head_dim),
                               jnp.tile(k_norm_g[l], att_cols // head_dim)]).reshape(1, qk_cols)
        qk, v, su, sv = _inproj(
            h, norm_mix_g[l].reshape(1, d), w_in[l].astype(BF16), gqk, pos_rows, freq_col, expand, ones_blk,
            sgu_ln_g[l].reshape(1, sgu_cols), sgu_ln_b[l].reshape(1, sgu_cols),
            qk_cols=qk_cols, v_cols=att_cols, sgu_cols=sgu_cols, head_dim=head_dim,
            rope_half=rope_half)

        lam_vecs = jnp.stack([lambda_q1[l], lambda_k1[l], lambda_q2[l], lambda_k2[l]]).astype(F32)
        score_bound = (1.02 * head_dim * scale * jnp.max(jnp.abs(q_norm_g[l]))
                       * jnp.max(jnp.abs(k_norm_g[l])))
        need_shift = (score_bound > MAX_UNSHIFTED_SCORE).astype(I32).reshape(1)
        attn_args = (lam_vecs, diff_out_norm_g[l].reshape(1, vd), qk.reshape(b, s, qk_cols),
                     v.reshape(b, s, att_cols))
        attn_kw = dict(n_heads=n_heads, head_dim=head_dim, lambda_init=lambda_init)
        att = lax.cond(need_shift[0] != 0,
                       lambda *a: _attention(need_shift, *a, **attn_kw),
                       lambda *a: _attention_static(*a, **attn_kw),
                       *attn_args).reshape(t, att_cols)

        rw = jnp.concatenate([router_group_w[l], router_expert_w[l]], axis=1)
        rw = jnp.pad(rw, ((0, 0), (0, LANES - rw.shape[1])))
        rw_hi, rw_lo = _split_bf16(rw)
        rb = jnp.pad(jnp.concatenate([router_group_b[l], router_expert_b[l]]),
                     (0, LANES - N_GROUPS - N_EXPERTS)).reshape(1, LANES)
        h, hn, route, route_t, counts_row = _mixout(
            att, su, sv, h, sgu_w[l], sgu_b[l].T, sgu_out_norm_g[l], w_out[l].astype(BF16),
            norm_ffn_g[l].reshape(1, d), jnp.concatenate([rw_hi, rw_lo], axis=1), rw_hi, rb, lstrict,
            att_cols=att_cols)

        dests, visits, row_lo, row_hi = _plan(route_t, counts_row)
        xs = _dispatch(hn, dests)
        ys = _experts(visits, row_lo, row_hi, xs, expert_w_gate[l], expert_w_up[l], expert_w_down[l])
        h = _combine(dests, h, route, ys)
    return h.reshape(b, s, d)
```

```python
import functools
import math

import jax
import jax.numpy as jnp
from jax import lax
from jax.experimental import pallas as pl
from jax.experimental.pallas import tpu as pltpu

F32 = jnp.float32
BF16 = jnp.bfloat16
I32 = jnp.int32

RMS_EPS = 1e-6
NEG_INF = -1e30
ROPE_THETA = 500000.0
MAX_UNSHIFTED_SCORE = 40.0

LANES = 128
SUBLANES = 8
MXU_WIDTH = 256
VMEM_LIMIT_BYTES = 56 * 1024 * 1024

N_GROUPS = 4
EXPERTS_PER_GROUP = 8
N_EXPERTS = N_GROUPS * EXPERTS_PER_GROUP
TOP_K = 2
ROPE_FRACTION = 4
CHUNK = 128
QBLOCK = 256
ATTN_HEADS_PER_STEP = 2
ROW_TILE = 512
EXPERT_TILE = 512
DISPATCH_TILE = 1024
COMBINE_TILE = 512

ROUTE_EXPERT = 0
ROUTE_GATE = 2
ROUTE_RANK = 4


def _params(n_grid_dims):
    return pltpu.CompilerParams(
        dimension_semantics=("arbitrary",) * n_grid_dims,
        vmem_limit_bytes=VMEM_LIMIT_BYTES,
    )


def _gelu(x):
    return 0.5 * x * (1.0 + lax.erf(x * (1.0 / math.sqrt(2.0))))


def _split_bf16(x):
    hi = x.astype(BF16)
    return hi, (x - hi.astype(F32)).astype(BF16)


U32 = jnp.uint32
BF16_BITS = 16
HIGH_HALF = 0xFFFF0000


def _pack_bf16_pairs(x):
    n = x.shape[1] // 2
    hi = lax.bitcast_convert_type(x[:, :n].astype(BF16).astype(F32), U32)
    lo = lax.bitcast_convert_type(x[:, n:].astype(BF16).astype(F32), U32)
    return hi | (lo >> BF16_BITS)


def _unpack_bf16_pairs(w):
    hi = lax.bitcast_convert_type(w & jnp.uint32(HIGH_HALF), F32)
    lo = lax.bitcast_convert_type(w << BF16_BITS, F32)
    return jnp.concatenate([hi, lo], axis=1)


def _inproj_kernel(x_ref, g_ref, w_ref, gqk_ref, pos_ref, freq_ref, expand_ref, ones_ref, lng_ref, lnb_ref,
                   qk_ref, v_ref, su_ref, sv_ref, proj_ref, ss_ref, *, qk_cols, v_cols, sgu_cols, head_dim,
                   rope_half):
    x = x_ref[...]
    ms = jnp.mean(x * x, axis=-1, keepdims=True)
    xn = (x * lax.rsqrt(ms + RMS_EPS) * g_ref[...]).astype(BF16)
    cw = MXU_WIDTH

    ang_t = freq_ref[...] * pos_ref[0]
    cs_t = jnp.concatenate([jnp.cos(ang_t), jnp.sin(ang_t), jnp.ones_like(ang_t),
                            jnp.zeros_like(ang_t)], axis=0)
    contract0 = (((0,), (0,)), ((), ()))
    cs_hi, cs_lo = _split_bf16(cs_t)
    tab = (lax.dot_general(cs_hi, expand_ref[...], contract0, preferred_element_type=F32)
           + lax.dot_general(cs_lo, expand_ref[...], contract0, preferred_element_type=F32))
    cos = tab[:, 0:LANES]
    sa = tab[:, LANES:2 * LANES]
    sb = tab[:, 2 * LANES:3 * LANES]

    def stage(c):
        cols = slice(c * cw, (c + 1) * cw)
        proj_ref[:, cols] = jnp.dot(xn, w_ref[:, cols], preferred_element_type=F32)

    n_qk = qk_cols // cw
    for c in range(n_qk):
        stage(c)
    for c in range(n_qk):
        p = proj_ref[:, c * cw:(c + 1) * cw]
        ss_ref[:, c * cw:(c + 1) * cw] = jnp.dot((p * p).astype(BF16), ones_ref[...],
                                                 preferred_element_type=F32)
    n_v = v_cols // cw
    for c in range(n_qk + n_v, w_ref.shape[1] // cw):
        stage(c)
    for c in range(n_qk, n_qk + n_v):
        stage(c)

    def proj(col0):
        return proj_ref[:, col0:col0 + cw]

    for c in range(qk_cols // cw):
        p = proj(c * cw)
        ss = ss_ref[:, c * cw:(c + 1) * cw]
        pn = p * lax.rsqrt(ss * (1.0 / head_dim) + RMS_EPS) * gqk_ref[:, c * cw:(c + 1) * cw]
        for hh in range(cw // LANES):
            blk = pn[:, hh * LANES:(hh + 1) * LANES]
            rot = (blk * cos
                   + pltpu.roll(blk, LANES - rope_half, 1) * sa
                   + pltpu.roll(blk, rope_half, 1) * sb)
            col = c * cw + hh * LANES
            qk_ref[:, col:col + LANES] = rot.astype(BF16)

    for c in range(v_cols // cw):
        v_ref[:, c * cw:(c + 1) * cw] = proj(qk_cols + c * cw).astype(BF16)

    for c in range(sgu_cols // cw):
        su_ref[:, c * cw:(c + 1) * cw] = _gelu(proj(qk_cols + v_cols + c * cw)).astype(BF16)

    sv = jnp.concatenate(
        [_gelu(proj(qk_cols + v_cols + sgu_cols + c * cw)) for c in range(sgu_cols // cw)], axis=1)
    mu = jnp.mean(sv, axis=-1, keepdims=True)
    svc = sv - mu
    var = jnp.mean(svc * svc, axis=-1, keepdims=True)
    sv_ref[...] = (svc * lax.rsqrt(var + RMS_EPS) * lng_ref[...] + lnb_ref[...]).astype(BF16)


def _inproj(x2d, norm_g, w_in, gqk, pos_rows, freq_col, expand, ones_blk, ln_g, ln_b, *, qk_cols, v_cols,
            sgu_cols, head_dim, rope_half):
    t, d = x2d.shape
    tm = ROW_TILE
    row = lambda i: (i, 0)
    fixed = lambda i: (0, 0)
    kern = functools.partial(_inproj_kernel, qk_cols=qk_cols, v_cols=v_cols, sgu_cols=sgu_cols,
                             head_dim=head_dim, rope_half=rope_half)
    return pl.pallas_call(
        kern,
        grid=(t // tm,),
        in_specs=[
            pl.BlockSpec((tm, d), row),
            pl.BlockSpec((1, d), fixed),
            pl.BlockSpec(w_in.shape, fixed),
            pl.BlockSpec((1, qk_cols), fixed),
            pl.BlockSpec((1, 1, tm), lambda i: (i, 0, 0)),
            pl.BlockSpec(freq_col.shape, fixed),
            pl.BlockSpec(expand.shape, fixed),
            pl.BlockSpec((MXU_WIDTH, MXU_WIDTH), fixed),
            pl.BlockSpec((1, sgu_cols), fixed),
            pl.BlockSpec((1, sgu_cols), fixed),
        ],
        out_specs=[
            pl.BlockSpec((tm, qk_cols), row),
            pl.BlockSpec((tm, v_cols), row),
            pl.BlockSpec((tm, sgu_cols), row),
            pl.BlockSpec((tm, sgu_cols), row),
        ],
        out_shape=[
            jax.ShapeDtypeStruct((t, qk_cols), BF16),
            jax.ShapeDtypeStruct((t, v_cols), BF16),
            jax.ShapeDtypeStruct((t, sgu_cols), BF16),
            jax.ShapeDtypeStruct((t, sgu_cols), BF16),
        ],
        scratch_shapes=[pltpu.VMEM((tm, w_in.shape[1]), F32), pltpu.VMEM((tm, qk_cols), F32)],
        compiler_params=_params(1),
        name="inproj",
    )(x2d, norm_g, w_in, gqk, pos_rows, freq_col, expand, ones_blk, ln_g, ln_b)


def _attn_running_max_kernel(lam_ref, gout_ref, q_ref, k_ref, v_ref, o_ref, qm_ref, m_ref, acc_ref, *, tq,
                             n_heads, lambda_init, head_dim):
    qi = pl.program_id(1)
    hw = 2 * head_dim
    vd = v_ref.shape[-1] // n_heads
    lv = lam_ref[...]
    lam = (jnp.exp(jnp.sum(lv[0:1] * lv[1:2], axis=-1, keepdims=True))
           - jnp.exp(jnp.sum(lv[2:3] * lv[3:4], axis=-1, keepdims=True))
           + lambda_init)

    first = lax.broadcasted_iota(jnp.int32, (1, hw), 1) < head_dim
    for h in range(n_heads):
        q = q_ref[0, :, h * hw:(h + 1) * hw]
        zero = jnp.zeros_like(q)
        qm_ref[2 * h] = jnp.where(first, q, zero)
        qm_ref[2 * h + 1] = jnp.where(first, zero, q)
    acc_ref[...] = jnp.zeros(acc_ref.shape, F32)
    m_ref[...] = jnp.full(m_ref.shape, NEG_INF, F32)

    ones = jnp.ones((tq, vd), BF16)
    causal = (lax.broadcasted_iota(jnp.int32, (tq, tq), 1)
              <= lax.broadcasted_iota(jnp.int32, (tq, tq), 0))

    def kv_step(j, masked):
        start = pl.multiple_of(j * tq, tq)
        for h in range(n_heads):
            kc = k_ref[0, pl.ds(start, tq), h * hw:(h + 1) * hw]
            vext = jnp.concatenate([v_ref[0, pl.ds(start, tq), h * vd:(h + 1) * vd], ones], axis=1)
            for mp in range(2):
                c = 2 * h + mp
                s = lax.dot_general(qm_ref[c], kc, (((1,), (1,)), ((), ())), preferred_element_type=F32)
                if masked:
                    s = jnp.where(causal, s, NEG_INF)
                m_old = m_ref[c]
                m_new = jnp.maximum(m_old, jnp.max(s, axis=-1, keepdims=True))
                p = jnp.exp2(s - m_new).astype(BF16)
                acc_ref[c] = (jnp.exp2(m_old - m_new) * acc_ref[c]
                              + jnp.dot(p, vext, preferred_element_type=F32))
                m_ref[c] = m_new

    def full_tile(j, carry):
        kv_step(j, False)
        return carry

    lax.fori_loop(0, qi, full_tile, 0)
    kv_step(qi, True)

    for h in range(n_heads):
        a1 = acc_ref[2 * h]
        a2 = acc_ref[2 * h + 1]
        att = a1[:, :vd] / a1[:, vd:] - lam * (a2[:, :vd] / a2[:, vd:])
        ms = jnp.mean(att * att, axis=-1, keepdims=True)
        o_ref[0, :, h * vd:(h + 1) * vd] = (
            att * lax.rsqrt(ms + RMS_EPS) * gout_ref[...] * (1.0 - lambda_init)).astype(BF16)


def _attn_static_kernel(lam_ref, gout_ref, q_ref, k_ref, v_ref, o_ref, vext_ref, *, tq, heads, lambda_init,
                        head_dim):
    s_len = q_ref.shape[1]
    hw = 2 * head_dim
    vd = v_ref.shape[-1] // heads
    lv = lam_ref[...]
    lam = (jnp.exp(jnp.sum(lv[0:1] * lv[1:2], axis=-1, keepdims=True))
           - jnp.exp(jnp.sum(lv[2:3] * lv[3:4], axis=-1, keepdims=True))
           + lambda_init)
    first = lax.broadcasted_iota(jnp.int32, (1, hw), 1) < head_dim
    causal = (lax.broadcasted_iota(jnp.int32, (tq, tq), 1)
              <= lax.broadcasted_iota(jnp.int32, (tq, tq), 0))
    nt = (((1,), (1,)), ((), ()))

    for h in range(heads):
        vext_ref[h, :, 0:vd] = v_ref[0, :, h * vd:(h + 1) * vd]
        vext_ref[h, :, vd:2 * vd] = jnp.ones((s_len, vd), BF16)

    for h in range(heads):
        kcols = slice(h * hw, (h + 1) * hw)
        for qi in range(s_len // tq):
            lo = qi * tq
            q = q_ref[0, lo:lo + tq, kcols]
            zero = jnp.zeros_like(q)
            acc = []
            for mp in range(2):
                qm = jnp.where(first, q, zero) if mp == 0 else jnp.where(first, zero, q)
                sd = lax.dot_general(qm, k_ref[0, lo:lo + tq, kcols], nt, preferred_element_type=F32)
                pd = jnp.exp2(jnp.where(causal, sd, NEG_INF)).astype(BF16)
                pv = jnp.dot(pd, vext_ref[h, lo:lo + tq, :], preferred_element_type=F32)
                if qi > 0:
                    s = lax.dot_general(qm, k_ref[0, 0:lo, kcols], nt, preferred_element_type=F32)
                    pv = pv + jnp.dot(jnp.exp2(s).astype(BF16), vext_ref[h, 0:lo, :],
                                      preferred_element_type=F32)
                acc.append(pv)
            a1, a2 = acc
            att = a1[:, :vd] / a1[:, vd:] - lam * (a2[:, :vd] / a2[:, vd:])
            ms = jnp.mean(att * att, axis=-1, keepdims=True)
            o_ref[0, lo:lo + tq, h * vd:(h + 1) * vd] = (
                att * lax.rsqrt(ms + RMS_EPS) * gout_ref[...] * (1.0 - lambda_init)).astype(BF16)


def _attention_static(lam_vecs, gout, qk3, v3, *, n_heads, head_dim, lambda_init):
    b, s, _ = qk3.shape
    vd = v3.shape[-1] // n_heads
    heads = ATTN_HEADS_PER_STEP
    groups = n_heads // heads
    qw = heads * 2 * head_dim
    kern = functools.partial(_attn_static_kernel, tq=QBLOCK, heads=heads, lambda_init=lambda_init,
                             head_dim=head_dim)
    return pl.pallas_call(
        kern,
        grid=(b, groups),
        in_specs=[
            pl.BlockSpec(lam_vecs.shape, lambda bi, g: (0, 0)),
            pl.BlockSpec((1, vd), lambda bi, g: (0, 0)),
            pl.BlockSpec((1, s, qw), lambda bi, g: (bi, 0, g)),
            pl.BlockSpec((1, s, qw), lambda bi, g: (bi, 0, groups + g)),
            pl.BlockSpec((1, s, heads * vd), lambda bi, g: (bi, 0, g)),
        ],
        out_specs=pl.BlockSpec((1, s, heads * vd), lambda bi, g: (bi, 0, g)),
        out_shape=jax.ShapeDtypeStruct(v3.shape, BF16),
        scratch_shapes=[pltpu.VMEM((heads, s, 2 * vd), BF16)],
        compiler_params=_params(2),
        name="diff_attn_static",
    )(lam_vecs, gout, qk3, qk3, v3)


def _attention_running_max(lam_vecs, gout, qk3, v3, *, n_heads, head_dim, lambda_init):
    b, s, _ = qk3.shape
    att_cols = v3.shape[-1]
    vd = att_cols // n_heads
    tq = QBLOCK
    kern = functools.partial(_attn_running_max_kernel, tq=tq, n_heads=n_heads, lambda_init=lambda_init,
                             head_dim=head_dim)
    return pl.pallas_call(
        kern,
        grid=(b, s // tq),
        in_specs=[
            pl.BlockSpec(lam_vecs.shape, lambda bi, qi: (0, 0)),
            pl.BlockSpec((1, vd), lambda bi, qi: (0, 0)),
            pl.BlockSpec((1, tq, att_cols), lambda bi, qi: (bi, qi, 0)),
            pl.BlockSpec((1, s, att_cols), lambda bi, qi: (bi, 0, 1)),
            pl.BlockSpec((1, s, att_cols), lambda bi, qi: (bi, 0, 0)),
        ],
        out_specs=pl.BlockSpec((1, tq, att_cols), lambda bi, qi: (bi, qi, 0)),
        out_shape=jax.ShapeDtypeStruct(v3.shape, BF16),
        scratch_shapes=[
            pltpu.VMEM((2 * n_heads, tq, 2 * head_dim), BF16),
            pltpu.VMEM((2 * n_heads, tq, 1), F32),
            pltpu.VMEM((2 * n_heads, tq, 2 * vd), F32),
        ],
        compiler_params=_params(2),
        name="diff_attn_running_max",
    )(lam_vecs, gout, qk3, qk3, v3)


def _mixout_kernel(att_ref, su_ref, sv_ref, x_ref, sw_ref, sbt_ref, sgn_ref, wo_ref, gffn_ref,
                   rwc_ref, rwh_ref, rb_ref, lstrict_ref, h_ref, hn_ref, route_ref, route_t_ref, counts_ref,
                   cnt_ref, h_prev_ref, acc_ref, *, n_sgu_groups, att_cols):
    step = pl.program_id(0)
    tm = x_ref.shape[0]
    gw = CHUNK
    r_id = lax.broadcasted_iota(jnp.int32, (gw, gw), 0)
    c_id = lax.broadcasted_iota(jnp.int32, (gw, gw), 1)
    tril = c_id <= r_id

    @pl.when(step == 0)
    def _():
        cnt_ref[...] = jnp.zeros_like(cnt_ref)
        h_prev_ref[...] = jnp.zeros_like(h_prev_ref)

    h_prev = h_prev_ref[...]
    ms_prev = jnp.mean(h_prev * h_prev, axis=-1, keepdims=True)
    hn = h_prev * lax.rsqrt(ms_prev + RMS_EPS) * gffn_ref[...]
    hn_ref[...] = _pack_bf16_pairs(hn)
    _route_tile(hn, (step > 0).astype(F32), rwc_ref, rwh_ref, rb_ref, lstrict_ref, route_ref, route_t_ref,
                counts_ref, cnt_ref)

    acc_ref[...] = jnp.dot(att_ref[...], wo_ref[0:att_cols, :], preferred_element_type=F32)

    sg_cols = []
    for g in range(n_sgu_groups):
        w = jnp.where(tril, sw_ref[g], 0.0).astype(BF16)
        bias = sbt_ref[:, g:g + 1]
        gn = sgn_ref[g:g + 1, :]
        rows = []
        for c in range(tm // gw):
            vblk = sv_ref[c * gw:(c + 1) * gw, g * gw:(g + 1) * gw]
            s = jnp.dot(w, vblk, preferred_element_type=F32) + bias
            sg = su_ref[c * gw:(c + 1) * gw, g * gw:(g + 1) * gw].astype(F32) * s
            ms = jnp.mean(sg * sg, axis=-1, keepdims=True)
            rows.append((sg * lax.rsqrt(ms + RMS_EPS) * gn).astype(BF16))
        sg_cols.append(jnp.concatenate(rows, axis=0))
    sgn = jnp.concatenate(sg_cols, axis=1)
    acc = acc_ref[...] + jnp.dot(sgn, wo_ref[att_cols:, :], preferred_element_type=F32)

    h = x_ref[...] + acc
    h_ref[...] = h
    h_prev_ref[...] = h


def _route_tile(hn, live, rwc_ref, rwh_ref, rb_ref, lstrict_ref, route_ref, route_t_ref, counts_ref, cnt_ref):
    hn_bf, hn_lo = _split_bf16(hn)
    both = jnp.dot(hn_bf, rwc_ref[...], preferred_element_type=F32)
    logits = (both[:, :LANES] + both[:, LANES:]
              + jnp.dot(hn_lo, rwh_ref[...], preferred_element_type=F32)
              + rb_ref[...])

    lane = lax.broadcasted_iota(jnp.int32, logits.shape, 1).astype(F32)
    big = float(LANES)
    is_grp = lane < N_GROUPS
    gl = jnp.where(is_grp, logits, NEG_INF)
    gmax = jnp.max(gl, axis=-1, keepdims=True)
    gidx = jnp.min(jnp.where(gl == gmax, lane, big), axis=-1, keepdims=True)
    psum = jnp.sum(jnp.where(is_grp, jnp.exp(gl - gmax), 0.0), axis=-1, keepdims=True)
    p_grp = 1.0 / psum
    e_lo = N_GROUPS + gidx * EXPERTS_PER_GROUP
    in_grp = (lane >= e_lo) & (lane < e_lo + EXPERTS_PER_GROUP)
    el = jnp.where(in_grp, logits, NEG_INF)
    t1 = jnp.max(el, axis=-1, keepdims=True)
    i1 = jnp.min(jnp.where(el == t1, lane, big), axis=-1, keepdims=True)
    el2 = jnp.where(lane == i1, NEG_INF, el)
    t2 = jnp.max(el2, axis=-1, keepdims=True)
    i2 = jnp.min(jnp.where(el2 == t2, lane, big), axis=-1, keepdims=True)
    e21 = jnp.exp(t2 - t1)
    den = 1.0 + e21
    g1 = p_grp / den
    g2 = p_grp * (e21 / den)
    e1 = i1 - N_GROUPS
    e2 = i2 - N_GROUPS

    hit1 = lane == e1
    hit2 = lane == e2
    onehot = jnp.where(hit1 | hit2, live, 0.0)
    before = jnp.dot(lstrict_ref[...], onehot.astype(BF16), preferred_element_type=F32) + cnt_ref[...]
    r1 = jnp.sum(jnp.where(hit1, before, 0.0), axis=-1, keepdims=True)
    r2 = jnp.sum(jnp.where(hit2, before, 0.0), axis=-1, keepdims=True)
    cnt_ref[...] = cnt_ref[...] + jnp.sum(onehot, axis=0, keepdims=True)
    counts_ref[...] = cnt_ref[...]

    route = jnp.zeros_like(logits)
    for k, val in enumerate((e1, e2, g1, g2, r1, r2)):
        route = jnp.where(lane == k, val, route)
    route_ref[...] = route
    route_t_ref[...] = route.T[0:SUBLANES, :]


def _mixout(att, su, sv, x2d, sgu_w, sgu_bt, sgu_gn, w_out, g_ffn, rw_cat, rw_hi, rb, lstrict, *,
            att_cols):
    t, d = x2d.shape
    tm = ROW_TILE
    n = t // tm
    row = lambda i: (jnp.minimum(i, n - 1), 0)
    routed = lambda i: (jnp.maximum(i - 1, 0), 0)
    fixed2 = lambda i: (0, 0)
    n_groups = sgu_w.shape[0]
    kern = functools.partial(_mixout_kernel, n_sgu_groups=n_groups, att_cols=att_cols)
    return pl.pallas_call(
        kern,
        grid=(n + 1,),
        in_specs=[
            pl.BlockSpec((tm, att.shape[1]), row),
            pl.BlockSpec((tm, su.shape[1]), row),
            pl.BlockSpec((tm, sv.shape[1]), row),
            pl.BlockSpec((tm, d), row),
            pl.BlockSpec(sgu_w.shape, lambda i: (0, 0, 0)),
            pl.BlockSpec(sgu_bt.shape, fixed2),
            pl.BlockSpec(sgu_gn.shape, fixed2),
            pl.BlockSpec(w_out.shape, fixed2),
            pl.BlockSpec((1, d), fixed2),
            pl.BlockSpec(rw_cat.shape, fixed2),
            pl.BlockSpec(rw_hi.shape, fixed2),
            pl.BlockSpec((1, LANES), fixed2),
            pl.BlockSpec((tm, tm), fixed2),
        ],
        out_specs=[
            pl.BlockSpec((tm, d), row),
            pl.BlockSpec((tm, d // 2), routed),
            pl.BlockSpec((tm, LANES), routed),
            pl.BlockSpec((SUBLANES, tm), lambda i: (0, jnp.maximum(i - 1, 0))),
            pl.BlockSpec((1, LANES), fixed2),
        ],
        out_shape=[
            jax.ShapeDtypeStruct((t, d), F32),
            jax.ShapeDtypeStruct((t, d // 2), U32),
            jax.ShapeDtypeStruct((t, LANES), F32),
            jax.ShapeDtypeStruct((SUBLANES, t), F32),
            jax.ShapeDtypeStruct((1, LANES), F32),
        ],
        scratch_shapes=[pltpu.VMEM((1, LANES), F32), pltpu.VMEM((tm, d), F32), pltpu.VMEM((tm, d), F32)],
        compiler_params=_params(1),
        name="mixout",
    )(att, su, sv, x2d, sgu_w, sgu_bt, sgu_gn, w_out, g_ffn, rw_cat, rw_hi, rb, lstrict)


def _dispatch_kernel(*refs):
    dest_refs, (hn_ref, xs_ref, sem) = refs[:TOP_K], refs[TOP_K:]
    rows = hn_ref.shape[0]

    for r in range(rows):
        for k in range(TOP_K):
            pltpu.make_async_copy(hn_ref.at[pl.ds(r, 1), :],
                                  xs_ref.at[pl.ds(dest_refs[k][r], 1), :], sem).start(priority=k)
    for k in range(TOP_K):
        pltpu.make_async_copy(hn_ref, xs_ref.at[pl.ds(0, rows), :], sem).wait()


def _dispatch(hn, dests):
    t, d = hn.shape
    rows = DISPATCH_TILE
    index_spec = pl.BlockSpec((rows,), lambda i: (i,), memory_space=pltpu.SMEM)
    return pl.pallas_call(
        _dispatch_kernel,
        grid=(t // rows,),
        in_specs=[index_spec] * TOP_K + [pl.BlockSpec((rows, d), lambda i: (i, 0))],
        out_specs=pl.BlockSpec(memory_space=pl.ANY),
        out_shape=jax.ShapeDtypeStruct((t * TOP_K, d), hn.dtype),
        scratch_shapes=[pltpu.SemaphoreType.DMA],
        compiler_params=_params(1),
        name="dispatch",
    )(*dests, hn)


def _experts_kernel(vt_ref, ve_ref, vf_ref, vn_ref, vs_ref, vx_ref, lo_ref, hi_ref, xs_ref, wg_hbm, wu_hbm,
                    wd_hbm, ys_ref, wg_f32, wu_f32, wd_f32, wg_bf, wu_bf, wd_bf, act_ref, wsem):
    v = pl.program_id(0)
    flag = vf_ref[v]
    tm = xs_ref.shape[0]

    def weight_copies(expert, slot):
        return [pltpu.make_async_copy(src.at[expert], dst.at[slot], wsem.at[slot])
                for src, dst in ((wg_hbm, wg_f32), (wu_hbm, wu_f32), (wd_hbm, wd_f32))]

    @pl.when(v == 0)
    def _():
        for cp in weight_copies(ve_ref[0], 0):
            cp.start()

    @pl.when(vn_ref[v] == 1)
    def _():
        slot = vs_ref[v]
        for cp in weight_copies(ve_ref[v], slot):
            cp.wait()

        @pl.when(vx_ref[v] >= 0)
        def _():
            for cp in weight_copies(vx_ref[v], 1 - slot):
                cp.start()

        wg_bf[...] = wg_f32[slot].astype(BF16)
        wu_bf[...] = wu_f32[slot].astype(BF16)
        wd_bf[...] = wd_f32[slot].astype(BF16)

    @pl.when(flag > 0)
    def _():
        e = ve_ref[v]
        base = vt_ref[v] * tm
        rid = lax.broadcasted_iota(jnp.int32, (tm, 1), 0) + base
        mine = (rid >= lo_ref[e]) & (rid < hi_ref[e])
        x = _unpack_bf16_pairs(xs_ref[...]).astype(BF16)
        for n in range(act_ref.shape[1] // MXU_WIDTH):
            cols = slice(n * MXU_WIDTH, (n + 1) * MXU_WIDTH)
            hg = jnp.dot(x, wg_bf[:, cols], preferred_element_type=F32)
            hu = jnp.dot(x, wu_bf[:, cols], preferred_element_type=F32)
            act_ref[:, cols] = (hg * (1.0 / (1.0 + jnp.exp(-hg))) * hu).astype(BF16)
        y = _pack_bf16_pairs(jnp.dot(act_ref[...], wd_bf[...], preferred_element_type=F32))

        @pl.when(flag == 1)
        def _():
            ys_ref[...] = jnp.where(mine, y, jnp.zeros_like(y))

        @pl.when(flag == 2)
        def _():
            ys_ref[...] = jnp.where(mine, y, ys_ref[...])


def _experts(visits, row_lo, row_hi, xs, w_gate, w_up, w_down):
    p, words = xs.shape
    d = 2 * words
    tm = EXPERT_TILE
    f = w_gate.shape[-1]
    rows_map = lambda v, vt, *_: (vt[v], 0)
    whole = pl.BlockSpec(memory_space=pl.ANY)
    grid_spec = pltpu.PrefetchScalarGridSpec(
        num_scalar_prefetch=len(visits) + 2,
        grid=(visits[0].shape[0],),
        in_specs=[pl.BlockSpec((tm, words), rows_map), whole, whole, whole],
        out_specs=pl.BlockSpec((tm, words), rows_map),
        scratch_shapes=[pltpu.VMEM((2, d, f), F32), pltpu.VMEM((2, d, f), F32), pltpu.VMEM((2, f, d), F32),
                        pltpu.VMEM((d, f), BF16), pltpu.VMEM((d, f), BF16), pltpu.VMEM((f, d), BF16),
                        pltpu.VMEM((tm, f), BF16), pltpu.SemaphoreType.DMA((2,))],
    )
    return pl.pallas_call(
        _experts_kernel,
        grid_spec=grid_spec,
        out_shape=jax.ShapeDtypeStruct(xs.shape, xs.dtype),
        compiler_params=_params(1),
        name="experts",
    )(*visits, row_lo, row_hi, xs, w_gate, w_up, w_down)


def _combine_kernel(*refs):
    dest_refs, dest_next_refs = refs[:TOP_K], refs[TOP_K:2 * TOP_K]
    h_ref, route_ref, ys_ref, o_ref, ybuf, sem = refs[2 * TOP_K:]
    i = pl.program_id(0)
    rows = h_ref.shape[0]
    slot = i % 2

    def issue(idx_refs, into):
        for r in range(rows):
            for k in range(TOP_K):
                pltpu.make_async_copy(ys_ref.at[pl.ds(idx_refs[k][r], 1), :],
                                      ybuf.at[into, k, pl.ds(r, 1), :], sem.at[into]).start(priority=k)

    @pl.when(i == 0)
    def _():
        issue(dest_refs, 0)

    for parity in range(2):
        @pl.when((i + 1 < pl.num_programs(0)) & (slot == parity))
        def _():
            issue(dest_next_refs, 1 - parity)

    for k in range(TOP_K):
        pltpu.make_async_copy(ys_ref.at[pl.ds(0, rows), :], ybuf.at[slot, k], sem.at[slot]).wait()

    route = route_ref[...]
    moe = None
    for k in range(TOP_K):
        gated = route[:, ROUTE_GATE + k:ROUTE_GATE + k + 1] * _unpack_bf16_pairs(ybuf[slot, k])
        moe = gated if moe is None else moe + gated
    o_ref[...] = h_ref[...] + moe


def _combine(dests, h, route, ys):
    t, d = h.shape
    rows = COMBINE_TILE
    n = t // rows
    this_step = pl.BlockSpec((rows,), lambda i: (i,), memory_space=pltpu.SMEM)
    next_step = pl.BlockSpec((rows,), lambda i: (jnp.minimum(i + 1, n - 1),), memory_space=pltpu.SMEM)
    return pl.pallas_call(
        _combine_kernel,
        grid=(n,),
        in_specs=[this_step] * TOP_K + [next_step] * TOP_K + [
            pl.BlockSpec((rows, d), lambda i: (i, 0)),
            pl.BlockSpec((rows, LANES), lambda i: (i, 0)),
            pl.BlockSpec(memory_space=pl.ANY),
        ],
        out_specs=pl.BlockSpec((rows, d), lambda i: (i, 0)),
        out_shape=jax.ShapeDtypeStruct((t, d), F32),
        scratch_shapes=[pltpu.VMEM((2, TOP_K, rows, ys.shape[1]), ys.dtype), pltpu.SemaphoreType.DMA((2,))],
        compiler_params=_params(1),
        name="combine",
    )(*dests, *dests, h, route, ys)


def _plan(route_t, counts_row):
    t = route_t.shape[1]
    counts = counts_row[0, :N_EXPERTS].astype(I32)
    ends = jnp.cumsum(counts)
    offs = ends - counts
    experts = route_t[ROUTE_EXPERT:ROUTE_EXPERT + TOP_K].astype(I32)
    rank = route_t[ROUTE_RANK:ROUTE_RANK + TOP_K].astype(I32)
    expert_ids = jnp.arange(N_EXPERTS, dtype=I32)[:, None, None]
    first_row = jnp.sum(jnp.where(experts[None] == expert_ids, offs[:, None, None], 0), axis=0)
    dest = first_row + rank
    dests = tuple(dest[k] for k in range(TOP_K))

    tm = EXPERT_TILE
    n_visits = (t * TOP_K) // tm + N_EXPERTS - 1
    first_tile = offs // tm
    last_tile = (ends - 1) // tm
    nvis = jnp.where(counts > 0, last_tile - first_tile + 1, 0)
    vend = jnp.cumsum(nvis)
    vstart = vend - nvis
    v = jnp.arange(n_visits, dtype=I32)
    ev = jnp.minimum(jnp.sum((vend[None, :] <= v[:, None]).astype(I32), axis=1), N_EXPERTS - 1)
    tv = jnp.take(first_tile, ev) + (v - jnp.take(vstart, ev))
    valid = v < vend[-1]
    last_v = vend[-1] - 1
    ev = jnp.where(valid, ev, jnp.take(ev, last_v))
    tv = jnp.where(valid, tv, jnp.take(tv, last_v))
    prev_t = jnp.concatenate([jnp.full((1,), -1, I32), tv[:-1]])
    prev_e = jnp.concatenate([jnp.full((1,), -1, I32), ev[:-1]])
    flag = jnp.where(valid, jnp.where(tv != prev_t, 1, 2), 0).astype(I32)
    new_expert = (ev != prev_e).astype(I32)
    slot = (jnp.cumsum(new_expert) - 1) % 2
    ids = jnp.arange(N_EXPERTS, dtype=I32)
    later = (ids[None, :] > ids[:, None]) & (counts[None, :] > 0)
    next_active = jnp.min(jnp.where(later, ids[None, :], N_EXPERTS), axis=1)
    next_active = jnp.where(next_active < N_EXPERTS, next_active, -1)
    visits = (tv.astype(I32), ev.astype(I32), flag, new_expert, slot.astype(I32),
              jnp.take(next_active, ev).astype(I32))
    return dests, visits, offs.astype(I32), ends.astype(I32)


def _rope_constants(head_dim, rope_dim):
    rope_half = rope_dim // 2
    inv_freq = ROPE_THETA ** (-jnp.arange(0, rope_dim, 2, dtype=F32) / rope_dim)
    dim = jnp.arange(LANES, dtype=I32)[None, :] % head_dim
    k = jnp.arange(4 * rope_half, dtype=I32)[:, None]
    c_mat = jnp.where(dim < rope_dim, k == dim % rope_half, k == rope_dim)
    sa_mat = (dim < rope_half) & (k == rope_half + dim)
    sb_mat = (dim >= rope_half) & (dim < rope_dim) & (k == dim)
    expand = jnp.concatenate([c_mat.astype(F32), -sa_mat.astype(F32), sb_mat.astype(F32)], axis=1)
    return inv_freq.reshape(rope_half, 1), expand.astype(BF16)


def kernel(x, positions, norm_mix_g, w_in, q_norm_g, k_norm_g, lambda_q1, lambda_k1, lambda_q2,
           lambda_k2, diff_out_norm_g, sgu_ln_g, sgu_ln_b, sgu_w, sgu_b, sgu_out_norm_g, w_out,
           norm_ffn_g, router_group_w, router_group_b, router_expert_w, router_expert_b,
           expert_w_gate, expert_w_up, expert_w_down):
    b, s, d = x.shape
    t = b * s
    depth = w_in.shape[0]
    head_dim = q_norm_g.shape[-1]
    vd = diff_out_norm_g.shape[-1]
    n_sgu_groups = sgu_w.shape[1]
    sgu_cols = n_sgu_groups * sgu_w.shape[-1]
    in_cols = w_in.shape[-1]
    att_cols = (in_cols - 2 * sgu_cols) // 3
    n_heads = att_cols // vd
    qk_cols = 2 * att_cols
    rope_dim = head_dim // ROPE_FRACTION
    rope_half = rope_dim // 2
    scale = head_dim ** -0.5

    pos_rows = positions.reshape(t // ROW_TILE, 1, ROW_TILE).astype(F32)
    freq_col, expand = _rope_constants(head_dim, rope_dim)
    blk = jnp.arange(MXU_WIDTH, dtype=I32) // head_dim
    ones_blk = (blk[:, None] == blk[None, :]).astype(BF16)
    tri = jnp.arange(ROW_TILE, dtype=I32)
    lstrict = (tri[None, :] < tri[:, None]).astype(BF16)

    h = x.reshape(t, d)
    for l in range(depth):
        lambda_init = 0.8 - 0.6 * math.exp(-0.3 * l)
        gqk = jnp.concatenate([jnp.tile(q_norm_g[l] * (scale * math.log2(math.e)), att_cols // head_dim),
                               jnp.tile(k_norm_g[l], att_cols // head_dim)]).reshape(1, qk_cols)
        qk, v, su, sv = _inproj(
            h, norm_mix_g[l].reshape(1, d), w_in[l].astype(BF16), gqk, pos_rows, freq_col, expand, ones_blk,
            sgu_ln_g[l].reshape(1, sgu_cols), sgu_ln_b[l].reshape(1, sgu_cols),
            qk_cols=qk_cols, v_cols=att_cols, sgu_cols=sgu_cols, head_dim=head_dim,
            rope_half=rope_half)

        lam_vecs = jnp.stack([lambda_q1[l], lambda_k1[l], lambda_q2[l], lambda_k2[l]]).astype(F32)
        score_bound = (1.02 * head_dim * scale * jnp.max(jnp.abs(q_norm_g[l]))
                       * jnp.max(jnp.abs(k_norm_g[l])))
        attn_args = (lam_vecs, diff_out_norm_g[l].reshape(1, vd), qk.reshape(b, s, qk_cols),
                     v.reshape(b, s, att_cols))
        attn_kw = dict(n_heads=n_heads, head_dim=head_dim, lambda_init=lambda_init)
        att = lax.cond(score_bound > MAX_UNSHIFTED_SCORE,
                       functools.partial(_attention_running_max, **attn_kw),
                       functools.partial(_attention_static, **attn_kw),
                       *attn_args).reshape(t, att_cols)

        rw = jnp.concatenate([router_group_w[l], router_expert_w[l]], axis=1)
        rw = jnp.pad(rw, ((0, 0), (0, LANES - rw.shape[1])))
        rw_hi, rw_lo = _split_bf16(rw)
        rb = jnp.pad(jnp.concatenate([router_group_b[l], router_expert_b[l]]),
                     (0, LANES - N_GROUPS - N_EXPERTS)).reshape(1, LANES)
        h, hn, route, route_t, counts_row = _mixout(
            att, su, sv, h, sgu_w[l], sgu_b[l].T, sgu_out_norm_g[l], w_out[l].astype(BF16),
            norm_ffn_g[l].reshape(1, d), jnp.concatenate([rw_hi, rw_lo], axis=1), rw_hi, rb, lstrict,
            att_cols=att_cols)

        dests, visits, row_lo, row_hi = _plan(route_t, counts_row)
        xs = _dispatch(hn, dests)
        ys = _experts(visits, row_lo, row_hi, xs, expert_w_gate[l], expert_w_up[l], expert_w_down[l])
        h = _combine(dests, h, route, ys)
    return h.reshape(b, s, d)
```

```python
import functools
import math

import jax
import jax.numpy as jnp
from jax import lax
from jax.experimental import pallas as pl
from jax.experimental.pallas import tpu as pltpu

F32 = jnp.float32
BF16 = jnp.bfloat16
I32 = jnp.int32

RMS_EPS = 1e-6
NEG_INF = -1e30
ROPE_THETA = 500000.0
MAX_UNSHIFTED_SCORE = 40.0

LANES = 128
SUBLANES = 8
MXU_WIDTH = 256
VMEM_LIMIT_BYTES = 56 * 1024 * 1024

N_GROUPS = 4
EXPERTS_PER_GROUP = 8
N_EXPERTS = N_GROUPS * EXPERTS_PER_GROUP
TOP_K = 2
ROPE_FRACTION = 4
CHUNK = 128
QBLOCK = 256
ATTN_HEADS_PER_STEP = 2
ROW_TILE = 512
EXPERT_TILE = 512
DISPATCH_TILE = 1024
DISPATCH_RING = 3
COMBINE_TILE = 512

ROUTE_EXPERT = 0
ROUTE_GATE = 2
ROUTE_RANK = 4


def _params(n_grid_dims):
    return pltpu.CompilerParams(
        dimension_semantics=("arbitrary",) * n_grid_dims,
        vmem_limit_bytes=VMEM_LIMIT_BYTES,
    )


def _gelu(x):
    return 0.5 * x * (1.0 + lax.erf(x * (1.0 / math.sqrt(2.0))))


def _split_bf16(x):
    hi = x.astype(BF16)
    return hi, (x - hi.astype(F32)).astype(BF16)


U32 = jnp.uint32
BF16_BITS = 16
HIGH_HALF = 0xFFFF0000


def _pack_bf16_pairs(x):
    n = x.shape[1] // 2
    hi = lax.bitcast_convert_type(x[:, :n].astype(BF16).astype(F32), U32)
    lo = lax.bitcast_convert_type(x[:, n:].astype(BF16).astype(F32), U32)
    return hi | (lo >> BF16_BITS)


def _unpack_bf16_pairs(w):
    hi = lax.bitcast_convert_type(w & jnp.uint32(HIGH_HALF), F32)
    lo = lax.bitcast_convert_type(w << BF16_BITS, F32)
    return jnp.concatenate([hi, lo], axis=1)


def _inproj_kernel(x_ref, g_ref, w_ref, gqk_ref, pos_ref, freq_ref, expand_ref, ones_ref, lng_ref, lnb_ref,
                   qk_ref, v_ref, su_ref, sv_ref, proj_ref, ss_ref, *, qk_cols, v_cols, sgu_cols, head_dim,
                   rope_half):
    x = x_ref[...]
    ms = jnp.mean(x * x, axis=-1, keepdims=True)
    xn = (x * lax.rsqrt(ms + RMS_EPS) * g_ref[...]).astype(BF16)
    cw = MXU_WIDTH

    ang_t = freq_ref[...] * pos_ref[0]
    cs_t = jnp.concatenate([jnp.cos(ang_t), jnp.sin(ang_t), jnp.ones_like(ang_t),
                            jnp.zeros_like(ang_t)], axis=0)
    contract0 = (((0,), (0,)), ((), ()))
    cs_hi, cs_lo = _split_bf16(cs_t)
    tab = (lax.dot_general(cs_hi, expand_ref[...], contract0, preferred_element_type=F32)
           + lax.dot_general(cs_lo, expand_ref[...], contract0, preferred_element_type=F32))
    cos = tab[:, 0:LANES]
    sa = tab[:, LANES:2 * LANES]
    sb = tab[:, 2 * LANES:3 * LANES]

    def stage(c):
        cols = slice(c * cw, (c + 1) * cw)
        proj_ref[:, cols] = jnp.dot(xn, w_ref[:, cols], preferred_element_type=F32)

    n_qk = qk_cols // cw
    for c in range(n_qk):
        stage(c)
    for c in range(n_qk):
        p = proj_ref[:, c * cw:(c + 1) * cw]
        ss_ref[:, c * cw:(c + 1) * cw] = jnp.dot((p * p).astype(BF16), ones_ref[...],
                                                 preferred_element_type=F32)
    n_v = v_cols // cw
    for c in range(n_qk + n_v, w_ref.shape[1] // cw):
        stage(c)
    for c in range(n_qk, n_qk + n_v):
        stage(c)

    def proj(col0):
        return proj_ref[:, col0:col0 + cw]

    for c in range(qk_cols // cw):
        p = proj(c * cw)
        ss = ss_ref[:, c * cw:(c + 1) * cw]
        pn = p * lax.rsqrt(ss * (1.0 / head_dim) + RMS_EPS) * gqk_ref[:, c * cw:(c + 1) * cw]
        for hh in range(cw // LANES):
            blk = pn[:, hh * LANES:(hh + 1) * LANES]
            rot = (blk * cos
                   + pltpu.roll(blk, LANES - rope_half, 1) * sa
                   + pltpu.roll(blk, rope_half, 1) * sb)
            col = c * cw + hh * LANES
            qk_ref[:, col:col + LANES] = rot.astype(BF16)

    for c in range(v_cols // cw):
        v_ref[:, c * cw:(c + 1) * cw] = proj(qk_cols + c * cw).astype(BF16)

    for c in range(sgu_cols // cw):
        su_ref[:, c * cw:(c + 1) * cw] = _gelu(proj(qk_cols + v_cols + c * cw)).astype(BF16)

    sv = jnp.concatenate(
        [_gelu(proj(qk_cols + v_cols + sgu_cols + c * cw)) for c in range(sgu_cols // cw)], axis=1)
    mu = jnp.mean(sv, axis=-1, keepdims=True)
    svc = sv - mu
    var = jnp.mean(svc * svc, axis=-1, keepdims=True)
    sv_ref[...] = (svc * lax.rsqrt(var + RMS_EPS) * lng_ref[...] + lnb_ref[...]).astype(BF16)


def _inproj(x2d, norm_g, w_in, gqk, pos_rows, freq_col, expand, ones_blk, ln_g, ln_b, *, qk_cols, v_cols,
            sgu_cols, head_dim, rope_half):
    t, d = x2d.shape
    tm = ROW_TILE
    row = lambda i: (i, 0)
    fixed = lambda i: (0, 0)
    kern = functools.partial(_inproj_kernel, qk_cols=qk_cols, v_cols=v_cols, sgu_cols=sgu_cols,
                             head_dim=head_dim, rope_half=rope_half)
    return pl.pallas_call(
        kern,
        grid=(t // tm,),
        in_specs=[
            pl.BlockSpec((tm, d), row),
            pl.BlockSpec((1, d), fixed),
            pl.BlockSpec(w_in.shape, fixed),
            pl.BlockSpec((1, qk_cols), fixed),
            pl.BlockSpec((1, 1, tm), lambda i: (i, 0, 0)),
            pl.BlockSpec(freq_col.shape, fixed),
            pl.BlockSpec(expand.shape, fixed),
            pl.BlockSpec((MXU_WIDTH, MXU_WIDTH), fixed),
            pl.BlockSpec((1, sgu_cols), fixed),
            pl.BlockSpec((1, sgu_cols), fixed),
        ],
        out_specs=[
            pl.BlockSpec((tm, qk_cols), row),
            pl.BlockSpec((tm, v_cols), row),
            pl.BlockSpec((tm, sgu_cols), row),
            pl.BlockSpec((tm, sgu_cols), row),
        ],
        out_shape=[
            jax.ShapeDtypeStruct((t, qk_cols), BF16),
            jax.ShapeDtypeStruct((t, v_cols), BF16),
            jax.ShapeDtypeStruct((t, sgu_cols), BF16),
            jax.ShapeDtypeStruct((t, sgu_cols), BF16),
        ],
        scratch_shapes=[pltpu.VMEM((tm, w_in.shape[1]), F32), pltpu.VMEM((tm, qk_cols), F32)],
        compiler_params=_params(1),
        name="inproj",
    )(x2d, norm_g, w_in, gqk, pos_rows, freq_col, expand, ones_blk, ln_g, ln_b)


def _attn_running_max_kernel(lam_ref, gout_ref, q_ref, k_ref, v_ref, o_ref, qm_ref, m_ref, acc_ref, *, tq,
                             n_heads, lambda_init, head_dim):
    qi = pl.program_id(1)
    hw = 2 * head_dim
    vd = v_ref.shape[-1] // n_heads
    lv = lam_ref[...]
    lam = (jnp.exp(jnp.sum(lv[0:1] * lv[1:2], axis=-1, keepdims=True))
           - jnp.exp(jnp.sum(lv[2:3] * lv[3:4], axis=-1, keepdims=True))
           + lambda_init)

    first = lax.broadcasted_iota(jnp.int32, (1, hw), 1) < head_dim
    for h in range(n_heads):
        q = q_ref[0, :, h * hw:(h + 1) * hw]
        zero = jnp.zeros_like(q)
        qm_ref[2 * h] = jnp.where(first, q, zero)
        qm_ref[2 * h + 1] = jnp.where(first, zero, q)
    acc_ref[...] = jnp.zeros(acc_ref.shape, F32)
    m_ref[...] = jnp.full(m_ref.shape, NEG_INF, F32)

    ones = jnp.ones((tq, vd), BF16)
    causal = (lax.broadcasted_iota(jnp.int32, (tq, tq), 1)
              <= lax.broadcasted_iota(jnp.int32, (tq, tq), 0))

    def kv_step(j, masked):
        start = pl.multiple_of(j * tq, tq)
        for h in range(n_heads):
            kc = k_ref[0, pl.ds(start, tq), h * hw:(h + 1) * hw]
            vext = jnp.concatenate([v_ref[0, pl.ds(start, tq), h * vd:(h + 1) * vd], ones], axis=1)
            for mp in range(2):
                c = 2 * h + mp
                s = lax.dot_general(qm_ref[c], kc, (((1,), (1,)), ((), ())), preferred_element_type=F32)
                if masked:
                    s = jnp.where(causal, s, NEG_INF)
                m_old = m_ref[c]
                m_new = jnp.maximum(m_old, jnp.max(s, axis=-1, keepdims=True))
                p = jnp.exp2(s - m_new).astype(BF16)
                acc_ref[c] = (jnp.exp2(m_old - m_new) * acc_ref[c]
                              + jnp.dot(p, vext, preferred_element_type=F32))
                m_ref[c] = m_new

    def full_tile(j, carry):
        kv_step(j, False)
        return carry

    lax.fori_loop(0, qi, full_tile, 0)
    kv_step(qi, True)

    for h in range(n_heads):
        a1 = acc_ref[2 * h]
        a2 = acc_ref[2 * h + 1]
        att = a1[:, :vd] / a1[:, vd:] - lam * (a2[:, :vd] / a2[:, vd:])
        ms = jnp.mean(att * att, axis=-1, keepdims=True)
        o_ref[0, :, h * vd:(h + 1) * vd] = (
            att * lax.rsqrt(ms + RMS_EPS) * gout_ref[...] * (1.0 - lambda_init)).astype(BF16)


def _attn_static_kernel(lam_ref, gout_ref, q_ref, k_ref, v_ref, o_ref, vext_ref, *, tq, heads, lambda_init,
                        head_dim):
    s_len = q_ref.shape[1]
    hw = 2 * head_dim
    vd = v_ref.shape[-1] // heads
    lv = lam_ref[...]
    lam = (jnp.exp(jnp.sum(lv[0:1] * lv[1:2], axis=-1, keepdims=True))
           - jnp.exp(jnp.sum(lv[2:3] * lv[3:4], axis=-1, keepdims=True))
           + lambda_init)
    first = lax.broadcasted_iota(jnp.int32, (1, hw), 1) < head_dim
    causal = (lax.broadcasted_iota(jnp.int32, (tq, tq), 1)
              <= lax.broadcasted_iota(jnp.int32, (tq, tq), 0))
    nt = (((1,), (1,)), ((), ()))

    for h in range(heads):
        vext_ref[h, :, 0:vd] = v_ref[0, :, h * vd:(h + 1) * vd]
        vext_ref[h, :, vd:2 * vd] = jnp.ones((s_len, vd), BF16)

    for h in range(heads):
        kcols = slice(h * hw, (h + 1) * hw)
        for qi in range(s_len // tq):
            lo = qi * tq
            q = q_ref[0, lo:lo + tq, kcols]
            zero = jnp.zeros_like(q)
            acc = []
            for mp in range(2):
                qm = jnp.where(first, q, zero) if mp == 0 else jnp.where(first, zero, q)
                sd = lax.dot_general(qm, k_ref[0, lo:lo + tq, kcols], nt, preferred_element_type=F32)
                pd = jnp.exp2(jnp.where(causal, sd, NEG_INF)).astype(BF16)
                pv = jnp.dot(pd, vext_ref[h, lo:lo + tq, :], preferred_element_type=F32)
                if qi > 0:
                    s = lax.dot_general(qm, k_ref[0, 0:lo, kcols], nt, preferred_element_type=F32)
                    pv = pv + jnp.dot(jnp.exp2(s).astype(BF16), vext_ref[h, 0:lo, :],
                                      preferred_element_type=F32)
                acc.append(pv)
            a1, a2 = acc
            att = a1[:, :vd] / a1[:, vd:] - lam * (a2[:, :vd] / a2[:, vd:])
            ms = jnp.mean(att * att, axis=-1, keepdims=True)
            o_ref[0, lo:lo + tq, h * vd:(h + 1) * vd] = (
                att * lax.rsqrt(ms + RMS_EPS) * gout_ref[...] * (1.0 - lambda_init)).astype(BF16)


def _attention_static(lam_vecs, gout, qk3, v3, *, n_heads, head_dim, lambda_init):
    b, s, _ = qk3.shape
    vd = v3.shape[-1] // n_heads
    heads = ATTN_HEADS_PER_STEP
    groups = n_heads // heads
    qw = heads * 2 * head_dim
    kern = functools.partial(_attn_static_kernel, tq=QBLOCK, heads=heads, lambda_init=lambda_init,
                             head_dim=head_dim)
    return pl.pallas_call(
        kern,
        grid=(b, groups),
        in_specs=[
            pl.BlockSpec(lam_vecs.shape, lambda bi, g: (0, 0)),
            pl.BlockSpec((1, vd), lambda bi, g: (0, 0)),
            pl.BlockSpec((1, s, qw), lambda bi, g: (bi, 0, g)),
            pl.BlockSpec((1, s, qw), lambda bi, g: (bi, 0, groups + g)),
            pl.BlockSpec((1, s, heads * vd), lambda bi, g: (bi, 0, g)),
        ],
        out_specs=pl.BlockSpec((1, s, heads * vd), lambda bi, g: (bi, 0, g)),
        out_shape=jax.ShapeDtypeStruct(v3.shape, BF16),
        scratch_shapes=[pltpu.VMEM((heads, s, 2 * vd), BF16)],
        compiler_params=_params(2),
        name="diff_attn_static",
    )(lam_vecs, gout, qk3, qk3, v3)


def _attention_running_max(lam_vecs, gout, qk3, v3, *, n_heads, head_dim, lambda_init):
    b, s, _ = qk3.shape
    att_cols = v3.shape[-1]
    vd = att_cols // n_heads
    tq = QBLOCK
    kern = functools.partial(_attn_running_max_kernel, tq=tq, n_heads=n_heads, lambda_init=lambda_init,
                             head_dim=head_dim)
    return pl.pallas_call(
        kern,
        grid=(b, s // tq),
        in_specs=[
            pl.BlockSpec(lam_vecs.shape, lambda bi, qi: (0, 0)),
            pl.BlockSpec((1, vd), lambda bi, qi: (0, 0)),
            pl.BlockSpec((1, tq, att_cols), lambda bi, qi: (bi, qi, 0)),
            pl.BlockSpec((1, s, att_cols), lambda bi, qi: (bi, 0, 1)),
            pl.BlockSpec((1, s, att_cols), lambda bi, qi: (bi, 0, 0)),
        ],
        out_specs=pl.BlockSpec((1, tq, att_cols), lambda bi, qi: (bi, qi, 0)),
        out_shape=jax.ShapeDtypeStruct(v3.shape, BF16),
        scratch_shapes=[
            pltpu.VMEM((2 * n_heads, tq, 2 * head_dim), BF16),
            pltpu.VMEM((2 * n_heads, tq, 1), F32),
            pltpu.VMEM((2 * n_heads, tq, 2 * vd), F32),
        ],
        compiler_params=_params(2),
        name="diff_attn_running_max",
    )(lam_vecs, gout, qk3, qk3, v3)


def _mixout_kernel(att_ref, su_ref, sv_ref, x_ref, sw_ref, sbt_ref, sgn_ref, wo_ref, gffn_ref,
                   rwc_ref, rwh_ref, rb_ref, lstrict_ref, h_ref, hn_ref, route_ref, route_t_ref, counts_ref,
                   cnt_ref, h_prev_ref, acc_ref, *, n_sgu_groups, att_cols):
    step = pl.program_id(0)
    tm = x_ref.shape[0]
    gw = CHUNK
    r_id = lax.broadcasted_iota(jnp.int32, (gw, gw), 0)
    c_id = lax.broadcasted_iota(jnp.int32, (gw, gw), 1)
    tril = c_id <= r_id

    @pl.when(step == 0)
    def _():
        cnt_ref[...] = jnp.zeros_like(cnt_ref)
        h_prev_ref[...] = jnp.zeros_like(h_prev_ref)

    h_prev = h_prev_ref[...]
    ms_prev = jnp.mean(h_prev * h_prev, axis=-1, keepdims=True)
    hn = h_prev * lax.rsqrt(ms_prev + RMS_EPS) * gffn_ref[...]
    hn_ref[...] = _pack_bf16_pairs(hn)
    _route_tile(hn, (step > 0).astype(F32), rwc_ref, rwh_ref, rb_ref, lstrict_ref, route_ref, route_t_ref,
                counts_ref, cnt_ref)

    acc_ref[...] = jnp.dot(att_ref[...], wo_ref[0:att_cols, :], preferred_element_type=F32)

    sg_cols = []
    for g in range(n_sgu_groups):
        w = jnp.where(tril, sw_ref[g], 0.0).astype(BF16)
        bias = sbt_ref[:, g:g + 1]
        gn = sgn_ref[g:g + 1, :]
        rows = []
        for c in range(tm // gw):
            vblk = sv_ref[c * gw:(c + 1) * gw, g * gw:(g + 1) * gw]
            s = jnp.dot(w, vblk, preferred_element_type=F32) + bias
            sg = su_ref[c * gw:(c + 1) * gw, g * gw:(g + 1) * gw].astype(F32) * s
            ms = jnp.mean(sg * sg, axis=-1, keepdims=True)
            rows.append((sg * lax.rsqrt(ms + RMS_EPS) * gn).astype(BF16))
        sg_cols.append(jnp.concatenate(rows, axis=0))
    sgn = jnp.concatenate(sg_cols, axis=1)
    acc = acc_ref[...] + jnp.dot(sgn, wo_ref[att_cols:, :], preferred_element_type=F32)

    h = x_ref[...] + acc
    h_ref[...] = h
    h_prev_ref[...] = h


def _route_tile(hn, live, rwc_ref, rwh_ref, rb_ref, lstrict_ref, route_ref, route_t_ref, counts_ref, cnt_ref):
    hn_bf, hn_lo = _split_bf16(hn)
    both = jnp.dot(hn_bf, rwc_ref[...], preferred_element_type=F32)
    logits = (both[:, :LANES] + both[:, LANES:]
              + jnp.dot(hn_lo, rwh_ref[...], preferred_element_type=F32)
              + rb_ref[...])

    lane = lax.broadcasted_iota(jnp.int32, logits.shape, 1).astype(F32)
    big = float(LANES)
    is_grp = lane < N_GROUPS
    gl = jnp.where(is_grp, logits, NEG_INF)
    gmax = jnp.max(gl, axis=-1, keepdims=True)
    gidx = jnp.min(jnp.where(gl == gmax, lane, big), axis=-1, keepdims=True)
    psum = jnp.sum(jnp.where(is_grp, jnp.exp(gl - gmax), 0.0), axis=-1, keepdims=True)
    p_grp = 1.0 / psum
    e_lo = N_GROUPS + gidx * EXPERTS_PER_GROUP
    in_grp = (lane >= e_lo) & (lane < e_lo + EXPERTS_PER_GROUP)
    el = jnp.where(in_grp, logits, NEG_INF)
    t1 = jnp.max(el, axis=-1, keepdims=True)
    i1 = jnp.min(jnp.where(el == t1, lane, big), axis=-1, keepdims=True)
    el2 = jnp.where(lane == i1, NEG_INF, el)
    t2 = jnp.max(el2, axis=-1, keepdims=True)
    i2 = jnp.min(jnp.where(el2 == t2, lane, big), axis=-1, keepdims=True)
    e21 = jnp.exp(t2 - t1)
    den = 1.0 + e21
    g1 = p_grp / den
    g2 = p_grp * (e21 / den)
    e1 = i1 - N_GROUPS
    e2 = i2 - N_GROUPS

    hit1 = lane == e1
    hit2 = lane == e2
    onehot = jnp.where(hit1 | hit2, live, 0.0)
    before = jnp.dot(lstrict_ref[...], onehot.astype(BF16), preferred_element_type=F32) + cnt_ref[...]
    r1 = jnp.sum(jnp.where(hit1, before, 0.0), axis=-1, keepdims=True)
    r2 = jnp.sum(jnp.where(hit2, before, 0.0), axis=-1, keepdims=True)
    cnt_ref[...] = cnt_ref[...] + jnp.sum(onehot, axis=0, keepdims=True)
    counts_ref[...] = cnt_ref[...]

    route = jnp.zeros_like(logits)
    for k, val in enumerate((e1, e2, g1, g2, r1, r2)):
        route = jnp.where(lane == k, val, route)
    route_ref[...] = route
    route_t_ref[...] = route.T[0:SUBLANES, :]


def _mixout(att, su, sv, x2d, sgu_w, sgu_bt, sgu_gn, w_out, g_ffn, rw_cat, rw_hi, rb, lstrict, *,
            att_cols):
    t, d = x2d.shape
    tm = ROW_TILE
    n = t // tm
    row = lambda i: (jnp.minimum(i, n - 1), 0)
    routed = lambda i: (jnp.maximum(i - 1, 0), 0)
    fixed2 = lambda i: (0, 0)
    n_groups = sgu_w.shape[0]
    kern = functools.partial(_mixout_kernel, n_sgu_groups=n_groups, att_cols=att_cols)
    return pl.pallas_call(
        kern,
        grid=(n + 1,),
        in_specs=[
            pl.BlockSpec((tm, att.shape[1]), row),
            pl.BlockSpec((tm, su.shape[1]), row),
            pl.BlockSpec((tm, sv.shape[1]), row),
            pl.BlockSpec((tm, d), row),
            pl.BlockSpec(sgu_w.shape, lambda i: (0, 0, 0)),
            pl.BlockSpec(sgu_bt.shape, fixed2),
            pl.BlockSpec(sgu_gn.shape, fixed2),
            pl.BlockSpec(w_out.shape, fixed2),
            pl.BlockSpec((1, d), fixed2),
            pl.BlockSpec(rw_cat.shape, fixed2),
            pl.BlockSpec(rw_hi.shape, fixed2),
            pl.BlockSpec((1, LANES), fixed2),
            pl.BlockSpec((tm, tm), fixed2),
        ],
        out_specs=[
            pl.BlockSpec((tm, d), row),
            pl.BlockSpec((tm, d // 2), routed),
            pl.BlockSpec((tm, LANES), routed),
            pl.BlockSpec((SUBLANES, tm), lambda i: (0, jnp.maximum(i - 1, 0))),
            pl.BlockSpec((1, LANES), fixed2),
        ],
        out_shape=[
            jax.ShapeDtypeStruct((t, d), F32),
            jax.ShapeDtypeStruct((t, d // 2), U32),
            jax.ShapeDtypeStruct((t, LANES), F32),
            jax.ShapeDtypeStruct((SUBLANES, t), F32),
            jax.ShapeDtypeStruct((1, LANES), F32),
        ],
        scratch_shapes=[pltpu.VMEM((1, LANES), F32), pltpu.VMEM((tm, d), F32), pltpu.VMEM((tm, d), F32)],
        compiler_params=_params(1),
        name="mixout",
    )(att, su, sv, x2d, sgu_w, sgu_bt, sgu_gn, w_out, g_ffn, rw_cat, rw_hi, rb, lstrict)


def _dispatch_kernel(*refs):
    dest_refs, (hn_hbm, xs_ref, ring, in_sem, out_sem) = refs[:TOP_K], refs[TOP_K:]
    i = pl.program_id(0)
    n = pl.num_programs(0)
    n_slots, rows, _ = ring.shape
    parity = i % 2

    def tile_fetch(step, slot):
        return pltpu.make_async_copy(hn_hbm.at[pl.ds(step * rows, rows), :], ring.at[slot], in_sem.at[slot])

    def rows_done(which):
        for _ in range(TOP_K):
            pltpu.make_async_copy(ring.at[0], xs_ref.at[pl.ds(0, rows), :], out_sem.at[which]).wait()

    @pl.when(i == 0)
    def _():
        tile_fetch(0, 0).start()

    for slot in range(n_slots):
        @pl.when(i % n_slots == slot)
        def _():
            @pl.when(i + 1 < n)
            def _():
                tile_fetch(i + 1, (slot + 1) % n_slots).start()

            tile_fetch(i, slot).wait()
            for r in range(rows):
                for k in range(TOP_K):
                    pltpu.make_async_copy(ring.at[slot, pl.ds(r, 1), :],
                                          xs_ref.at[pl.ds(dest_refs[k][r], 1), :],
                                          out_sem.at[parity]).start(priority=k)

    @pl.when(i > 0)
    def _():
        rows_done(1 - parity)

    @pl.when(i == n - 1)
    def _():
        rows_done(parity)


def _dispatch(hn, dests):
    t, d = hn.shape
    rows = DISPATCH_TILE
    index_spec = pl.BlockSpec((rows,), lambda i: (i,), memory_space=pltpu.SMEM)
    return pl.pallas_call(
        _dispatch_kernel,
        grid=(t // rows,),
        in_specs=[index_spec] * TOP_K + [pl.BlockSpec(memory_space=pl.ANY)],
        out_specs=pl.BlockSpec(memory_space=pl.ANY),
        out_shape=jax.ShapeDtypeStruct((t * TOP_K, d), hn.dtype),
        scratch_shapes=[pltpu.VMEM((DISPATCH_RING, rows, d), hn.dtype),
                        pltpu.SemaphoreType.DMA((DISPATCH_RING,)), pltpu.SemaphoreType.DMA((2,))],
        compiler_params=_params(1),
        name="dispatch",
    )(*dests, hn)


def _experts_kernel(vt_ref, ve_ref, vf_ref, vn_ref, vs_ref, vx_ref, lo_ref, hi_ref, xs_ref, wg_hbm, wu_hbm,
                    wd_hbm, ys_ref, wg_f32, wu_f32, wd_f32, wg_bf, wu_bf, wd_bf, act_ref, wsem):
    v = pl.program_id(0)
    flag = vf_ref[v]
    tm = xs_ref.shape[0]

    def weight_copies(expert, slot):
        return [pltpu.make_async_copy(src.at[expert], dst.at[slot], wsem.at[slot])
                for src, dst in ((wg_hbm, wg_f32), (wu_hbm, wu_f32), (wd_hbm, wd_f32))]

    @pl.when(v == 0)
    def _():
        for cp in weight_copies(ve_ref[0], 0):
            cp.start()

    @pl.when(vn_ref[v] == 1)
    def _():
        slot = vs_ref[v]
        for cp in weight_copies(ve_ref[v], slot):
            cp.wait()

        @pl.when(vx_ref[v] >= 0)
        def _():
            for cp in weight_copies(vx_ref[v], 1 - slot):
                cp.start()

        wg_bf[...] = wg_f32[slot].astype(BF16)
        wu_bf[...] = wu_f32[slot].astype(BF16)
        wd_bf[...] = wd_f32[slot].astype(BF16)

    @pl.when(flag > 0)
    def _():
        e = ve_ref[v]
        base = vt_ref[v] * tm
        rid = lax.broadcasted_iota(jnp.int32, (tm, 1), 0) + base
        mine = (rid >= lo_ref[e]) & (rid < hi_ref[e])
        x = _unpack_bf16_pairs(xs_ref[...]).astype(BF16)
        for n in range(act_ref.shape[1] // MXU_WIDTH):
            cols = slice(n * MXU_WIDTH, (n + 1) * MXU_WIDTH)
            hg = jnp.dot(x, wg_bf[:, cols], preferred_element_type=F32)
            hu = jnp.dot(x, wu_bf[:, cols], preferred_element_type=F32)
            act_ref[:, cols] = (hg * (1.0 / (1.0 + jnp.exp(-hg))) * hu).astype(BF16)
        y = _pack_bf16_pairs(jnp.dot(act_ref[...], wd_bf[...], preferred_element_type=F32))

        @pl.when(flag == 1)
        def _():
            ys_ref[...] = jnp.where(mine, y, jnp.zeros_like(y))

        @pl.when(flag == 2)
        def _():
            ys_ref[...] = jnp.where(mine, y, ys_ref[...])


def _experts(visits, row_lo, row_hi, xs, w_gate, w_up, w_down):
    p, words = xs.shape
    d = 2 * words
    tm = EXPERT_TILE
    f = w_gate.shape[-1]
    rows_map = lambda v, vt, *_: (vt[v], 0)
    whole = pl.BlockSpec(memory_space=pl.ANY)
    grid_spec = pltpu.PrefetchScalarGridSpec(
        num_scalar_prefetch=len(visits) + 2,
        grid=(visits[0].shape[0],),
        in_specs=[pl.BlockSpec((tm, words), rows_map), whole, whole, whole],
        out_specs=pl.BlockSpec((tm, words), rows_map),
        scratch_shapes=[pltpu.VMEM((2, d, f), F32), pltpu.VMEM((2, d, f), F32), pltpu.VMEM((2, f, d), F32),
                        pltpu.VMEM((d, f), BF16), pltpu.VMEM((d, f), BF16), pltpu.VMEM((f, d), BF16),
                        pltpu.VMEM((tm, f), BF16), pltpu.SemaphoreType.DMA((2,))],
    )
    return pl.pallas_call(
        _experts_kernel,
        grid_spec=grid_spec,
        out_shape=jax.ShapeDtypeStruct(xs.shape, xs.dtype),
        compiler_params=_params(1),
        name="experts",
    )(*visits, row_lo, row_hi, xs, w_gate, w_up, w_down)


def _combine_kernel(*refs):
    dest_refs, dest_next_refs = refs[:TOP_K], refs[TOP_K:2 * TOP_K]
    h_ref, route_ref, ys_ref, o_ref, ybuf, sem = refs[2 * TOP_K:]
    i = pl.program_id(0)
    rows = h_ref.shape[0]
    slot = i % 2

    def issue(idx_refs, into):
        for r in range(rows):
            for k in range(TOP_K):
                pltpu.make_async_copy(ys_ref.at[pl.ds(idx_refs[k][r], 1), :],
                                      ybuf.at[into, k, pl.ds(r, 1), :], sem.at[into]).start(priority=k)

    @pl.when(i == 0)
    def _():
        issue(dest_refs, 0)

    for parity in range(2):
        @pl.when((i + 1 < pl.num_programs(0)) & (slot == parity))
        def _():
            issue(dest_next_refs, 1 - parity)

    for k in range(TOP_K):
        pltpu.make_async_copy(ys_ref.at[pl.ds(0, rows), :], ybuf.at[slot, k], sem.at[slot]).wait()

    route = route_ref[...]
    moe = None
    for k in range(TOP_K):
        gated = route[:, ROUTE_GATE + k:ROUTE_GATE + k + 1] * _unpack_bf16_pairs(ybuf[slot, k])
        moe = gated if moe is None else moe + gated
    o_ref[...] = h_ref[...] + moe


def _combine(dests, h, route, ys):
    t, d = h.shape
    rows = COMBINE_TILE
    n = t // rows
    this_step = pl.BlockSpec((rows,), lambda i: (i,), memory_space=pltpu.SMEM)
    next_step = pl.BlockSpec((rows,), lambda i: (jnp.minimum(i + 1, n - 1),), memory_space=pltpu.SMEM)
    return pl.pallas_call(
        _combine_kernel,
        grid=(n,),
        in_specs=[this_step] * TOP_K + [next_step] * TOP_K + [
            pl.BlockSpec((rows, d), lambda i: (i, 0)),
            pl.BlockSpec((rows, LANES), lambda i: (i, 0)),
            pl.BlockSpec(memory_space=pl.ANY),
        ],
        out_specs=pl.BlockSpec((rows, d), lambda i: (i, 0)),
        out_shape=jax.ShapeDtypeStruct((t, d), F32),
        scratch_shapes=[pltpu.VMEM((2, TOP_K, rows, ys.shape[1]), ys.dtype), pltpu.SemaphoreType.DMA((2,))],
        compiler_params=_params(1),
        name="combine",
    )(*dests, *dests, h, route, ys)


def _plan(route_t, counts_row):
    t = route_t.shape[1]
    counts = counts_row[0, :N_EXPERTS].astype(I32)
    ends = jnp.cumsum(counts)
    offs = ends - counts
    experts = route_t[ROUTE_EXPERT:ROUTE_EXPERT + TOP_K].astype(I32)
    rank = route_t[ROUTE_RANK:ROUTE_RANK + TOP_K].astype(I32)
    expert_ids = jnp.arange(N_EXPERTS, dtype=I32)[:, None, None]
    first_row = jnp.sum(jnp.where(experts[None] == expert_ids, offs[:, None, None], 0), axis=0)
    dest = first_row + rank
    dests = tuple(dest[k] for k in range(TOP_K))

    tm = EXPERT_TILE
    n_visits = (t * TOP_K) // tm + N_EXPERTS - 1
    first_tile = offs // tm
    last_tile = (ends - 1) // tm
    nvis = jnp.where(counts > 0, last_tile - first_tile + 1, 0)
    vend = jnp.cumsum(nvis)
    vstart = vend - nvis
    v = jnp.arange(n_visits, dtype=I32)
    ev = jnp.minimum(jnp.sum((vend[None, :] <= v[:, None]).astype(I32), axis=1), N_EXPERTS - 1)
    tv = jnp.take(first_tile, ev) + (v - jnp.take(vstart, ev))
    valid = v < vend[-1]
    last_v = vend[-1] - 1
    ev = jnp.where(valid, ev, jnp.take(ev, last_v))
    tv = jnp.where(valid, tv, jnp.take(tv, last_v))
    prev_t = jnp.concatenate([jnp.full((1,), -1, I32), tv[:-1]])
    prev_e = jnp.concatenate([jnp.full((1,), -1, I32), ev[:-1]])
    flag = jnp.where(valid, jnp.where(tv != prev_t, 1, 2), 0).astype(I32)
    new_expert = (ev != prev_e).astype(I32)
    slot = (jnp.cumsum(new_expert) - 1) % 2
    ids = jnp.arange(N_EXPERTS, dtype=I32)
    later = (ids[None, :] > ids[:, None]) & (counts[None, :] > 0)
    next_active = jnp.min(jnp.where(later, ids[None, :], N_EXPERTS), axis=1)
    next_active = jnp.where(next_active < N_EXPERTS, next_active, -1)
    visits = (tv.astype(I32), ev.astype(I32), flag, new_expert, slot.astype(I32),
              jnp.take(next_active, ev).astype(I32))
    return dests, visits, offs.astype(I32), ends.astype(I32)


def _rope_constants(head_dim, rope_dim):
    rope_half = rope_dim // 2
    inv_freq = ROPE_THETA ** (-jnp.arange(0, rope_dim, 2, dtype=F32) / rope_dim)
    dim = jnp.arange(LANES, dtype=I32)[None, :] % head_dim
    k = jnp.arange(4 * rope_half, dtype=I32)[:, None]
    c_mat = jnp.where(dim < rope_dim, k == dim % rope_half, k == rope_dim)
    sa_mat = (dim < rope_half) & (k == rope_half + dim)
    sb_mat = (dim >= rope_half) & (dim < rope_dim) & (k == dim)
    expand = jnp.concatenate([c_mat.astype(F32), -sa_mat.astype(F32), sb_mat.astype(F32)], axis=1)
    return inv_freq.reshape(rope_half, 1), expand.astype(BF16)


def kernel(x, positions, norm_mix_g, w_in, q_norm_g, k_norm_g, lambda_q1, lambda_k1, lambda_q2,
           lambda_k2, diff_out_norm_g, sgu_ln_g, sgu_ln_b, sgu_w, sgu_b, sgu_out_norm_g, w_out,
           norm_ffn_g, router_group_w, router_group_b, router_expert_w, router_expert_b,
           expert_w_gate, expert_w_up, expert_w_down):
    b, s, d = x.shape
    t = b * s
    depth = w_in.shape[0]
    head_dim = q_norm_g.shape[-1]
    vd = diff_out_norm_g.shape[-1]
    n_sgu_groups = sgu_w.shape[1]
    sgu_cols = n_sgu_groups * sgu_w.shape[-1]
    in_cols = w_in.shape[-1]
    att_cols = (in_cols - 2 * sgu_cols) // 3
    n_heads = att_cols // vd
    qk_cols = 2 * att_cols
    rope_dim = head_dim // ROPE_FRACTION
    rope_half = rope_dim // 2
    scale = head_dim ** -0.5

    pos_rows = positions.reshape(t // ROW_TILE, 1, ROW_TILE).astype(F32)
    freq_col, expand = _rope_constants(head_dim, rope_dim)
    blk = jnp.arange(MXU_WIDTH, dtype=I32) // head_dim
    ones_blk = (blk[:, None] == blk[None, :]).astype(BF16)
    tri = jnp.arange(ROW_TILE, dtype=I32)
    lstrict = (tri[None, :] < tri[:, None]).astype(BF16)

    h = x.reshape(t, d)
    for l in range(depth):
        lambda_init = 0.8 - 0.6 * math.exp(-0.3 * l)
        gqk = jnp.concatenate([jnp.tile(q_norm_g[l] * (scale * math.log2(math.e)), att_cols // head_dim),
                               jnp.tile(k_norm_g[l], att_cols // head_dim)]).reshape(1, qk_cols)
        qk, v, su, sv = _inproj(
            h, norm_mix_g[l].reshape(1, d), w_in[l].astype(BF16), gqk, pos_rows, freq_col, expand, ones_blk,
            sgu_ln_g[l].reshape(1, sgu_cols), sgu_ln_b[l].reshape(1, sgu_cols),
            qk_cols=qk_cols, v_cols=att_cols, sgu_cols=sgu_cols, head_dim=head_dim,
            rope_half=rope_half)

        lam_vecs = jnp.stack([lambda_q1[l], lambda_k1[l], lambda_q2[l], lambda_k2[l]]).astype(F32)
        score_bound = (1.02 * head_dim * scale * jnp.max(jnp.abs(q_norm_g[l]))
                       * jnp.max(jnp.abs(k_norm_g[l])))
        attn_args = (lam_vecs, diff_out_norm_g[l].reshape(1, vd), qk.reshape(b, s, qk_cols),
                     v.reshape(b, s, att_cols))
        attn_kw = dict(n_heads=n_heads, head_dim=head_dim, lambda_init=lambda_init)
        att = lax.cond(score_bound > MAX_UNSHIFTED_SCORE,
                       functools.partial(_attention_running_max, **attn_kw),
                       functools.partial(_attention_static, **attn_kw),
                       *attn_args).reshape(t, att_cols)

        rw = jnp.concatenate([router_group_w[l], router_expert_w[l]], axis=1)
        rw = jnp.pad(rw, ((0, 0), (0, LANES - rw.shape[1])))
        rw_hi, rw_lo = _split_bf16(rw)
        rb = jnp.pad(jnp.concatenate([router_group_b[l], router_expert_b[l]]),
                     (0, LANES - N_GROUPS - N_EXPERTS)).reshape(1, LANES)
        h, hn, route, route_t, counts_row = _mixout(
            att, su, sv, h, sgu_w[l], sgu_b[l].T, sgu_out_norm_g[l], w_out[l].astype(BF16),
            norm_ffn_g[l].reshape(1, d), jnp.concatenate([rw_hi, rw_lo], axis=1), rw_hi, rb, lstrict,
            att_cols=att_cols)

        dests, visits, row_lo, row_hi = _plan(route_t, counts_row)
        xs = _dispatch(hn, dests)
        ys = _experts(visits, row_lo, row_hi, xs, expert_w_gate[l], expert_w_up[l], expert_w_down[l])
        h = _combine(dests, h, route, ys)
    return h.reshape(b, s, d)
```

```python
import functools
import math

import jax
import jax.numpy as jnp
from jax import lax
from jax.experimental import pallas as pl
from jax.experimental.pallas import tpu as pltpu

F32 = jnp.float32
BF16 = jnp.bfloat16
I32 = jnp.int32

RMS_EPS = 1e-6
NEG_INF = -1e30
ROPE_THETA = 500000.0
MAX_UNSHIFTED_SCORE = 40.0

LANES = 128
SUBLANES = 8
MXU_WIDTH = 256
VMEM_LIMIT_BYTES = 56 * 1024 * 1024

N_GROUPS = 4
EXPERTS_PER_GROUP = 8
N_EXPERTS = N_GROUPS * EXPERTS_PER_GROUP
TOP_K = 2
ROPE_FRACTION = 4
CHUNK = 128
QBLOCK = 256
ATTN_HEADS_PER_STEP = 2
ROW_TILE = 512
EXPERT_TILE = 512
DISPATCH_TILE = 1024
DISPATCH_RING = 3
COMBINE_TILE = 512

ROUTE_EXPERT = 0
ROUTE_GATE = 2
ROUTE_RANK = 4


def _params(n_grid_dims):
    return pltpu.CompilerParams(
        dimension_semantics=("arbitrary",) * n_grid_dims,
        vmem_limit_bytes=VMEM_LIMIT_BYTES,
    )


def _gelu(x):
    return 0.5 * x * (1.0 + lax.erf(x * (1.0 / math.sqrt(2.0))))


def _split_bf16(x):
    hi = x.astype(BF16)
    return hi, (x - hi.astype(F32)).astype(BF16)


U32 = jnp.uint32
BF16_BITS = 16
HIGH_HALF = 0xFFFF0000


def _pack_bf16_pairs(x):
    n = x.shape[1] // 2
    hi = lax.bitcast_convert_type(x[:, :n].astype(BF16).astype(F32), U32)
    lo = lax.bitcast_convert_type(x[:, n:].astype(BF16).astype(F32), U32)
    return hi | (lo >> BF16_BITS)


def _unpack_bf16_pairs(w):
    hi = lax.bitcast_convert_type(w & jnp.uint32(HIGH_HALF), F32)
    lo = lax.bitcast_convert_type(w << BF16_BITS, F32)
    return jnp.concatenate([hi, lo], axis=1)


def _inproj_kernel(x_ref, g_ref, w_ref, gqk_ref, pos_ref, freq_ref, expand_ref, ones_ref, lng_ref, lnb_ref,
                   qk_ref, v_ref, su_ref, sv_ref, proj_ref, ss_ref, *, qk_cols, v_cols, sgu_cols, head_dim,
                   rope_half):
    x = x_ref[...]
    ms = jnp.mean(x * x, axis=-1, keepdims=True)
    xn = (x * lax.rsqrt(ms + RMS_EPS) * g_ref[...]).astype(BF16)
    cw = MXU_WIDTH

    ang_t = freq_ref[...] * pos_ref[0]
    cs_t = jnp.concatenate([jnp.cos(ang_t), jnp.sin(ang_t), jnp.ones_like(ang_t),
                            jnp.zeros_like(ang_t)], axis=0)
    contract0 = (((0,), (0,)), ((), ()))
    cs_hi, cs_lo = _split_bf16(cs_t)
    tab = (lax.dot_general(cs_hi, expand_ref[...], contract0, preferred_element_type=F32)
           + lax.dot_general(cs_lo, expand_ref[...], contract0, preferred_element_type=F32))
    cos = tab[:, 0:LANES]
    sa = tab[:, LANES:2 * LANES]
    sb = tab[:, 2 * LANES:3 * LANES]

    def stage(c):
        cols = slice(c * cw, (c + 1) * cw)
        proj_ref[:, cols] = jnp.dot(xn, w_ref[:, cols], preferred_element_type=F32)

    n_qk = qk_cols // cw
    for c in range(n_qk):
        stage(c)
    for c in range(n_qk):
        p = proj_ref[:, c * cw:(c + 1) * cw]
        ss_ref[:, c * cw:(c + 1) * cw] = jnp.dot((p * p).astype(BF16), ones_ref[...],
                                                 preferred_element_type=F32)
    n_v = v_cols // cw
    for c in range(n_qk + n_v, w_ref.shape[1] // cw):
        stage(c)
    for c in range(n_qk, n_qk + n_v):
        stage(c)

    def proj(col0):
        return proj_ref[:, col0:col0 + cw]

    for c in range(qk_cols // cw):
        p = proj(c * cw)
        ss = ss_ref[:, c * cw:(c + 1) * cw]
        pn = p * lax.rsqrt(ss * (1.0 / head_dim) + RMS_EPS) * gqk_ref[:, c * cw:(c + 1) * cw]
        for hh in range(cw // LANES):
            blk = pn[:, hh * LANES:(hh + 1) * LANES]
            rot = (blk * cos
                   + pltpu.roll(blk, LANES - rope_half, 1) * sa
                   + pltpu.roll(blk, rope_half, 1) * sb)
            col = c * cw + hh * LANES
            qk_ref[:, col:col + LANES] = rot.astype(BF16)

    for c in range(v_cols // cw):
        v_ref[:, c * cw:(c + 1) * cw] = proj(qk_cols + c * cw).astype(BF16)

    for c in range(sgu_cols // cw):
        su_ref[:, c * cw:(c + 1) * cw] = _gelu(proj(qk_cols + v_cols + c * cw)).astype(BF16)

    sv = jnp.concatenate(
        [_gelu(proj(qk_cols + v_cols + sgu_cols + c * cw)) for c in range(sgu_cols // cw)], axis=1)
    mu = jnp.mean(sv, axis=-1, keepdims=True)
    svc = sv - mu
    var = jnp.mean(svc * svc, axis=-1, keepdims=True)
    sv_ref[...] = (svc * lax.rsqrt(var + RMS_EPS) * lng_ref[...] + lnb_ref[...]).astype(BF16)


def _inproj(x2d, norm_g, w_in, gqk, pos_rows, freq_col, expand, ones_blk, ln_g, ln_b, *, qk_cols, v_cols,
            sgu_cols, head_dim, rope_half):
    t, d = x2d.shape
    tm = ROW_TILE
    row = lambda i: (i, 0)
    fixed = lambda i: (0, 0)
    kern = functools.partial(_inproj_kernel, qk_cols=qk_cols, v_cols=v_cols, sgu_cols=sgu_cols,
                             head_dim=head_dim, rope_half=rope_half)
    return pl.pallas_call(
        kern,
        grid=(t // tm,),
        in_specs=[
            pl.BlockSpec((tm, d), row),
            pl.BlockSpec((1, d), fixed),
            pl.BlockSpec(w_in.shape, fixed),
            pl.BlockSpec((1, qk_cols), fixed),
            pl.BlockSpec((1, 1, tm), lambda i: (i, 0, 0)),
            pl.BlockSpec(freq_col.shape, fixed),
            pl.BlockSpec(expand.shape, fixed),
            pl.BlockSpec((MXU_WIDTH, MXU_WIDTH), fixed),
            pl.BlockSpec((1, sgu_cols), fixed),
            pl.BlockSpec((1, sgu_cols), fixed),
        ],
        out_specs=[
            pl.BlockSpec((tm, qk_cols), row),
            pl.BlockSpec((tm, v_cols), row),
            pl.BlockSpec((tm, sgu_cols), row),
            pl.BlockSpec((tm, sgu_cols), row),
        ],
        out_shape=[
            jax.ShapeDtypeStruct((t, qk_cols), BF16),
            jax.ShapeDtypeStruct((t, v_cols), BF16),
            jax.ShapeDtypeStruct((t, sgu_cols), BF16),
            jax.ShapeDtypeStruct((t, sgu_cols), BF16),
        ],
        scratch_shapes=[pltpu.VMEM((tm, w_in.shape[1]), F32), pltpu.VMEM((tm, qk_cols), F32)],
        compiler_params=_params(1),
        name="inproj",
    )(x2d, norm_g, w_in, gqk, pos_rows, freq_col, expand, ones_blk, ln_g, ln_b)


def _attn_running_max_kernel(lam_ref, gout_ref, q_ref, k_ref, v_ref, o_ref, qm_ref, m_ref, acc_ref, *, tq,
                             n_heads, lambda_init, head_dim):
    qi = pl.program_id(1)
    hw = 2 * head_dim
    vd = v_ref.shape[-1] // n_heads
    lv = lam_ref[...]
    lam = (jnp.exp(jnp.sum(lv[0:1] * lv[1:2], axis=-1, keepdims=True))
           - jnp.exp(jnp.sum(lv[2:3] * lv[3:4], axis=-1, keepdims=True))
           + lambda_init)

    first = lax.broadcasted_iota(jnp.int32, (1, hw), 1) < head_dim
    for h in range(n_heads):
        q = q_ref[0, :, h * hw:(h + 1) * hw]
        zero = jnp.zeros_like(q)
        qm_ref[2 * h] = jnp.where(first, q, zero)
        qm_ref[2 * h + 1] = jnp.where(first, zero, q)
    acc_ref[...] = jnp.zeros(acc_ref.shape, F32)
    m_ref[...] = jnp.full(m_ref.shape, NEG_INF, F32)

    ones = jnp.ones((tq, vd), BF16)
    causal = (lax.broadcasted_iota(jnp.int32, (tq, tq), 1)
              <= lax.broadcasted_iota(jnp.int32, (tq, tq), 0))

    def kv_step(j, masked):
        start = pl.multiple_of(j * tq, tq)
        for h in range(n_heads):
            kc = k_ref[0, pl.ds(start, tq), h * hw:(h + 1) * hw]
            vext = jnp.concatenate([v_ref[0, pl.ds(start, tq), h * vd:(h + 1) * vd], ones], axis=1)
            for mp in range(2):
                c = 2 * h + mp
                s = lax.dot_general(qm_ref[c], kc, (((1,), (1,)), ((), ())), preferred_element_type=F32)
                if masked:
                    s = jnp.where(causal, s, NEG_INF)
                m_old = m_ref[c]
                m_new = jnp.maximum(m_old, jnp.max(s, axis=-1, keepdims=True))
                p = jnp.exp2(s - m_new).astype(BF16)
                acc_ref[c] = (jnp.exp2(m_old - m_new) * acc_ref[c]
                              + jnp.dot(p, vext, preferred_element_type=F32))
                m_ref[c] = m_new

    def full_tile(j, carry):
        kv_step(j, False)
        return carry

    lax.fori_loop(0, qi, full_tile, 0)
    kv_step(qi, True)

    for h in range(n_heads):
        a1 = acc_ref[2 * h]
        a2 = acc_ref[2 * h + 1]
        att = a1[:, :vd] / a1[:, vd:] - lam * (a2[:, :vd] / a2[:, vd:])
        ms = jnp.mean(att * att, axis=-1, keepdims=True)
        o_ref[0, :, h * vd:(h + 1) * vd] = (
            att * lax.rsqrt(ms + RMS_EPS) * gout_ref[...] * (1.0 - lambda_init)).astype(BF16)


def _attn_static_kernel(lam_ref, gout_ref, q_ref, k_ref, v_ref, o_ref, vext_ref, *, tq, heads, lambda_init,
                        head_dim):
    s_len = q_ref.shape[1]
    hw = 2 * head_dim
    vd = v_ref.shape[-1] // heads
    lv = lam_ref[...]
    lam = (jnp.exp(jnp.sum(lv[0:1] * lv[1:2], axis=-1, keepdims=True))
           - jnp.exp(jnp.sum(lv[2:3] * lv[3:4], axis=-1, keepdims=True))
           + lambda_init)
    first = lax.broadcasted_iota(jnp.int32, (1, hw), 1) < head_dim
    causal = (lax.broadcasted_iota(jnp.int32, (tq, tq), 1)
              <= lax.broadcasted_iota(jnp.int32, (tq, tq), 0))
    nt = (((1,), (1,)), ((), ()))

    for h in range(heads):
        vext_ref[h, :, 0:vd] = v_ref[0, :, h * vd:(h + 1) * vd]
        vext_ref[h, :, vd:2 * vd] = jnp.ones((s_len, vd), BF16)

    for h in range(heads):
        kcols = slice(h * hw, (h + 1) * hw)
        for qi in range(s_len // tq):
            lo = qi * tq
            q = q_ref[0, lo:lo + tq, kcols]
            zero = jnp.zeros_like(q)
            acc = []
            for mp in range(2):
                qm = jnp.where(first, q, zero) if mp == 0 else jnp.where(first, zero, q)
                sd = lax.dot_general(qm, k_ref[0, lo:lo + tq, kcols], nt, preferred_element_type=F32)
                pd = jnp.exp2(jnp.where(causal, sd, NEG_INF)).astype(BF16)
                pv = jnp.dot(pd, vext_ref[h, lo:lo + tq, :], preferred_element_type=F32)
                if qi > 0:
                    s = lax.dot_general(qm, k_ref[0, 0:lo, kcols], nt, preferred_element_type=F32)
                    pv = pv + jnp.dot(jnp.exp2(s).astype(BF16), vext_ref[h, 0:lo, :],
                                      preferred_element_type=F32)
                acc.append(pv)
            a1, a2 = acc
            att = a1[:, :vd] / a1[:, vd:] - lam * (a2[:, :vd] / a2[:, vd:])
            ms = jnp.mean(att * att, axis=-1, keepdims=True)
            o_ref[0, lo:lo + tq, h * vd:(h + 1) * vd] = (
                att * lax.rsqrt(ms + RMS_EPS) * gout_ref[...] * (1.0 - lambda_init)).astype(BF16)


def _attention_static(lam_vecs, gout, qk3, v3, *, n_heads, head_dim, lambda_init):
    b, s, _ = qk3.shape
    vd = v3.shape[-1] // n_heads
    heads = ATTN_HEADS_PER_STEP
    groups = n_heads // heads
    qw = heads * 2 * head_dim
    kern = functools.partial(_attn_static_kernel, tq=QBLOCK, heads=heads, lambda_init=lambda_init,
                             head_dim=head_dim)
    return pl.pallas_call(
        kern,
        grid=(b, groups),
        in_specs=[
            pl.BlockSpec(lam_vecs.shape, lambda bi, g: (0, 0)),
            pl.BlockSpec((1, vd), lambda bi, g: (0, 0)),
            pl.BlockSpec((1, s, qw), lambda bi, g: (bi, 0, g)),
            pl.BlockSpec((1, s, qw), lambda bi, g: (bi, 0, groups + g)),
            pl.BlockSpec((1, s, heads * vd), lambda bi, g: (bi, 0, g)),
        ],
        out_specs=pl.BlockSpec((1, s, heads * vd), lambda bi, g: (bi, 0, g)),
        out_shape=jax.ShapeDtypeStruct(v3.shape, BF16),
        scratch_shapes=[pltpu.VMEM((heads, s, 2 * vd), BF16)],
        compiler_params=_params(2),
        name="diff_attn_static",
    )(lam_vecs, gout, qk3, qk3, v3)


def _attention_running_max(lam_vecs, gout, qk3, v3, *, n_heads, head_dim, lambda_init):
    b, s, _ = qk3.shape
    att_cols = v3.shape[-1]
    vd = att_cols // n_heads
    tq = QBLOCK
    kern = functools.partial(_attn_running_max_kernel, tq=tq, n_heads=n_heads, lambda_init=lambda_init,
                             head_dim=head_dim)
    return pl.pallas_call(
        kern,
        grid=(b, s // tq),
        in_specs=[
            pl.BlockSpec(lam_vecs.shape, lambda bi, qi: (0, 0)),
            pl.BlockSpec((1, vd), lambda bi, qi: (0, 0)),
            pl.BlockSpec((1, tq, att_cols), lambda bi, qi: (bi, qi, 0)),
            pl.BlockSpec((1, s, att_cols), lambda bi, qi: (bi, 0, 1)),
            pl.BlockSpec((1, s, att_cols), lambda bi, qi: (bi, 0, 0)),
        ],
        out_specs=pl.BlockSpec((1, tq, att_cols), lambda bi, qi: (bi, qi, 0)),
        out_shape=jax.ShapeDtypeStruct(v3.shape, BF16),
        scratch_shapes=[
            pltpu.VMEM((2 * n_heads, tq, 2 * head_dim), BF16),
            pltpu.VMEM((2 * n_heads, tq, 1), F32),
            pltpu.VMEM((2 * n_heads, tq, 2 * vd), F32),
        ],
        compiler_params=_params(2),
        name="diff_attn_running_max",
    )(lam_vecs, gout, qk3, qk3, v3)


def _mixout_kernel(att_ref, su_ref, sv_ref, x_ref, sw_ref, sbt_ref, sgn_ref, wo_ref, gffn_ref,
                   rwc_ref, rwh_ref, rb_ref, lstrict_ref, h_ref, hn_ref, route_ref, route_t_ref, counts_ref,
                   cnt_ref, h_prev_ref, acc_ref, *, n_sgu_groups, att_cols):
    step = pl.program_id(0)
    tm = x_ref.shape[0]
    gw = CHUNK
    r_id = lax.broadcasted_iota(jnp.int32, (gw, gw), 0)
    c_id = lax.broadcasted_iota(jnp.int32, (gw, gw), 1)
    tril = c_id <= r_id

    @pl.when(step == 0)
    def _():
        cnt_ref[...] = jnp.zeros_like(cnt_ref)
        h_prev_ref[...] = jnp.zeros_like(h_prev_ref)

    h_prev = h_prev_ref[...]
    ms_prev = jnp.mean(h_prev * h_prev, axis=-1, keepdims=True)
    hn = h_prev * lax.rsqrt(ms_prev + RMS_EPS) * gffn_ref[...]
    hn_ref[...] = _pack_bf16_pairs(hn)
    _route_tile(hn, (step > 0).astype(F32), rwc_ref, rwh_ref, rb_ref, lstrict_ref, route_ref, route_t_ref,
                counts_ref, cnt_ref)

    acc_ref[...] = jnp.dot(att_ref[...], wo_ref[0:att_cols, :], preferred_element_type=F32)

    sg_cols = []
    for g in range(n_sgu_groups):
        w = jnp.where(tril, sw_ref[g], 0.0).astype(BF16)
        bias = sbt_ref[:, g:g + 1]
        gn = sgn_ref[g:g + 1, :]
        rows = []
        for c in range(tm // gw):
            vblk = sv_ref[c * gw:(c + 1) * gw, g * gw:(g + 1) * gw]
            s = jnp.dot(w, vblk, preferred_element_type=F32) + bias
            sg = su_ref[c * gw:(c + 1) * gw, g * gw:(g + 1) * gw].astype(F32) * s
            ms = jnp.mean(sg * sg, axis=-1, keepdims=True)
            rows.append((sg * lax.rsqrt(ms + RMS_EPS) * gn).astype(BF16))
        sg_cols.append(jnp.concatenate(rows, axis=0))
    sgn = jnp.concatenate(sg_cols, axis=1)
    acc = acc_ref[...] + jnp.dot(sgn, wo_ref[att_cols:, :], preferred_element_type=F32)

    h = x_ref[...] + acc
    h_ref[...] = h
    h_prev_ref[...] = h


def _route_tile(hn, live, rwc_ref, rwh_ref, rb_ref, lstrict_ref, route_ref, route_t_ref, counts_ref, cnt_ref):
    hn_bf, hn_lo = _split_bf16(hn)
    both = jnp.dot(hn_bf, rwc_ref[...], preferred_element_type=F32)
    logits = (both[:, :LANES] + both[:, LANES:]
              + jnp.dot(hn_lo, rwh_ref[...], preferred_element_type=F32)
              + rb_ref[...])

    lane = lax.broadcasted_iota(jnp.int32, logits.shape, 1).astype(F32)
    big = float(LANES)
    is_grp = lane < N_GROUPS
    gl = jnp.where(is_grp, logits, NEG_INF)
    gmax = jnp.max(gl, axis=-1, keepdims=True)
    gidx = jnp.min(jnp.where(gl == gmax, lane, big), axis=-1, keepdims=True)
    psum = jnp.sum(jnp.where(is_grp, jnp.exp(gl - gmax), 0.0), axis=-1, keepdims=True)
    p_grp = 1.0 / psum
    e_lo = N_GROUPS + gidx * EXPERTS_PER_GROUP
    in_grp = (lane >= e_lo) & (lane < e_lo + EXPERTS_PER_GROUP)
    el = jnp.where(in_grp, logits, NEG_INF)
    t1 = jnp.max(el, axis=-1, keepdims=True)
    i1 = jnp.min(jnp.where(el == t1, lane, big), axis=-1, keepdims=True)
    el2 = jnp.where(lane == i1, NEG_INF, el)
    t2 = jnp.max(el2, axis=-1, keepdims=True)
    i2 = jnp.min(jnp.where(el2 == t2, lane, big), axis=-1, keepdims=True)
    e21 = jnp.exp(t2 - t1)
    den = 1.0 + e21
    g1 = p_grp / den
    g2 = p_grp * (e21 / den)
    e1 = i1 - N_GROUPS
    e2 = i2 - N_GROUPS

    hit1 = lane == e1
    hit2 = lane == e2
    onehot = jnp.where(hit1 | hit2, live, 0.0)
    before = jnp.dot(lstrict_ref[...], onehot.astype(BF16), preferred_element_type=F32) + cnt_ref[...]
    r1 = jnp.sum(jnp.where(hit1, before, 0.0), axis=-1, keepdims=True)
    r2 = jnp.sum(jnp.where(hit2, before, 0.0), axis=-1, keepdims=True)
    cnt_ref[...] = cnt_ref[...] + jnp.sum(onehot, axis=0, keepdims=True)
    counts_ref[...] = cnt_ref[...]

    route = jnp.zeros_like(logits)
    for k, val in enumerate((e1, e2, g1, g2, r1, r2)):
        route = jnp.where(lane == k, val, route)
    route_ref[...] = route
    route_t_ref[...] = route.T[0:SUBLANES, :]


def _mixout(att, su, sv, x2d, sgu_w, sgu_bt, sgu_gn, w_out, g_ffn, rw_cat, rw_hi, rb, lstrict, *,
            att_cols):
    t, d = x2d.shape
    tm = ROW_TILE
    n = t // tm
    row = lambda i: (jnp.minimum(i, n - 1), 0)
    routed = lambda i: (jnp.maximum(i - 1, 0), 0)
    fixed2 = lambda i: (0, 0)
    n_groups = sgu_w.shape[0]
    kern = functools.partial(_mixout_kernel, n_sgu_groups=n_groups, att_cols=att_cols)
    return pl.pallas_call(
        kern,
        grid=(n + 1,),
        in_specs=[
            pl.BlockSpec((tm, att.shape[1]), row),
            pl.BlockSpec((tm, su.shape[1]), row),
            pl.BlockSpec((tm, sv.shape[1]), row),
            pl.BlockSpec((tm, d), row),
            pl.BlockSpec(sgu_w.shape, lambda i: (0, 0, 0)),
            pl.BlockSpec(sgu_bt.shape, fixed2),
            pl.BlockSpec(sgu_gn.shape, fixed2),
            pl.BlockSpec(w_out.shape, fixed2),
            pl.BlockSpec((1, d), fixed2),
            pl.BlockSpec(rw_cat.shape, fixed2),
            pl.BlockSpec(rw_hi.shape, fixed2),
            pl.BlockSpec((1, LANES), fixed2),
            pl.BlockSpec((tm, tm), fixed2),
        ],
        out_specs=[
            pl.BlockSpec((tm, d), row),
            pl.BlockSpec((tm, d // 2), routed),
            pl.BlockSpec((tm, LANES), routed),
            pl.BlockSpec((SUBLANES, tm), lambda i: (0, jnp.maximum(i - 1, 0))),
            pl.BlockSpec((1, LANES), fixed2),
        ],
        out_shape=[
            jax.ShapeDtypeStruct((t, d), F32),
            jax.ShapeDtypeStruct((t, d // 2), U32),
            jax.ShapeDtypeStruct((t, LANES), F32),
            jax.ShapeDtypeStruct((SUBLANES, t), F32),
            jax.ShapeDtypeStruct((1, LANES), F32),
        ],
        scratch_shapes=[pltpu.VMEM((1, LANES), F32), pltpu.VMEM((tm, d), F32), pltpu.VMEM((tm, d), F32)],
        compiler_params=_params(1),
        name="mixout",
    )(att, su, sv, x2d, sgu_w, sgu_bt, sgu_gn, w_out, g_ffn, rw_cat, rw_hi, rb, lstrict)


def _dispatch_kernel(*refs):
    dest_refs, (hn_hbm, xs_ref, ring, in_sem, out_sem) = refs[:TOP_K], refs[TOP_K:]
    i = pl.program_id(0)
    n = pl.num_programs(0)
    n_slots, rows, _ = ring.shape
    parity = i % 2

    def tile_fetch(step, slot):
        return pltpu.make_async_copy(hn_hbm.at[pl.ds(step * rows, rows), :], ring.at[slot], in_sem.at[slot])

    def rows_done(which):
        for _ in range(TOP_K):
            pltpu.make_async_copy(ring.at[0], xs_ref.at[pl.ds(0, rows), :], out_sem.at[which]).wait()

    @pl.when(i == 0)
    def _():
        tile_fetch(0, 0).start()

    for slot in range(n_slots):
        @pl.when(i % n_slots == slot)
        def _():
            @pl.when(i + 1 < n)
            def _():
                tile_fetch(i + 1, (slot + 1) % n_slots).start()

            tile_fetch(i, slot).wait()
            for r in range(rows):
                for k in range(TOP_K):
                    pltpu.make_async_copy(ring.at[slot, pl.ds(r, 1), :],
                                          xs_ref.at[pl.ds(dest_refs[k][r], 1), :],
                                          out_sem.at[parity]).start(priority=k)

    @pl.when(i > 0)
    def _():
        rows_done(1 - parity)

    @pl.when(i == n - 1)
    def _():
        rows_done(parity)


def _dispatch(hn, dests):
    t, d = hn.shape
    rows = DISPATCH_TILE
    index_spec = pl.BlockSpec((rows,), lambda i: (i,), memory_space=pltpu.SMEM)
    return pl.pallas_call(
        _dispatch_kernel,
        grid=(t // rows,),
        in_specs=[index_spec] * TOP_K + [pl.BlockSpec(memory_space=pl.ANY)],
        out_specs=pl.BlockSpec(memory_space=pl.ANY),
        out_shape=jax.ShapeDtypeStruct((t * TOP_K, d), hn.dtype),
        scratch_shapes=[pltpu.VMEM((DISPATCH_RING, rows, d), hn.dtype),
                        pltpu.SemaphoreType.DMA((DISPATCH_RING,)), pltpu.SemaphoreType.DMA((2,))],
        compiler_params=_params(1),
        name="dispatch",
    )(*dests, hn)


def _experts_kernel(vt_ref, ve_ref, vf_ref, vn_ref, vs_ref, vx_ref, lo_ref, hi_ref, xs_ref, wg_hbm, wu_hbm,
                    wd_hbm, ys_ref, wg_f32, wu_f32, wd_f32, wg_bf, wu_bf, wd_bf, act_ref, wsem):
    v = pl.program_id(0)
    flag = vf_ref[v]
    tm = xs_ref.shape[0]

    def weight_copies(expert, slot):
        return [pltpu.make_async_copy(src.at[expert], dst.at[slot], wsem.at[slot])
                for src, dst in ((wg_hbm, wg_f32), (wu_hbm, wu_f32), (wd_hbm, wd_f32))]

    @pl.when(v == 0)
    def _():
        for cp in weight_copies(ve_ref[0], 0):
            cp.start()

    @pl.when(vn_ref[v] == 1)
    def _():
        slot = vs_ref[v]
        for cp in weight_copies(ve_ref[v], slot):
            cp.wait()

        @pl.when(vx_ref[v] >= 0)
        def _():
            for cp in weight_copies(vx_ref[v], 1 - slot):
                cp.start()

        wg_bf[...] = wg_f32[slot].astype(BF16)
        wu_bf[...] = wu_f32[slot].astype(BF16)
        wd_bf[...] = wd_f32[slot].astype(BF16)

    e = ve_ref[v]
    base = vt_ref[v] * tm
    lo = lo_ref[e]
    hi = hi_ref[e]

    def visit_rows(r0, nrows):
        rows = slice(r0, r0 + nrows)
        rid = lax.broadcasted_iota(jnp.int32, (nrows, 1), 0) + (base + r0)
        mine = (rid >= lo) & (rid < hi)
        x = _unpack_bf16_pairs(xs_ref[rows, :]).astype(BF16)
        for n in range(act_ref.shape[1] // MXU_WIDTH):
            cols = slice(n * MXU_WIDTH, (n + 1) * MXU_WIDTH)
            hg = jnp.dot(x, wg_bf[:, cols], preferred_element_type=F32)
            hu = jnp.dot(x, wu_bf[:, cols], preferred_element_type=F32)
            act_ref[rows, cols] = (hg * (1.0 / (1.0 + jnp.exp(-hg))) * hu).astype(BF16)
        y = _pack_bf16_pairs(jnp.dot(act_ref[rows, :], wd_bf[...], preferred_element_type=F32))

        @pl.when(flag == 1)
        def _():
            ys_ref[rows, :] = jnp.where(mine, y, jnp.zeros_like(y))

        @pl.when(flag == 2)
        def _():
            ys_ref[rows, :] = jnp.where(mine, y, ys_ref[rows, :])

    half = tm // 2
    top = (lo < base + half) & (hi > base)
    bottom = (hi > base + half) & (lo < base + tm)

    @pl.when((flag > 0) & top & bottom)
    def _():
        visit_rows(0, tm)

    for r0, wanted, other in ((0, top, bottom), (half, bottom, top)):
        @pl.when((flag > 0) & wanted & jnp.logical_not(other))
        def _():
            visit_rows(r0, half)

        @pl.when((flag == 1) & jnp.logical_not(wanted))
        def _():
            ys_ref[r0:r0 + half, :] = jnp.zeros((half, ys_ref.shape[1]), ys_ref.dtype)


def _experts(visits, row_lo, row_hi, xs, w_gate, w_up, w_down):
    p, words = xs.shape
    d = 2 * words
    tm = EXPERT_TILE
    f = w_gate.shape[-1]
    rows_map = lambda v, vt, *_: (vt[v], 0)
    whole = pl.BlockSpec(memory_space=pl.ANY)
    grid_spec = pltpu.PrefetchScalarGridSpec(
        num_scalar_prefetch=len(visits) + 2,
        grid=(visits[0].shape[0],),
        in_specs=[pl.BlockSpec((tm, words), rows_map), whole, whole, whole],
        out_specs=pl.BlockSpec((tm, words), rows_map),
        scratch_shapes=[pltpu.VMEM((2, d, f), F32), pltpu.VMEM((2, d, f), F32), pltpu.VMEM((2, f, d), F32),
                        pltpu.VMEM((d, f), BF16), pltpu.VMEM((d, f), BF16), pltpu.VMEM((f, d), BF16),
                        pltpu.VMEM((tm, f), BF16), pltpu.SemaphoreType.DMA((2,))],
    )
    return pl.pallas_call(
        _experts_kernel,
        grid_spec=grid_spec,
        out_shape=jax.ShapeDtypeStruct(xs.shape, xs.dtype),
        compiler_params=_params(1),
        name="experts",
    )(*visits, row_lo, row_hi, xs, w_gate, w_up, w_down)


def _combine_kernel(*refs):
    dest_refs, dest_next_refs = refs[:TOP_K], refs[TOP_K:2 * TOP_K]
    h_ref, route_ref, ys_ref, o_ref, ybuf, sem = refs[2 * TOP_K:]
    i = pl.program_id(0)
    rows = h_ref.shape[0]
    slot = i % 2

    def issue(idx_refs, into):
        for r in range(rows):
            for k in range(TOP_K):
                pltpu.make_async_copy(ys_ref.at[pl.ds(idx_refs[k][r], 1), :],
                                      ybuf.at[into, k, pl.ds(r, 1), :], sem.at[into]).start(priority=k)

    @pl.when(i == 0)
    def _():
        issue(dest_refs, 0)

    for parity in range(2):
        @pl.when((i + 1 < pl.num_programs(0)) & (slot == parity))
        def _():
            issue(dest_next_refs, 1 - parity)

    for k in range(TOP_K):
        pltpu.make_async_copy(ys_ref.at[pl.ds(0, rows), :], ybuf.at[slot, k], sem.at[slot]).wait()

    route = route_ref[...]
    moe = None
    for k in range(TOP_K):
        gated = route[:, ROUTE_GATE + k:ROUTE_GATE + k + 1] * _unpack_bf16_pairs(ybuf[slot, k])
        moe = gated if moe is None else moe + gated
    o_ref[...] = h_ref[...] + moe


def _combine(dests, h, route, ys):
    t, d = h.shape
    rows = COMBINE_TILE
    n = t // rows
    this_step = pl.BlockSpec((rows,), lambda i: (i,), memory_space=pltpu.SMEM)
    next_step = pl.BlockSpec((rows,), lambda i: (jnp.minimum(i + 1, n - 1),), memory_space=pltpu.SMEM)
    return pl.pallas_call(
        _combine_kernel,
        grid=(n,),
        in_specs=[this_step] * TOP_K + [next_step] * TOP_K + [
            pl.BlockSpec((rows, d), lambda i: (i, 0)),
            pl.BlockSpec((rows, LANES), lambda i: (i, 0)),
            pl.BlockSpec(memory_space=pl.ANY),
        ],
        out_specs=pl.BlockSpec((rows, d), lambda i: (i, 0)),
        out_shape=jax.ShapeDtypeStruct((t, d), F32),
        scratch_shapes=[pltpu.VMEM((2, TOP_K, rows, ys.shape[1]), ys.dtype), pltpu.SemaphoreType.DMA((2,))],
        compiler_params=_params(1),
        name="combine",
    )(*dests, *dests, h, route, ys)


def _plan(route_t, counts_row):
    t = route_t.shape[1]
    counts = counts_row[0, :N_EXPERTS].astype(I32)
    ends = jnp.cumsum(counts)
    offs = ends - counts
    experts = route_t[ROUTE_EXPERT:ROUTE_EXPERT + TOP_K].astype(I32)
    rank = route_t[ROUTE_RANK:ROUTE_RANK + TOP_K].astype(I32)
    expert_ids = jnp.arange(N_EXPERTS, dtype=I32)[:, None, None]
    first_row = jnp.sum(jnp.where(experts[None] == expert_ids, offs[:, None, None], 0), axis=0)
    dest = first_row + rank
    dests = tuple(dest[k] for k in range(TOP_K))

    tm = EXPERT_TILE
    n_visits = (t * TOP_K) // tm + N_EXPERTS - 1
    first_tile = offs // tm
    last_tile = (ends - 1) // tm
    nvis = jnp.where(counts > 0, last_tile - first_tile + 1, 0)
    vend = jnp.cumsum(nvis)
    vstart = vend - nvis
    v = jnp.arange(n_visits, dtype=I32)
    ev = jnp.minimum(jnp.sum((vend[None, :] <= v[:, None]).astype(I32), axis=1), N_EXPERTS - 1)
    tv = jnp.take(first_tile, ev) + (v - jnp.take(vstart, ev))
    valid = v < vend[-1]
    last_v = vend[-1] - 1
    ev = jnp.where(valid, ev, jnp.take(ev, last_v))
    tv = jnp.where(valid, tv, jnp.take(tv, last_v))
    prev_t = jnp.concatenate([jnp.full((1,), -1, I32), tv[:-1]])
    prev_e = jnp.concatenate([jnp.full((1,), -1, I32), ev[:-1]])
    flag = jnp.where(valid, jnp.where(tv != prev_t, 1, 2), 0).astype(I32)
    new_expert = (ev != prev_e).astype(I32)
    slot = (jnp.cumsum(new_expert) - 1) % 2
    ids = jnp.arange(N_EXPERTS, dtype=I32)
    later = (ids[None, :] > ids[:, None]) & (counts[None, :] > 0)
    next_active = jnp.min(jnp.where(later, ids[None, :], N_EXPERTS), axis=1)
    next_active = jnp.where(next_active < N_EXPERTS, next_active, -1)
    visits = (tv.astype(I32), ev.astype(I32), flag, new_expert, slot.astype(I32),
              jnp.take(next_active, ev).astype(I32))
    return dests, visits, offs.astype(I32), ends.astype(I32)


def _rope_constants(head_dim, rope_dim):
    rope_half = rope_dim // 2
    inv_freq = ROPE_THETA ** (-jnp.arange(0, rope_dim, 2, dtype=F32) / rope_dim)
    dim = jnp.arange(LANES, dtype=I32)[None, :] % head_dim
    k = jnp.arange(4 * rope_half, dtype=I32)[:, None]
    c_mat = jnp.where(dim < rope_dim, k == dim % rope_half, k == rope_dim)
    sa_mat = (dim < rope_half) & (k == rope_half + dim)
    sb_mat = (dim >= rope_half) & (dim < rope_dim) & (k == dim)
    expand = jnp.concatenate([c_mat.astype(F32), -sa_mat.astype(F32), sb_mat.astype(F32)], axis=1)
    return inv_freq.reshape(rope_half, 1), expand.astype(BF16)


def kernel(x, positions, norm_mix_g, w_in, q_norm_g, k_norm_g, lambda_q1, lambda_k1, lambda_q2,
           lambda_k2, diff_out_norm_g, sgu_ln_g, sgu_ln_b, sgu_w, sgu_b, sgu_out_norm_g, w_out,
           norm_ffn_g, router_group_w, router_group_b, router_expert_w, router_expert_b,
           expert_w_gate, expert_w_up, expert_w_down):
    b, s, d = x.shape
    t = b * s
    depth = w_in.shape[0]
    head_dim = q_norm_g.shape[-1]
    vd = diff_out_norm_g.shape[-1]
    n_sgu_groups = sgu_w.shape[1]
    sgu_cols = n_sgu_groups * sgu_w.shape[-1]
    in_cols = w_in.shape[-1]
    att_cols = (in_cols - 2 * sgu_cols) // 3
    n_heads = att_cols // vd
    qk_cols = 2 * att_cols
    rope_dim = head_dim // ROPE_FRACTION
    rope_half = rope_dim // 2
    scale = head_dim ** -0.5

    pos_rows = positions.reshape(t // ROW_TILE, 1, ROW_TILE).astype(F32)
    freq_col, expand = _rope_constants(head_dim, rope_dim)
    blk = jnp.arange(MXU_WIDTH, dtype=I32) // head_dim
    ones_blk = (blk[:, None] == blk[None, :]).astype(BF16)
    tri = jnp.arange(ROW_TILE, dtype=I32)
    lstrict = (tri[None, :] < tri[:, None]).astype(BF16)

    h = x.reshape(t, d)
    for l in range(depth):
        lambda_init = 0.8 - 0.6 * math.exp(-0.3 * l)
        gqk = jnp.concatenate([jnp.tile(q_norm_g[l] * (scale * math.log2(math.e)), att_cols // head_dim),
                               jnp.tile(k_norm_g[l], att_cols // head_dim)]).reshape(1, qk_cols)
        qk, v, su, sv = _inproj(
            h, norm_mix_g[l].reshape(1, d), w_in[l].astype(BF16), gqk, pos_rows, freq_col, expand, ones_blk,
            sgu_ln_g[l].reshape(1, sgu_cols), sgu_ln_b[l].reshape(1, sgu_cols),
            qk_cols=qk_cols, v_cols=att_cols, sgu_cols=sgu_cols, head_dim=head_dim,
            rope_half=rope_half)

        lam_vecs = jnp.stack([lambda_q1[l], lambda_k1[l], lambda_q2[l], lambda_k2[l]]).astype(F32)
        score_bound = (1.02 * head_dim * scale * jnp.max(jnp.abs(q_norm_g[l]))
                       * jnp.max(jnp.abs(k_norm_g[l])))
        attn_args = (lam_vecs, diff_out_norm_g[l].reshape(1, vd), qk.reshape(b, s, qk_cols),
                     v.reshape(b, s, att_cols))
        attn_kw = dict(n_heads=n_heads, head_dim=head_dim, lambda_init=lambda_init)
        att = lax.cond(score_bound > MAX_UNSHIFTED_SCORE,
                       functools.partial(_attention_running_max, **attn_kw),
                       functools.partial(_attention_static, **attn_kw),
                       *attn_args).reshape(t, att_cols)

        rw = jnp.concatenate([router_group_w[l], router_expert_w[l]], axis=1)
        rw = jnp.pad(rw, ((0, 0), (0, LANES - rw.shape[1])))
        rw_hi, rw_lo = _split_bf16(rw)
        rb = jnp.pad(jnp.concatenate([router_group_b[l], router_expert_b[l]]),
                     (0, LANES - N_GROUPS - N_EXPERTS)).reshape(1, LANES)
        h, hn, route, route_t, counts_row = _mixout(
            att, su, sv, h, sgu_w[l], sgu_b[l].T, sgu_out_norm_g[l], w_out[l].astype(BF16),
            norm_ffn_g[l].reshape(1, d), jnp.concatenate([rw_hi, rw_lo], axis=1), rw_hi, rb, lstrict,
            att_cols=att_cols)

        dests, visits, row_lo, row_hi = _plan(route_t, counts_row)
        xs = _dispatch(hn, dests)
        ys = _experts(visits, row_lo, row_hi, xs, expert_w_gate[l], expert_w_up[l], expert_w_down[l])
        h = _combine(dests, h, route, ys)
    return h.reshape(b, s, d)
```

```python
import functools
import math

import jax
import jax.numpy as jnp
from jax import lax
from jax.experimental import pallas as pl
from jax.experimental.pallas import tpu as pltpu

F32 = jnp.float32
BF16 = jnp.bfloat16
I32 = jnp.int32

RMS_EPS = 1e-6
NEG_INF = -1e30
ROPE_THETA = 500000.0
MAX_UNSHIFTED_SCORE = 40.0

LANES = 128
SUBLANES = 8
MXU_WIDTH = 256
VMEM_LIMIT_BYTES = 56 * 1024 * 1024

N_GROUPS = 4
EXPERTS_PER_GROUP = 8
N_EXPERTS = N_GROUPS * EXPERTS_PER_GROUP
TOP_K = 2
ROPE_FRACTION = 4
CHUNK = 128
QBLOCK = 256
ATTN_HEADS_PER_STEP = 2
ROW_TILE = 512
EXPERT_TILE = 512
DISPATCH_TILE = 1024
DISPATCH_RING = 3
COMBINE_TILE = 512

ROUTE_EXPERT = 0
ROUTE_GATE = 2
ROUTE_RANK = 4


def _params(n_grid_dims, fuse_inputs=None):
    return pltpu.CompilerParams(
        dimension_semantics=("arbitrary",) * n_grid_dims,
        vmem_limit_bytes=VMEM_LIMIT_BYTES,
        allow_input_fusion=fuse_inputs,
    )


def _gelu(x):
    return 0.5 * x * (1.0 + lax.erf(x * (1.0 / math.sqrt(2.0))))


def _split_bf16(x):
    hi = x.astype(BF16)
    return hi, (x - hi.astype(F32)).astype(BF16)


U32 = jnp.uint32
BF16_BITS = 16
HIGH_HALF = 0xFFFF0000


def _pack_bf16_pairs(x):
    n = x.shape[1] // 2
    hi = lax.bitcast_convert_type(x[:, :n].astype(BF16).astype(F32), U32)
    lo = lax.bitcast_convert_type(x[:, n:].astype(BF16).astype(F32), U32)
    return hi | (lo >> BF16_BITS)


def _unpack_bf16_pairs(w):
    hi = lax.bitcast_convert_type(w & jnp.uint32(HIGH_HALF), F32)
    lo = lax.bitcast_convert_type(w << BF16_BITS, F32)
    return jnp.concatenate([hi, lo], axis=1)


def _inproj_kernel(x_ref, g_ref, w_ref, gqk_ref, pos_ref, freq_ref, expand_ref, ones_ref, lng_ref, lnb_ref,
                   qk_ref, v_ref, su_ref, sv_ref, proj_ref, ss_ref, *, qk_cols, v_cols, sgu_cols, head_dim,
                   rope_half):
    x = x_ref[...]
    ms = jnp.mean(x * x, axis=-1, keepdims=True)
    xn = (x * lax.rsqrt(ms + RMS_EPS) * g_ref[...]).astype(BF16)
    cw = MXU_WIDTH

    ang_t = freq_ref[...] * pos_ref[0]
    cs_t = jnp.concatenate([jnp.cos(ang_t), jnp.sin(ang_t), jnp.ones_like(ang_t),
                            jnp.zeros_like(ang_t)], axis=0)
    contract0 = (((0,), (0,)), ((), ()))
    cs_hi, cs_lo = _split_bf16(cs_t)
    tab = (lax.dot_general(cs_hi, expand_ref[...], contract0, preferred_element_type=F32)
           + lax.dot_general(cs_lo, expand_ref[...], contract0, preferred_element_type=F32))
    cos = tab[:, 0:LANES]
    sa = tab[:, LANES:2 * LANES]
    sb = tab[:, 2 * LANES:3 * LANES]

    def stage(c):
        cols = slice(c * cw, (c + 1) * cw)
        proj_ref[:, cols] = jnp.dot(xn, w_ref[:, cols], preferred_element_type=F32)

    n_qk = qk_cols // cw
    for c in range(n_qk):
        stage(c)
    for c in range(n_qk):
        p = proj_ref[:, c * cw:(c + 1) * cw]
        ss_ref[:, c * cw:(c + 1) * cw] = jnp.dot((p * p).astype(BF16), ones_ref[...],
                                                 preferred_element_type=F32)
    n_v = v_cols // cw
    for c in range(n_qk + n_v, w_ref.shape[1] // cw):
        stage(c)
    for c in range(n_qk, n_qk + n_v):
        stage(c)

    def proj(col0):
        return proj_ref[:, col0:col0 + cw]

    for c in range(qk_cols // cw):
        p = proj(c * cw)
        ss = ss_ref[:, c * cw:(c + 1) * cw]
        pn = p * lax.rsqrt(ss * (1.0 / head_dim) + RMS_EPS) * gqk_ref[:, c * cw:(c + 1) * cw]
        for hh in range(cw // LANES):
            blk = pn[:, hh * LANES:(hh + 1) * LANES]
            rot = (blk * cos
                   + pltpu.roll(blk, LANES - rope_half, 1) * sa
                   + pltpu.roll(blk, rope_half, 1) * sb)
            col = c * cw + hh * LANES
            qk_ref[:, col:col + LANES] = rot.astype(BF16)

    for c in range(v_cols // cw):
        v_ref[:, c * cw:(c + 1) * cw] = proj(qk_cols + c * cw).astype(BF16)

    for c in range(sgu_cols // cw):
        su_ref[:, c * cw:(c + 1) * cw] = _gelu(proj(qk_cols + v_cols + c * cw)).astype(BF16)

    sv = jnp.concatenate(
        [_gelu(proj(qk_cols + v_cols + sgu_cols + c * cw)) for c in range(sgu_cols // cw)], axis=1)
    mu = jnp.mean(sv, axis=-1, keepdims=True)
    svc = sv - mu
    var = jnp.mean(svc * svc, axis=-1, keepdims=True)
    sv_ref[...] = (svc * lax.rsqrt(var + RMS_EPS) * lng_ref[...] + lnb_ref[...]).astype(BF16)


def _inproj(x2d, norm_g, w_in, gqk, pos_rows, freq_col, expand, ones_blk, ln_g, ln_b, *, qk_cols, v_cols,
            sgu_cols, head_dim, rope_half):
    t, d = x2d.shape
    tm = ROW_TILE
    row = lambda i: (i, 0)
    fixed = lambda i: (0, 0)
    kern = functools.partial(_inproj_kernel, qk_cols=qk_cols, v_cols=v_cols, sgu_cols=sgu_cols,
                             head_dim=head_dim, rope_half=rope_half)
    return pl.pallas_call(
        kern,
        grid=(t // tm,),
        in_specs=[
            pl.BlockSpec((tm, d), row),
            pl.BlockSpec((1, d), fixed),
            pl.BlockSpec(w_in.shape, fixed),
            pl.BlockSpec((1, qk_cols), fixed),
            pl.BlockSpec((1, 1, tm), lambda i: (i, 0, 0)),
            pl.BlockSpec(freq_col.shape, fixed),
            pl.BlockSpec(expand.shape, fixed),
            pl.BlockSpec((MXU_WIDTH, MXU_WIDTH), fixed),
            pl.BlockSpec((1, sgu_cols), fixed),
            pl.BlockSpec((1, sgu_cols), fixed),
        ],
        out_specs=[
            pl.BlockSpec((tm, qk_cols), row),
            pl.BlockSpec((tm, v_cols), row),
            pl.BlockSpec((tm, sgu_cols), row),
            pl.BlockSpec((tm, sgu_cols), row),
        ],
        out_shape=[
            jax.ShapeDtypeStruct((t, qk_cols), BF16),
            jax.ShapeDtypeStruct((t, v_cols), BF16),
            jax.ShapeDtypeStruct((t, sgu_cols), BF16),
            jax.ShapeDtypeStruct((t, sgu_cols), BF16),
        ],
        scratch_shapes=[pltpu.VMEM((tm, w_in.shape[1]), F32), pltpu.VMEM((tm, qk_cols), F32)],
        compiler_params=_params(1, fuse_inputs=tuple(i == 2 for i in range(10))),
        name="inproj",
    )(x2d, norm_g, w_in, gqk, pos_rows, freq_col, expand, ones_blk, ln_g, ln_b)


def _attn_running_max_kernel(lam_ref, gout_ref, q_ref, k_ref, v_ref, o_ref, qm_ref, m_ref, acc_ref, *, tq,
                             n_heads, lambda_init, head_dim):
    qi = pl.program_id(1)
    hw = 2 * head_dim
    vd = v_ref.shape[-1] // n_heads
    lv = lam_ref[...]
    lam = (jnp.exp(jnp.sum(lv[0:1] * lv[1:2], axis=-1, keepdims=True))
           - jnp.exp(jnp.sum(lv[2:3] * lv[3:4], axis=-1, keepdims=True))
           + lambda_init)

    first = lax.broadcasted_iota(jnp.int32, (1, hw), 1) < head_dim
    for h in range(n_heads):
        q = q_ref[0, :, h * hw:(h + 1) * hw]
        zero = jnp.zeros_like(q)
        qm_ref[2 * h] = jnp.where(first, q, zero)
        qm_ref[2 * h + 1] = jnp.where(first, zero, q)
    acc_ref[...] = jnp.zeros(acc_ref.shape, F32)
    m_ref[...] = jnp.full(m_ref.shape, NEG_INF, F32)

    ones = jnp.ones((tq, vd), BF16)
    causal = (lax.broadcasted_iota(jnp.int32, (tq, tq), 1)
              <= lax.broadcasted_iota(jnp.int32, (tq, tq), 0))

    def kv_step(j, masked):
        start = pl.multiple_of(j * tq, tq)
        for h in range(n_heads):
            kc = k_ref[0, pl.ds(start, tq), h * hw:(h + 1) * hw]
            vext = jnp.concatenate([v_ref[0, pl.ds(start, tq), h * vd:(h + 1) * vd], ones], axis=1)
            for mp in range(2):
                c = 2 * h + mp
                s = lax.dot_general(qm_ref[c], kc, (((1,), (1,)), ((), ())), preferred_element_type=F32)
                if masked:
                    s = jnp.where(causal, s, NEG_INF)
                m_old = m_ref[c]
                m_new = jnp.maximum(m_old, jnp.max(s, axis=-1, keepdims=True))
                p = jnp.exp2(s - m_new).astype(BF16)
                acc_ref[c] = (jnp.exp2(m_old - m_new) * acc_ref[c]
                              + jnp.dot(p, vext, preferred_element_type=F32))
                m_ref[c] = m_new

    def full_tile(j, carry):
        kv_step(j, False)
        return carry

    lax.fori_loop(0, qi, full_tile, 0)
    kv_step(qi, True)

    for h in range(n_heads):
        a1 = acc_ref[2 * h]
        a2 = acc_ref[2 * h + 1]
        att = a1[:, :vd] / a1[:, vd:] - lam * (a2[:, :vd] / a2[:, vd:])
        ms = jnp.mean(att * att, axis=-1, keepdims=True)
        o_ref[0, :, h * vd:(h + 1) * vd] = (
            att * lax.rsqrt(ms + RMS_EPS) * gout_ref[...] * (1.0 - lambda_init)).astype(BF16)


def _attn_static_kernel(lam_ref, gout_ref, q_ref, k_ref, v_ref, o_ref, vext_ref, *, tq, heads, lambda_init,
                        head_dim):
    s_len = q_ref.shape[1]
    hw = 2 * head_dim
    vd = v_ref.shape[-1] // heads
    lv = lam_ref[...]
    lam = (jnp.exp(jnp.sum(lv[0:1] * lv[1:2], axis=-1, keepdims=True))
           - jnp.exp(jnp.sum(lv[2:3] * lv[3:4], axis=-1, keepdims=True))
           + lambda_init)
    first = lax.broadcasted_iota(jnp.int32, (1, hw), 1) < head_dim
    causal = (lax.broadcasted_iota(jnp.int32, (tq, tq), 1)
              <= lax.broadcasted_iota(jnp.int32, (tq, tq), 0))
    nt = (((1,), (1,)), ((), ()))

    for h in range(heads):
        vext_ref[h, :, 0:vd] = v_ref[0, :, h * vd:(h + 1) * vd]
        vext_ref[h, :, vd:2 * vd] = jnp.ones((s_len, vd), BF16)

    for h in range(heads):
        kcols = slice(h * hw, (h + 1) * hw)
        for qi in range(s_len // tq):
            lo = qi * tq
            q = q_ref[0, lo:lo + tq, kcols]
            zero = jnp.zeros_like(q)
            acc = []
            for mp in range(2):
                qm = jnp.where(first, q, zero) if mp == 0 else jnp.where(first, zero, q)
                sd = lax.dot_general(qm, k_ref[0, lo:lo + tq, kcols], nt, preferred_element_type=F32)
                pd = jnp.exp2(jnp.where(causal, sd, NEG_INF)).astype(BF16)
                pv = jnp.dot(pd, vext_ref[h, lo:lo + tq, :], preferred_element_type=F32)
                if qi > 0:
                    s = lax.dot_general(qm, k_ref[0, 0:lo, kcols], nt, preferred_element_type=F32)
                    pv = pv + jnp.dot(jnp.exp2(s).astype(BF16), vext_ref[h, 0:lo, :],
                                      preferred_element_type=F32)
                acc.append(pv)
            a1, a2 = acc
            att = a1[:, :vd] / a1[:, vd:] - lam * (a2[:, :vd] / a2[:, vd:])
            ms = jnp.mean(att * att, axis=-1, keepdims=True)
            o_ref[0, lo:lo + tq, h * vd:(h + 1) * vd] = (
                att * lax.rsqrt(ms + RMS_EPS) * gout_ref[...] * (1.0 - lambda_init)).astype(BF16)


def _attention_static(lam_vecs, gout, qk3, v3, *, n_heads, head_dim, lambda_init):
    b, s, _ = qk3.shape
    vd = v3.shape[-1] // n_heads
    heads = ATTN_HEADS_PER_STEP
    groups = n_heads // heads
    qw = heads * 2 * head_dim
    kern = functools.partial(_attn_static_kernel, tq=QBLOCK, heads=heads, lambda_init=lambda_init,
                             head_dim=head_dim)
    return pl.pallas_call(
        kern,
        grid=(b, groups),
        in_specs=[
            pl.BlockSpec(lam_vecs.shape, lambda bi, g: (0, 0)),
            pl.BlockSpec((1, vd), lambda bi, g: (0, 0)),
            pl.BlockSpec((1, s, qw), lambda bi, g: (bi, 0, g)),
            pl.BlockSpec((1, s, qw), lambda bi, g: (bi, 0, groups + g)),
            pl.BlockSpec((1, s, heads * vd), lambda bi, g: (bi, 0, g)),
        ],
        out_specs=pl.BlockSpec((1, s, heads * vd), lambda bi, g: (bi, 0, g)),
        out_shape=jax.ShapeDtypeStruct(v3.shape, BF16),
        scratch_shapes=[pltpu.VMEM((heads, s, 2 * vd), BF16)],
        compiler_params=_params(2),
        name="diff_attn_static",
    )(lam_vecs, gout, qk3, qk3, v3)


def _attention_running_max(lam_vecs, gout, qk3, v3, *, n_heads, head_dim, lambda_init):
    b, s, _ = qk3.shape
    att_cols = v3.shape[-1]
    vd = att_cols // n_heads
    tq = QBLOCK
    kern = functools.partial(_attn_running_max_kernel, tq=tq, n_heads=n_heads, lambda_init=lambda_init,
                             head_dim=head_dim)
    return pl.pallas_call(
        kern,
        grid=(b, s // tq),
        in_specs=[
            pl.BlockSpec(lam_vecs.shape, lambda bi, qi: (0, 0)),
            pl.BlockSpec((1, vd), lambda bi, qi: (0, 0)),
            pl.BlockSpec((1, tq, att_cols), lambda bi, qi: (bi, qi, 0)),
            pl.BlockSpec((1, s, att_cols), lambda bi, qi: (bi, 0, 1)),
            pl.BlockSpec((1, s, att_cols), lambda bi, qi: (bi, 0, 0)),
        ],
        out_specs=pl.BlockSpec((1, tq, att_cols), lambda bi, qi: (bi, qi, 0)),
        out_shape=jax.ShapeDtypeStruct(v3.shape, BF16),
        scratch_shapes=[
            pltpu.VMEM((2 * n_heads, tq, 2 * head_dim), BF16),
            pltpu.VMEM((2 * n_heads, tq, 1), F32),
            pltpu.VMEM((2 * n_heads, tq, 2 * vd), F32),
        ],
        compiler_params=_params(2),
        name="diff_attn_running_max",
    )(lam_vecs, gout, qk3, qk3, v3)


def _mixout_kernel(att_ref, su_ref, sv_ref, x_ref, sw_ref, sbt_ref, sgn_ref, wo_ref, gffn_ref,
                   rwc_ref, rwh_ref, rb_ref, lstrict_ref, h_ref, hn_ref, route_ref, route_t_ref, counts_ref,
                   cnt_ref, h_prev_ref, acc_ref, *, n_sgu_groups, att_cols):
    step = pl.program_id(0)
    tm = x_ref.shape[0]
    gw = CHUNK
    r_id = lax.broadcasted_iota(jnp.int32, (gw, gw), 0)
    c_id = lax.broadcasted_iota(jnp.int32, (gw, gw), 1)
    tril = c_id <= r_id

    @pl.when(step == 0)
    def _():
        cnt_ref[...] = jnp.zeros_like(cnt_ref)
        h_prev_ref[...] = jnp.zeros_like(h_prev_ref)

    h_prev = h_prev_ref[...]
    ms_prev = jnp.mean(h_prev * h_prev, axis=-1, keepdims=True)
    hn = h_prev * lax.rsqrt(ms_prev + RMS_EPS) * gffn_ref[...]
    hn_ref[...] = _pack_bf16_pairs(hn)
    _route_tile(hn, (step > 0).astype(F32), rwc_ref, rwh_ref, rb_ref, lstrict_ref, route_ref, route_t_ref,
                counts_ref, cnt_ref)

    acc_ref[...] = jnp.dot(att_ref[...], wo_ref[0:att_cols, :], preferred_element_type=F32)

    sg_cols = []
    for g in range(n_sgu_groups):
        w = jnp.where(tril, sw_ref[g], 0.0).astype(BF16)
        bias = sbt_ref[:, g:g + 1]
        gn = sgn_ref[g:g + 1, :]
        rows = []
        for c in range(tm // gw):
            vblk = sv_ref[c * gw:(c + 1) * gw, g * gw:(g + 1) * gw]
            s = jnp.dot(w, vblk, preferred_element_type=F32) + bias
            sg = su_ref[c * gw:(c + 1) * gw, g * gw:(g + 1) * gw].astype(F32) * s
            ms = jnp.mean(sg * sg, axis=-1, keepdims=True)
            rows.append((sg * lax.rsqrt(ms + RMS_EPS) * gn).astype(BF16))
        sg_cols.append(jnp.concatenate(rows, axis=0))
    sgn = jnp.concatenate(sg_cols, axis=1)
    acc = acc_ref[...] + jnp.dot(sgn, wo_ref[att_cols:, :], preferred_element_type=F32)

    h = x_ref[...] + acc
    h_ref[...] = h
    h_prev_ref[...] = h


def _route_tile(hn, live, rwc_ref, rwh_ref, rb_ref, lstrict_ref, route_ref, route_t_ref, counts_ref, cnt_ref):
    hn_bf, hn_lo = _split_bf16(hn)
    both = jnp.dot(hn_bf, rwc_ref[...], preferred_element_type=F32)
    logits = (both[:, :LANES] + both[:, LANES:]
              + jnp.dot(hn_lo, rwh_ref[...], preferred_element_type=F32)
              + rb_ref[...])

    lane = lax.broadcasted_iota(jnp.int32, logits.shape, 1).astype(F32)
    big = float(LANES)
    is_grp = lane < N_GROUPS
    gl = jnp.where(is_grp, logits, NEG_INF)
    gmax = jnp.max(gl, axis=-1, keepdims=True)
    gidx = jnp.min(jnp.where(gl == gmax, lane, big), axis=-1, keepdims=True)
    psum = jnp.sum(jnp.where(is_grp, jnp.exp(gl - gmax), 0.0), axis=-1, keepdims=True)
    p_grp = 1.0 / psum
    e_lo = N_GROUPS + gidx * EXPERTS_PER_GROUP
    in_grp = (lane >= e_lo) & (lane < e_lo + EXPERTS_PER_GROUP)
    el = jnp.where(in_grp, logits, NEG_INF)
    t1 = jnp.max(el, axis=-1, keepdims=True)
    i1 = jnp.min(jnp.where(el == t1, lane, big), axis=-1, keepdims=True)
    el2 = jnp.where(lane == i1, NEG_INF, el)
    t2 = jnp.max(el2, axis=-1, keepdims=True)
    i2 = jnp.min(jnp.where(el2 == t2, lane, big), axis=-1, keepdims=True)
    e21 = jnp.exp(t2 - t1)
    den = 1.0 + e21
    g1 = p_grp / den
    g2 = p_grp * (e21 / den)
    e1 = i1 - N_GROUPS
    e2 = i2 - N_GROUPS

    hit1 = lane == e1
    hit2 = lane == e2
    onehot = jnp.where(hit1 | hit2, live, 0.0)
    before = jnp.dot(lstrict_ref[...], onehot.astype(BF16), preferred_element_type=F32) + cnt_ref[...]
    r1 = jnp.sum(jnp.where(hit1, before, 0.0), axis=-1, keepdims=True)
    r2 = jnp.sum(jnp.where(hit2, before, 0.0), axis=-1, keepdims=True)
    cnt_ref[...] = cnt_ref[...] + jnp.sum(onehot, axis=0, keepdims=True)
    counts_ref[...] = cnt_ref[...]

    route = jnp.zeros_like(logits)
    for k, val in enumerate((e1, e2, g1, g2, r1, r2)):
        route = jnp.where(lane == k, val, route)
    route_ref[...] = route
    route_t_ref[...] = route.T[0:SUBLANES, :]


def _mixout(att, su, sv, x2d, sgu_w, sgu_bt, sgu_gn, w_out, g_ffn, rw_cat, rw_hi, rb, lstrict, *,
            att_cols):
    t, d = x2d.shape
    tm = ROW_TILE
    n = t // tm
    row = lambda i: (jnp.minimum(i, n - 1), 0)
    routed = lambda i: (jnp.maximum(i - 1, 0), 0)
    fixed2 = lambda i: (0, 0)
    n_groups = sgu_w.shape[0]
    kern = functools.partial(_mixout_kernel, n_sgu_groups=n_groups, att_cols=att_cols)
    return pl.pallas_call(
        kern,
        grid=(n + 1,),
        in_specs=[
            pl.BlockSpec((tm, att.shape[1]), row),
            pl.BlockSpec((tm, su.shape[1]), row),
            pl.BlockSpec((tm, sv.shape[1]), row),
            pl.BlockSpec((tm, d), row),
            pl.BlockSpec(sgu_w.shape, lambda i: (0, 0, 0)),
            pl.BlockSpec(sgu_bt.shape, fixed2),
            pl.BlockSpec(sgu_gn.shape, fixed2),
            pl.BlockSpec(w_out.shape, fixed2),
            pl.BlockSpec((1, d), fixed2),
            pl.BlockSpec(rw_cat.shape, fixed2),
            pl.BlockSpec(rw_hi.shape, fixed2),
            pl.BlockSpec((1, LANES), fixed2),
            pl.BlockSpec((tm, tm), fixed2),
        ],
        out_specs=[
            pl.BlockSpec((tm, d), row),
            pl.BlockSpec((tm, d // 2), routed),
            pl.BlockSpec((tm, LANES), routed),
            pl.BlockSpec((SUBLANES, tm), lambda i: (0, jnp.maximum(i - 1, 0))),
            pl.BlockSpec((1, LANES), fixed2),
        ],
        out_shape=[
            jax.ShapeDtypeStruct((t, d), F32),
            jax.ShapeDtypeStruct((t, d // 2), U32),
            jax.ShapeDtypeStruct((t, LANES), F32),
            jax.ShapeDtypeStruct((SUBLANES, t), F32),
            jax.ShapeDtypeStruct((1, LANES), F32),
        ],
        scratch_shapes=[pltpu.VMEM((1, LANES), F32), pltpu.VMEM((tm, d), F32), pltpu.VMEM((tm, d), F32)],
        compiler_params=_params(1, fuse_inputs=tuple(i == 7 for i in range(13))),
        name="mixout",
    )(att, su, sv, x2d, sgu_w, sgu_bt, sgu_gn, w_out, g_ffn, rw_cat, rw_hi, rb, lstrict)


def _dispatch_kernel(*refs):
    dest_refs, (hn_hbm, xs_ref, ring, in_sem, out_sem) = refs[:TOP_K], refs[TOP_K:]
    i = pl.program_id(0)
    n = pl.num_programs(0)
    n_slots, rows, _ = ring.shape
    parity = i % 2

    def tile_fetch(step, slot):
        return pltpu.make_async_copy(hn_hbm.at[pl.ds(step * rows, rows), :], ring.at[slot], in_sem.at[slot])

    def rows_done(which):
        for _ in range(TOP_K):
            pltpu.make_async_copy(ring.at[0], xs_ref.at[pl.ds(0, rows), :], out_sem.at[which]).wait()

    @pl.when(i == 0)
    def _():
        tile_fetch(0, 0).start()

    for slot in range(n_slots):
        @pl.when(i % n_slots == slot)
        def _():
            @pl.when(i + 1 < n)
            def _():
                tile_fetch(i + 1, (slot + 1) % n_slots).start()

            tile_fetch(i, slot).wait()
            for r in range(rows):
                for k in range(TOP_K):
                    pltpu.make_async_copy(ring.at[slot, pl.ds(r, 1), :],
                                          xs_ref.at[pl.ds(dest_refs[k][r], 1), :],
                                          out_sem.at[parity]).start(priority=k)

    @pl.when(i > 0)
    def _():
        rows_done(1 - parity)

    @pl.when(i == n - 1)
    def _():
        rows_done(parity)


def _dispatch(hn, dests):
    t, d = hn.shape
    rows = DISPATCH_TILE
    index_spec = pl.BlockSpec((rows,), lambda i: (i,), memory_space=pltpu.SMEM)
    return pl.pallas_call(
        _dispatch_kernel,
        grid=(t // rows,),
        in_specs=[index_spec] * TOP_K + [pl.BlockSpec(memory_space=pl.ANY)],
        out_specs=pl.BlockSpec(memory_space=pl.ANY),
        out_shape=jax.ShapeDtypeStruct((t * TOP_K, d), hn.dtype),
        scratch_shapes=[pltpu.VMEM((DISPATCH_RING, rows, d), hn.dtype),
                        pltpu.SemaphoreType.DMA((DISPATCH_RING,)), pltpu.SemaphoreType.DMA((2,))],
        compiler_params=_params(1),
        name="dispatch",
    )(*dests, hn)


def _experts_kernel(vt_ref, ve_ref, vf_ref, vn_ref, vs_ref, vx_ref, lo_ref, hi_ref, xs_ref, wg_hbm, wu_hbm,
                    wd_hbm, ys_ref, wg_f32, wu_f32, wd_f32, wg_bf, wu_bf, wd_bf, act_ref, wsem):
    v = pl.program_id(0)
    flag = vf_ref[v]
    tm = xs_ref.shape[0]

    def weight_copies(expert, slot):
        return [pltpu.make_async_copy(src.at[expert], dst.at[slot], wsem.at[slot])
                for src, dst in ((wg_hbm, wg_f32), (wu_hbm, wu_f32), (wd_hbm, wd_f32))]

    @pl.when(v == 0)
    def _():
        for cp in weight_copies(ve_ref[0], 0):
            cp.start()

    @pl.when(vn_ref[v] == 1)
    def _():
        slot = vs_ref[v]
        for cp in weight_copies(ve_ref[v], slot):
            cp.wait()

        @pl.when(vx_ref[v] >= 0)
        def _():
            for cp in weight_copies(vx_ref[v], 1 - slot):
                cp.start()

        wg_bf[...] = wg_f32[slot].astype(BF16)
        wu_bf[...] = wu_f32[slot].astype(BF16)
        wd_bf[...] = wd_f32[slot].astype(BF16)

    e = ve_ref[v]
    base = vt_ref[v] * tm
    lo = lo_ref[e]
    hi = hi_ref[e]

    def visit_rows(r0, nrows):
        rows = slice(r0, r0 + nrows)
        rid = lax.broadcasted_iota(jnp.int32, (nrows, 1), 0) + (base + r0)
        mine = (rid >= lo) & (rid < hi)
        x = _unpack_bf16_pairs(xs_ref[rows, :]).astype(BF16)
        for n in range(act_ref.shape[1] // MXU_WIDTH):
            cols = slice(n * MXU_WIDTH, (n + 1) * MXU_WIDTH)
            hg = jnp.dot(x, wg_bf[:, cols], preferred_element_type=F32)
            hu = jnp.dot(x, wu_bf[:, cols], preferred_element_type=F32)
            act_ref[rows, cols] = (hg * (1.0 / (1.0 + jnp.exp(-hg))) * hu).astype(BF16)
        y = _pack_bf16_pairs(jnp.dot(act_ref[rows, :], wd_bf[...], preferred_element_type=F32))

        @pl.when(flag == 1)
        def _():
            ys_ref[rows, :] = jnp.where(mine, y, jnp.zeros_like(y))

        @pl.when(flag == 2)
        def _():
            ys_ref[rows, :] = jnp.where(mine, y, ys_ref[rows, :])

    half = tm // 2
    top = (lo < base + half) & (hi > base)
    bottom = (hi > base + half) & (lo < base + tm)

    @pl.when((flag > 0) & top & bottom)
    def _():
        visit_rows(0, tm)

    for r0, wanted, other in ((0, top, bottom), (half, bottom, top)):
        @pl.when((flag > 0) & wanted & jnp.logical_not(other))
        def _():
            visit_rows(r0, half)

        @pl.when((flag == 1) & jnp.logical_not(wanted))
        def _():
            ys_ref[r0:r0 + half, :] = jnp.zeros((half, ys_ref.shape[1]), ys_ref.dtype)


def _experts(visits, row_lo, row_hi, xs, w_gate, w_up, w_down):
    p, words = xs.shape
    d = 2 * words
    tm = EXPERT_TILE
    f = w_gate.shape[-1]
    rows_map = lambda v, vt, *_: (vt[v], 0)
    whole = pl.BlockSpec(memory_space=pl.ANY)
    grid_spec = pltpu.PrefetchScalarGridSpec(
        num_scalar_prefetch=len(visits) + 2,
        grid=(visits[0].shape[0],),
        in_specs=[pl.BlockSpec((tm, words), rows_map), whole, whole, whole],
        out_specs=pl.BlockSpec((tm, words), rows_map),
        scratch_shapes=[pltpu.VMEM((2, d, f), F32), pltpu.VMEM((2, d, f), F32), pltpu.VMEM((2, f, d), F32),
                        pltpu.VMEM((d, f), BF16), pltpu.VMEM((d, f), BF16), pltpu.VMEM((f, d), BF16),
                        pltpu.VMEM((tm, f), BF16), pltpu.SemaphoreType.DMA((2,))],
    )
    return pl.pallas_call(
        _experts_kernel,
        grid_spec=grid_spec,
        out_shape=jax.ShapeDtypeStruct(xs.shape, xs.dtype),
        compiler_params=_params(1),
        name="experts",
    )(*visits, row_lo, row_hi, xs, w_gate, w_up, w_down)


def _combine_kernel(*refs):
    dest_refs, dest_next_refs = refs[:TOP_K], refs[TOP_K:2 * TOP_K]
    h_ref, route_ref, ys_ref, o_ref, ybuf, sem = refs[2 * TOP_K:]
    i = pl.program_id(0)
    rows = h_ref.shape[0]
    slot = i % 2

    def issue(idx_refs, into):
        for r in range(rows):
            for k in range(TOP_K):
                pltpu.make_async_copy(ys_ref.at[pl.ds(idx_refs[k][r], 1), :],
                                      ybuf.at[into, k, pl.ds(r, 1), :], sem.at[into]).start(priority=k)

    @pl.when(i == 0)
    def _():
        issue(dest_refs, 0)

    for parity in range(2):
        @pl.when((i + 1 < pl.num_programs(0)) & (slot == parity))
        def _():
            issue(dest_next_refs, 1 - parity)

    for k in range(TOP_K):
        pltpu.make_async_copy(ys_ref.at[pl.ds(0, rows), :], ybuf.at[slot, k], sem.at[slot]).wait()

    route = route_ref[...]
    moe = None
    for k in range(TOP_K):
        gated = route[:, ROUTE_GATE + k:ROUTE_GATE + k + 1] * _unpack_bf16_pairs(ybuf[slot, k])
        moe = gated if moe is None else moe + gated
    o_ref[...] = h_ref[...] + moe


def _combine(dests, h, route, ys):
    t, d = h.shape
    rows = COMBINE_TILE
    n = t // rows
    this_step = pl.BlockSpec((rows,), lambda i: (i,), memory_space=pltpu.SMEM)
    next_step = pl.BlockSpec((rows,), lambda i: (jnp.minimum(i + 1, n - 1),), memory_space=pltpu.SMEM)
    return pl.pallas_call(
        _combine_kernel,
        grid=(n,),
        in_specs=[this_step] * TOP_K + [next_step] * TOP_K + [
            pl.BlockSpec((rows, d), lambda i: (i, 0)),
            pl.BlockSpec((rows, LANES), lambda i: (i, 0)),
            pl.BlockSpec(memory_space=pl.ANY),
        ],
        out_specs=pl.BlockSpec((rows, d), lambda i: (i, 0)),
        out_shape=jax.ShapeDtypeStruct((t, d), F32),
        scratch_shapes=[pltpu.VMEM((2, TOP_K, rows, ys.shape[1]), ys.dtype), pltpu.SemaphoreType.DMA((2,))],
        compiler_params=_params(1),
        name="combine",
    )(*dests, *dests, h, route, ys)


def _plan(route_t, counts_row):
    t = route_t.shape[1]
    counts = counts_row[0, :N_EXPERTS].astype(I32)
    ends = jnp.cumsum(counts)
    offs = ends - counts
    experts = route_t[ROUTE_EXPERT:ROUTE_EXPERT + TOP_K].astype(I32)
    rank = route_t[ROUTE_RANK:ROUTE_RANK + TOP_K].astype(I32)
    expert_ids = jnp.arange(N_EXPERTS, dtype=I32)[:, None, None]
    first_row = jnp.sum(jnp.where(experts[None] == expert_ids, offs[:, None, None], 0), axis=0)
    dest = first_row + rank
    dests = tuple(dest[k] for k in range(TOP_K))

    tm = EXPERT_TILE
    n_visits = (t * TOP_K) // tm + N_EXPERTS - 1
    first_tile = offs // tm
    last_tile = (ends - 1) // tm
    nvis = jnp.where(counts > 0, last_tile - first_tile + 1, 0)
    vend = jnp.cumsum(nvis)
    vstart = vend - nvis
    v = jnp.arange(n_visits, dtype=I32)
    ev = jnp.minimum(jnp.sum((vend[None, :] <= v[:, None]).astype(I32), axis=1), N_EXPERTS - 1)
    tv = jnp.take(first_tile, ev) + (v - jnp.take(vstart, ev))
    valid = v < vend[-1]
    last_v = vend[-1] - 1
    ev = jnp.where(valid, ev, jnp.take(ev, last_v))
    tv = jnp.where(valid, tv, jnp.take(tv, last_v))
    prev_t = jnp.concatenate([jnp.full((1,), -1, I32), tv[:-1]])
    prev_e = jnp.concatenate([jnp.full((1,), -1, I32), ev[:-1]])
    flag = jnp.where(valid, jnp.where(tv != prev_t, 1, 2), 0).astype(I32)
    new_expert = (ev != prev_e).astype(I32)
    slot = (jnp.cumsum(new_expert) - 1) % 2
    ids = jnp.arange(N_EXPERTS, dtype=I32)
    later = (ids[None, :] > ids[:, None]) & (counts[None, :] > 0)
    next_active = jnp.min(jnp.where(later, ids[None, :], N_EXPERTS), axis=1)
    next_active = jnp.where(next_active < N_EXPERTS, next_active, -1)
    visits = (tv.astype(I32), ev.astype(I32), flag, new_expert, slot.astype(I32),
              jnp.take(next_active, ev).astype(I32))
    return dests, visits, offs.astype(I32), ends.astype(I32)


def _rope_constants(head_dim, rope_dim):
    rope_half = rope_dim // 2
    inv_freq = ROPE_THETA ** (-jnp.arange(0, rope_dim, 2, dtype=F32) / rope_dim)
    dim = jnp.arange(LANES, dtype=I32)[None, :] % head_dim
    k = jnp.arange(4 * rope_half, dtype=I32)[:, None]
    c_mat = jnp.where(dim < rope_dim, k == dim % rope_half, k == rope_dim)
    sa_mat = (dim < rope_half) & (k == rope_half + dim)
    sb_mat = (dim >= rope_half) & (dim < rope_dim) & (k == dim)
    expand = jnp.concatenate([c_mat.astype(F32), -sa_mat.astype(F32), sb_mat.astype(F32)], axis=1)
    return inv_freq.reshape(rope_half, 1), expand.astype(BF16)


def kernel(x, positions, norm_mix_g, w_in, q_norm_g, k_norm_g, lambda_q1, lambda_k1, lambda_q2,
           lambda_k2, diff_out_norm_g, sgu_ln_g, sgu_ln_b, sgu_w, sgu_b, sgu_out_norm_g, w_out,
           norm_ffn_g, router_group_w, router_group_b, router_expert_w, router_expert_b,
           expert_w_gate, expert_w_up, expert_w_down):
    b, s, d = x.shape
    t = b * s
    depth = w_in.shape[0]
    head_dim = q_norm_g.shape[-1]
    vd = diff_out_norm_g.shape[-1]
    n_sgu_groups = sgu_w.shape[1]
    sgu_cols = n_sgu_groups * sgu_w.shape[-1]
    in_cols = w_in.shape[-1]
    att_cols = (in_cols - 2 * sgu_cols) // 3
    n_heads = att_cols // vd
    qk_cols = 2 * att_cols
    rope_dim = head_dim // ROPE_FRACTION
    rope_half = rope_dim // 2
    scale = head_dim ** -0.5

    pos_rows = positions.reshape(t // ROW_TILE, 1, ROW_TILE).astype(F32)
    freq_col, expand = _rope_constants(head_dim, rope_dim)
    blk = jnp.arange(MXU_WIDTH, dtype=I32) // head_dim
    ones_blk = (blk[:, None] == blk[None, :]).astype(BF16)
    tri = jnp.arange(ROW_TILE, dtype=I32)
    lstrict = (tri[None, :] < tri[:, None]).astype(BF16)

    h = x.reshape(t, d)
    for l in range(depth):
        lambda_init = 0.8 - 0.6 * math.exp(-0.3 * l)
        gqk = jnp.concatenate([jnp.tile(q_norm_g[l] * (scale * math.log2(math.e)), att_cols // head_dim),
                               jnp.tile(k_norm_g[l], att_cols // head_dim)]).reshape(1, qk_cols)
        qk, v, su, sv = _inproj(
            h, norm_mix_g[l].reshape(1, d), w_in[l].astype(BF16), gqk, pos_rows, freq_col, expand, ones_blk,
            sgu_ln_g[l].reshape(1, sgu_cols), sgu_ln_b[l].reshape(1, sgu_cols),
            qk_cols=qk_cols, v_cols=att_cols, sgu_cols=sgu_cols, head_dim=head_dim,
            rope_half=rope_half)

        lam_vecs = jnp.stack([lambda_q1[l], lambda_k1[l], lambda_q2[l], lambda_k2[l]]).astype(F32)
        score_bound = (1.02 * head_dim * scale * jnp.max(jnp.abs(q_norm_g[l]))
                       * jnp.max(jnp.abs(k_norm_g[l])))
        attn_args = (lam_vecs, diff_out_norm_g[l].reshape(1, vd), qk.reshape(b, s, qk_cols),
                     v.reshape(b, s, att_cols))
        attn_kw = dict(n_heads=n_heads, head_dim=head_dim, lambda_init=lambda_init)
        att = lax.cond(score_bound > MAX_UNSHIFTED_SCORE,
                       functools.partial(_attention_running_max, **attn_kw),
                       functools.partial(_attention_static, **attn_kw),
                       *attn_args).reshape(t, att_cols)

        rw = jnp.concatenate([router_group_w[l], router_expert_w[l]], axis=1)
        rw = jnp.pad(rw, ((0, 0), (0, LANES - rw.shape[1])))
        rw_hi, rw_lo = _split_bf16(rw)
        rb = jnp.pad(jnp.concatenate([router_group_b[l], router_expert_b[l]]),
                     (0, LANES - N_GROUPS - N_EXPERTS)).reshape(1, LANES)
        h, hn, route, route_t, counts_row = _mixout(
            att, su, sv, h, sgu_w[l], sgu_b[l].T, sgu_out_norm_g[l], w_out[l].astype(BF16),
            norm_ffn_g[l].reshape(1, d), jnp.concatenate([rw_hi, rw_lo], axis=1), rw_hi, rb, lstrict,
            att_cols=att_cols)

        dests, visits, row_lo, row_hi = _plan(route_t, counts_row)
        xs = _dispatch(hn, dests)
        ys = _experts(visits, row_lo, row_hi, xs, expert_w_gate[l], expert_w_up[l], expert_w_down[l])
        h = _combine(dests, h, route, ys)
    return h.reshape(b, s, d)
```
